```python
import jax, jax.numpy as jnp
from jax import lax
import numpy as np

D_MODEL = 4096
BATCH = 1
SEQ = 8192
DEPTH = 2
DEC_BATCH = 16
DEC_SEQ = 64
PAST_LEN = 1024

CHUNK = 64
Q_BLOCK = 128
HEAD_DIM = 128
ROPE_THETA = 10000.0
A_HEADS = 8
IDX_HEADS = 32
IDX_DIM = 64
TOPK_MAX = 256
B_HEADS = 8
FORGET_BIAS_INIT = 2.0
C_HEADS = 16
C_Q_RANK = 768
C_KV_RANK = 256
C_NOPE = 128
C_ROPE = 64
C_V = 128
MIX_WIDTH = (A_HEADS + B_HEADS) * HEAD_DIM + C_HEADS * C_V
N_MEM = 256
MEM_HEADS = 4
MEM_DIM = 128
N_GROUPS = 4
EXPERTS_PER_GROUP = 4
N_EXPERTS = N_GROUPS * EXPERTS_PER_GROUP
TOPK_IN_GROUP = 2
EXPERT_FF = 512
DEEPNORM_ALPHA = (2 * DEPTH) ** 0.25
DEEPNORM_BETA = (8 * DEPTH) ** -0.25
LN_EPS = 1e-5
RMS_EPS = 1e-6
IN_SIZES = (A_HEADS * HEAD_DIM, HEAD_DIM, HEAD_DIM, IDX_HEADS * IDX_DIM, IDX_DIM, IDX_HEADS,
            B_HEADS * HEAD_DIM, B_HEADS * HEAD_DIM, B_HEADS * HEAD_DIM, B_HEADS,
            C_Q_RANK, C_KV_RANK, C_ROPE)
IN_WIDTH = sum(IN_SIZES)

kernel_name = 'hybrid_streaming_encoder_step'


def _split_cols(z, sizes):
    cuts = [int(c) for c in np.cumsum(sizes)[:-1]]
    return jnp.split(z, cuts, axis=-1)


def _rope(x, pos):
    d = x.shape[-1]
    half = d // 2
    inv = ROPE_THETA ** (-jnp.arange(half, dtype=jnp.float32) * 2.0 / d)
    ang = pos.astype(jnp.float32)[:, None] * inv[None, :]
    shape = (1, pos.shape[0]) + (1,) * (x.ndim - 3) + (half,)
    cos = jnp.cos(ang).reshape(shape)
    sin = jnp.sin(ang).reshape(shape)
    xf = x.astype(jnp.float32)
    x1, x2 = xf[..., :half], xf[..., half:]
    return jnp.concatenate([x1 * cos - x2 * sin, x2 * cos + x1 * sin], axis=-1).astype(x.dtype)


def _layer_norm(x, g, b):
    xf = x.astype(jnp.float32)
    xc = xf - jnp.mean(xf, axis=-1, keepdims=True)
    var = jnp.mean(xc * xc, axis=-1, keepdims=True)
    return (xc * lax.rsqrt(var + LN_EPS) * g.astype(jnp.float32) + b.astype(jnp.float32)).astype(x.dtype)


def _rms_norm(x, g):
    xf = x.astype(jnp.float32)
    y = xf * lax.rsqrt(jnp.mean(xf * xf, axis=-1, keepdims=True) + RMS_EPS)
    return (y * g.astype(jnp.float32)).astype(x.dtype)


def _chunk_visible(q_pos, k_pos):
    return (k_pos[None, :] // CHUNK) <= (q_pos[:, None] // CHUNK)


def _sweep(fn, q_args, q_pos, kv_args):
    T = q_pos.shape[0]
    if T <= Q_BLOCK:
        return fn(q_args, q_pos, kv_args)
    nb = T // Q_BLOCK
    qb = tuple(jnp.moveaxis(a.reshape((a.shape[0], nb, Q_BLOCK) + a.shape[2:]), 1, 0) for a in q_args)
    pb = q_pos.reshape(nb, Q_BLOCK)
    out = lax.map(lambda xs: fn(xs[0], xs[1], kv_args), (qb, pb))
    out = jnp.moveaxis(out, 0, 1)
    return out.reshape((out.shape[0], T) + out.shape[3:])


def _dsa_block(q_args, q_pos, kv_args):
    q, qi, wi = q_args
    k, v, ki, k_pos = kv_args
    S = k.shape[1]
    topk = min(TOPK_MAX, S // 4)
    vis = _chunk_visible(q_pos, k_pos)
    isc = jnp.einsum('bqhd,bsd->bqhs', qi, ki).astype(jnp.float32) * (IDX_DIM ** -0.5)
    isc = jnp.einsum('bqhs,bqh->bqs', jax.nn.relu(isc), wi.astype(jnp.float32))
    isc = jnp.where(vis[None], isc, -jnp.inf)
    _, idx = lax.top_k(isc, topk)
    kg = jax.vmap(lambda kb, ib: kb[ib])(k, idx)
    vg = jax.vmap(lambda vb, ib: vb[ib])(v, idx)
    valid = (k_pos[idx] // CHUNK) <= (q_pos[None, :, None] // CHUNK)
    s = jnp.einsum('bqhd,bqkd->bqhk', q, kg).astype(jnp.float32) * (HEAD_DIM ** -0.5)
    s = jnp.where(valid[:, :, None, :], s, -jnp.inf)
    p = jax.nn.softmax(s, axis=-1).astype(v.dtype)
    return jnp.einsum('bqhk,bqkd->bqhd', p, vg)


def _fox_block(q_args, q_pos, kv_args):
    q, cq = q_args
    k, v, ckt, k_pos = kv_args
    s = jnp.einsum('bqhd,bshd->bhqs', q, k).astype(jnp.float32) * (HEAD_DIM ** -0.5)
    s = s + jnp.transpose(cq, (0, 2, 1))[..., None] - ckt[:, :, None, :]
    s = jnp.where((k_pos[None, :] <= q_pos[:, None])[None, None], s, -jnp.inf)
    p = jax.nn.softmax(s, axis=-1).astype(v.dtype)
    return jnp.einsum('bhqs,bshd->bqhd', p, v)


def _mla_block(q_args, q_pos, kv_args):
    qn, qr = q_args
    kn, kr, v, k_pos = kv_args
    s = (jnp.einsum('bqhd,bshd->bhqs', qn, kn) + jnp.einsum('bqhr,bsr->bhqs', qr, kr)).astype(jnp.float32)
    s = s * ((C_NOPE + C_ROPE) ** -0.5)
    s = jnp.where(_chunk_visible(q_pos, k_pos)[None, None], s, -jnp.inf)
    p = jax.nn.softmax(s, axis=-1).astype(v.dtype)
    return jnp.einsum('bhqs,bshd->bqhd', p, v)


def _project(h, pos, l, P):
    B, T, _ = h.shape
    z = jnp.einsum('btd,dn->btn', h, P['w_in'][l])
    aq, ak, av, aqi, aki, aw, bq, bk, bv, bf, cq, ckv, ckr = _split_cols(z, IN_SIZES)
    cq = jnp.einsum('btr,rn->btn', _rms_norm(cq, P['c_q_norm'][l]), P['w_c_uq'][l])
    cq = cq.reshape(B, T, C_HEADS, C_NOPE + C_ROPE)
    return dict(
        aq=_rope(aq.reshape(B, T, A_HEADS, HEAD_DIM), pos),
        ak=_rope(ak, pos),
        av=av,
        aqi=_rope(aqi.reshape(B, T, IDX_HEADS, IDX_DIM), pos),
        aki=_rope(aki, pos),
        aw=aw * (IDX_HEADS ** -0.5),
        bq=bq.reshape(B, T, B_HEADS, HEAD_DIM),
        bk=bk.reshape(B, T, B_HEADS, HEAD_DIM),
        bv=bv.reshape(B, T, B_HEADS, HEAD_DIM),
        blogf=jax.nn.log_sigmoid(bf.astype(jnp.float32) + P['b_f'][l].astype(jnp.float32)),
        cqn=cq[..., :C_NOPE],
        cqr=_rope(cq[..., C_NOPE:], pos),
        clat=_rms_norm(ckv, P['c_kv_norm'][l]),
        ckr=_rope(ckr, pos))


def _token_mixers(h, pos, l, P, past):
    B, T, _ = h.shape
    n = _project(h, pos, l, P)
    rows = (n['ak'], n['av'], n['aki'], n['bk'], n['bv'], n['blogf'], n['clat'], n['ckr'])
    if past is None:
        full, k_pos = rows, pos
    else:
        full = tuple(jnp.concatenate([c.astype(r.dtype), r], axis=1) for c, r in zip(past, rows))
        k_pos = jnp.concatenate([jnp.arange(past[0].shape[1], dtype=jnp.int32), pos])
    ka, va, kia, kb, vb, logf, lat, kr = full
    S = ka.shape[1]
    oa = _sweep(_dsa_block, (n['aq'], n['aqi'], n['aw']), pos, (ka, va, kia, k_pos))
    csum = jnp.cumsum(logf, axis=1)
    ob = _sweep(_fox_block, (n['bq'], csum[:, S - T:]), pos, (kb, vb, jnp.transpose(csum, (0, 2, 1)), k_pos))
    kn = jnp.einsum('bsr,rn->bsn', lat, P['w_c_uk'][l]).reshape(B, S, C_HEADS, C_NOPE)
    vc = jnp.einsum('bsr,rn->bsn', lat, P['w_c_uv'][l]).reshape(B, S, C_HEADS, C_V)
    oc = _sweep(_mla_block, (n['cqn'], n['cqr']), pos, (kn, kr, vc, k_pos))
    mixed = jnp.concatenate([oa.reshape(B, T, -1), ob.reshape(B, T, -1), oc.reshape(B, T, -1)], axis=-1)
    return jnp.einsum('btm,md->btd', mixed, P['w_o'][l]), rows


def _mem_kv(mem, l, P):
    B, N, _ = mem.shape
    mk = jnp.einsum('bnd,de->bne', mem, P['w_mk'][l]).reshape(B, N, MEM_HEADS, MEM_DIM)
    mv = jnp.einsum('bnd,de->bne', mem, P['w_mv'][l]).reshape(B, N, MEM_HEADS, MEM_DIM)
    return mk, mv


def _mem_attend(h, mk, mv, l, P):
    B, T, _ = h.shape
    q = jnp.einsum('btd,de->bte', h, P['w_mq'][l]).reshape(B, T, MEM_HEADS, MEM_DIM)
    s = jnp.einsum('bqhd,bmhd->bhqm', q, mk.astype(q.dtype)).astype(jnp.float32) * (MEM_DIM ** -0.5)
    p = jax.nn.softmax(s, axis=-1).astype(q.dtype)
    o = jnp.einsum('bhqm,bmhd->bqhd', p, mv.astype(q.dtype)).reshape(B, T, MEM_HEADS * MEM_DIM)
    return jnp.einsum('bte,ed->btd', o, P['w_mo'][l])


def _moe(h, l, P):
    B, T, D = h.shape
    hf = h.reshape(B * T, D)
    N = hf.shape[0]
    rows = jnp.arange(N)
    lg = jnp.einsum('nd,dg->ng', hf, P['w_rg'][l]).astype(jnp.float32) + P['b_rg'][l].astype(jnp.float32)
    grp = jnp.argmax(lg, axis=-1)
    g1 = jax.nn.softmax(lg, axis=-1)[rows, grp]
    le = jnp.einsum('nd,gde->nge', hf, P['w_re'][l]).astype(jnp.float32) + P['b_re'][l].astype(jnp.float32)
    le = le[rows, grp]
    tv, ti = lax.top_k(le, TOPK_IN_GROUP)
    gate = g1[:, None] * jax.nn.softmax(tv, axis=-1)
    eid = grp[:, None] * EXPERTS_PER_GROUP + ti
    comb = jnp.sum(jax.nn.one_hot(eid, N_EXPERTS, dtype=jnp.float32) * gate[..., None], axis=1)
    hg = jnp.einsum('nd,edf->nef', hf, P['w_gate'][l])
    hu = jnp.einsum('nd,edf->nef', hf, P['w_up'][l])
    act = jax.nn.silu(hg) * hu * comb[..., None].astype(hf.dtype)
    return jnp.einsum('nef,efd->nd', act, P['w_down'][l]).reshape(B, T, D)


def _layer(x, pos, l, P, mk, mv, past):
    o, rows = _token_mixers(x, pos, l, P, past)
    x = _layer_norm(DEEPNORM_ALPHA * x + o, P['ln1_g'][l], P['ln1_b'][l])
    x = _layer_norm(DEEPNORM_ALPHA * x + _mem_attend(x, mk, mv, l, P), P['ln2_g'][l], P['ln2_b'][l])
    x = _layer_norm(DEEPNORM_ALPHA * x + _moe(x, l, P), P['ln3_g'][l], P['ln3_b'][l])
    return x, rows


def setup_inputs(seed: int = 0) -> dict:
    key = jax.random.key(seed)
    ks = iter(jax.random.split(key, 48))

    def nrm(shape, scale=1.0):
        return jax.random.normal(next(ks), shape, jnp.float32) * scale

    beta = DEEPNORM_BETA
    d = {}
    d['x_prompt'] = nrm((BATCH, SEQ, D_MODEL))
    d['x_sample'] = nrm((DEC_BATCH, DEC_SEQ, D_MODEL))
    d['cache_a_k'] = nrm((DEPTH, DEC_BATCH, PAST_LEN, HEAD_DIM))
    d['cache_a_v'] = nrm((DEPTH, DEC_BATCH, PAST_LEN, HEAD_DIM))
    d['cache_a_kidx'] = nrm((DEPTH, DEC_BATCH, PAST_LEN, IDX_DIM))
    d['cache_b_k'] = nrm((DEPTH, DEC_BATCH, PAST_LEN, B_HEADS, HEAD_DIM))
    d['cache_b_v'] = nrm((DEPTH, DEC_BATCH, PAST_LEN, B_HEADS, HEAD_DIM))
    d['cache_b_logf'] = jax.nn.log_sigmoid(FORGET_BIAS_INIT + nrm((DEPTH, DEC_BATCH, PAST_LEN, B_HEADS), 0.5))
    d['cache_c_latent'] = nrm((DEPTH, DEC_BATCH, PAST_LEN, C_KV_RANK))
    d['cache_c_krope'] = nrm((DEPTH, DEC_BATCH, PAST_LEN, C_ROPE))
    d['cache_mem_k'] = nrm((DEPTH, DEC_BATCH, N_MEM, MEM_HEADS, MEM_DIM))
    d['cache_mem_v'] = nrm((DEPTH, DEC_BATCH, N_MEM, MEM_HEADS, MEM_DIM))
    d['mem_prompt'] = nrm((BATCH, N_MEM, D_MODEL))
    d['w_in'] = nrm((DEPTH, D_MODEL, IN_WIDTH), D_MODEL ** -0.5)
    d['b_f'] = FORGET_BIAS_INIT + nrm((DEPTH, B_HEADS), 0.5)
    d['c_q_norm'] = 1.0 + nrm((DEPTH, C_Q_RANK), 0.05)
    d['w_c_uq'] = nrm((DEPTH, C_Q_RANK, C_HEADS * (C_NOPE + C_ROPE)), C_Q_RANK ** -0.5)
    d['c_kv_norm'] = 1.0 + nrm((DEPTH, C_KV_RANK), 0.05)
    d['w_c_uk'] = nrm((DEPTH, C_KV_RANK, C_HEADS * C_NOPE), C_KV_RANK ** -0.5)
    d['w_c_uv'] = nrm((DEPTH, C_KV_RANK, C_HEADS * C_V), C_KV_RANK ** -0.5)
    d['w_o'] = nrm((DEPTH, MIX_WIDTH, D_MODEL), beta * MIX_WIDTH ** -0.5)
    d['ln1_g'] = 1.0 + nrm((DEPTH, D_MODEL), 0.05)
    d['ln1_b'] = nrm((DEPTH, D_MODEL), 0.02)
    d['w_mq'] = nrm((DEPTH, D_MODEL, MEM_HEADS * MEM_DIM), D_MODEL ** -0.5)
    d['w_mk'] = nrm((DEPTH, D_MODEL, MEM_HEADS * MEM_DIM), D_MODEL ** -0.5)
    d['w_mv'] = nrm((DEPTH, D_MODEL, MEM_HEADS * MEM_DIM), D_MODEL ** -0.5)
    d['w_mo'] = nrm((DEPTH, MEM_HEADS * MEM_DIM, D_MODEL), beta * (MEM_HEADS * MEM_DIM) ** -0.5)
    d['ln2_g'] = 1.0 + nrm((DEPTH, D_MODEL), 0.05)
    d['ln2_b'] = nrm((DEPTH, D_MODEL), 0.02)
    d['w_rg'] = nrm((DEPTH, D_MODEL, N_GROUPS), D_MODEL ** -0.5)
    d['b_rg'] = nrm((DEPTH, N_GROUPS), 0.01)
    d['w_re'] = nrm((DEPTH, N_GROUPS, D_MODEL, EXPERTS_PER_GROUP), D_MODEL ** -0.5)
    d['b_re'] = nrm((DEPTH, N_GROUPS, EXPERTS_PER_GROUP), 0.01)
    d['w_gate'] = nrm((DEPTH, N_EXPERTS, D_MODEL, EXPERT_FF), D_MODEL ** -0.5)
    d['w_up'] = nrm((DEPTH, N_EXPERTS, D_MODEL, EXPERT_FF), D_MODEL ** -0.5)
    d['w_down'] = nrm((DEPTH, N_EXPERTS, EXPERT_FF, D_MODEL), beta * EXPERT_FF ** -0.5)
    d['ln3_g'] = 1.0 + nrm((DEPTH, D_MODEL), 0.05)
    d['ln3_b'] = nrm((DEPTH, D_MODEL), 0.02)
    return d


def reference(x_prompt, x_sample, cache_a_k, cache_a_v, cache_a_kidx, cache_b_k, cache_b_v, cache_b_logf,
              cache_c_latent, cache_c_krope, cache_mem_k, cache_mem_v, mem_prompt,
              w_in, b_f, c_q_norm, w_c_uq, c_kv_norm, w_c_uk, w_c_uv, w_o, ln1_g, ln1_b,
              w_mq, w_mk, w_mv, w_mo, ln2_g, ln2_b, w_rg, b_rg, w_re, b_re,
              w_gate, w_up, w_down, ln3_g, ln3_b):
    P = dict(w_in=w_in, b_f=b_f, c_q_norm=c_q_norm, w_c_uq=w_c_uq, c_kv_norm=c_kv_norm, w_c_uk=w_c_uk,
             w_c_uv=w_c_uv, w_o=w_o, ln1_g=ln1_g, ln1_b=ln1_b, w_mq=w_mq, w_mk=w_mk, w_mv=w_mv, w_mo=w_mo,
             ln2_g=ln2_g, ln2_b=ln2_b, w_rg=w_rg, b_rg=b_rg, w_re=w_re, b_re=b_re, w_gate=w_gate,
             w_up=w_up, w_down=w_down, ln3_g=ln3_g, ln3_b=ln3_b)
    pos_p = jnp.arange(x_prompt.shape[1], dtype=jnp.int32)
    pos_s = cache_a_k.shape[2] + jnp.arange(x_sample.shape[1], dtype=jnp.int32)
    xp, xs = x_prompt, x_sample
    rows_p, rows_s, mk_p, mv_p = [], [], [], []
    for l in range(DEPTH):
        mk, mv = _mem_kv(mem_prompt, l, P)
        xp, rp = _layer(xp, pos_p, l, P, mk, mv, None)
        past = (cache_a_k[l], cache_a_v[l], cache_a_kidx[l], cache_b_k[l], cache_b_v[l], cache_b_logf[l],
                cache_c_latent[l], cache_c_krope[l])
        xs, rs = _layer(xs, pos_s, l, P, cache_mem_k[l], cache_mem_v[l], past)
        rows_p.append(rp)
        rows_s.append(rs)
        mk_p.append(mk)
        mv_p.append(mv)
    (p_a_k, p_a_v, p_a_kidx, p_b_k, p_b_v, p_b_logf, p_c_lat, p_c_kr) = [
        jnp.stack([r[i] for r in rows_p]) for i in range(8)]
    (s_a_k, s_a_v, s_a_kidx, s_b_k, s_b_v, s_b_logf, s_c_lat, s_c_kr) = [
        jnp.stack([r[i] for r in rows_s]) for i in range(8)]
    p_mem_k = jnp.stack(mk_p)
    p_mem_v = jnp.stack(mv_p)
    return (xp, xs, p_a_k, p_a_v, p_a_kidx, p_b_k, p_b_v, p_b_logf, p_c_lat, p_c_kr, p_mem_k, p_mem_v,
            s_a_k, s_a_v, s_a_kidx, s_b_k, s_b_v, s_b_logf, s_c_lat, s_c_kr)
```

```python
import functools

import jax
import jax.numpy as jnp
import numpy as np
from jax import lax
from jax.experimental import pallas as pl
from jax.experimental.pallas import tpu as pltpu

F32 = jnp.float32
BF16 = jnp.bfloat16
I32 = jnp.int32

LANES = 128
VMEM_LIMIT = 56 * 1024 * 1024

CHUNK_SHIFT = 6
HEAD_DIM = 128
ROPE_THETA = 10000.0
A_HEADS = 8
IDX_HEADS = 32
IDX_DIM = 64
TOPK_MAX = 256
B_HEADS = 8
C_HEADS = 16
C_Q_RANK = 768
C_KV_RANK = 256
C_NOPE = 128
C_ROPE = 64
C_V = 128
MEM_HEADS = 4
MEM_DIM = 128
N_GROUPS = 4
EXPERTS_PER_GROUP = 4
N_EXPERTS = N_GROUPS * EXPERTS_PER_GROUP
LN_EPS = 1e-5
RMS_EPS = 1e-6

_IN_SIZES = (A_HEADS * HEAD_DIM, HEAD_DIM, HEAD_DIM, IDX_HEADS * IDX_DIM, IDX_DIM, IDX_HEADS,
             B_HEADS * HEAD_DIM, B_HEADS * HEAD_DIM, B_HEADS * HEAD_DIM, B_HEADS,
             C_Q_RANK, C_KV_RANK, C_ROPE)
_IN_OFF = np.concatenate([[0], np.cumsum(_IN_SIZES)]).astype(int)

Z_AQ = 0
Z_AK = Z_AQ + A_HEADS * HEAD_DIM
Z_AV = Z_AK + HEAD_DIM
Z_AQI = Z_AV + HEAD_DIM
Z_BQ = Z_AQI + IDX_HEADS * IDX_DIM
Z_BK = Z_BQ + B_HEADS * HEAD_DIM
Z_BV = Z_BK + B_HEADS * HEAD_DIM
Z_CQ = Z_BV + B_HEADS * HEAD_DIM
Z_CKV = Z_CQ + C_Q_RANK
Z_SM1 = Z_CKV + C_KV_RANK
Z_SM2 = Z_SM1 + LANES
Z_WIDTH = Z_SM2 + LANES
SM2_BF = IDX_HEADS

MASKED = -1e30
INT_MIN = -2 ** 31
INT_MAX = 2 ** 31 - 1
KEY_BLOCK = 256


def _params(*sem):
    return pltpu.CompilerParams(dimension_semantics=sem, vmem_limit_bytes=VMEM_LIMIT)


def _mm_kernel(x_ref, w_ref, o_ref):
    x = x_ref[...].astype(BF16)
    o_ref[...] = jnp.dot(x, w_ref[...], preferred_element_type=F32).astype(o_ref.dtype)


def _matmul(x, w, *, bm, bn, out_dtype, name):
    m, k = x.shape
    n = w.shape[1]
    assert m % bm == 0 and n % bn == 0, (x.shape, w.shape, bm, bn)
    return pl.pallas_call(
        _mm_kernel,
        grid=(m // bm, n // bn),
        in_specs=[pl.BlockSpec((bm, k), lambda i, j: (i, 0)),
                  pl.BlockSpec((k, bn), lambda i, j: (0, j))],
        out_specs=pl.BlockSpec((bm, bn), lambda i, j: (i, j)),
        out_shape=jax.ShapeDtypeStruct((m, n), out_dtype),
        compiler_params=_params("parallel", "parallel"),
        name=name,
    )(x, w)


def _rope64(x, a, b, c):
    return x * a + pltpu.roll(x, 96, 1) * b + pltpu.roll(x, 32, 1) * c


def _rope128(x, a, b):
    return x * a + pltpu.roll(x, 64, 1) * b


def _mm_rope_kernel(x_ref, w_ref, a_ref, b_ref, c_ref, o_ref, *, first_rope_block):
    j = pl.program_id(1)
    acc = jnp.dot(x_ref[...], w_ref[...], preferred_element_type=F32)

    @pl.when(j < first_rope_block)
    def _():
        o_ref[...] = acc.astype(o_ref.dtype)

    @pl.when(j >= first_rope_block)
    def _():
        a, b, c = a_ref[...], b_ref[...], c_ref[...]
        for g in range(acc.shape[1] // LANES):
            sl = slice(g * LANES, (g + 1) * LANES)
            o_ref[:, sl] = _rope64(acc[:, sl], a, b, c).astype(o_ref.dtype)


def _matmul_rope(x, w, tabs, *, bm, bn, first_rope_col, name):
    m, k = x.shape
    n = w.shape[1]
    assert m % bm == 0 and n % bn == 0 and first_rope_col % bn == 0
    tab_spec = pl.BlockSpec((bm, LANES), lambda i, j: (i, 0))
    return pl.pallas_call(
        functools.partial(_mm_rope_kernel, first_rope_block=first_rope_col // bn),
        grid=(m // bm, n // bn),
        in_specs=[pl.BlockSpec((bm, k), lambda i, j: (i, 0)),
                  pl.BlockSpec((k, bn), lambda i, j: (0, j)),
                  tab_spec, tab_spec, tab_spec],
        out_specs=pl.BlockSpec((bm, bn), lambda i, j: (i, j)),
        out_shape=jax.ShapeDtypeStruct((m, n), BF16),
        compiler_params=_params("parallel", "parallel"),
        name=name,
    )(x, w, *tabs)


def _post_kernel(z_ref, a128_ref, b128_ref, a64_ref, b64_ref, c64_ref, bf_ref, gq_ref, gkv_ref,
                 aq_ref, ak_ref, aqi_ref, cqn_ref, clat_ref, sm1_ref, sm2_ref):
    a128, b128 = a128_ref[...], b128_ref[...]
    a64, b64, c64 = a64_ref[...], b64_ref[...], c64_ref[...]
    for g in range(A_HEADS):
        src = slice(Z_AQ + g * LANES, Z_AQ + (g + 1) * LANES)
        aq_ref[:, g * LANES:(g + 1) * LANES] = _rope128(z_ref[:, src], a128, b128).astype(BF16)
    ak_ref[...] = _rope128(z_ref[:, Z_AK:Z_AK + LANES], a128, b128)
    for g in range(IDX_HEADS * IDX_DIM // LANES):
        src = slice(Z_AQI + g * LANES, Z_AQI + (g + 1) * LANES)
        aqi_ref[:, g * LANES:(g + 1) * LANES] = _rope64(z_ref[:, src], a64, b64, c64).astype(BF16)
    cq = z_ref[:, Z_CQ:Z_CQ + C_Q_RANK]
    cq = cq * lax.rsqrt(jnp.mean(cq * cq, axis=-1, keepdims=True) + RMS_EPS) * gq_ref[...]
    cqn_ref[...] = cq.astype(BF16)
    ckv = z_ref[:, Z_CKV:Z_CKV + C_KV_RANK]
    clat_ref[...] = ckv * lax.rsqrt(jnp.mean(ckv * ckv, axis=-1, keepdims=True) + RMS_EPS) * gkv_ref[...]
    sm1_ref[...] = _rope64(z_ref[:, Z_SM1:Z_SM1 + LANES], a64, b64, c64)
    s2 = z_ref[:, Z_SM2:Z_SM2 + LANES]
    lane = lax.broadcasted_iota(I32, s2.shape, 1)
    f = s2 + bf_ref[...]
    logf = jnp.minimum(f, 0.0) - jnp.log1p(jnp.exp(-jnp.abs(f)))
    sm2_ref[...] = jnp.where(lane < SM2_BF, s2 * (IDX_HEADS ** -0.5),
                             jnp.where(lane < SM2_BF + B_HEADS, logf, 0.0))


def _post(z, tabs128, tabs64, bf_row, gq, gkv, *, bm):
    n = z.shape[0]
    row = lambda w: pl.BlockSpec((bm, w), lambda i: (i, 0))
    const = lambda w: pl.BlockSpec((1, w), lambda i: (0, 0))
    widths = (A_HEADS * HEAD_DIM, LANES, IDX_HEADS * IDX_DIM, C_Q_RANK, C_KV_RANK, LANES, LANES)
    dtypes = (BF16, F32, BF16, BF16, F32, F32, F32)
    return pl.pallas_call(
        _post_kernel,
        grid=(n // bm,),
        in_specs=[row(Z_WIDTH)] + [row(LANES)] * 5 + [const(LANES), const(C_Q_RANK), const(C_KV_RANK)],
        out_specs=[row(w) for w in widths],
        out_shape=[jax.ShapeDtypeStruct((n, w), d) for w, d in zip(widths, dtypes)],
        compiler_params=_params("parallel"),
        name="post_projection",
    )(z, *tabs128, *tabs64, bf_row, gq, gkv)


def _layer_norm(y, g, b):
    mu = jnp.mean(y, axis=-1, keepdims=True)
    yc = y - mu
    var = jnp.mean(yc * yc, axis=-1, keepdims=True)
    return yc * lax.rsqrt(var + LN_EPS) * g + b


def _mm_res_ln_kernel(x_ref, w_ref, r_ref, g_ref, b_ref, of_ref, ob_ref, acc_ref, *, nk, alpha):
    k = pl.program_id(1)

    @pl.when(k == 0)
    def _():
        acc_ref[...] = jnp.zeros_like(acc_ref)

    acc_ref[...] += jnp.dot(x_ref[...], w_ref[...], preferred_element_type=F32)

    @pl.when(k == nk - 1)
    def _():
        out = _layer_norm(alpha * r_ref[...] + acc_ref[...], g_ref[...], b_ref[...])
        of_ref[...] = out
        ob_ref[...] = out.astype(BF16)


def _mm_res_ln(x, w, res, g, b, *, bm, bk, alpha, name):
    m, k = x.shape
    d = w.shape[1]
    nk = k // bk
    return pl.pallas_call(
        functools.partial(_mm_res_ln_kernel, nk=nk, alpha=alpha),
        grid=(m // bm, nk),
        in_specs=[pl.BlockSpec((bm, bk), lambda i, kk: (i, kk)),
                  pl.BlockSpec((bk, d), lambda i, kk: (kk, 0)),
                  pl.BlockSpec((bm, d), lambda i, kk: (i, 0)),
                  pl.BlockSpec((1, d), lambda i, kk: (0, 0)),
                  pl.BlockSpec((1, d), lambda i, kk: (0, 0))],
        out_specs=[pl.BlockSpec((bm, d), lambda i, kk: (i, 0)),
                   pl.BlockSpec((bm, d), lambda i, kk: (i, 0))],
        out_shape=[jax.ShapeDtypeStruct((m, d), F32), jax.ShapeDtypeStruct((m, d), BF16)],
        scratch_shapes=[pltpu.VMEM((bm, d), F32)],
        compiler_params=_params("parallel", "arbitrary"),
        name=name,
    )(x, w, res, g, b)


def _flash_kernel(nkv_ref, *refs, mode, scale, two_part, has_bias, nk):
    refs = list(refs)
    q1_ref = refs.pop(0)
    q2_ref = refs.pop(0) if two_part else None
    k1_ref = refs.pop(0)
    k2_ref = refs.pop(0) if two_part else None
    v_ref = refs.pop(0)
    cq_ref = refs.pop(0) if has_bias else None
    ck_ref = refs.pop(0) if has_bias else None
    qpos_ref = refs.pop(0) if mode != "none" else None
    kpos_ref = refs.pop(0) if mode != "none" else None
    o_ref, m_ref, l_ref, acc_ref = refs
    qi = pl.program_id(2)
    ki = pl.program_id(3)

    @pl.when(ki == 0)
    def _():
        m_ref[...] = jnp.full_like(m_ref, MASKED)
        l_ref[...] = jnp.zeros_like(l_ref)
        acc_ref[...] = jnp.zeros_like(acc_ref)

    @pl.when(ki < nkv_ref[qi])
    def _():
        q = q1_ref[...].astype(BF16)
        k = k1_ref[...].astype(BF16)
        if two_part:
            q = jnp.concatenate([q, q2_ref[...].astype(BF16)], axis=-1)
            k = jnp.concatenate([k, k2_ref[...].astype(BF16)], axis=-1)
        s = lax.dot_general(q, k, (((1,), (1,)), ((), ())), preferred_element_type=F32) * scale
        if has_bias:
            s = s + cq_ref[...] - ck_ref[...]
        if mode == "chunk":
            mask = (kpos_ref[...] >> CHUNK_SHIFT) <= (qpos_ref[...] >> CHUNK_SHIFT)
        elif mode == "causal":
            mask = kpos_ref[...] <= qpos_ref[...]
        else:
            mask = None
        if mask is not None:
            s = jnp.where(mask, s, MASKED)
        m_old = m_ref[...]
        m_new = jnp.maximum(m_old, jnp.max(s, axis=-1, keepdims=True))
        p = jnp.exp(s - m_new)
        if mask is not None:
            p = jnp.where(mask, p, 0.0)
        alpha = jnp.exp(m_old - m_new)
        l_ref[...] = alpha * l_ref[...] + jnp.sum(p, axis=-1, keepdims=True)
        acc_ref[...] = alpha * acc_ref[...] + jnp.dot(p.astype(BF16), v_ref[...].astype(BF16),
                                                      preferred_element_type=F32)
        m_ref[...] = m_new

    @pl.when(ki == nk - 1)
    def _():
        o_ref[...] = (acc_ref[...] / l_ref[...]).astype(o_ref.dtype)


def _flash(q1, k1, v, *, heads, t, s, bq, bk, scale, mode, nkv=None, q2=None, k2=None, cq=None, ck=None,
           qpos=None, kpos=None, q1_col=0, q2_col=0, k1_col=0, v_col=0, k2_per_head=False, name):
    nb = q1.shape[0]
    nq, nk = t // bq, s // bk
    assert t % bq == 0 and s % bk == 0
    if nkv is None:
        nkv = jnp.full((nq,), nk, I32)
    two_part = q2 is not None
    has_bias = cq is not None

    def kblk(ki_, qi_, nkv_):
        return jnp.minimum(ki_, nkv_[qi_] - 1)

    qspec = lambda col: pl.BlockSpec((None, bq, LANES), lambda b, h, qi, ki, nkv_: (b, qi, col + h))
    kspec = lambda col, per_head: pl.BlockSpec(
        (None, bk, LANES),
        lambda b, h, qi, ki, nkv_: (b, kblk(ki, qi, nkv_), col + (h if per_head else 0)))
    in_specs = [qspec(q1_col)]
    args = [q1]
    if two_part:
        in_specs.append(qspec(q2_col))
        args.append(q2)
    in_specs.append(kspec(k1_col, True))
    args.append(k1)
    if two_part:
        in_specs.append(kspec(0, k2_per_head))
        args.append(k2)
    in_specs.append(kspec(v_col, True))
    args.append(v)
    if has_bias:
        in_specs.append(pl.BlockSpec((None, None, bq, 1), lambda b, h, qi, ki, nkv_: (b, h, qi, 0)))
        in_specs.append(pl.BlockSpec((None, None, 1, bk),
                                     lambda b, h, qi, ki, nkv_: (b, h, 0, kblk(ki, qi, nkv_))))
        args += [cq, ck]
    if mode != "none":
        in_specs.append(pl.BlockSpec((bq, 1), lambda b, h, qi, ki, nkv_: (qi, 0)))
        in_specs.append(pl.BlockSpec((1, bk), lambda b, h, qi, ki, nkv_: (0, kblk(ki, qi, nkv_))))
        args += [qpos, kpos]
    grid_spec = pltpu.PrefetchScalarGridSpec(
        num_scalar_prefetch=1,
        grid=(nb, heads, nq, nk),
        in_specs=in_specs,
        out_specs=pl.BlockSpec((None, bq, LANES), lambda b, h, qi, ki, nkv_: (b, qi, h)),
        scratch_shapes=[pltpu.VMEM((bq, 1), F32), pltpu.VMEM((bq, 1), F32), pltpu.VMEM((bq, LANES), F32)],
    )
    return pl.pallas_call(
        functools.partial(_flash_kernel, mode=mode, scale=scale, two_part=two_part, has_bias=has_bias, nk=nk),
        grid_spec=grid_spec,
        out_shape=jax.ShapeDtypeStruct((nb, t, heads * LANES), BF16),
        compiler_params=_params("parallel", "parallel", "parallel", "arbitrary"),
        name=name,
    )(nkv, *args)


def _sparse_kernel(nvis_ref, aq_ref, aqi_ref, aw_ref, qpos_ref, kpos_ref, ka_ref, kb_ref, k_ref, vt_ref,
                   o_ref, key_ref, m_ref, l_ref, acc_ref, j_ref, *, nq, topk, index_bits):
    b = pl.program_id(0)
    qi = pl.program_id(1)
    nblk = nvis_ref[b * nq + qi]
    bq = aq_ref.shape[-1]
    kb = KEY_BLOCK
    half = kb // 2
    qchunk = qpos_ref[...] >> CHUNK_SHIFT
    int_min = jnp.int32(INT_MIN)

    def score_block(j, carry):
        for part in range(2):
            off = pl.multiple_of(j * kb + part * half, half)
            ka = ka_ref[pl.ds(off, half), :]
            kbb = kb_ref[pl.ds(off, half), :]
            acc = jnp.zeros((half, bq), F32)
            for hp in range(IDX_HEADS // 2):
                qp = aqi_ref[hp * LANES:(hp + 1) * LANES, :]
                ze = jnp.dot(ka, qp, preferred_element_type=F32)
                zo = jnp.dot(kbb, qp, preferred_element_type=F32)
                acc = acc + jnp.maximum(ze, 0.0) * aw_ref[2 * hp:2 * hp + 1, :]
                acc = acc + jnp.maximum(zo, 0.0) * aw_ref[2 * hp + 1:2 * hp + 2, :]
            bits = pltpu.bitcast(acc, I32)
            key = bits ^ ((bits >> 31) & jnp.int32(INT_MAX))
            vis = (kpos_ref[pl.ds(off, half), :] >> CHUNK_SHIFT) <= qchunk
            key_ref[pl.ds(off, half), :] = jnp.where(vis, key, int_min)
        return carry

    lax.fori_loop(0, nblk, score_block, 0)

    def count(pred):
        def body(j, c):
            off = pl.multiple_of(j * kb, kb)
            kk = key_ref[pl.ds(off, kb), :]
            idx = off + lax.broadcasted_iota(I32, (kb, bq), 0)
            return c + jnp.sum(jnp.where(pred(kk, idx), 1.0, 0.0), axis=0, keepdims=True)
        return lax.fori_loop(0, nblk, body, jnp.zeros((1, bq), F32))

    def bit_step(i, t_u):
        cand_u = t_u | (jnp.int32(1) << (31 - i))
        cand_s = cand_u ^ int_min
        cnt = count(lambda kk, idx: kk >= cand_s)
        return jnp.where(cnt >= topk, cand_u, t_u)

    t_u = lax.fori_loop(0, 32, bit_step, jnp.zeros((1, bq), I32))
    t_s = t_u ^ int_min
    n_gt = count(lambda kk, idx: kk > t_s)
    n_ge = count(lambda kk, idx: kk >= t_s)
    want = topk - n_gt
    need = jnp.logical_and(n_ge > topk, t_u != 0)
    j_ref[...] = jnp.full(j_ref.shape, INT_MAX, I32)

    @pl.when(jnp.max(jnp.where(need, 1.0, 0.0)) > 0.0)
    def _():
        def idx_step(i, jc):
            cand = jc | (jnp.int32(1) << (index_bits - 1 - i))
            cnt = count(lambda kk, idx: jnp.logical_and(kk == t_s, idx < cand))
            return jnp.where(cnt < want, cand, jc)
        jc = lax.fori_loop(0, index_bits, idx_step, jnp.zeros((1, bq), I32))
        j_ref[...] = jnp.where(need, jc, jnp.int32(INT_MAX))

    jcut = j_ref[...]
    m_ref[...] = jnp.full(m_ref.shape, MASKED, F32)
    l_ref[...] = jnp.zeros(l_ref.shape, F32)
    acc_ref[...] = jnp.zeros(acc_ref.shape, F32)
    scale = HEAD_DIM ** -0.5

    def attend_block(j, carry):
        off = pl.multiple_of(j * kb, kb)
        kk = key_ref[pl.ds(off, kb), :]
        idx = off + lax.broadcasted_iota(I32, (kb, bq), 0)
        sel = jnp.logical_or(kk > t_s, jnp.logical_and(kk == t_s, idx <= jcut))
        sel = jnp.logical_and(sel, kk != int_min)
        kblk = k_ref[pl.ds(off, kb), :]
        vt = vt_ref[j]
        for h in range(A_HEADS):
            hs = slice(h * HEAD_DIM, (h + 1) * HEAD_DIM)
            s = jnp.dot(kblk, aq_ref[hs, :], preferred_element_type=F32) * scale
            s = jnp.where(sel, s, MASKED)
            m_old = m_ref[h:h + 1, :]
            m_new = jnp.maximum(m_old, jnp.max(s, axis=0, keepdims=True))
            p = jnp.where(sel, jnp.exp(s - m_new), 0.0)
            alpha = jnp.exp(m_old - m_new)
            l_ref[h:h + 1, :] = alpha * l_ref[h:h + 1, :] + jnp.sum(p, axis=0, keepdims=True)
            acc_ref[hs, :] = alpha * acc_ref[hs, :] + jnp.dot(vt, p.astype(BF16), preferred_element_type=F32)
            m_ref[h:h + 1, :] = m_new
        return carry

    lax.fori_loop(0, nblk, attend_block, 0)
    for h in range(A_HEADS):
        hs = slice(h * HEAD_DIM, (h + 1) * HEAD_DIM)
        o_ref[hs, :] = (acc_ref[hs, :] / l_ref[h:h + 1, :]).astype(o_ref.dtype)


def _sparse_attention(aq_t, aqi_t, aw_t, qpos_row, kpos_col, ka, kb, k, vt3, nvis, *, bq, topk, name):
    nb, _, t = aq_t.shape
    s = k.shape[1]
    nq = t // bq
    assert t % bq == 0 and s % KEY_BLOCK == 0
    qspec = lambda rows: pl.BlockSpec((None, rows, bq), lambda b, qi, nv: (b, 0, qi))
    kspec = pl.BlockSpec((None, s, LANES), lambda b, qi, nv: (b, 0, 0))
    grid_spec = pltpu.PrefetchScalarGridSpec(
        num_scalar_prefetch=1,
        grid=(nb, nq),
        in_specs=[qspec(A_HEADS * HEAD_DIM), qspec(IDX_HEADS * IDX_DIM), qspec(IDX_HEADS),
                  pl.BlockSpec((1, bq), lambda b, qi, nv: (0, qi)),
                  pl.BlockSpec((s, 1), lambda b, qi, nv: (0, 0)),
                  kspec, kspec, kspec,
                  pl.BlockSpec((None, s // KEY_BLOCK, LANES, KEY_BLOCK), lambda b, qi, nv: (b, 0, 0, 0))],
        out_specs=qspec(A_HEADS * HEAD_DIM),
        scratch_shapes=[pltpu.VMEM((s, bq), I32), pltpu.VMEM((A_HEADS, bq), F32), pltpu.VMEM((A_HEADS, bq), F32),
                        pltpu.VMEM((A_HEADS * HEAD_DIM, bq), F32), pltpu.VMEM((1, bq), I32)],
    )
    return pl.pallas_call(
        functools.partial(_sparse_kernel, nq=nq, topk=topk, index_bits=max(1, (s - 1).bit_length())),
        grid_spec=grid_spec,
        out_shape=jax.ShapeDtypeStruct((nb, A_HEADS * HEAD_DIM, t), BF16),
        compiler_params=_params("parallel", "parallel"),
        name=name,
    )(nvis, aq_t, aqi_t, aw_t, qpos_row, kpos_col, ka, kb, k, vt3)


def _router_kernel(x_ref, whi_ref, wlo_ref, bias_ref, ids_ref, gates_ref):
    x = x_ref[...]
    xh = x.astype(BF16)
    xl = (x - xh.astype(F32)).astype(BF16)
    whi, wlo = whi_ref[...], wlo_ref[...]
    lg = (jnp.dot(xh, whi, preferred_element_type=F32) + jnp.dot(xh, wlo, preferred_element_type=F32)
          + jnp.dot(xl, whi, preferred_element_type=F32)) + bias_ref[...]
    lane = lax.broadcasted_iota(I32, lg.shape, 1)
    neg = -jnp.inf
    is_group = lane < N_GROUPS
    gl = jnp.where(is_group, lg, neg)
    gmax = jnp.max(gl, axis=-1, keepdims=True)
    grp = jnp.min(jnp.where(gl == gmax, lane, LANES), axis=-1, keepdims=True)
    g1 = 1.0 / jnp.sum(jnp.where(is_group, jnp.exp(gl - gmax), 0.0), axis=-1, keepdims=True)
    lo = N_GROUPS + grp * EXPERTS_PER_GROUP
    el = jnp.where(jnp.logical_and(lane >= lo, lane < lo + EXPERTS_PER_GROUP), lg, neg)
    v1 = jnp.max(el, axis=-1, keepdims=True)
    i1 = jnp.min(jnp.where(el == v1, lane, LANES), axis=-1, keepdims=True)
    el2 = jnp.where(lane == i1, neg, el)
    v2 = jnp.max(el2, axis=-1, keepdims=True)
    i2 = jnp.min(jnp.where(el2 == v2, lane, LANES), axis=-1, keepdims=True)
    e21 = jnp.exp(v2 - v1)
    den = 1.0 + e21
    ids_ref[...] = jnp.where(lane == 0, i1 - N_GROUPS, jnp.where(lane == 1, i2 - N_GROUPS, 0))
    gates_ref[...] = jnp.where(lane == 0, g1 * (1.0 / den), jnp.where(lane == 1, g1 * (e21 / den), 0.0))


def _router(x, whi, wlo, bias, *, bm):
    n, d = x.shape
    row = pl.BlockSpec((bm, LANES), lambda i: (i, 0))
    return pl.pallas_call(
        _router_kernel,
        grid=(n // bm,),
        in_specs=[pl.BlockSpec((bm, d), lambda i: (i, 0)),
                  pl.BlockSpec((d, LANES), lambda i: (0, 0)),
                  pl.BlockSpec((d, LANES), lambda i: (0, 0)),
                  pl.BlockSpec((1, LANES), lambda i: (0, 0))],
        out_specs=[row, row],
        out_shape=[jax.ShapeDtypeStruct((n, LANES), I32), jax.ShapeDtypeStruct((n, LANES), F32)],
        compiler_params=_params("parallel"),
        name="router",
    )(x, whi, wlo, bias)


def _expert_kernel(te_ref, nv_ref, src_ref, dst_ref, x_hbm, gate_ref, wg_ref, wu_ref, wd_ref, y_hbm,
                   xbuf, ybuf, sem_in, sem_out, *, bm):
    t = pl.program_id(0)
    nvalid = nv_ref[t]
    base = t * bm

    def row_in(r, tok):
        return pltpu.make_async_copy(x_hbm.at[pl.ds(tok, 1), :], xbuf.at[pl.ds(r, 1), :], sem_in)

    def row_out(r, dst):
        return pltpu.make_async_copy(ybuf.at[pl.ds(r, 1), :], y_hbm.at[pl.ds(dst, 1), :], sem_out)

    @pl.when(nvalid > 0)
    def _():
        def start_in(r, c):
            row_in(r, src_ref[base + r]).start()
            return c
        lax.fori_loop(0, bm, start_in, 0)

        def wait_in(r, c):
            row_in(r, 0).wait()
            return c
        lax.fori_loop(0, bm, wait_in, 0)

        x = xbuf[...].astype(BF16)
        hg = jnp.dot(x, wg_ref[...], preferred_element_type=F32)
        hu = jnp.dot(x, wu_ref[...], preferred_element_type=F32)
        act = (hg * (1.0 / (1.0 + jnp.exp(-hg)))) * hu * gate_ref[...]
        ybuf[...] = jnp.dot(act.astype(BF16), wd_ref[...], preferred_element_type=F32)

        def start_out(r, c):
            row_out(r, dst_ref[base + r]).start()
            return c
        lax.fori_loop(0, nvalid, start_out, 0)

        def wait_out(r, c):
            row_out(r, 0).wait()
            return c
        lax.fori_loop(0, nvalid, wait_out, 0)


def _experts(x, gate_p, wg, wu, wd, tile_e, tile_nvalid, src_tok, dst_row, *, bm):
    n, d = x.shape
    f = wg.shape[-1]
    n_tiles = tile_e.shape[0]
    grid_spec = pltpu.PrefetchScalarGridSpec(
        num_scalar_prefetch=4,
        grid=(n_tiles,),
        in_specs=[pl.BlockSpec(memory_space=pl.ANY),
                  pl.BlockSpec((bm, 1), lambda t, te, nv, s, dd: (t, 0)),
                  pl.BlockSpec((None, d, f), lambda t, te, nv, s, dd: (te[t], 0, 0)),
                  pl.BlockSpec((None, d, f), lambda t, te, nv, s, dd: (te[t], 0, 0)),
                  pl.BlockSpec((None, f, d), lambda t, te, nv, s, dd: (te[t], 0, 0))],
        out_specs=pl.BlockSpec(memory_space=pl.ANY),
        scratch_shapes=[pltpu.VMEM((bm, d), F32), pltpu.VMEM((bm, d), F32),
                        pltpu.SemaphoreType.DMA, pltpu.SemaphoreType.DMA],
    )
    return pl.pallas_call(
        functools.partial(_expert_kernel, bm=bm),
        grid_spec=grid_spec,
        out_shape=jax.ShapeDtypeStruct((2 * n, d), F32),
        compiler_params=_params("arbitrary"),
        name="experts",
    )(tile_e, tile_nvalid, src_tok, dst_row, x, gate_p, wg, wu, wd)


def _expert_plan(eid, gate, *, bm):
    n = eid.shape[0]
    na = 2 * n
    flat_e = eid.reshape(na)
    order = jnp.argsort(flat_e, stable=True).astype(I32)
    sorted_e = flat_e[order]
    counts = jnp.sum(jax.nn.one_hot(flat_e, N_EXPERTS, dtype=I32), axis=0)
    padded = ((counts + bm - 1) // bm) * bm
    pend = jnp.cumsum(padded)
    pstart = pend - padded
    start = jnp.cumsum(counts) - counts
    ppos = pstart[sorted_e] + (jnp.arange(na, dtype=I32) - start[sorted_e])
    total = na + N_EXPERTS * bm
    n_tiles = total // bm
    src_tok = jnp.zeros((total,), I32).at[ppos].set(order // 2)
    dst_row = jnp.zeros((total,), I32).at[ppos].set(order)
    gate_p = jnp.zeros((total,), F32).at[ppos].set(gate.reshape(na)[order])
    tile_start = jnp.arange(n_tiles, dtype=I32) * bm
    tile_e = jnp.minimum(jnp.searchsorted(pend, tile_start, side="right"), N_EXPERTS - 1).astype(I32)
    tile_nvalid = jnp.clip(pstart[tile_e] + counts[tile_e] - tile_start, 0, bm).astype(I32)
    return tile_e, tile_nvalid, src_tok, dst_row, gate_p.reshape(total, 1)


def _combine_ln_kernel(x_ref, y_ref, g_ref, b_ref, of_ref, ob_ref, *, alpha):
    d = x_ref.shape[-1]
    y = alpha * x_ref[...] + y_ref[:, :d] + y_ref[:, d:]
    out = _layer_norm(y, g_ref[...], b_ref[...])
    of_ref[...] = out
    ob_ref[...] = out.astype(BF16)


def _combine_ln(x, y2, g, b, *, bm, alpha):
    n, d = x.shape
    return pl.pallas_call(
        functools.partial(_combine_ln_kernel, alpha=alpha),
        grid=(n // bm,),
        in_specs=[pl.BlockSpec((bm, d), lambda i: (i, 0)),
                  pl.BlockSpec((bm, 2 * d), lambda i: (i, 0)),
                  pl.BlockSpec((1, d), lambda i: (0, 0)),
                  pl.BlockSpec((1, d), lambda i: (0, 0))],
        out_specs=[pl.BlockSpec((bm, d), lambda i: (i, 0)), pl.BlockSpec((bm, d), lambda i: (i, 0))],
        out_shape=[jax.ShapeDtypeStruct((n, d), F32), jax.ShapeDtypeStruct((n, d), BF16)],
        compiler_params=_params("parallel"),
        name="combine_ln",
    )(x, y2, g, b)


def _rope_tables(pos):
    posf = pos.astype(F32)[:, None]
    inv128 = ROPE_THETA ** (-jnp.arange(HEAD_DIM // 2, dtype=F32) * 2.0 / HEAD_DIM)
    ang = posf * inv128[None, :]
    cos, sin = jnp.cos(ang), jnp.sin(ang)
    a128 = jnp.concatenate([cos, cos], axis=-1)
    b128 = jnp.concatenate([-sin, sin], axis=-1)
    inv64 = ROPE_THETA ** (-jnp.arange(IDX_DIM // 2, dtype=F32) * 2.0 / IDX_DIM)
    ang = posf * inv64[None, :]
    cos, sin = jnp.cos(ang), jnp.sin(ang)
    zero = jnp.zeros_like(sin)
    a64 = jnp.concatenate([cos, cos, cos, cos], axis=-1)
    b64 = jnp.concatenate([-sin, zero, -sin, zero], axis=-1)
    c64 = jnp.concatenate([zero, sin, zero, sin], axis=-1)
    return (a128, b128), (a64, b64, c64)


def _z_weight(w_in):
    o = _IN_OFF
    d = w_in.shape[0]
    cols = [w_in[:, o[0]:o[4]],
            w_in[:, o[6]:o[9]],
            w_in[:, o[10]:o[12]],
            w_in[:, o[4]:o[5]],
            w_in[:, o[12]:o[13]],
            w_in[:, o[5]:o[6]],
            w_in[:, o[9]:o[10]],
            jnp.zeros((d, LANES - IDX_HEADS - B_HEADS), w_in.dtype)]
    return jnp.concatenate(cols, axis=1).astype(BF16)


def _uq_weight(w):
    r = w.shape[0]
    w3 = w.reshape(r, C_HEADS, C_NOPE + C_ROPE)
    nope = w3[:, :, :C_NOPE].reshape(r, C_HEADS * C_NOPE)
    rope = jnp.pad(w3[:, :, C_NOPE:], ((0, 0), (0, 0), (0, LANES - C_ROPE))).reshape(r, C_HEADS * LANES)
    return jnp.concatenate([nope, rope], axis=1).astype(BF16)


def _pad_lanes(x):
    return jnp.pad(x, [(0, 0)] * (x.ndim - 1) + [(0, LANES - x.shape[-1])])


def _pad_rows(x, rows):
    return jnp.pad(x, [(0, 0), (0, rows - x.shape[1])] + [(0, 0)] * (x.ndim - 2))


def _block_transposed(v):
    nb, s, w = v.shape
    return jnp.transpose(v.reshape(nb, s // KEY_BLOCK, KEY_BLOCK, w), (0, 1, 3, 2))


def _pick(n, *cands):
    for c in cands:
        if n % c == 0:
            return c
    return n


def kernel(x_prompt, x_sample, cache_a_k, cache_a_v, cache_a_kidx, cache_b_k, cache_b_v, cache_b_logf,
           cache_c_latent, cache_c_krope, cache_mem_k, cache_mem_v, mem_prompt,
           w_in, b_f, c_q_norm, w_c_uq, c_kv_norm, w_c_uk, w_c_uv, w_o, ln1_g, ln1_b,
           w_mq, w_mk, w_mv, w_mo, ln2_g, ln2_b, w_rg, b_rg, w_re, b_re,
           w_gate, w_up, w_down, ln3_g, ln3_b):
    depth = w_in.shape[0]
    bp, tp, d = x_prompt.shape
    bs, ts, _ = x_sample.shape
    past = cache_a_k.shape[2]
    n_mem = mem_prompt.shape[1]
    assert bp == 1
    n_p, n_s = bp * tp, bs * ts
    n = n_p + n_s
    alpha = (2 * depth) ** 0.25
    s_s = past + ts
    s_pad = -(-s_s // KEY_BLOCK) * KEY_BLOCK

    bm_tok = _pick(n, 1024, 512, 256, 128, 64)
    bm_post = _pick(n, 256, 128, 64)
    bm_ln = _pick(n, 256, 128, 64)
    bq_p = _pick(tp, 512, 256, 128)
    bk_p = _pick(tp, 512, 256, 128)
    bq_a = _pick(tp, 256, 128)
    bm_moe = 256

    x = jnp.concatenate([x_prompt.reshape(n_p, d), x_sample.reshape(n_s, d)], axis=0)
    xb = x.astype(BF16)
    pos_p = jnp.arange(tp, dtype=I32)
    pos_s = past + jnp.arange(ts, dtype=I32)
    pos = jnp.concatenate([pos_p, jnp.tile(pos_s, bs)])
    tabs128, tabs64 = _rope_tables(pos)

    kpos_s = jnp.concatenate([jnp.arange(past, dtype=I32), pos_s])
    kpos_s_pad = jnp.concatenate([kpos_s, jnp.full((s_pad - s_s,), 2 ** 30, I32)])
    q_end = (jnp.arange(tp // bq_p, dtype=I32) + 1) * bq_p - 1
    nkv_chunk = (((q_end >> CHUNK_SHIFT) + 1) * (1 << CHUNK_SHIFT) + bk_p - 1) // bk_p
    nkv_causal = (q_end + bk_p) // bk_p
    qa_end = (jnp.arange(tp // bq_a, dtype=I32) + 1) * bq_a - 1
    nvis_p = jnp.minimum((((qa_end >> CHUNK_SHIFT) + 1) * (1 << CHUNK_SHIFT) + KEY_BLOCK - 1) // KEY_BLOCK,
                         tp // KEY_BLOCK).astype(I32)
    nvis_s = jnp.full((bs,), s_pad // KEY_BLOCK, I32)

    state_p = [[] for _ in range(8)]
    state_s = [[] for _ in range(8)]
    mem_k_out, mem_v_out = [], []

    for l in range(depth):
        z = _matmul(xb, _z_weight(w_in[l]), bm=bm_tok, bn=512, out_dtype=F32, name="in_proj")
        bf_row = jnp.zeros((1, LANES), F32).at[0, SM2_BF:SM2_BF + B_HEADS].set(b_f[l].astype(F32))
        aq, ak, aqi, cqn, clat, sm1, sm2 = _post(
            z, tabs128, tabs64, bf_row, c_q_norm[l].reshape(1, -1).astype(F32),
            c_kv_norm[l].reshape(1, -1).astype(F32), bm=bm_post)
        cq = _matmul_rope(cqn, _uq_weight(w_c_uq[l]), tabs64, bm=bm_tok, bn=512,
                          first_rope_col=C_HEADS * C_NOPE, name="c_q_up")
        av = z[:, Z_AV:Z_AV + HEAD_DIM]
        aki = sm1[:, :IDX_DIM]
        ckr = sm1[:, IDX_DIM:]
        aw = sm2[:, :IDX_HEADS]
        blogf = sm2[:, SM2_BF:SM2_BF + B_HEADS]
        bk_ = z[:, Z_BK:Z_BK + B_HEADS * HEAD_DIM]
        bv_ = z[:, Z_BV:Z_BV + B_HEADS * HEAD_DIM]
        rows = (ak, av, aki, bk_.reshape(n, B_HEADS, HEAD_DIM), bv_.reshape(n, B_HEADS, HEAD_DIM),
                blogf, clat, ckr)
        for i, r in enumerate(rows):
            state_p[i].append(r[:n_p].reshape((bp, tp) + r.shape[1:]))
            state_s[i].append(r[n_p:].reshape((bs, ts) + r.shape[1:]))

        z3 = z.reshape(1, n, Z_WIDTH)
        zs = z[n_p:].reshape(bs, ts, Z_WIDTH)
        cq3 = cq.reshape(1, n, -1)
        cqs = cq[n_p:].reshape(bs, ts, -1)
        w_ukv = jnp.concatenate([w_c_uk[l], w_c_uv[l]], axis=1).astype(BF16)
        kv_cols = C_HEADS * C_NOPE // LANES

        aw_scaled = aw * (IDX_DIM ** -0.5)

        def sparse(aq_g, aqi_g, aw_g, qpos, kpos, ki, k, v, nvis, bq, topk, name):
            ka = _pad_lanes(ki).astype(BF16)
            kb = jnp.concatenate([jnp.zeros_like(ki), ki], axis=-1).astype(BF16)
            out_t = _sparse_attention(
                jnp.swapaxes(aq_g, 1, 2), jnp.swapaxes(aqi_g, 1, 2), jnp.swapaxes(aw_g, 1, 2),
                qpos.reshape(1, -1), kpos.reshape(-1, 1), ka, kb, k.astype(BF16),
                _block_transposed(v.astype(BF16)), nvis, bq=bq, topk=topk, name=name)
            return jnp.swapaxes(out_t, 1, 2)

        oa_p = sparse(aq[:n_p].reshape(bp, tp, -1), aqi[:n_p].reshape(bp, tp, -1),
                      aw_scaled[:n_p].reshape(bp, tp, -1), pos_p, pos_p,
                      aki[:n_p].reshape(bp, tp, -1), ak[:n_p].reshape(bp, tp, -1), av[:n_p].reshape(bp, tp, -1),
                      nvis_p, bq_a, min(TOPK_MAX, tp // 4), "sparse_prompt")
        cat = lambda c, r: _pad_rows(jnp.concatenate([c.astype(F32), r], axis=1), s_pad)
        oa_s = sparse(aq[n_p:].reshape(bs, ts, -1), aqi[n_p:].reshape(bs, ts, -1),
                      aw_scaled[n_p:].reshape(bs, ts, -1), pos_s, kpos_s_pad,
                      cat(cache_a_kidx[l], aki[n_p:].reshape(bs, ts, -1)),
                      cat(cache_a_k[l], ak[n_p:].reshape(bs, ts, -1)),
                      cat(cache_a_v[l], av[n_p:].reshape(bs, ts, -1)),
                      nvis_s, ts, min(TOPK_MAX, s_s // 4), "sparse_sample")

        csum_p = jnp.cumsum(blogf[:n_p].reshape(bp, tp, B_HEADS), axis=1)
        ob_p = _flash(z3, z3, z3, heads=B_HEADS, t=tp, s=tp, bq=bq_p, bk=bk_p, scale=HEAD_DIM ** -0.5,
                      mode="causal", nkv=nkv_causal.astype(I32),
                      cq=jnp.transpose(csum_p, (0, 2, 1))[..., None], ck=jnp.transpose(csum_p, (0, 2, 1))[:, :, None, :],
                      qpos=pos_p.reshape(-1, 1), kpos=pos_p.reshape(1, -1),
                      q1_col=Z_BQ // LANES, k1_col=Z_BK // LANES, v_col=Z_BV // LANES, name="fox_prompt")
        logf_s = jnp.concatenate([cache_b_logf[l].astype(F32), blogf[n_p:].reshape(bs, ts, B_HEADS)], axis=1)
        csum_s = jnp.transpose(jnp.cumsum(logf_s, axis=1), (0, 2, 1))
        kb_s = jnp.concatenate([cache_b_k[l].reshape(bs, past, -1), bk_[n_p:].reshape(bs, ts, -1)], axis=1).astype(BF16)
        vb_s = jnp.concatenate([cache_b_v[l].reshape(bs, past, -1), bv_[n_p:].reshape(bs, ts, -1)], axis=1).astype(BF16)
        ob_s = _flash(zs, kb_s, vb_s, heads=B_HEADS, t=ts, s=s_s, bq=ts, bk=s_s, scale=HEAD_DIM ** -0.5,
                      mode="causal", cq=csum_s[:, :, past:, None], ck=csum_s[:, :, None, :],
                      qpos=pos_s.reshape(-1, 1), kpos=kpos_s.reshape(1, -1),
                      q1_col=Z_BQ // LANES, name="fox_sample")

        kv_p = _matmul(clat[:n_p], w_ukv, bm=_pick(n_p, 1024, 512, 256), bn=w_ukv.shape[1], out_dtype=BF16,
                       name="c_kv_up_prompt").reshape(1, n_p, -1)
        kr_p = _pad_lanes(ckr[:n_p]).astype(BF16).reshape(1, n_p, LANES)
        oc_p = _flash(cq3, kv_p, kv_p, heads=C_HEADS, t=tp, s=tp, bq=bq_p, bk=bk_p,
                      scale=(C_NOPE + C_ROPE) ** -0.5, mode="chunk", nkv=nkv_chunk.astype(I32),
                      q2=cq3, k2=kr_p, qpos=pos_p.reshape(-1, 1), kpos=pos_p.reshape(1, -1),
                      q2_col=kv_cols, v_col=kv_cols, name="mla_prompt")
        lat_s = jnp.concatenate([cache_c_latent[l].astype(F32), clat[n_p:].reshape(bs, ts, -1)], axis=1)
        kv_s = _matmul(lat_s.reshape(bs * s_s, -1), w_ukv, bm=_pick(bs * s_s, 1024, 512, 256, 128, 64),
                       bn=w_ukv.shape[1], out_dtype=BF16, name="c_kv_up_sample").reshape(bs, s_s, -1)
        kr_s = _pad_lanes(jnp.concatenate([cache_c_krope[l].astype(F32), ckr[n_p:].reshape(bs, ts, -1)],
                                          axis=1)).astype(BF16)
        oc_s = _flash(cqs, kv_s, kv_s, heads=C_HEADS, t=ts, s=s_s, bq=ts, bk=s_s,
                      scale=(C_NOPE + C_ROPE) ** -0.5, mode="chunk",
                      q2=cqs, k2=kr_s, qpos=pos_s.reshape(-1, 1), kpos=kpos_s.reshape(1, -1),
                      q2_col=kv_cols, v_col=kv_cols, name="mla_sample")

        mixed = jnp.concatenate([
            jnp.concatenate([oa_p.reshape(n_p, -1), oa_s.reshape(n_s, -1)], axis=0),
            jnp.concatenate([ob_p.reshape(n_p, -1), ob_s.reshape(n_s, -1)], axis=0),
            jnp.concatenate([oc_p.reshape(n_p, -1), oc_s.reshape(n_s, -1)], axis=0)], axis=1)
        x1, x1b = _mm_res_ln(mixed, w_o[l].astype(BF16), x, ln1_g[l].reshape(1, d), ln1_b[l].reshape(1, d),
                             bm=bm_ln, bk=512, alpha=alpha, name="out_proj_ln1")

        w_mkv = jnp.concatenate([w_mk[l], w_mv[l]], axis=1).astype(BF16)
        mkv = _matmul(mem_prompt.reshape(bp * n_mem, d), w_mkv, bm=_pick(bp * n_mem, 256, 128, 64), bn=512,
                      out_dtype=F32, name="mem_kv")
        mw = MEM_HEADS * MEM_DIM
        mem_k_out.append(mkv[:, :mw].reshape(bp, n_mem, MEM_HEADS, MEM_DIM))
        mem_v_out.append(mkv[:, mw:].reshape(bp, n_mem, MEM_HEADS, MEM_DIM))
        qm = _matmul(x1b, w_mq[l].astype(BF16), bm=bm_tok, bn=mw, out_dtype=BF16, name="mem_q")
        mkv3 = mkv.reshape(bp, n_mem, 2 * mw)
        om_p = _flash(qm.reshape(1, n, mw), mkv3, mkv3, heads=MEM_HEADS, t=tp, s=n_mem, bq=bq_p, bk=n_mem,
                      scale=MEM_DIM ** -0.5, mode="none", v_col=mw // LANES, name="mem_prompt")
        om_s = _flash(qm[n_p:].reshape(bs, ts, mw), cache_mem_k[l].reshape(bs, n_mem, mw).astype(BF16),
                      cache_mem_v[l].reshape(bs, n_mem, mw).astype(BF16), heads=MEM_HEADS, t=ts, s=n_mem,
                      bq=ts, bk=n_mem, scale=MEM_DIM ** -0.5, mode="none", name="mem_sample")
        om = jnp.concatenate([om_p.reshape(n_p, mw), om_s.reshape(n_s, mw)], axis=0)
        x2, _ = _mm_res_ln(om, w_mo[l].astype(BF16), x1, ln2_g[l].reshape(1, d), ln2_b[l].reshape(1, d),
                           bm=bm_ln, bk=mw, alpha=alpha, name="mem_out_ln2")

        w_r = jnp.concatenate([w_rg[l], jnp.transpose(w_re[l], (1, 0, 2)).reshape(d, N_EXPERTS)], axis=1).astype(F32)
        w_r = _pad_lanes(w_r)
        w_r_hi = w_r.astype(BF16)
        w_r_lo = (w_r - w_r_hi.astype(F32)).astype(BF16)
        b_r = _pad_lanes(jnp.concatenate([b_rg[l], b_re[l].reshape(-1)]).astype(F32).reshape(1, -1))
        ids, gates = _router(x2, w_r_hi, w_r_lo, b_r, bm=_pick(n, 512, 256, 128, 64))
        tile_e, tile_nvalid, src_tok, dst_row, gate_p = _expert_plan(ids[:, :2], gates[:, :2], bm=bm_moe)
        y2 = _experts(x2, gate_p, w_gate[l].astype(BF16), w_up[l].astype(BF16), w_down[l].astype(BF16),
                      tile_e, tile_nvalid, src_tok, dst_row, bm=bm_moe)
        x, xb = _combine_ln(x2, y2.reshape(n, 2 * d), ln3_g[l].reshape(1, d), ln3_b[l].reshape(1, d),
                            bm=_pick(n, 256, 128, 64), alpha=alpha)

    outs = [x[:n_p].reshape(bp, tp, d), x[n_p:].reshape(bs, ts, d)]
    outs += [jnp.stack(c) for c in state_p]
    outs += [jnp.stack(mem_k_out), jnp.stack(mem_v_out)]
    outs += [jnp.stack(c) for c in state_s]
    return tuple(outs)
```

```python
import functools
import math

import jax
import jax.numpy as jnp
import numpy as np
from jax import lax
from jax.experimental import pallas as pl
from jax.experimental.pallas import tpu as pltpu

F32 = jnp.float32
BF16 = jnp.bfloat16
I32 = jnp.int32

LANES = 128
SUBLANES = 8
VMEM_LIMIT = 56 * 1024 * 1024

CHUNK_SHIFT = 6
HEAD_DIM = 128
ROPE_THETA = 10000.0
A_HEADS = 8
IDX_HEADS = 32
IDX_DIM = 64
TOPK_MAX = 256
B_HEADS = 8
C_HEADS = 16
C_Q_RANK = 768
C_KV_RANK = 256
C_NOPE = 128
C_ROPE = 64
C_V = 128
MEM_HEADS = 4
MEM_DIM = 128
N_GROUPS = 4
EXPERTS_PER_GROUP = 4
N_EXPERTS = N_GROUPS * EXPERTS_PER_GROUP
LN_EPS = 1e-5
RMS_EPS = 1e-6
LOG2E = math.log2(math.e)

_IN_SIZES = (A_HEADS * HEAD_DIM, HEAD_DIM, HEAD_DIM, IDX_HEADS * IDX_DIM, IDX_DIM, IDX_HEADS,
             B_HEADS * HEAD_DIM, B_HEADS * HEAD_DIM, B_HEADS * HEAD_DIM, B_HEADS,
             C_Q_RANK, C_KV_RANK, C_ROPE)
_IN_OFF = np.concatenate([[0], np.cumsum(_IN_SIZES)]).astype(int)

Z_AQ = 0
Z_AK = Z_AQ + A_HEADS * HEAD_DIM
Z_AV = Z_AK + HEAD_DIM
Z_AQI = Z_AV + HEAD_DIM
Z_BQ = Z_AQI + IDX_HEADS * IDX_DIM
Z_BK = Z_BQ + B_HEADS * HEAD_DIM
Z_BV = Z_BK + B_HEADS * HEAD_DIM
Z_CQ = Z_BV + B_HEADS * HEAD_DIM
Z_CKV = Z_CQ + C_Q_RANK
Z_SM1 = Z_CKV + C_KV_RANK
Z_SM2 = Z_SM1 + LANES
Z_WIDTH = Z_SM2 + LANES
SM2_BF = IDX_HEADS

M_INIT = -1e30
INT_MIN = -2 ** 31
INT_MAX = 2 ** 31 - 1
SCORE_ROWS = 128
SCORE_COLS = 256
COUNT_ROWS = 256


def _params(*sem):
    return pltpu.CompilerParams(dimension_semantics=sem, vmem_limit_bytes=VMEM_LIMIT)


def _mm_kernel(x_ref, w_ref, o_ref, *, out_scale):
    x = x_ref[...].astype(BF16)
    acc = jnp.dot(x, w_ref[...], preferred_element_type=F32)
    if out_scale != 1.0:
        acc = acc * out_scale
    o_ref[...] = acc.astype(o_ref.dtype)


def _matmul(x, w, *, bm, bn, out_dtype, name, out_scale=1.0):
    m, k = x.shape
    n = w.shape[1]
    assert m % bm == 0 and n % bn == 0, (x.shape, w.shape, bm, bn)
    return pl.pallas_call(
        functools.partial(_mm_kernel, out_scale=out_scale),
        grid=(m // bm, n // bn),
        in_specs=[pl.BlockSpec((bm, k), lambda i, j: (i, 0)),
                  pl.BlockSpec((k, bn), lambda i, j: (0, j))],
        out_specs=pl.BlockSpec((bm, bn), lambda i, j: (i, j)),
        out_shape=jax.ShapeDtypeStruct((m, n), out_dtype),
        compiler_params=_params("parallel", "parallel"),
        name=name,
    )(x, w)


def _rope64(x, a, b, c):
    return x * a + pltpu.roll(x, 96, 1) * b + pltpu.roll(x, 32, 1) * c


def _rope128(x, a, b):
    return x * a + pltpu.roll(x, 64, 1) * b


def _mm_rope_kernel(x_ref, w_ref, a_ref, b_ref, c_ref, o_ref, *, first_rope_block, out_scale):
    j = pl.program_id(1)
    acc = jnp.dot(x_ref[...], w_ref[...], preferred_element_type=F32) * out_scale

    @pl.when(j < first_rope_block)
    def _():
        o_ref[...] = acc.astype(o_ref.dtype)

    @pl.when(j >= first_rope_block)
    def _():
        a, b, c = a_ref[...], b_ref[...], c_ref[...]
        for g in range(acc.shape[1] // LANES):
            sl = slice(g * LANES, (g + 1) * LANES)
            o_ref[:, sl] = _rope64(acc[:, sl], a, b, c).astype(o_ref.dtype)


def _matmul_rope(x, w, tabs, *, bm, bn, first_rope_col, out_scale, name):
    m, k = x.shape
    n = w.shape[1]
    assert m % bm == 0 and n % bn == 0 and first_rope_col % bn == 0
    tab_spec = pl.BlockSpec((bm, LANES), lambda i, j: (i, 0))
    return pl.pallas_call(
        functools.partial(_mm_rope_kernel, first_rope_block=first_rope_col // bn, out_scale=out_scale),
        grid=(m // bm, n // bn),
        in_specs=[pl.BlockSpec((bm, k), lambda i, j: (i, 0)),
                  pl.BlockSpec((k, bn), lambda i, j: (0, j)),
                  tab_spec, tab_spec, tab_spec],
        out_specs=pl.BlockSpec((bm, bn), lambda i, j: (i, j)),
        out_shape=jax.ShapeDtypeStruct((m, n), BF16),
        compiler_params=_params("parallel", "parallel"),
        name=name,
    )(x, w, *tabs)


def _post_kernel(z_ref, a128_ref, b128_ref, a64_ref, b64_ref, c64_ref, bf_ref, gq_ref, gkv_ref,
                 aq_ref, ak_ref, aqi_ref, bq_ref, cqn_ref, clat_ref, sm1_ref, sm2_ref):
    a128, b128 = a128_ref[...], b128_ref[...]
    a64, b64, c64 = a64_ref[...], b64_ref[...], c64_ref[...]
    qscale = HEAD_DIM ** -0.5 * LOG2E
    for g in range(A_HEADS):
        src = slice(Z_AQ + g * LANES, Z_AQ + (g + 1) * LANES)
        aq_ref[:, g * LANES:(g + 1) * LANES] = (_rope128(z_ref[:, src], a128, b128) * qscale).astype(BF16)
    ak_ref[...] = _rope128(z_ref[:, Z_AK:Z_AK + LANES], a128, b128)
    for g in range(IDX_HEADS * IDX_DIM // LANES):
        src = slice(Z_AQI + g * LANES, Z_AQI + (g + 1) * LANES)
        aqi_ref[:, g * LANES:(g + 1) * LANES] = _rope64(z_ref[:, src], a64, b64, c64).astype(BF16)
    bq_ref[...] = (z_ref[:, Z_BQ:Z_BQ + B_HEADS * HEAD_DIM] * qscale).astype(BF16)
    cq = z_ref[:, Z_CQ:Z_CQ + C_Q_RANK]
    cq = cq * lax.rsqrt(jnp.mean(cq * cq, axis=-1, keepdims=True) + RMS_EPS) * gq_ref[...]
    cqn_ref[...] = cq.astype(BF16)
    ckv = z_ref[:, Z_CKV:Z_CKV + C_KV_RANK]
    clat_ref[...] = ckv * lax.rsqrt(jnp.mean(ckv * ckv, axis=-1, keepdims=True) + RMS_EPS) * gkv_ref[...]
    sm1_ref[...] = _rope64(z_ref[:, Z_SM1:Z_SM1 + LANES], a64, b64, c64)
    s2 = z_ref[:, Z_SM2:Z_SM2 + LANES]
    lane = lax.broadcasted_iota(I32, s2.shape, 1)
    f = s2 + bf_ref[...]
    logf = jnp.minimum(f, 0.0) - jnp.log1p(jnp.exp(-jnp.abs(f)))
    sm2_ref[...] = jnp.where(lane < SM2_BF, s2 * (IDX_HEADS ** -0.5),
                             jnp.where(lane < SM2_BF + B_HEADS, logf, 0.0))


def _post(z, tabs128, tabs64, bf_row, gq, gkv, *, bm):
    n = z.shape[0]
    row = lambda w: pl.BlockSpec((bm, w), lambda i: (i, 0))
    const = lambda w: pl.BlockSpec((1, w), lambda i: (0, 0))
    widths = (A_HEADS * HEAD_DIM, LANES, IDX_HEADS * IDX_DIM, B_HEADS * HEAD_DIM, C_Q_RANK, C_KV_RANK, LANES, LANES)
    dtypes = (BF16, F32, BF16, BF16, BF16, F32, F32, F32)
    return pl.pallas_call(
        _post_kernel,
        grid=(n // bm,),
        in_specs=[row(Z_WIDTH)] + [row(LANES)] * 5 + [const(LANES), const(C_Q_RANK), const(C_KV_RANK)],
        out_specs=[row(w) for w in widths],
        out_shape=[jax.ShapeDtypeStruct((n, w), d) for w, d in zip(widths, dtypes)],
        compiler_params=_params("parallel"),
        name="post_projection",
    )(z, *tabs128, *tabs64, bf_row, gq, gkv)


def _layer_norm(y, g, b):
    mu = jnp.mean(y, axis=-1, keepdims=True)
    yc = y - mu
    var = jnp.mean(yc * yc, axis=-1, keepdims=True)
    return yc * lax.rsqrt(var + LN_EPS) * g + b


def _mm_res_ln_kernel(x_ref, w_ref, r_ref, g_ref, b_ref, of_ref, ob_ref, acc_ref, *, nk, alpha):
    k = pl.program_id(1)

    @pl.when(k == 0)
    def _():
        acc_ref[...] = jnp.zeros_like(acc_ref)

    acc_ref[...] += jnp.dot(x_ref[...], w_ref[...], preferred_element_type=F32)

    @pl.when(k == nk - 1)
    def _():
        out = _layer_norm(alpha * r_ref[...] + acc_ref[...], g_ref[...], b_ref[...])
        of_ref[...] = out
        ob_ref[...] = out.astype(BF16)


def _mm_res_ln(x, w, res, g, b, *, bm, bk, alpha, name):
    m, k = x.shape
    d = w.shape[1]
    nk = k // bk
    return pl.pallas_call(
        functools.partial(_mm_res_ln_kernel, nk=nk, alpha=alpha),
        grid=(m // bm, nk),
        in_specs=[pl.BlockSpec((bm, bk), lambda i, kk: (i, kk)),
                  pl.BlockSpec((bk, d), lambda i, kk: (kk, 0)),
                  pl.BlockSpec((bm, d), lambda i, kk: (i, 0)),
                  pl.BlockSpec((1, d), lambda i, kk: (0, 0)),
                  pl.BlockSpec((1, d), lambda i, kk: (0, 0))],
        out_specs=[pl.BlockSpec((bm, d), lambda i, kk: (i, 0)),
                   pl.BlockSpec((bm, d), lambda i, kk: (i, 0))],
        out_shape=[jax.ShapeDtypeStruct((m, d), F32), jax.ShapeDtypeStruct((m, d), BF16)],
        scratch_shapes=[pltpu.VMEM((bm, d), F32)],
        compiler_params=_params("parallel", "arbitrary"),
        name=name,
    )(x, w, res, g, b)


def _tflash_kernel(nfull_ref, nkv_ref, *refs, n_parts, k_shared, hb, bk, mode, q_off):
    refs = list(refs)
    q_refs = [refs.pop(0) for _ in range(n_parts)]
    k_refs = [refs.pop(0) for _ in range(n_parts)]
    vt_ref = refs.pop(0)
    o_ref, acc_ref = refs
    qi = pl.program_id(2)
    bq = o_ref.shape[-1]
    qs = [jnp.concatenate([qr[h] for qr in q_refs], axis=0) if n_parts > 1 else q_refs[0][h]
          for h in range(hb)]
    acc_ref[...] = jnp.zeros(acc_ref.shape, F32)

    def block(j, carry, masked):
        ms, ls = carry
        off = pl.multiple_of(j * bk, bk)
        if masked:
            kpos = off + lax.broadcasted_iota(I32, (bk, bq), 0)
            qpos = q_off + qi * bq + lax.broadcasted_iota(I32, (bk, bq), 1)
            if mode == "chunk":
                mask = (kpos >> CHUNK_SHIFT) <= (qpos >> CHUNK_SHIFT)
            else:
                mask = kpos <= qpos
        new_m, new_l = [], []
        for h in range(hb):
            parts = []
            for p_i, kr in enumerate(k_refs):
                hs = 0 if k_shared[p_i] else h
                parts.append(kr[pl.ds(off, bk), hs * LANES:(hs + 1) * LANES])
            k = jnp.concatenate(parts, axis=1) if n_parts > 1 else parts[0]
            s = jnp.dot(k, qs[h], preferred_element_type=F32)
            if masked:
                s = jnp.where(mask, s, -jnp.inf)
            m_new = jnp.maximum(ms[h], jnp.max(s, axis=0, keepdims=True))
            p = jnp.exp2(s - m_new)
            alpha = jnp.exp2(ms[h] - m_new)
            new_l.append(alpha * ls[h] + jnp.sum(p, axis=0, keepdims=True))
            new_m.append(m_new)
            acc_ref[h] = alpha * acc_ref[h] + jnp.dot(vt_ref[h, j], p.astype(BF16), preferred_element_type=F32)
        return tuple(new_m), tuple(new_l)

    init = (tuple(jnp.full((1, bq), M_INIT, F32) for _ in range(hb)),
            tuple(jnp.zeros((1, bq), F32) for _ in range(hb)))
    carry = lax.fori_loop(0, nfull_ref[qi], functools.partial(block, masked=False), init)
    if mode != "none":
        carry = lax.fori_loop(nfull_ref[qi], nkv_ref[qi], functools.partial(block, masked=True), carry)
    for h in range(hb):
        o_ref[h] = (acc_ref[h] / carry[1][h]).astype(o_ref.dtype)


def _tflash(q_parts, q_head0, k_parts, k_shared, vt, nfull, nkv, *, hb, bq, bk, mode, q_off, name):
    nb, heads = vt.shape[0], vt.shape[1]
    s = vt.shape[2] * bk
    nq = nfull.shape[0]
    assert heads % hb == 0 and all(h0 % hb == 0 for h0 in q_head0)
    in_specs = []
    for h0 in q_head0:
        in_specs.append(pl.BlockSpec((None, hb, LANES, bq),
                                     lambda b, g, qi, nf, nk, h0=h0: (b, h0 // hb + g, 0, qi)))
    for shared in k_shared:
        if shared:
            in_specs.append(pl.BlockSpec((None, s, LANES), lambda b, g, qi, nf, nk: (b, 0, 0)))
        else:
            in_specs.append(pl.BlockSpec((None, s, hb * LANES), lambda b, g, qi, nf, nk: (b, 0, g)))
    in_specs.append(pl.BlockSpec((None, hb, s // bk, LANES, bk), lambda b, g, qi, nf, nk: (b, g, 0, 0, 0)))
    grid_spec = pltpu.PrefetchScalarGridSpec(
        num_scalar_prefetch=2,
        grid=(nb, heads // hb, nq),
        in_specs=in_specs,
        out_specs=pl.BlockSpec((None, hb, LANES, bq), lambda b, g, qi, nf, nk: (b, g, 0, qi)),
        scratch_shapes=[pltpu.VMEM((hb, LANES, bq), F32)],
    )
    return pl.pallas_call(
        functools.partial(_tflash_kernel, n_parts=len(q_parts), k_shared=tuple(k_shared), hb=hb, bk=bk,
                          mode=mode, q_off=q_off),
        grid_spec=grid_spec,
        out_shape=jax.ShapeDtypeStruct((nb, heads, LANES, nq * bq), BF16),
        compiler_params=_params("parallel", "parallel", "arbitrary"),
        name=name,
    )(nfull, nkv, *q_parts, *k_parts, vt)


def _sparse_kernel(nvis_ref, aq_ref, aqi_ref, aw_ref, ka_ref, kb_ref, k_ref, vt_ref,
                   o_ref, key_ref, bias_ref, acc_ref, j_ref, *, nq, kb, s_real, topk, index_bits, q_off):
    b = pl.program_id(0)
    qi = pl.program_id(1)
    nblk = nvis_ref[b * nq + qi]
    bq = aq_ref.shape[-1]
    cols = min(bq, SCORE_COLS)
    int_min = jnp.int32(INT_MIN)
    qchunk = (q_off + qi * bq + lax.broadcasted_iota(I32, (1, bq), 1)) >> CHUNK_SHIFT

    def score_tile(t, carry):
        off = pl.multiple_of(t * SCORE_ROWS, SCORE_ROWS)
        ka = ka_ref[pl.ds(off, SCORE_ROWS), :]
        kbb = kb_ref[pl.ds(off, SCORE_ROWS), :]
        kidx = off + lax.broadcasted_iota(I32, (SCORE_ROWS, cols), 0)
        kchunk = jnp.where(kidx < s_real, kidx >> CHUNK_SHIFT, jnp.int32(INT_MAX))
        for c in range(bq // cols):
            cs = slice(c * cols, (c + 1) * cols)
            acc = jnp.zeros((SCORE_ROWS, cols), F32)
            for hp in range(IDX_HEADS // 2):
                qp = aqi_ref[hp * LANES:(hp + 1) * LANES, cs]
                ze = jnp.dot(ka, qp, preferred_element_type=F32)
                zo = jnp.dot(kbb, qp, preferred_element_type=F32)
                acc = acc + jnp.maximum(ze, 0.0) * aw_ref[2 * hp:2 * hp + 1, cs]
                acc = acc + jnp.maximum(zo, 0.0) * aw_ref[2 * hp + 1:2 * hp + 2, cs]
            bits = pltpu.bitcast(acc, I32)
            key = bits ^ ((bits >> 31) & jnp.int32(INT_MAX))
            key_ref[pl.ds(off, SCORE_ROWS), cs] = jnp.where(kchunk <= qchunk[:, cs], key, int_min)
        return carry

    lax.fori_loop(0, nblk * (kb // SCORE_ROWS), score_tile, 0)

    def count(pred):
        def body(t, c):
            off = pl.multiple_of(t * COUNT_ROWS, COUNT_ROWS)
            kk = key_ref[pl.ds(off, COUNT_ROWS), :]
            idx = off + lax.broadcasted_iota(I32, (COUNT_ROWS, bq), 0)
            hit = jnp.where(pred(kk, idx), 1.0, 0.0)
            return c + jnp.sum(hit.reshape(COUNT_ROWS // SUBLANES, SUBLANES, bq), axis=0)
        c = lax.fori_loop(0, nblk * (kb // COUNT_ROWS), body, jnp.zeros((SUBLANES, bq), F32))
        return jnp.sum(c, axis=0, keepdims=True)

    def bit_step(i, carry):
        t_u, n_keep = carry
        cand_u = t_u | (jnp.int32(1) << (31 - i))
        cand_s = cand_u ^ int_min
        cnt = count(lambda kk, idx: kk >= cand_s)
        take = cnt >= topk
        return jnp.where(take, cand_u, t_u), jnp.where(take, cnt, n_keep)

    t_u, n_ge = lax.fori_loop(0, 32, bit_step, (jnp.zeros((1, bq), I32), jnp.zeros((1, bq), F32)))
    t_s = t_u ^ int_min
    need = jnp.logical_and(n_ge > topk, t_u != 0)
    j_ref[...] = jnp.full(j_ref.shape, INT_MAX, I32)

    @pl.when(jnp.max(jnp.where(need, 1.0, 0.0)) > 0.0)
    def _():
        want = topk - count(lambda kk, idx: kk > t_s)

        def idx_step(i, jc):
            cand = jc | (jnp.int32(1) << (index_bits - 1 - i))
            cnt = count(lambda kk, idx: jnp.logical_and(kk == t_s, idx < cand))
            return jnp.where(cnt < want, cand, jc)
        jc = lax.fori_loop(0, index_bits, idx_step, jnp.zeros((1, bq), I32))
        j_ref[...] = jnp.where(need, jc, jnp.int32(INT_MAX))

    jcut = j_ref[...]
    acc_ref[...] = jnp.zeros(acc_ref.shape, F32)

    def attend_block(j, carry):
        ms, ls = carry
        off = pl.multiple_of(j * kb, kb)
        kk = key_ref[pl.ds(off, kb), :]
        idx = off + lax.broadcasted_iota(I32, (kb, bq), 0)
        sel = jnp.logical_or(kk > t_s, jnp.logical_and(kk == t_s, idx <= jcut))
        sel = jnp.logical_and(sel, kk != int_min)
        bias_ref[...] = jnp.where(sel, 0.0, -jnp.inf)
        kblk = k_ref[pl.ds(off, kb), :]
        vt = vt_ref[j]
        new_m, new_l = [], []
        for h in range(A_HEADS):
            hs = slice(h * HEAD_DIM, (h + 1) * HEAD_DIM)
            s = jnp.dot(kblk, aq_ref[hs, :], preferred_element_type=F32) + bias_ref[...]
            m_new = jnp.maximum(ms[h], jnp.max(s, axis=0, keepdims=True))
            p = jnp.exp2(s - m_new)
            alpha = jnp.exp2(ms[h] - m_new)
            new_l.append(alpha * ls[h] + jnp.sum(p, axis=0, keepdims=True))
            new_m.append(m_new)
            acc_ref[hs, :] = alpha * acc_ref[hs, :] + jnp.dot(vt, p.astype(BF16), preferred_element_type=F32)
        return tuple(new_m), tuple(new_l)

    init = (tuple(jnp.full((1, bq), M_INIT, F32) for _ in range(A_HEADS)),
            tuple(jnp.zeros((1, bq), F32) for _ in range(A_HEADS)))
    _, ls = lax.fori_loop(0, nblk, attend_block, init)
    for h in range(A_HEADS):
        hs = slice(h * HEAD_DIM, (h + 1) * HEAD_DIM)
        o_ref[hs, :] = (acc_ref[hs, :] / ls[h]).astype(o_ref.dtype)


def _sparse_attention(aq_t, aqi_t, aw_t, ka, kb_, k, vt3, nvis, *, bq, kb, s_real, topk, q_off, name):
    nb, s = k.shape[0], k.shape[1]
    nq = nvis.shape[0] // nb
    assert s % kb == 0 and kb % COUNT_ROWS == 0 and kb % SCORE_ROWS == 0 and bq % min(bq, SCORE_COLS) == 0
    qspec = lambda rows: pl.BlockSpec((None, rows, bq), lambda b, qi, nv: (b, 0, qi))
    kspec = pl.BlockSpec((None, s, LANES), lambda b, qi, nv: (b, 0, 0))
    grid_spec = pltpu.PrefetchScalarGridSpec(
        num_scalar_prefetch=1,
        grid=(nb, nq),
        in_specs=[qspec(A_HEADS * HEAD_DIM), qspec(IDX_HEADS * IDX_DIM), qspec(IDX_HEADS),
                  kspec, kspec, kspec,
                  pl.BlockSpec((None, s // kb, LANES, kb), lambda b, qi, nv: (b, 0, 0, 0))],
        out_specs=qspec(A_HEADS * HEAD_DIM),
        scratch_shapes=[pltpu.VMEM((s, bq), I32), pltpu.VMEM((kb, bq), F32),
                        pltpu.VMEM((A_HEADS * HEAD_DIM, bq), F32), pltpu.VMEM((1, bq), I32)],
    )
    return pl.pallas_call(
        functools.partial(_sparse_kernel, nq=nq, kb=kb, s_real=s_real, topk=topk,
                          index_bits=max(1, (s - 1).bit_length()), q_off=q_off),
        grid_spec=grid_spec,
        out_shape=jax.ShapeDtypeStruct((nb, A_HEADS * HEAD_DIM, nq * bq), BF16),
        compiler_params=_params("parallel", "arbitrary"),
        name=name,
    )(nvis, aq_t, aqi_t, aw_t, ka, kb_, k, vt3)


def _router_kernel(x_ref, whi_ref, wlo_ref, bias_ref, ids_ref, g0_ref, g1_ref):
    x = x_ref[...]
    xh = x.astype(BF16)
    xl = (x - xh.astype(F32)).astype(BF16)
    whi, wlo = whi_ref[...], wlo_ref[...]
    lg = (jnp.dot(xh, whi, preferred_element_type=F32) + jnp.dot(xh, wlo, preferred_element_type=F32)
          + jnp.dot(xl, whi, preferred_element_type=F32)) + bias_ref[...]
    lane = lax.broadcasted_iota(I32, lg.shape, 1)
    neg = -jnp.inf
    is_group = lane < N_GROUPS
    gl = jnp.where(is_group, lg, neg)
    gmax = jnp.max(gl, axis=-1, keepdims=True)
    grp = jnp.min(jnp.where(gl == gmax, lane, LANES), axis=-1, keepdims=True)
    g1 = 1.0 / jnp.sum(jnp.where(is_group, jnp.exp(gl - gmax), 0.0), axis=-1, keepdims=True)
    lo = N_GROUPS + grp * EXPERTS_PER_GROUP
    el = jnp.where(jnp.logical_and(lane >= lo, lane < lo + EXPERTS_PER_GROUP), lg, neg)
    v1 = jnp.max(el, axis=-1, keepdims=True)
    i1 = jnp.min(jnp.where(el == v1, lane, LANES), axis=-1, keepdims=True)
    el2 = jnp.where(lane == i1, neg, el)
    v2 = jnp.max(el2, axis=-1, keepdims=True)
    i2 = jnp.min(jnp.where(el2 == v2, lane, LANES), axis=-1, keepdims=True)
    e21 = jnp.exp(v2 - v1)
    den = 1.0 + e21
    ids_ref[...] = jnp.where(lane == 0, i1 - N_GROUPS, jnp.where(lane == 1, i2 - N_GROUPS, 0))
    g0_ref[...] = jnp.broadcast_to(g1 * (1.0 / den), g0_ref.shape)
    g1_ref[...] = jnp.broadcast_to(g1 * (e21 / den), g1_ref.shape)


def _router(x, whi, wlo, bias, *, bm):
    n, d = x.shape
    row = pl.BlockSpec((bm, LANES), lambda i: (i, 0))
    return pl.pallas_call(
        _router_kernel,
        grid=(n // bm,),
        in_specs=[pl.BlockSpec((bm, d), lambda i: (i, 0)),
                  pl.BlockSpec((d, LANES), lambda i: (0, 0)),
                  pl.BlockSpec((d, LANES), lambda i: (0, 0)),
                  pl.BlockSpec((1, LANES), lambda i: (0, 0))],
        out_specs=[row, row, row],
        out_shape=[jax.ShapeDtypeStruct((n, LANES), I32), jax.ShapeDtypeStruct((n, LANES), F32),
                   jax.ShapeDtypeStruct((n, LANES), F32)],
        compiler_params=_params("parallel"),
        name="router",
    )(x, whi, wlo, bias)


def _dispatch_kernel(ppos_ref, x_hbm, buf_hbm, xs_hbm, sem, *, rows):
    del buf_hbm
    base = pl.program_id(0) * rows

    def row_copy(a):
        return pltpu.make_async_copy(x_hbm.at[pl.ds(a >> 1, 1), :], xs_hbm.at[pl.ds(ppos_ref[a], 1), :], sem)

    def start(r, c):
        row_copy(base + r).start()
        return c
    lax.fori_loop(0, rows, start, 0)

    def wait(r, c):
        row_copy(base + r).wait()
        return c
    lax.fori_loop(0, rows, wait, 0)


def _dispatch(x, ppos, buf, *, rows):
    na = ppos.shape[0]
    assert na % rows == 0
    grid_spec = pltpu.PrefetchScalarGridSpec(
        num_scalar_prefetch=1,
        grid=(na // rows,),
        in_specs=[pl.BlockSpec(memory_space=pl.ANY), pl.BlockSpec(memory_space=pl.ANY)],
        out_specs=pl.BlockSpec(memory_space=pl.ANY),
        scratch_shapes=[pltpu.SemaphoreType.DMA],
    )
    return pl.pallas_call(
        functools.partial(_dispatch_kernel, rows=rows),
        grid_spec=grid_spec,
        out_shape=jax.ShapeDtypeStruct(buf.shape, buf.dtype),
        input_output_aliases={2: 0},
        compiler_params=_params("arbitrary"),
        name="dispatch",
    )(ppos, x, buf)


def _expert_kernel(te_ref, nv_ref, x_ref, wg_ref, wu_ref, wd_ref, y_ref):
    t = pl.program_id(0)

    @pl.when(nv_ref[t] > 0)
    def _():
        x = x_ref[...].astype(BF16)
        hg = jnp.dot(x, wg_ref[...], preferred_element_type=F32)
        hu = jnp.dot(x, wu_ref[...], preferred_element_type=F32)
        act = (hg * (1.0 / (1.0 + jnp.exp(-hg)))) * hu
        y_ref[...] = jnp.dot(act.astype(BF16), wd_ref[...], preferred_element_type=F32)

    @pl.when(nv_ref[t] == 0)
    def _():
        y_ref[...] = jnp.zeros(y_ref.shape, F32)


def _experts(xs, wg, wu, wd, tile_e, tile_nvalid, *, bm):
    p, d = xs.shape
    f = wg.shape[-1]
    grid_spec = pltpu.PrefetchScalarGridSpec(
        num_scalar_prefetch=2,
        grid=(p // bm,),
        in_specs=[pl.BlockSpec((bm, d), lambda t, te, nv: (t, 0)),
                  pl.BlockSpec((None, d, f), lambda t, te, nv: (te[t], 0, 0)),
                  pl.BlockSpec((None, d, f), lambda t, te, nv: (te[t], 0, 0)),
                  pl.BlockSpec((None, f, d), lambda t, te, nv: (te[t], 0, 0))],
        out_specs=pl.BlockSpec((bm, d), lambda t, te, nv: (t, 0)),
    )
    return pl.pallas_call(
        _expert_kernel,
        grid_spec=grid_spec,
        out_shape=jax.ShapeDtypeStruct((p, d), F32),
        compiler_params=_params("arbitrary"),
        name="experts",
    )(tile_e, tile_nvalid, xs, wg, wu, wd)


def _expert_plan(eid, *, bm):
    na = eid.shape[0] * 2
    flat_e = eid.reshape(na)
    onehot = (flat_e[:, None] == jnp.arange(N_EXPERTS, dtype=I32)[None, :]).astype(I32)
    csum = jnp.cumsum(onehot, axis=0)
    counts = csum[-1]
    padded = ((counts + bm - 1) // bm) * bm
    pend = jnp.cumsum(padded)
    pstart = pend - padded
    ppos = jnp.sum(onehot * (pstart[None, :] + csum - 1), axis=1).astype(I32)
    total = na + N_EXPERTS * bm
    tile_start = jnp.arange(total // bm, dtype=I32) * bm
    tile_e = jnp.minimum(jnp.sum((tile_start[:, None] >= pend[None, :]).astype(I32), axis=1), N_EXPERTS - 1)
    tile_nvalid = jnp.clip(pstart[tile_e] + counts[tile_e] - tile_start, 0, bm).astype(I32)
    return ppos, tile_e.astype(I32), tile_nvalid, total


def _combine_ln_kernel(ppos_ref, x_ref, g0_ref, g1_ref, y_hbm, g_ref, b_ref, of_ref, ob_ref, ybuf, sem,
                       *, bm, alpha):
    base = pl.program_id(0) * bm * 2
    d = x_ref.shape[-1]

    def row_copy(a):
        return pltpu.make_async_copy(y_hbm.at[pl.ds(ppos_ref[base + a], 1), :],
                                     ybuf.at[a & 1, pl.ds(a >> 1, 1), :], sem)

    def start(a, c):
        row_copy(a).start()
        return c
    lax.fori_loop(0, 2 * bm, start, 0)

    def wait(a, c):
        row_copy(a).wait()
        return c
    lax.fori_loop(0, 2 * bm, wait, 0)

    reps = d // LANES
    y = (alpha * x_ref[...] + jnp.tile(g0_ref[...], (1, reps)) * ybuf[0]
         + jnp.tile(g1_ref[...], (1, reps)) * ybuf[1])
    out = _layer_norm(y, g_ref[...], b_ref[...])
    of_ref[...] = out
    ob_ref[...] = out.astype(BF16)


def _combine_ln(x, ys, ppos, g0, g1, g, b, *, bm, alpha):
    n, d = x.shape
    row = lambda w: pl.BlockSpec((bm, w), lambda i, pp: (i, 0))
    grid_spec = pltpu.PrefetchScalarGridSpec(
        num_scalar_prefetch=1,
        grid=(n // bm,),
        in_specs=[row(d), row(LANES), row(LANES), pl.BlockSpec(memory_space=pl.ANY),
                  pl.BlockSpec((1, d), lambda i, pp: (0, 0)), pl.BlockSpec((1, d), lambda i, pp: (0, 0))],
        out_specs=[row(d), row(d)],
        scratch_shapes=[pltpu.VMEM((2, bm, d), F32), pltpu.SemaphoreType.DMA],
    )
    return pl.pallas_call(
        functools.partial(_combine_ln_kernel, bm=bm, alpha=alpha),
        grid_spec=grid_spec,
        out_shape=[jax.ShapeDtypeStruct((n, d), F32), jax.ShapeDtypeStruct((n, d), BF16)],
        compiler_params=_params("arbitrary"),
        name="combine_ln",
    )(ppos, x, g0, g1, ys, g, b)


def _rope_tables(pos):
    posf = pos.astype(F32)[:, None]
    inv128 = ROPE_THETA ** (-jnp.arange(HEAD_DIM // 2, dtype=F32) * 2.0 / HEAD_DIM)
    ang = posf * inv128[None, :]
    cos, sin = jnp.cos(ang), jnp.sin(ang)
    a128 = jnp.concatenate([cos, cos], axis=-1)
    b128 = jnp.concatenate([-sin, sin], axis=-1)
    inv64 = ROPE_THETA ** (-jnp.arange(IDX_DIM // 2, dtype=F32) * 2.0 / IDX_DIM)
    ang = posf * inv64[None, :]
    cos, sin = jnp.cos(ang), jnp.sin(ang)
    zero = jnp.zeros_like(sin)
    a64 = jnp.concatenate([cos, cos, cos, cos], axis=-1)
    b64 = jnp.concatenate([-sin, zero, -sin, zero], axis=-1)
    c64 = jnp.concatenate([zero, sin, zero, sin], axis=-1)
    return (a128, b128), (a64, b64, c64)


def _z_weight(w_in):
    o = _IN_OFF
    d = w_in.shape[0]
    cols = [w_in[:, o[0]:o[4]],
            w_in[:, o[6]:o[9]],
            w_in[:, o[10]:o[12]],
            w_in[:, o[4]:o[5]],
            w_in[:, o[12]:o[13]],
            w_in[:, o[5]:o[6]],
            w_in[:, o[9]:o[10]],
            jnp.zeros((d, LANES - IDX_HEADS - B_HEADS), w_in.dtype)]
    return jnp.concatenate(cols, axis=1).astype(BF16)


def _uq_weight(w):
    r = w.shape[0]
    w3 = w.reshape(r, C_HEADS, C_NOPE + C_ROPE)
    nope = w3[:, :, :C_NOPE].reshape(r, C_HEADS * C_NOPE)
    rope = jnp.pad(w3[:, :, C_NOPE:], ((0, 0), (0, 0), (0, LANES - C_ROPE))).reshape(r, C_HEADS * LANES)
    return jnp.concatenate([nope, rope], axis=1).astype(BF16)


def _pad_lanes(x):
    return jnp.pad(x, [(0, 0)] * (x.ndim - 1) + [(0, LANES - x.shape[-1])])


def _pad_rows(x, rows):
    return jnp.pad(x, [(0, 0), (0, rows - x.shape[1])] + [(0, 0)] * (x.ndim - 2))


def _heads_t(x, heads):
    nb, t, _ = x.shape
    return jnp.transpose(x.reshape(nb, t, heads, LANES), (0, 2, 3, 1))


def _heads_back(o_t):
    nb, h, w, t = o_t.shape
    return jnp.transpose(o_t, (0, 3, 1, 2)).reshape(nb * t, h * w)


def _v_blocks(v, heads, bk):
    nb, s, _ = v.shape
    return jnp.transpose(v.reshape(nb, s // bk, bk, heads, LANES), (0, 3, 1, 4, 2))


def _bf16_head(c):
    bits = lax.bitcast_convert_type(c, I32) & jnp.int32(-65536)
    return lax.bitcast_convert_type(bits, F32)


def _split3(c):
    hi = _bf16_head(c)
    mid = _bf16_head(c - hi)
    lo = c - hi - mid
    return hi.astype(BF16), mid.astype(BF16), lo.astype(BF16)


def _fox_bias_parts(csum_q, csum_k):
    nb, t, h = csum_q.shape
    s = csum_k.shape[1]
    one_q = jnp.ones((nb, t, h), BF16)
    qrows = jnp.stack(list(_split3(csum_q * LOG2E)) + [one_q] * 3, axis=-1)
    qb = jnp.pad(qrows, ((0, 0), (0, 0), (0, 0), (0, LANES - 6)))
    one_k = jnp.ones((nb, s, h), BF16)
    krows = jnp.stack([one_k] * 3 + [-p for p in _split3(csum_k * LOG2E)], axis=-1)
    kbias = jnp.pad(krows, ((0, 0), (0, 0), (0, 0), (0, LANES - 6))).reshape(nb, s, h * LANES)
    return jnp.transpose(qb, (0, 2, 3, 1)), kbias


def _pick(n, *cands):
    for c in cands:
        if n % c == 0:
            return c
    return n


def kernel(x_prompt, x_sample, cache_a_k, cache_a_v, cache_a_kidx, cache_b_k, cache_b_v, cache_b_logf,
           cache_c_latent, cache_c_krope, cache_mem_k, cache_mem_v, mem_prompt,
           w_in, b_f, c_q_norm, w_c_uq, c_kv_norm, w_c_uk, w_c_uv, w_o, ln1_g, ln1_b,
           w_mq, w_mk, w_mv, w_mo, ln2_g, ln2_b, w_rg, b_rg, w_re, b_re,
           w_gate, w_up, w_down, ln3_g, ln3_b):
    depth = w_in.shape[0]
    bp, tp, d = x_prompt.shape
    bs, ts, _ = x_sample.shape
    past = cache_a_k.shape[2]
    n_mem = mem_prompt.shape[1]
    assert bp == 1
    n_p, n_s = bp * tp, bs * ts
    n = n_p + n_s
    alpha = (2 * depth) ** 0.25
    s_s = past + ts
    s_pad = -(-s_s // COUNT_ROWS) * COUNT_ROWS

    bm_tok = _pick(n, 1024, 512, 256, 128, 64)
    bm_post = _pick(n, 256, 128, 64)
    bm_ln = _pick(n, 256, 128, 64)
    bq_f = _pick(tp, 1024, 512, 256, 128)
    bq_a = _pick(tp, 512, 256, 128)
    kb_a = _pick(tp, 1024, 512, 256)
    bm_moe = 256

    x = jnp.concatenate([x_prompt.reshape(n_p, d), x_sample.reshape(n_s, d)], axis=0)
    xb = x.astype(BF16)
    pos_p = jnp.arange(tp, dtype=I32)
    pos_s = past + jnp.arange(ts, dtype=I32)
    tabs128, tabs64 = _rope_tables(jnp.concatenate([pos_p, jnp.tile(pos_s, bs)]))

    nq_f = tp // bq_f
    nfull_p = jnp.arange(nq_f, dtype=I32)
    nkv_p = nfull_p + 1
    one = jnp.ones((1,), I32)
    zero = jnp.zeros((1,), I32)
    nvis_p = (((jnp.arange(tp // bq_a, dtype=I32) + 1) * bq_a + kb_a - 1) // kb_a).astype(I32)
    nvis_s = jnp.ones((bs,), I32)

    xs_buf = jnp.zeros((2 * n + N_EXPERTS * bm_moe, d), F32)
    state_p = [[] for _ in range(8)]
    state_s = [[] for _ in range(8)]
    mem_k_out, mem_v_out = [], []

    for l in range(depth):
        z = _matmul(xb, _z_weight(w_in[l]), bm=bm_tok, bn=512, out_dtype=F32, name="in_proj")
        bf_row = jnp.zeros((1, LANES), F32).at[0, SM2_BF:SM2_BF + B_HEADS].set(b_f[l].astype(F32))
        aq, ak, aqi, bq_, cqn, clat, sm1, sm2 = _post(
            z, tabs128, tabs64, bf_row, c_q_norm[l].reshape(1, -1).astype(F32),
            c_kv_norm[l].reshape(1, -1).astype(F32), bm=bm_post)
        cq = _matmul_rope(cqn, _uq_weight(w_c_uq[l]), tabs64, bm=bm_tok, bn=512,
                          first_rope_col=C_HEADS * C_NOPE, out_scale=(C_NOPE + C_ROPE) ** -0.5 * LOG2E,
                          name="c_q_up")
        av = z[:, Z_AV:Z_AV + HEAD_DIM]
        aki = sm1[:, :IDX_DIM]
        ckr = sm1[:, IDX_DIM:]
        aw = sm2[:, :IDX_HEADS]
        blogf = sm2[:, SM2_BF:SM2_BF + B_HEADS]
        bk_ = z[:, Z_BK:Z_BK + B_HEADS * HEAD_DIM]
        bv_ = z[:, Z_BV:Z_BV + B_HEADS * HEAD_DIM]
        rows = (ak, av, aki, bk_.reshape(n, B_HEADS, HEAD_DIM), bv_.reshape(n, B_HEADS, HEAD_DIM),
                blogf, clat, ckr)
        for i, r in enumerate(rows):
            state_p[i].append(r[:n_p].reshape((bp, tp) + r.shape[1:]))
            state_s[i].append(r[n_p:].reshape((bs, ts) + r.shape[1:]))
        grp_p = lambda a: a[:n_p].reshape(bp, tp, -1)
        grp_s = lambda a: a[n_p:].reshape(bs, ts, -1)
        cat_s = lambda c, r: jnp.concatenate([c.reshape(bs, past, -1).astype(F32), grp_s(r)], axis=1)

        aw_scaled = aw * (IDX_DIM ** -0.5)

        def sparse(aq_g, aqi_g, aw_g, ki, k, v, nvis, *, bq, kb, s_real, q_off, name):
            ka = _pad_lanes(ki).astype(BF16)
            kb_ = jnp.concatenate([jnp.zeros_like(ki), ki], axis=-1).astype(BF16)
            vt3 = _v_blocks(v.astype(BF16), 1, kb)[:, 0]
            out_t = _sparse_attention(
                jnp.swapaxes(aq_g, 1, 2), jnp.swapaxes(aqi_g, 1, 2), jnp.swapaxes(aw_g, 1, 2),
                ka, kb_, k.astype(BF16), vt3, nvis, bq=bq, kb=kb, s_real=s_real,
                topk=min(TOPK_MAX, s_real // 4), q_off=q_off, name=name)
            return jnp.swapaxes(out_t, 1, 2).reshape(-1, A_HEADS * HEAD_DIM)

        oa_p = sparse(grp_p(aq), grp_p(aqi), grp_p(aw_scaled), grp_p(aki), grp_p(ak), grp_p(av), nvis_p,
                      bq=bq_a, kb=kb_a, s_real=tp, q_off=0, name="sparse_prompt")
        oa_s = sparse(grp_s(aq), grp_s(aqi), grp_s(aw_scaled),
                      _pad_rows(cat_s(cache_a_kidx[l], aki), s_pad), _pad_rows(cat_s(cache_a_k[l], ak), s_pad),
                      _pad_rows(cat_s(cache_a_v[l], av), s_pad), nvis_s,
                      bq=ts, kb=s_pad, s_real=s_s, q_off=past, name="sparse_sample")

        csum_p = jnp.cumsum(grp_p(blogf), axis=1)
        qb_p, kbias_p = _fox_bias_parts(csum_p, csum_p)
        ob_p = _tflash([_heads_t(grp_p(bq_), B_HEADS), qb_p], (0, 0),
                       [grp_p(bk_).astype(BF16), kbias_p], (False, False),
                       _v_blocks(grp_p(bv_).astype(BF16), B_HEADS, bq_f), nfull_p, nkv_p,
                       hb=2, bq=bq_f, bk=bq_f, mode="causal", q_off=0, name="fox_prompt")
        csum_s = jnp.cumsum(cat_s(cache_b_logf[l], blogf), axis=1)
        qb_s, kbias_s = _fox_bias_parts(csum_s[:, past:], csum_s)
        ob_s = _tflash([_heads_t(grp_s(bq_), B_HEADS), qb_s], (0, 0),
                       [cat_s(cache_b_k[l], bk_).astype(BF16), kbias_s], (False, False),
                       _v_blocks(cat_s(cache_b_v[l], bv_).astype(BF16), B_HEADS, s_s), zero, one,
                       hb=2, bq=ts, bk=s_s, mode="causal", q_off=past, name="fox_sample")

        w_ukv = jnp.concatenate([w_c_uk[l], w_c_uv[l]], axis=1).astype(BF16)
        kw = C_HEADS * C_NOPE
        kv_p = _matmul(clat[:n_p], w_ukv, bm=_pick(n_p, 1024, 512, 256), bn=w_ukv.shape[1], out_dtype=BF16,
                       name="c_kv_up_prompt").reshape(bp, tp, -1)
        cq_t_p = _heads_t(grp_p(cq), 2 * C_HEADS)
        oc_p = _tflash([cq_t_p, cq_t_p], (0, C_HEADS),
                       [kv_p[..., :kw], _pad_lanes(grp_p(ckr)).astype(BF16)], (False, True),
                       _v_blocks(kv_p[..., kw:], C_HEADS, bq_f), nfull_p, nkv_p,
                       hb=2, bq=bq_f, bk=bq_f, mode="chunk", q_off=0, name="mla_prompt")
        lat_s = cat_s(cache_c_latent[l], clat)
        kv_s = _matmul(lat_s.reshape(bs * s_s, -1), w_ukv, bm=_pick(bs * s_s, 1024, 512, 256, 128, 64),
                       bn=w_ukv.shape[1], out_dtype=BF16, name="c_kv_up_sample").reshape(bs, s_s, -1)
        cq_t_s = _heads_t(grp_s(cq), 2 * C_HEADS)
        oc_s = _tflash([cq_t_s, cq_t_s], (0, C_HEADS),
                       [kv_s[..., :kw], _pad_lanes(cat_s(cache_c_krope[l], ckr)).astype(BF16)], (False, True),
                       _v_blocks(kv_s[..., kw:], C_HEADS, s_s), zero, one,
                       hb=2, bq=ts, bk=s_s, mode="chunk", q_off=past, name="mla_sample")

        mixed = jnp.concatenate([
            jnp.concatenate([oa_p, oa_s], axis=0),
            jnp.concatenate([_heads_back(ob_p), _heads_back(ob_s)], axis=0),
            jnp.concatenate([_heads_back(oc_p), _heads_back(oc_s)], axis=0)], axis=1)
        x1, x1b = _mm_res_ln(mixed, w_o[l].astype(BF16), x, ln1_g[l].reshape(1, d), ln1_b[l].reshape(1, d),
                             bm=bm_ln, bk=_pick(mixed.shape[1], 1024, 512), alpha=alpha, name="out_proj_ln1")

        mw = MEM_HEADS * MEM_DIM
        w_mkv = jnp.concatenate([w_mk[l], w_mv[l]], axis=1).astype(BF16)
        mkv = _matmul(mem_prompt.reshape(bp * n_mem, d), w_mkv, bm=_pick(bp * n_mem, 256, 128, 64), bn=512,
                      out_dtype=F32, name="mem_kv")
        mem_k_out.append(mkv[:, :mw].reshape(bp, n_mem, MEM_HEADS, MEM_DIM))
        mem_v_out.append(mkv[:, mw:].reshape(bp, n_mem, MEM_HEADS, MEM_DIM))
        qm = _matmul(x1b, w_mq[l].astype(BF16), bm=bm_tok, bn=mw, out_dtype=BF16, name="mem_q",
                     out_scale=MEM_DIM ** -0.5 * LOG2E)
        mkv3 = mkv.reshape(bp, n_mem, 2 * mw).astype(BF16)
        om_p = _tflash([_heads_t(grp_p(qm), MEM_HEADS)], (0,), [mkv3[..., :mw]], (False,),
                       _v_blocks(mkv3[..., mw:], MEM_HEADS, n_mem), jnp.ones((nq_f,), I32), jnp.ones((nq_f,), I32),
                       hb=2, bq=bq_f, bk=n_mem, mode="none", q_off=0, name="mem_prompt")
        om_s = _tflash([_heads_t(grp_s(qm), MEM_HEADS)], (0,),
                       [cache_mem_k[l].reshape(bs, n_mem, mw).astype(BF16)], (False,),
                       _v_blocks(cache_mem_v[l].reshape(bs, n_mem, mw).astype(BF16), MEM_HEADS, n_mem), one, one,
                       hb=2, bq=ts, bk=n_mem, mode="none", q_off=0, name="mem_sample")
        om = jnp.concatenate([_heads_back(om_p), _heads_back(om_s)], axis=0)
        x2, _ = _mm_res_ln(om, w_mo[l].astype(BF16), x1, ln2_g[l].reshape(1, d), ln2_b[l].reshape(1, d),
                           bm=bm_ln, bk=mw, alpha=alpha, name="mem_out_ln2")

        w_r = jnp.concatenate([w_rg[l], jnp.transpose(w_re[l], (1, 0, 2)).reshape(d, N_EXPERTS)], axis=1).astype(F32)
        w_r = _pad_lanes(w_r)
        w_r_head = _bf16_head(w_r)
        w_r_hi = w_r_head.astype(BF16)
        w_r_lo = (w_r - w_r_head).astype(BF16)
        b_r = _pad_lanes(jnp.concatenate([b_rg[l], b_re[l].reshape(-1)]).astype(F32).reshape(1, -1))
        ids, g0, g1 = _router(x2, w_r_hi, w_r_lo, b_r, bm=_pick(n, 512, 256, 128, 64))
        ppos, tile_e, tile_nvalid, total = _expert_plan(ids[:, :2], bm=bm_moe)
        assert total == xs_buf.shape[0]
        xs_buf = _dispatch(x2, ppos, xs_buf, rows=_pick(2 * n, 512, 256, 128))
        ys = _experts(xs_buf, w_gate[l].astype(BF16), w_up[l].astype(BF16), w_down[l].astype(BF16),
                      tile_e, tile_nvalid, bm=bm_moe)
        x, xb = _combine_ln(x2, ys, ppos, g0, g1, ln3_g[l].reshape(1, d), ln3_b[l].reshape(1, d),
                            bm=_pick(n, 256, 128, 64), alpha=alpha)

    outs = [x[:n_p].reshape(bp, tp, d), x[n_p:].reshape(bs, ts, d)]
    outs += [jnp.stack(c) for c in state_p]
    outs += [jnp.stack(mem_k_out), jnp.stack(mem_v_out)]
    outs += [jnp.stack(c) for c in state_s]
    return tuple(outs)
```

```python
import functools
import math

import jax
import jax.numpy as jnp
import numpy as np
from jax import lax
from jax.experimental import pallas as pl
from jax.experimental.pallas import tpu as pltpu

F32 = jnp.float32
BF16 = jnp.bfloat16
I32 = jnp.int32

LANES = 128
SUBLANES = 8
VMEM_LIMIT = 56 * 1024 * 1024

CHUNK_SHIFT = 6
HEAD_DIM = 128
ROPE_THETA = 10000.0
A_HEADS = 8
IDX_HEADS = 32
IDX_DIM = 64
TOPK_MAX = 256
B_HEADS = 8
C_HEADS = 16
C_Q_RANK = 768
C_KV_RANK = 256
C_NOPE = 128
C_ROPE = 64
C_V = 128
MEM_HEADS = 4
MEM_DIM = 128
N_GROUPS = 4
EXPERTS_PER_GROUP = 4
N_EXPERTS = N_GROUPS * EXPERTS_PER_GROUP
LN_EPS = 1e-5
RMS_EPS = 1e-6
LOG2E = math.log2(math.e)

_IN_SIZES = (A_HEADS * HEAD_DIM, HEAD_DIM, HEAD_DIM, IDX_HEADS * IDX_DIM, IDX_DIM, IDX_HEADS,
             B_HEADS * HEAD_DIM, B_HEADS * HEAD_DIM, B_HEADS * HEAD_DIM, B_HEADS,
             C_Q_RANK, C_KV_RANK, C_ROPE)
_IN_OFF = np.concatenate([[0], np.cumsum(_IN_SIZES)]).astype(int)

Z_AQ = 0
Z_AK = Z_AQ + A_HEADS * HEAD_DIM
Z_AV = Z_AK + HEAD_DIM
Z_AQI = Z_AV + HEAD_DIM
Z_BQ = Z_AQI + IDX_HEADS * IDX_DIM
Z_BK = Z_BQ + B_HEADS * HEAD_DIM
Z_BV = Z_BK + B_HEADS * HEAD_DIM
Z_CQ = Z_BV + B_HEADS * HEAD_DIM
Z_CKV = Z_CQ + C_Q_RANK
Z_SM1 = Z_CKV + C_KV_RANK
Z_SM2 = Z_SM1 + LANES
Z_WIDTH = Z_SM2 + LANES
SM2_BF = IDX_HEADS

M_INIT = -1e30
INT_MIN = -2 ** 31
INT_MAX = 2 ** 31 - 1
SCORE_ROWS = 128
SCORE_COLS = 256
COUNT_ROWS = 256


def _params(*sem):
    return pltpu.CompilerParams(dimension_semantics=sem, vmem_limit_bytes=VMEM_LIMIT)


def _mm_kernel(x_ref, w_ref, o_ref, *, out_scale):
    x = x_ref[...].astype(BF16)
    acc = jnp.dot(x, w_ref[...], preferred_element_type=F32)
    if out_scale != 1.0:
        acc = acc * out_scale
    o_ref[...] = acc.astype(o_ref.dtype)


def _matmul(x, w, *, bm, bn, out_dtype, name, out_scale=1.0):
    m, k = x.shape
    n = w.shape[1]
    assert m % bm == 0 and n % bn == 0, (x.shape, w.shape, bm, bn)
    return pl.pallas_call(
        functools.partial(_mm_kernel, out_scale=out_scale),
        grid=(m // bm, n // bn),
        in_specs=[pl.BlockSpec((bm, k), lambda i, j: (i, 0)),
                  pl.BlockSpec((k, bn), lambda i, j: (0, j))],
        out_specs=pl.BlockSpec((bm, bn), lambda i, j: (i, j)),
        out_shape=jax.ShapeDtypeStruct((m, n), out_dtype),
        compiler_params=_params("parallel", "parallel"),
        name=name,
    )(x, w)


def _rope64(x, a, b, c):
    return x * a + pltpu.roll(x, 96, 1) * b + pltpu.roll(x, 32, 1) * c


def _rope128(x, a, b):
    return x * a + pltpu.roll(x, 64, 1) * b


def _mm_rope_kernel(x_ref, w_ref, a_ref, b_ref, c_ref, o_ref, *, first_rope_block, out_scale):
    j = pl.program_id(1)
    acc = jnp.dot(x_ref[...], w_ref[...], preferred_element_type=F32) * out_scale

    @pl.when(j < first_rope_block)
    def _():
        o_ref[...] = acc.astype(o_ref.dtype)

    @pl.when(j >= first_rope_block)
    def _():
        a, b, c = a_ref[...], b_ref[...], c_ref[...]
        for g in range(acc.shape[1] // LANES):
            sl = slice(g * LANES, (g + 1) * LANES)
            o_ref[:, sl] = _rope64(acc[:, sl], a, b, c).astype(o_ref.dtype)


def _matmul_rope(x, w, tabs, *, bm, bn, first_rope_col, out_scale, name):
    m, k = x.shape
    n = w.shape[1]
    assert m % bm == 0 and n % bn == 0 and first_rope_col % bn == 0
    tab_spec = pl.BlockSpec((bm, LANES), lambda i, j: (i, 0))
    return pl.pallas_call(
        functools.partial(_mm_rope_kernel, first_rope_block=first_rope_col // bn, out_scale=out_scale),
        grid=(m // bm, n // bn),
        in_specs=[pl.BlockSpec((bm, k), lambda i, j: (i, 0)),
                  pl.BlockSpec((k, bn), lambda i, j: (0, j)),
                  tab_spec, tab_spec, tab_spec],
        out_specs=pl.BlockSpec((bm, bn), lambda i, j: (i, j)),
        out_shape=jax.ShapeDtypeStruct((m, n), BF16),
        compiler_params=_params("parallel", "parallel"),
        name=name,
    )(x, w, *tabs)


def _post_kernel(z_ref, a128_ref, b128_ref, a64_ref, b64_ref, c64_ref, bf_ref, gq_ref, gkv_ref,
                 aq_ref, ak_ref, aqi_ref, bq_ref, cqn_ref, clat_ref, sm1_ref, sm2_ref):
    a128, b128 = a128_ref[...], b128_ref[...]
    a64, b64, c64 = a64_ref[...], b64_ref[...], c64_ref[...]
    qscale = HEAD_DIM ** -0.5 * LOG2E
    for g in range(A_HEADS):
        src = slice(Z_AQ + g * LANES, Z_AQ + (g + 1) * LANES)
        aq_ref[:, g * LANES:(g + 1) * LANES] = (_rope128(z_ref[:, src], a128, b128) * qscale).astype(BF16)
    ak_ref[...] = _rope128(z_ref[:, Z_AK:Z_AK + LANES], a128, b128)
    for g in range(IDX_HEADS * IDX_DIM // LANES):
        src = slice(Z_AQI + g * LANES, Z_AQI + (g + 1) * LANES)
        aqi_ref[:, g * LANES:(g + 1) * LANES] = _rope64(z_ref[:, src], a64, b64, c64).astype(BF16)
    bq_ref[...] = (z_ref[:, Z_BQ:Z_BQ + B_HEADS * HEAD_DIM] * qscale).astype(BF16)
    cq = z_ref[:, Z_CQ:Z_CQ + C_Q_RANK]
    cq = cq * lax.rsqrt(jnp.mean(cq * cq, axis=-1, keepdims=True) + RMS_EPS) * gq_ref[...]
    cqn_ref[...] = cq.astype(BF16)
    ckv = z_ref[:, Z_CKV:Z_CKV + C_KV_RANK]
    clat_ref[...] = ckv * lax.rsqrt(jnp.mean(ckv * ckv, axis=-1, keepdims=True) + RMS_EPS) * gkv_ref[...]
    sm1_ref[...] = _rope64(z_ref[:, Z_SM1:Z_SM1 + LANES], a64, b64, c64)
    s2 = z_ref[:, Z_SM2:Z_SM2 + LANES]
    lane = lax.broadcasted_iota(I32, s2.shape, 1)
    f = s2 + bf_ref[...]
    logf = jnp.minimum(f, 0.0) - jnp.log1p(jnp.exp(-jnp.abs(f)))
    sm2_ref[...] = jnp.where(lane < SM2_BF, s2 * (IDX_HEADS ** -0.5),
                             jnp.where(lane < SM2_BF + B_HEADS, logf, 0.0))


def _post(z, tabs128, tabs64, bf_row, gq, gkv, *, bm):
    n = z.shape[0]
    row = lambda w: pl.BlockSpec((bm, w), lambda i: (i, 0))
    const = lambda w: pl.BlockSpec((1, w), lambda i: (0, 0))
    widths = (A_HEADS * HEAD_DIM, LANES, IDX_HEADS * IDX_DIM, B_HEADS * HEAD_DIM, C_Q_RANK, C_KV_RANK, LANES, LANES)
    dtypes = (BF16, F32, BF16, BF16, BF16, F32, F32, F32)
    return pl.pallas_call(
        _post_kernel,
        grid=(n // bm,),
        in_specs=[row(Z_WIDTH)] + [row(LANES)] * 5 + [const(LANES), const(C_Q_RANK), const(C_KV_RANK)],
        out_specs=[row(w) for w in widths],
        out_shape=[jax.ShapeDtypeStruct((n, w), d) for w, d in zip(widths, dtypes)],
        compiler_params=_params("parallel"),
        name="post_projection",
    )(z, *tabs128, *tabs64, bf_row, gq, gkv)


def _layer_norm(y, g, b):
    mu = jnp.mean(y, axis=-1, keepdims=True)
    yc = y - mu
    var = jnp.mean(yc * yc, axis=-1, keepdims=True)
    return yc * lax.rsqrt(var + LN_EPS) * g + b


def _mm_res_ln_kernel(x_ref, w_ref, r_ref, g_ref, b_ref, of_ref, ob_ref, acc_ref, *, nk, alpha):
    k = pl.program_id(1)

    @pl.when(k == 0)
    def _():
        acc_ref[...] = jnp.zeros_like(acc_ref)

    acc_ref[...] += jnp.dot(x_ref[...], w_ref[...], preferred_element_type=F32)

    @pl.when(k == nk - 1)
    def _():
        out = _layer_norm(alpha * r_ref[...] + acc_ref[...], g_ref[...], b_ref[...])
        of_ref[...] = out
        ob_ref[...] = out.astype(BF16)


def _mm_res_ln(x, w, res, g, b, *, bm, bk, alpha, name):
    m, k = x.shape
    d = w.shape[1]
    nk = k // bk
    return pl.pallas_call(
        functools.partial(_mm_res_ln_kernel, nk=nk, alpha=alpha),
        grid=(m // bm, nk),
        in_specs=[pl.BlockSpec((bm, bk), lambda i, kk: (i, kk)),
                  pl.BlockSpec((bk, d), lambda i, kk: (kk, 0)),
                  pl.BlockSpec((bm, d), lambda i, kk: (i, 0)),
                  pl.BlockSpec((1, d), lambda i, kk: (0, 0)),
                  pl.BlockSpec((1, d), lambda i, kk: (0, 0))],
        out_specs=[pl.BlockSpec((bm, d), lambda i, kk: (i, 0)),
                   pl.BlockSpec((bm, d), lambda i, kk: (i, 0))],
        out_shape=[jax.ShapeDtypeStruct((m, d), F32), jax.ShapeDtypeStruct((m, d), BF16)],
        scratch_shapes=[pltpu.VMEM((bm, d), F32)],
        compiler_params=_params("parallel", "arbitrary"),
        name=name,
    )(x, w, res, g, b)


def _tflash_kernel(nfull_ref, nkv_ref, *refs, n_parts, k_shared, hb, bk, mode, q_off):
    refs = list(refs)
    q_refs = [refs.pop(0) for _ in range(n_parts)]
    k_refs = [refs.pop(0) for _ in range(n_parts)]
    vt_ref = refs.pop(0)
    o_ref, acc_ref = refs
    qi = pl.program_id(2)
    bq = o_ref.shape[-1]
    qs = [jnp.concatenate([qr[h] for qr in q_refs], axis=0) if n_parts > 1 else q_refs[0][h]
          for h in range(hb)]
    acc_ref[...] = jnp.zeros(acc_ref.shape, F32)

    def block(j, carry, masked):
        ms, ls = carry
        off = pl.multiple_of(j * bk, bk)
        if masked:
            kpos = off + lax.broadcasted_iota(I32, (bk, bq), 0)
            qpos = q_off + qi * bq + lax.broadcasted_iota(I32, (bk, bq), 1)
            if mode == "chunk":
                mask = (kpos >> CHUNK_SHIFT) <= (qpos >> CHUNK_SHIFT)
            else:
                mask = kpos <= qpos
        new_m, new_l = [], []
        for h in range(hb):
            parts = []
            for p_i, kr in enumerate(k_refs):
                hs = 0 if k_shared[p_i] else h
                parts.append(kr[pl.ds(off, bk), hs * LANES:(hs + 1) * LANES])
            k = jnp.concatenate(parts, axis=1) if n_parts > 1 else parts[0]
            s = jnp.dot(k, qs[h], preferred_element_type=F32)
            if masked:
                s = jnp.where(mask, s, -jnp.inf)
            m_new = jnp.maximum(ms[h], jnp.max(s, axis=0, keepdims=True))
            p = jnp.exp2(s - m_new)
            alpha = jnp.exp2(ms[h] - m_new)
            new_l.append(alpha * ls[h] + jnp.sum(p, axis=0, keepdims=True))
            new_m.append(m_new)
            acc_ref[h] = alpha * acc_ref[h] + jnp.dot(vt_ref[h, j], p.astype(BF16), preferred_element_type=F32)
        return tuple(new_m), tuple(new_l)

    init = (tuple(jnp.full((1, bq), M_INIT, F32) for _ in range(hb)),
            tuple(jnp.zeros((1, bq), F32) for _ in range(hb)))
    carry = lax.fori_loop(0, nfull_ref[qi], functools.partial(block, masked=False), init)
    if mode != "none":
        carry = lax.fori_loop(nfull_ref[qi], nkv_ref[qi], functools.partial(block, masked=True), carry)
    for h in range(hb):
        o_ref[h] = (acc_ref[h] / carry[1][h]).astype(o_ref.dtype)


def _tflash(q_parts, q_head0, k_parts, k_shared, vt, nfull, nkv, *, hb, bq, bk, mode, q_off, name):
    nb, heads = vt.shape[0], vt.shape[1]
    s = vt.shape[2] * bk
    nq = nfull.shape[0]
    assert heads % hb == 0 and all(h0 % hb == 0 for h0 in q_head0)
    in_specs = []
    for h0 in q_head0:
        in_specs.append(pl.BlockSpec((None, hb, LANES, bq),
                                     lambda b, g, qi, nf, nk, h0=h0: (b, h0 // hb + g, 0, qi)))
    for shared in k_shared:
        if shared:
            in_specs.append(pl.BlockSpec((None, s, LANES), lambda b, g, qi, nf, nk: (b, 0, 0)))
        else:
            in_specs.append(pl.BlockSpec((None, s, hb * LANES), lambda b, g, qi, nf, nk: (b, 0, g)))
    in_specs.append(pl.BlockSpec((None, hb, s // bk, LANES, bk), lambda b, g, qi, nf, nk: (b, g, 0, 0, 0)))
    grid_spec = pltpu.PrefetchScalarGridSpec(
        num_scalar_prefetch=2,
        grid=(nb, heads // hb, nq),
        in_specs=in_specs,
        out_specs=pl.BlockSpec((None, hb, LANES, bq), lambda b, g, qi, nf, nk: (b, g, 0, qi)),
        scratch_shapes=[pltpu.VMEM((hb, LANES, bq), F32)],
    )
    return pl.pallas_call(
        functools.partial(_tflash_kernel, n_parts=len(q_parts), k_shared=tuple(k_shared), hb=hb, bk=bk,
                          mode=mode, q_off=q_off),
        grid_spec=grid_spec,
        out_shape=jax.ShapeDtypeStruct((nb, heads, LANES, nq * bq), BF16),
        compiler_params=_params("parallel", "parallel", "arbitrary"),
        name=name,
    )(nfull, nkv, *q_parts, *k_parts, vt)


def _sparse_kernel(nvis_ref, aq_ref, aqi_ref, aw_ref, ka_ref, kb_ref, k_ref, vt_ref,
                   o_ref, key_ref, bias_ref, acc_ref, j_ref, *, nq, kb, s_real, topk, index_bits, q_off):
    b = pl.program_id(0)
    qi = pl.program_id(1)
    nblk = nvis_ref[b * nq + qi]
    bq = aq_ref.shape[-1]
    cols = min(bq, SCORE_COLS)
    int_min = jnp.int32(INT_MIN)
    qchunk = (q_off + qi * bq + lax.broadcasted_iota(I32, (1, bq), 1)) >> CHUNK_SHIFT

    def score_tile(t, carry):
        off = pl.multiple_of(t * SCORE_ROWS, SCORE_ROWS)
        ka = ka_ref[pl.ds(off, SCORE_ROWS), :]
        kbb = kb_ref[pl.ds(off, SCORE_ROWS), :]
        kidx = off + lax.broadcasted_iota(I32, (SCORE_ROWS, cols), 0)
        kchunk = jnp.where(kidx < s_real, kidx >> CHUNK_SHIFT, jnp.int32(INT_MAX))
        for c in range(bq // cols):
            cs = slice(c * cols, (c + 1) * cols)
            acc = jnp.zeros((SCORE_ROWS, cols), F32)
            for hp in range(IDX_HEADS // 2):
                qp = aqi_ref[hp * LANES:(hp + 1) * LANES, cs]
                ze = jnp.dot(ka, qp, preferred_element_type=F32)
                zo = jnp.dot(kbb, qp, preferred_element_type=F32)
                acc = acc + jnp.maximum(ze, 0.0) * aw_ref[2 * hp:2 * hp + 1, cs]
                acc = acc + jnp.maximum(zo, 0.0) * aw_ref[2 * hp + 1:2 * hp + 2, cs]
            bits = pltpu.bitcast(acc, I32)
            key = bits ^ ((bits >> 31) & jnp.int32(INT_MAX))
            key_ref[pl.ds(off, SCORE_ROWS), cs] = jnp.where(kchunk <= qchunk[:, cs], key, int_min)
        return carry

    lax.fori_loop(0, nblk * (kb // SCORE_ROWS), score_tile, 0)

    def count(pred):
        def body(t, c):
            off = pl.multiple_of(t * COUNT_ROWS, COUNT_ROWS)
            kk = key_ref[pl.ds(off, COUNT_ROWS), :]
            idx = off + lax.broadcasted_iota(I32, (COUNT_ROWS, bq), 0)
            hit = jnp.where(pred(kk, idx), 1.0, 0.0)
            return c + jnp.sum(hit.reshape(COUNT_ROWS // SUBLANES, SUBLANES, bq), axis=0)
        c = lax.fori_loop(0, nblk * (kb // COUNT_ROWS), body, jnp.zeros((SUBLANES, bq), F32))
        return jnp.sum(c, axis=0, keepdims=True)

    def bit_step(i, carry):
        t_u, n_keep = carry
        cand_u = t_u | (jnp.int32(1) << (31 - i))
        cand_s = cand_u ^ int_min
        cnt = count(lambda kk, idx: kk >= cand_s)
        take = cnt >= topk
        return jnp.where(take, cand_u, t_u), jnp.where(take, cnt, n_keep)

    t_u, n_ge = lax.fori_loop(0, 32, bit_step, (jnp.zeros((1, bq), I32), jnp.zeros((1, bq), F32)))
    t_s = t_u ^ int_min
    need = jnp.logical_and(n_ge > topk, t_u != 0)
    j_ref[...] = jnp.full(j_ref.shape, INT_MAX, I32)

    @pl.when(jnp.max(jnp.where(need, 1.0, 0.0)) > 0.0)
    def _():
        want = topk - count(lambda kk, idx: kk > t_s)

        def idx_step(i, jc):
            cand = jc | (jnp.int32(1) << (index_bits - 1 - i))
            cnt = count(lambda kk, idx: jnp.logical_and(kk == t_s, idx < cand))
            return jnp.where(cnt < want, cand, jc)
        jc = lax.fori_loop(0, index_bits, idx_step, jnp.zeros((1, bq), I32))
        j_ref[...] = jnp.where(need, jc, jnp.int32(INT_MAX))

    jcut = j_ref[...]
    acc_ref[...] = jnp.zeros(acc_ref.shape, F32)

    def attend_block(j, carry):
        ms, ls = carry
        off = pl.multiple_of(j * kb, kb)
        kk = key_ref[pl.ds(off, kb), :]
        idx = off + lax.broadcasted_iota(I32, (kb, bq), 0)
        sel = jnp.logical_or(kk > t_s, jnp.logical_and(kk == t_s, idx <= jcut))
        sel = jnp.logical_and(sel, kk != int_min)
        bias_ref[...] = jnp.where(sel, 0.0, -jnp.inf)
        kblk = k_ref[pl.ds(off, kb), :]
        vt = vt_ref[j]
        new_m, new_l = [], []
        for h in range(A_HEADS):
            hs = slice(h * HEAD_DIM, (h + 1) * HEAD_DIM)
            s = jnp.dot(kblk, aq_ref[hs, :], preferred_element_type=F32) + bias_ref[...]
            m_new = jnp.maximum(ms[h], jnp.max(s, axis=0, keepdims=True))
            p = jnp.exp2(s - m_new)
            alpha = jnp.exp2(ms[h] - m_new)
            new_l.append(alpha * ls[h] + jnp.sum(p, axis=0, keepdims=True))
            new_m.append(m_new)
            acc_ref[hs, :] = alpha * acc_ref[hs, :] + jnp.dot(vt, p.astype(BF16), preferred_element_type=F32)
        return tuple(new_m), tuple(new_l)

    init = (tuple(jnp.full((1, bq), M_INIT, F32) for _ in range(A_HEADS)),
            tuple(jnp.zeros((1, bq), F32) for _ in range(A_HEADS)))
    _, ls = lax.fori_loop(0, nblk, attend_block, init)
    for h in range(A_HEADS):
        hs = slice(h * HEAD_DIM, (h + 1) * HEAD_DIM)
        o_ref[hs, :] = (acc_ref[hs, :] / ls[h]).astype(o_ref.dtype)


def _sparse_attention(aq_t, aqi_t, aw_t, ka, kb_, k, vt3, nvis, *, bq, kb, s_real, topk, q_off, name):
    nb, s = k.shape[0], k.shape[1]
    nq = nvis.shape[0] // nb
    assert s % kb == 0 and kb % COUNT_ROWS == 0 and kb % SCORE_ROWS == 0 and bq % min(bq, SCORE_COLS) == 0
    qspec = lambda rows: pl.BlockSpec((None, rows, bq), lambda b, qi, nv: (b, 0, qi))
    kspec = pl.BlockSpec((None, s, LANES), lambda b, qi, nv: (b, 0, 0))
    grid_spec = pltpu.PrefetchScalarGridSpec(
        num_scalar_prefetch=1,
        grid=(nb, nq),
        in_specs=[qspec(A_HEADS * HEAD_DIM), qspec(IDX_HEADS * IDX_DIM), qspec(IDX_HEADS),
                  kspec, kspec, kspec,
                  pl.BlockSpec((None, s // kb, LANES, kb), lambda b, qi, nv: (b, 0, 0, 0))],
        out_specs=qspec(A_HEADS * HEAD_DIM),
        scratch_shapes=[pltpu.VMEM((s, bq), I32), pltpu.VMEM((kb, bq), F32),
                        pltpu.VMEM((A_HEADS * HEAD_DIM, bq), F32), pltpu.VMEM((1, bq), I32)],
    )
    return pl.pallas_call(
        functools.partial(_sparse_kernel, nq=nq, kb=kb, s_real=s_real, topk=topk,
                          index_bits=max(1, (s - 1).bit_length()), q_off=q_off),
        grid_spec=grid_spec,
        out_shape=jax.ShapeDtypeStruct((nb, A_HEADS * HEAD_DIM, nq * bq), BF16),
        compiler_params=_params("parallel", "arbitrary"),
        name=name,
    )(nvis, aq_t, aqi_t, aw_t, ka, kb_, k, vt3)


def _router_kernel(x_ref, whi_ref, wlo_ref, bias_ref, ids_ref, g0_ref, g1_ref):
    x = x_ref[...]
    xh = x.astype(BF16)
    xl = (x - xh.astype(F32)).astype(BF16)
    whi, wlo = whi_ref[...], wlo_ref[...]
    lg = (jnp.dot(xh, whi, preferred_element_type=F32) + jnp.dot(xh, wlo, preferred_element_type=F32)
          + jnp.dot(xl, whi, preferred_element_type=F32)) + bias_ref[...]
    lane = lax.broadcasted_iota(I32, lg.shape, 1)
    neg = -jnp.inf
    is_group = lane < N_GROUPS
    gl = jnp.where(is_group, lg, neg)
    gmax = jnp.max(gl, axis=-1, keepdims=True)
    grp = jnp.min(jnp.where(gl == gmax, lane, LANES), axis=-1, keepdims=True)
    g1 = 1.0 / jnp.sum(jnp.where(is_group, jnp.exp(gl - gmax), 0.0), axis=-1, keepdims=True)
    lo = N_GROUPS + grp * EXPERTS_PER_GROUP
    el = jnp.where(jnp.logical_and(lane >= lo, lane < lo + EXPERTS_PER_GROUP), lg, neg)
    v1 = jnp.max(el, axis=-1, keepdims=True)
    i1 = jnp.min(jnp.where(el == v1, lane, LANES), axis=-1, keepdims=True)
    el2 = jnp.where(lane == i1, neg, el)
    v2 = jnp.max(el2, axis=-1, keepdims=True)
    i2 = jnp.min(jnp.where(el2 == v2, lane, LANES), axis=-1, keepdims=True)
    e21 = jnp.exp(v2 - v1)
    den = 1.0 + e21
    ids_ref[...] = jnp.where(lane == 0, i1 - N_GROUPS, jnp.where(lane == 1, i2 - N_GROUPS, 0))
    g0_ref[...] = jnp.broadcast_to(g1 * (1.0 / den), g0_ref.shape)
    g1_ref[...] = jnp.broadcast_to(g1 * (e21 / den), g1_ref.shape)


def _router(x, whi, wlo, bias, *, bm):
    n, d = x.shape
    row = pl.BlockSpec((bm, LANES), lambda i: (i, 0))
    return pl.pallas_call(
        _router_kernel,
        grid=(n // bm,),
        in_specs=[pl.BlockSpec((bm, d), lambda i: (i, 0)),
                  pl.BlockSpec((d, LANES), lambda i: (0, 0)),
                  pl.BlockSpec((d, LANES), lambda i: (0, 0)),
                  pl.BlockSpec((1, LANES), lambda i: (0, 0))],
        out_specs=[row, row, row],
        out_shape=[jax.ShapeDtypeStruct((n, LANES), I32), jax.ShapeDtypeStruct((n, LANES), F32),
                   jax.ShapeDtypeStruct((n, LANES), F32)],
        compiler_params=_params("parallel"),
        name="router",
    )(x, whi, wlo, bias)


def _dispatch_kernel(ppos_ref, x_ref, buf_hbm, xs_hbm, sems, *, bm):
    del buf_hbm
    base = pl.program_id(0) * bm * 2

    def row_copy(a):
        return pltpu.make_async_copy(x_ref.at[pl.ds(a >> 1, 1), :],
                                     xs_hbm.at[pl.ds(ppos_ref[base + a], 1), :], sems.at[a & 1])

    def start(a, c):
        row_copy(a).start()
        return c
    lax.fori_loop(0, 2 * bm, start, 0)

    def wait(a, c):
        row_copy(a).wait()
        return c
    lax.fori_loop(0, 2 * bm, wait, 0)


def _dispatch(x, ppos, buf, *, bm):
    n, d = x.shape
    assert n % bm == 0
    grid_spec = pltpu.PrefetchScalarGridSpec(
        num_scalar_prefetch=1,
        grid=(n // bm,),
        in_specs=[pl.BlockSpec((bm, d), lambda i, pp: (i, 0)), pl.BlockSpec(memory_space=pl.ANY)],
        out_specs=pl.BlockSpec(memory_space=pl.ANY),
        scratch_shapes=[pltpu.SemaphoreType.DMA((2,))],
    )
    return pl.pallas_call(
        functools.partial(_dispatch_kernel, bm=bm),
        grid_spec=grid_spec,
        out_shape=jax.ShapeDtypeStruct(buf.shape, buf.dtype),
        input_output_aliases={2: 0},
        compiler_params=_params("arbitrary"),
        name="dispatch",
    )(ppos, x, buf)


def _expert_kernel(te_ref, nv_ref, x_ref, wg_ref, wu_ref, wd_ref, y_ref):
    t = pl.program_id(0)

    @pl.when(nv_ref[t] > 0)
    def _():
        x = x_ref[...].astype(BF16)
        hg = jnp.dot(x, wg_ref[...], preferred_element_type=F32)
        hu = jnp.dot(x, wu_ref[...], preferred_element_type=F32)
        act = (hg * (1.0 / (1.0 + jnp.exp(-hg)))) * hu
        y_ref[...] = jnp.dot(act.astype(BF16), wd_ref[...], preferred_element_type=F32)

    @pl.when(nv_ref[t] == 0)
    def _():
        y_ref[...] = jnp.zeros(y_ref.shape, F32)


def _experts(xs, wg, wu, wd, tile_e, tile_nvalid, *, bm):
    p, d = xs.shape
    f = wg.shape[-1]
    grid_spec = pltpu.PrefetchScalarGridSpec(
        num_scalar_prefetch=2,
        grid=(p // bm,),
        in_specs=[pl.BlockSpec((bm, d), lambda t, te, nv: (t, 0)),
                  pl.BlockSpec((None, d, f), lambda t, te, nv: (te[t], 0, 0)),
                  pl.BlockSpec((None, d, f), lambda t, te, nv: (te[t], 0, 0)),
                  pl.BlockSpec((None, f, d), lambda t, te, nv: (te[t], 0, 0))],
        out_specs=pl.BlockSpec((bm, d), lambda t, te, nv: (t, 0)),
    )
    return pl.pallas_call(
        _expert_kernel,
        grid_spec=grid_spec,
        out_shape=jax.ShapeDtypeStruct((p, d), F32),
        compiler_params=_params("arbitrary"),
        name="experts",
    )(tile_e, tile_nvalid, xs, wg, wu, wd)


def _expert_plan(eid, *, bm):
    na = eid.shape[0] * 2
    flat_e = eid.reshape(na)
    onehot = (flat_e[:, None] == jnp.arange(N_EXPERTS, dtype=I32)[None, :]).astype(I32)
    csum = jnp.cumsum(onehot, axis=0)
    counts = csum[-1]
    padded = ((counts + bm - 1) // bm) * bm
    pend = jnp.cumsum(padded)
    pstart = pend - padded
    ppos = jnp.sum(onehot * (pstart[None, :] + csum - 1), axis=1).astype(I32)
    total = na + N_EXPERTS * bm
    tile_start = jnp.arange(total // bm, dtype=I32) * bm
    tile_e = jnp.minimum(jnp.sum((tile_start[:, None] >= pend[None, :]).astype(I32), axis=1), N_EXPERTS - 1)
    tile_nvalid = jnp.clip(pstart[tile_e] + counts[tile_e] - tile_start, 0, bm).astype(I32)
    return ppos, tile_e.astype(I32), tile_nvalid, total


def _combine_ln_kernel(ppos_ref, x_ref, g0_ref, g1_ref, y_hbm, g_ref, b_ref, of_ref, ob_ref, ybuf, sem,
                       *, bm, alpha):
    base = pl.program_id(0) * bm * 2
    d = x_ref.shape[-1]

    def row_copy(a):
        return pltpu.make_async_copy(y_hbm.at[pl.ds(ppos_ref[base + a], 1), :],
                                     ybuf.at[a & 1, pl.ds(a >> 1, 1), :], sem.at[a & 1])

    def start(a, c):
        row_copy(a).start()
        return c
    lax.fori_loop(0, 2 * bm, start, 0)

    def wait(a, c):
        row_copy(a).wait()
        return c
    lax.fori_loop(0, 2 * bm, wait, 0)

    reps = d // LANES
    y = (alpha * x_ref[...] + jnp.tile(g0_ref[...], (1, reps)) * ybuf[0]
         + jnp.tile(g1_ref[...], (1, reps)) * ybuf[1])
    out = _layer_norm(y, g_ref[...], b_ref[...])
    of_ref[...] = out
    ob_ref[...] = out.astype(BF16)


def _combine_ln(x, ys, ppos, g0, g1, g, b, *, bm, alpha):
    n, d = x.shape
    row = lambda w: pl.BlockSpec((bm, w), lambda i, pp: (i, 0))
    grid_spec = pltpu.PrefetchScalarGridSpec(
        num_scalar_prefetch=1,
        grid=(n // bm,),
        in_specs=[row(d), row(LANES), row(LANES), pl.BlockSpec(memory_space=pl.ANY),
                  pl.BlockSpec((1, d), lambda i, pp: (0, 0)), pl.BlockSpec((1, d), lambda i, pp: (0, 0))],
        out_specs=[row(d), row(d)],
        scratch_shapes=[pltpu.VMEM((2, bm, d), F32), pltpu.SemaphoreType.DMA((2,))],
    )
    return pl.pallas_call(
        functools.partial(_combine_ln_kernel, bm=bm, alpha=alpha),
        grid_spec=grid_spec,
        out_shape=[jax.ShapeDtypeStruct((n, d), F32), jax.ShapeDtypeStruct((n, d), BF16)],
        compiler_params=_params("arbitrary"),
        name="combine_ln",
    )(ppos, x, g0, g1, ys, g, b)


def _rope_tables(pos):
    posf = pos.astype(F32)[:, None]
    inv128 = ROPE_THETA ** (-jnp.arange(HEAD_DIM // 2, dtype=F32) * 2.0 / HEAD_DIM)
    ang = posf * inv128[None, :]
    cos, sin = jnp.cos(ang), jnp.sin(ang)
    a128 = jnp.concatenate([cos, cos], axis=-1)
    b128 = jnp.concatenate([-sin, sin], axis=-1)
    inv64 = ROPE_THETA ** (-jnp.arange(IDX_DIM // 2, dtype=F32) * 2.0 / IDX_DIM)
    ang = posf * inv64[None, :]
    cos, sin = jnp.cos(ang), jnp.sin(ang)
    zero = jnp.zeros_like(sin)
    a64 = jnp.concatenate([cos, cos, cos, cos], axis=-1)
    b64 = jnp.concatenate([-sin, zero, -sin, zero], axis=-1)
    c64 = jnp.concatenate([zero, sin, zero, sin], axis=-1)
    return (a128, b128), (a64, b64, c64)


def _z_weight(w_in):
    o = _IN_OFF
    d = w_in.shape[0]
    cols = [w_in[:, o[0]:o[4]],
            w_in[:, o[6]:o[9]],
            w_in[:, o[10]:o[12]],
            w_in[:, o[4]:o[5]],
            w_in[:, o[12]:o[13]],
            w_in[:, o[5]:o[6]],
            w_in[:, o[9]:o[10]],
            jnp.zeros((d, LANES - IDX_HEADS - B_HEADS), w_in.dtype)]
    return jnp.concatenate(cols, axis=1).astype(BF16)


def _uq_weight(w):
    r = w.shape[0]
    w3 = w.reshape(r, C_HEADS, C_NOPE + C_ROPE)
    nope = w3[:, :, :C_NOPE].reshape(r, C_HEADS * C_NOPE)
    rope = jnp.pad(w3[:, :, C_NOPE:], ((0, 0), (0, 0), (0, LANES - C_ROPE))).reshape(r, C_HEADS * LANES)
    return jnp.concatenate([nope, rope], axis=1).astype(BF16)


def _pad_lanes(x):
    return jnp.pad(x, [(0, 0)] * (x.ndim - 1) + [(0, LANES - x.shape[-1])])


def _pad_rows(x, rows):
    return jnp.pad(x, [(0, 0), (0, rows - x.shape[1])] + [(0, 0)] * (x.ndim - 2))


def _heads_t(x, heads):
    nb, t, _ = x.shape
    return jnp.transpose(x.reshape(nb, t, heads, LANES), (0, 2, 3, 1))


def _heads_back(o_t):
    nb, h, w, t = o_t.shape
    return jnp.transpose(o_t, (0, 3, 1, 2)).reshape(nb * t, h * w)


def _v_blocks(v, heads, bk):
    nb, s, _ = v.shape
    return jnp.transpose(v.reshape(nb, s // bk, bk, heads, LANES), (0, 3, 1, 4, 2))


def _bf16_head(c):
    bits = lax.bitcast_convert_type(c, I32) & jnp.int32(-65536)
    return lax.bitcast_convert_type(bits, F32)


def _split3(c):
    hi = _bf16_head(c)
    mid = _bf16_head(c - hi)
    lo = c - hi - mid
    return hi.astype(BF16), mid.astype(BF16), lo.astype(BF16)


def _fox_bias_parts(csum_q, csum_k):
    nb, t, h = csum_q.shape
    s = csum_k.shape[1]
    one_q = jnp.ones((nb, t, h), BF16)
    qrows = jnp.stack(list(_split3(csum_q * LOG2E)) + [one_q] * 3, axis=-1)
    qb = jnp.pad(qrows, ((0, 0), (0, 0), (0, 0), (0, LANES - 6)))
    one_k = jnp.ones((nb, s, h), BF16)
    krows = jnp.stack([one_k] * 3 + [-p for p in _split3(csum_k * LOG2E)], axis=-1)
    kbias = jnp.pad(krows, ((0, 0), (0, 0), (0, 0), (0, LANES - 6))).reshape(nb, s, h * LANES)
    return jnp.transpose(qb, (0, 2, 3, 1)), kbias


def _pick(n, *cands):
    for c in cands:
        if n % c == 0:
            return c
    return n


def kernel(x_prompt, x_sample, cache_a_k, cache_a_v, cache_a_kidx, cache_b_k, cache_b_v, cache_b_logf,
           cache_c_latent, cache_c_krope, cache_mem_k, cache_mem_v, mem_prompt,
           w_in, b_f, c_q_norm, w_c_uq, c_kv_norm, w_c_uk, w_c_uv, w_o, ln1_g, ln1_b,
           w_mq, w_mk, w_mv, w_mo, ln2_g, ln2_b, w_rg, b_rg, w_re, b_re,
           w_gate, w_up, w_down, ln3_g, ln3_b):
    depth = w_in.shape[0]
    bp, tp, d = x_prompt.shape
    bs, ts, _ = x_sample.shape
    past = cache_a_k.shape[2]
    n_mem = mem_prompt.shape[1]
    assert bp == 1
    n_p, n_s = bp * tp, bs * ts
    n = n_p + n_s
    alpha = (2 * depth) ** 0.25
    s_s = past + ts
    s_pad = -(-s_s // COUNT_ROWS) * COUNT_ROWS

    bm_tok = _pick(n, 1024, 512, 256, 128, 64)
    bm_post = _pick(n, 256, 128, 64)
    bm_ln = _pick(n, 256, 128, 64)
    bq_f = _pick(tp, 1024, 512, 256, 128)
    bq_a = _pick(tp, 512, 256, 128)
    kb_a = _pick(tp, 1024, 512, 256)
    bm_moe = 256

    x = jnp.concatenate([x_prompt.reshape(n_p, d), x_sample.reshape(n_s, d)], axis=0)
    xb = x.astype(BF16)
    pos_p = jnp.arange(tp, dtype=I32)
    pos_s = past + jnp.arange(ts, dtype=I32)
    tabs128, tabs64 = _rope_tables(jnp.concatenate([pos_p, jnp.tile(pos_s, bs)]))

    nq_f = tp // bq_f
    nfull_p = jnp.arange(nq_f, dtype=I32)
    nkv_p = nfull_p + 1
    one = jnp.ones((1,), I32)
    zero = jnp.zeros((1,), I32)
    nvis_p = (((jnp.arange(tp // bq_a, dtype=I32) + 1) * bq_a + kb_a - 1) // kb_a).astype(I32)
    nvis_s = jnp.ones((bs,), I32)

    xs_buf = jnp.zeros((2 * n + N_EXPERTS * bm_moe, d), F32)
    state_p = [[] for _ in range(8)]
    state_s = [[] for _ in range(8)]
    mem_k_out, mem_v_out = [], []

    for l in range(depth):
        z = _matmul(xb, _z_weight(w_in[l]), bm=bm_tok, bn=512, out_dtype=F32, name="in_proj")
        bf_row = jnp.zeros((1, LANES), F32).at[0, SM2_BF:SM2_BF + B_HEADS].set(b_f[l].astype(F32))
        aq, ak, aqi, bq_, cqn, clat, sm1, sm2 = _post(
            z, tabs128, tabs64, bf_row, c_q_norm[l].reshape(1, -1).astype(F32),
            c_kv_norm[l].reshape(1, -1).astype(F32), bm=bm_post)
        cq = _matmul_rope(cqn, _uq_weight(w_c_uq[l]), tabs64, bm=bm_tok, bn=512,
                          first_rope_col=C_HEADS * C_NOPE, out_scale=(C_NOPE + C_ROPE) ** -0.5 * LOG2E,
                          name="c_q_up")
        av = z[:, Z_AV:Z_AV + HEAD_DIM]
        aki = sm1[:, :IDX_DIM]
        ckr = sm1[:, IDX_DIM:]
        aw = sm2[:, :IDX_HEADS]
        blogf = sm2[:, SM2_BF:SM2_BF + B_HEADS]
        bk_ = z[:, Z_BK:Z_BK + B_HEADS * HEAD_DIM]
        bv_ = z[:, Z_BV:Z_BV + B_HEADS * HEAD_DIM]
        rows = (ak, av, aki, bk_.reshape(n, B_HEADS, HEAD_DIM), bv_.reshape(n, B_HEADS, HEAD_DIM),
                blogf, clat, ckr)
        for i, r in enumerate(rows):
            state_p[i].append(r[:n_p].reshape((bp, tp) + r.shape[1:]))
            state_s[i].append(r[n_p:].reshape((bs, ts) + r.shape[1:]))
        grp_p = lambda a: a[:n_p].reshape(bp, tp, -1)
        grp_s = lambda a: a[n_p:].reshape(bs, ts, -1)
        cat_s = lambda c, r: jnp.concatenate([c.reshape(bs, past, -1).astype(F32), grp_s(r)], axis=1)

        aw_scaled = aw * (IDX_DIM ** -0.5)

        def sparse(aq_g, aqi_g, aw_g, ki, k, v, nvis, *, bq, kb, s_real, q_off, name):
            ka = _pad_lanes(ki).astype(BF16)
            kb_ = jnp.concatenate([jnp.zeros_like(ki), ki], axis=-1).astype(BF16)
            vt3 = _v_blocks(v.astype(BF16), 1, kb)[:, 0]
            out_t = _sparse_attention(
                jnp.swapaxes(aq_g, 1, 2), jnp.swapaxes(aqi_g, 1, 2), jnp.swapaxes(aw_g, 1, 2),
                ka, kb_, k.astype(BF16), vt3, nvis, bq=bq, kb=kb, s_real=s_real,
                topk=min(TOPK_MAX, s_real // 4), q_off=q_off, name=name)
            return jnp.swapaxes(out_t, 1, 2).reshape(-1, A_HEADS * HEAD_DIM)

        oa_p = sparse(grp_p(aq), grp_p(aqi), grp_p(aw_scaled), grp_p(aki), grp_p(ak), grp_p(av), nvis_p,
                      bq=bq_a, kb=kb_a, s_real=tp, q_off=0, name="sparse_prompt")
        oa_s = sparse(grp_s(aq), grp_s(aqi), grp_s(aw_scaled),
                      _pad_rows(cat_s(cache_a_kidx[l], aki), s_pad), _pad_rows(cat_s(cache_a_k[l], ak), s_pad),
                      _pad_rows(cat_s(cache_a_v[l], av), s_pad), nvis_s,
                      bq=ts, kb=s_pad, s_real=s_s, q_off=past, name="sparse_sample")

        csum_p = jnp.cumsum(grp_p(blogf), axis=1)
        qb_p, kbias_p = _fox_bias_parts(csum_p, csum_p)
        ob_p = _tflash([_heads_t(grp_p(bq_), B_HEADS), qb_p], (0, 0),
                       [grp_p(bk_).astype(BF16), kbias_p], (False, False),
                       _v_blocks(grp_p(bv_).astype(BF16), B_HEADS, bq_f), nfull_p, nkv_p,
                       hb=2, bq=bq_f, bk=bq_f, mode="causal", q_off=0, name="fox_prompt")
        csum_s = jnp.cumsum(cat_s(cache_b_logf[l], blogf), axis=1)
        qb_s, kbias_s = _fox_bias_parts(csum_s[:, past:], csum_s)
        ob_s = _tflash([_heads_t(grp_s(bq_), B_HEADS), qb_s], (0, 0),
                       [cat_s(cache_b_k[l], bk_).astype(BF16), kbias_s], (False, False),
                       _v_blocks(cat_s(cache_b_v[l], bv_).astype(BF16), B_HEADS, s_s), zero, one,
                       hb=2, bq=ts, bk=s_s, mode="causal", q_off=past, name="fox_sample")

        w_ukv = jnp.concatenate([w_c_uk[l], w_c_uv[l]], axis=1).astype(BF16)
        kw = C_HEADS * C_NOPE
        kv_p = _matmul(clat[:n_p], w_ukv, bm=_pick(n_p, 1024, 512, 256), bn=w_ukv.shape[1], out_dtype=BF16,
                       name="c_kv_up_prompt").reshape(bp, tp, -1)
        cq_t_p = _heads_t(grp_p(cq), 2 * C_HEADS)
        oc_p = _tflash([cq_t_p, cq_t_p], (0, C_HEADS),
                       [kv_p[..., :kw], _pad_lanes(grp_p(ckr)).astype(BF16)], (False, True),
                       _v_blocks(kv_p[..., kw:], C_HEADS, bq_f), nfull_p, nkv_p,
                       hb=2, bq=bq_f, bk=bq_f, mode="chunk", q_off=0, name="mla_prompt")
        lat_s = cat_s(cache_c_latent[l], clat)
        kv_s = _matmul(lat_s.reshape(bs * s_s, -1), w_ukv, bm=_pick(bs * s_s, 1024, 512, 256, 128, 64),
                       bn=w_ukv.shape[1], out_dtype=BF16, name="c_kv_up_sample").reshape(bs, s_s, -1)
        cq_t_s = _heads_t(grp_s(cq), 2 * C_HEADS)
        oc_s = _tflash([cq_t_s, cq_t_s], (0, C_HEADS),
                       [kv_s[..., :kw], _pad_lanes(cat_s(cache_c_krope[l], ckr)).astype(BF16)], (False, True),
                       _v_blocks(kv_s[..., kw:], C_HEADS, s_s), zero, one,
                       hb=2, bq=ts, bk=s_s, mode="chunk", q_off=past, name="mla_sample")

        mixed = jnp.concatenate([
            jnp.concatenate([oa_p, oa_s], axis=0),
            jnp.concatenate([_heads_back(ob_p), _heads_back(ob_s)], axis=0),
            jnp.concatenate([_heads_back(oc_p), _heads_back(oc_s)], axis=0)], axis=1)
        x1, x1b = _mm_res_ln(mixed, w_o[l].astype(BF16), x, ln1_g[l].reshape(1, d), ln1_b[l].reshape(1, d),
                             bm=bm_ln, bk=_pick(mixed.shape[1], 1024, 512), alpha=alpha, name="out_proj_ln1")

        mw = MEM_HEADS * MEM_DIM
        w_mkv = jnp.concatenate([w_mk[l], w_mv[l]], axis=1).astype(BF16)
        mkv = _matmul(mem_prompt.reshape(bp * n_mem, d), w_mkv, bm=_pick(bp * n_mem, 256, 128, 64), bn=512,
                      out_dtype=F32, name="mem_kv")
        mem_k_out.append(mkv[:, :mw].reshape(bp, n_mem, MEM_HEADS, MEM_DIM))
        mem_v_out.append(mkv[:, mw:].reshape(bp, n_mem, MEM_HEADS, MEM_DIM))
        qm = _matmul(x1b, w_mq[l].astype(BF16), bm=bm_tok, bn=mw, out_dtype=BF16, name="mem_q",
                     out_scale=MEM_DIM ** -0.5 * LOG2E)
        mkv3 = mkv.reshape(bp, n_mem, 2 * mw).astype(BF16)
        om_p = _tflash([_heads_t(grp_p(qm), MEM_HEADS)], (0,), [mkv3[..., :mw]], (False,),
                       _v_blocks(mkv3[..., mw:], MEM_HEADS, n_mem), jnp.ones((nq_f,), I32), jnp.ones((nq_f,), I32),
                       hb=2, bq=bq_f, bk=n_mem, mode="none", q_off=0, name="mem_prompt")
        om_s = _tflash([_heads_t(grp_s(qm), MEM_HEADS)], (0,),
                       [cache_mem_k[l].reshape(bs, n_mem, mw).astype(BF16)], (False,),
                       _v_blocks(cache_mem_v[l].reshape(bs, n_mem, mw).astype(BF16), MEM_HEADS, n_mem), one, one,
                       hb=2, bq=ts, bk=n_mem, mode="none", q_off=0, name="mem_sample")
        om = jnp.concatenate([_heads_back(om_p), _heads_back(om_s)], axis=0)
        x2, _ = _mm_res_ln(om, w_mo[l].astype(BF16), x1, ln2_g[l].reshape(1, d), ln2_b[l].reshape(1, d),
                           bm=bm_ln, bk=mw, alpha=alpha, name="mem_out_ln2")

        w_r = jnp.concatenate([w_rg[l], jnp.transpose(w_re[l], (1, 0, 2)).reshape(d, N_EXPERTS)], axis=1).astype(F32)
        w_r = _pad_lanes(w_r)
        w_r_head = _bf16_head(w_r)
        w_r_hi = w_r_head.astype(BF16)
        w_r_lo = (w_r - w_r_head).astype(BF16)
        b_r = _pad_lanes(jnp.concatenate([b_rg[l], b_re[l].reshape(-1)]).astype(F32).reshape(1, -1))
        ids, g0, g1 = _router(x2, w_r_hi, w_r_lo, b_r, bm=_pick(n, 512, 256, 128, 64))
        ppos, tile_e, tile_nvalid, total = _expert_plan(ids[:, :2], bm=bm_moe)
        assert total == xs_buf.shape[0]
        xs_buf = _dispatch(x2, ppos, xs_buf, bm=_pick(n, 256, 128, 64))
        ys = _experts(xs_buf, w_gate[l].astype(BF16), w_up[l].astype(BF16), w_down[l].astype(BF16),
                      tile_e, tile_nvalid, bm=bm_moe)
        x, xb = _combine_ln(x2, ys, ppos, g0, g1, ln3_g[l].reshape(1, d), ln3_b[l].reshape(1, d),
                            bm=_pick(n, 256, 128, 64), alpha=alpha)

    outs = [x[:n_p].reshape(bp, tp, d), x[n_p:].reshape(bs, ts, d)]
    outs += [jnp.stack(c) for c in state_p]
    outs += [jnp.stack(mem_k_out), jnp.stack(mem_v_out)]
    outs += [jnp.stack(c) for c in state_s]
    return tuple(outs)
```

```python
import functools
import math

import jax
import jax.numpy as jnp
import numpy as np
from jax import lax
from jax.experimental import pallas as pl
from jax.experimental.pallas import tpu as pltpu

F32 = jnp.float32
BF16 = jnp.bfloat16
I32 = jnp.int32

LANES = 128
SUBLANES = 8
VMEM_LIMIT = 56 * 1024 * 1024

CHUNK_SHIFT = 6
HEAD_DIM = 128
ROPE_THETA = 10000.0
A_HEADS = 8
IDX_HEADS = 32
IDX_DIM = 64
TOPK_MAX = 256
B_HEADS = 8
C_HEADS = 16
C_Q_RANK = 768
C_KV_RANK = 256
C_NOPE = 128
C_ROPE = 64
C_V = 128
MEM_HEADS = 4
MEM_DIM = 128
N_GROUPS = 4
EXPERTS_PER_GROUP = 4
N_EXPERTS = N_GROUPS * EXPERTS_PER_GROUP
LN_EPS = 1e-5
RMS_EPS = 1e-6
LOG2E = math.log2(math.e)

_IN_SIZES = (A_HEADS * HEAD_DIM, HEAD_DIM, HEAD_DIM, IDX_HEADS * IDX_DIM, IDX_DIM, IDX_HEADS,
             B_HEADS * HEAD_DIM, B_HEADS * HEAD_DIM, B_HEADS * HEAD_DIM, B_HEADS,
             C_Q_RANK, C_KV_RANK, C_ROPE)
_IN_OFF = np.concatenate([[0], np.cumsum(_IN_SIZES)]).astype(int)

Z_AQ = 0
Z_AK = Z_AQ + A_HEADS * HEAD_DIM
Z_AV = Z_AK + HEAD_DIM
Z_AQI = Z_AV + HEAD_DIM
Z_BQ = Z_AQI + IDX_HEADS * IDX_DIM
Z_BK = Z_BQ + B_HEADS * HEAD_DIM
Z_BV = Z_BK + B_HEADS * HEAD_DIM
Z_CQ = Z_BV + B_HEADS * HEAD_DIM
Z_CKV = Z_CQ + C_Q_RANK
Z_SM1 = Z_CKV + C_KV_RANK
Z_SM2 = Z_SM1 + LANES
Z_WIDTH = Z_SM2 + LANES
SM2_BF = IDX_HEADS

M_INIT = -1e30
INT_MIN = -2 ** 31
INT_MAX = 2 ** 31 - 1
SCORE_ROWS = 128
SCORE_COLS = 256
COUNT_ROWS = 256


def _params(*sem):
    return pltpu.CompilerParams(dimension_semantics=sem, vmem_limit_bytes=VMEM_LIMIT)


def _mm_kernel(x_ref, w_ref, o_ref, *, out_scale):
    x = x_ref[...].astype(BF16)
    acc = jnp.dot(x, w_ref[...], preferred_element_type=F32)
    if out_scale != 1.0:
        acc = acc * out_scale
    o_ref[...] = acc.astype(o_ref.dtype)


def _matmul(x, w, *, bm, bn, out_dtype, name, out_scale=1.0):
    m, k = x.shape
    n = w.shape[1]
    assert m % bm == 0 and n % bn == 0, (x.shape, w.shape, bm, bn)
    return pl.pallas_call(
        functools.partial(_mm_kernel, out_scale=out_scale),
        grid=(m // bm, n // bn),
        in_specs=[pl.BlockSpec((bm, k), lambda i, j: (i, 0)),
                  pl.BlockSpec((k, bn), lambda i, j: (0, j))],
        out_specs=pl.BlockSpec((bm, bn), lambda i, j: (i, j)),
        out_shape=jax.ShapeDtypeStruct((m, n), out_dtype),
        compiler_params=_params("parallel", "parallel"),
        name=name,
    )(x, w)


def _rope64(x, a, b, c):
    return x * a + pltpu.roll(x, 96, 1) * b + pltpu.roll(x, 32, 1) * c


def _rope128(x, a, b):
    return x * a + pltpu.roll(x, 64, 1) * b


def _mm_rope_kernel(x_ref, w_ref, a_ref, b_ref, c_ref, o_ref, *, first_rope_block, out_scale):
    j = pl.program_id(1)
    acc = jnp.dot(x_ref[...], w_ref[...], preferred_element_type=F32) * out_scale

    @pl.when(j < first_rope_block)
    def _():
        o_ref[...] = acc.astype(o_ref.dtype)

    @pl.when(j >= first_rope_block)
    def _():
        a, b, c = a_ref[...], b_ref[...], c_ref[...]
        for g in range(acc.shape[1] // LANES):
            sl = slice(g * LANES, (g + 1) * LANES)
            o_ref[:, sl] = _rope64(acc[:, sl], a, b, c).astype(o_ref.dtype)


def _matmul_rope(x, w, tabs, *, bm, bn, first_rope_col, out_scale, name):
    m, k = x.shape
    n = w.shape[1]
    assert m % bm == 0 and n % bn == 0 and first_rope_col % bn == 0
    tab_spec = pl.BlockSpec((bm, LANES), lambda i, j: (i, 0))
    return pl.pallas_call(
        functools.partial(_mm_rope_kernel, first_rope_block=first_rope_col // bn, out_scale=out_scale),
        grid=(m // bm, n // bn),
        in_specs=[pl.BlockSpec((bm, k), lambda i, j: (i, 0)),
                  pl.BlockSpec((k, bn), lambda i, j: (0, j)),
                  tab_spec, tab_spec, tab_spec],
        out_specs=pl.BlockSpec((bm, bn), lambda i, j: (i, j)),
        out_shape=jax.ShapeDtypeStruct((m, n), BF16),
        compiler_params=_params("parallel", "parallel"),
        name=name,
    )(x, w, *tabs)


def _post_kernel(z_ref, a128_ref, b128_ref, a64_ref, b64_ref, c64_ref, bf_ref, gq_ref, gkv_ref,
                 aq_ref, ak_ref, aqi_ref, bq_ref, cqn_ref, clat_ref, sm1_ref, sm2_ref,
                 akb_ref, avb_ref, bkb_ref, bvb_ref):
    a128, b128 = a128_ref[...], b128_ref[...]
    a64, b64, c64 = a64_ref[...], b64_ref[...], c64_ref[...]
    qscale = HEAD_DIM ** -0.5 * LOG2E
    for g in range(A_HEADS):
        src = slice(Z_AQ + g * LANES, Z_AQ + (g + 1) * LANES)
        aq_ref[:, g * LANES:(g + 1) * LANES] = (_rope128(z_ref[:, src], a128, b128) * qscale).astype(BF16)
    ak = _rope128(z_ref[:, Z_AK:Z_AK + LANES], a128, b128)
    ak_ref[...] = ak
    akb_ref[...] = ak.astype(BF16)
    avb_ref[...] = z_ref[:, Z_AV:Z_AV + LANES].astype(BF16)
    bkb_ref[...] = z_ref[:, Z_BK:Z_BK + B_HEADS * HEAD_DIM].astype(BF16)
    bvb_ref[...] = z_ref[:, Z_BV:Z_BV + B_HEADS * HEAD_DIM].astype(BF16)
    for g in range(IDX_HEADS * IDX_DIM // LANES):
        src = slice(Z_AQI + g * LANES, Z_AQI + (g + 1) * LANES)
        aqi_ref[:, g * LANES:(g + 1) * LANES] = _rope64(z_ref[:, src], a64, b64, c64).astype(BF16)
    bq_ref[...] = (z_ref[:, Z_BQ:Z_BQ + B_HEADS * HEAD_DIM] * qscale).astype(BF16)
    cq = z_ref[:, Z_CQ:Z_CQ + C_Q_RANK]
    cq = cq * lax.rsqrt(jnp.mean(cq * cq, axis=-1, keepdims=True) + RMS_EPS) * gq_ref[...]
    cqn_ref[...] = cq.astype(BF16)
    ckv = z_ref[:, Z_CKV:Z_CKV + C_KV_RANK]
    clat_ref[...] = ckv * lax.rsqrt(jnp.mean(ckv * ckv, axis=-1, keepdims=True) + RMS_EPS) * gkv_ref[...]
    sm1_ref[...] = _rope64(z_ref[:, Z_SM1:Z_SM1 + LANES], a64, b64, c64)
    s2 = z_ref[:, Z_SM2:Z_SM2 + LANES]
    lane = lax.broadcasted_iota(I32, s2.shape, 1)
    f = s2 + bf_ref[...]
    logf = jnp.minimum(f, 0.0) - jnp.log1p(jnp.exp(-jnp.abs(f)))
    sm2_ref[...] = jnp.where(lane < SM2_BF, s2 * (IDX_HEADS ** -0.5),
                             jnp.where(lane < SM2_BF + B_HEADS, logf, 0.0))


def _post(z, tabs128, tabs64, bf_row, gq, gkv, *, bm):
    n = z.shape[0]
    row = lambda w: pl.BlockSpec((bm, w), lambda i: (i, 0))
    const = lambda w: pl.BlockSpec((1, w), lambda i: (0, 0))
    widths = (A_HEADS * HEAD_DIM, LANES, IDX_HEADS * IDX_DIM, B_HEADS * HEAD_DIM, C_Q_RANK, C_KV_RANK, LANES, LANES,
              LANES, LANES, B_HEADS * HEAD_DIM, B_HEADS * HEAD_DIM)
    dtypes = (BF16, F32, BF16, BF16, BF16, F32, F32, F32, BF16, BF16, BF16, BF16)
    return pl.pallas_call(
        _post_kernel,
        grid=(n // bm,),
        in_specs=[row(Z_WIDTH)] + [row(LANES)] * 5 + [const(LANES), const(C_Q_RANK), const(C_KV_RANK)],
        out_specs=[row(w) for w in widths],
        out_shape=[jax.ShapeDtypeStruct((n, w), d) for w, d in zip(widths, dtypes)],
        compiler_params=_params("parallel"),
        name="post_projection",
    )(z, *tabs128, *tabs64, bf_row, gq, gkv)


def _layer_norm(y, g, b):
    mu = jnp.mean(y, axis=-1, keepdims=True)
    yc = y - mu
    var = jnp.mean(yc * yc, axis=-1, keepdims=True)
    return yc * lax.rsqrt(var + LN_EPS) * g + b


def _mm_res_ln_kernel(x_ref, w_ref, r_ref, g_ref, b_ref, of_ref, ob_ref, acc_ref, *, nk, alpha):
    k = pl.program_id(1)

    @pl.when(k == 0)
    def _():
        acc_ref[...] = jnp.zeros_like(acc_ref)

    acc_ref[...] += jnp.dot(x_ref[...], w_ref[...], preferred_element_type=F32)

    @pl.when(k == nk - 1)
    def _():
        out = _layer_norm(alpha * r_ref[...] + acc_ref[...], g_ref[...], b_ref[...])
        of_ref[...] = out
        ob_ref[...] = out.astype(BF16)


def _mm_res_ln(x, w, res, g, b, *, bm, bk, alpha, name):
    m, k = x.shape
    d = w.shape[1]
    nk = k // bk
    return pl.pallas_call(
        functools.partial(_mm_res_ln_kernel, nk=nk, alpha=alpha),
        grid=(m // bm, nk),
        in_specs=[pl.BlockSpec((bm, bk), lambda i, kk: (i, kk)),
                  pl.BlockSpec((bk, d), lambda i, kk: (kk, 0)),
                  pl.BlockSpec((bm, d), lambda i, kk: (i, 0)),
                  pl.BlockSpec((1, d), lambda i, kk: (0, 0)),
                  pl.BlockSpec((1, d), lambda i, kk: (0, 0))],
        out_specs=[pl.BlockSpec((bm, d), lambda i, kk: (i, 0)),
                   pl.BlockSpec((bm, d), lambda i, kk: (i, 0))],
        out_shape=[jax.ShapeDtypeStruct((m, d), F32), jax.ShapeDtypeStruct((m, d), BF16)],
        scratch_shapes=[pltpu.VMEM((bm, d), F32)],
        compiler_params=_params("parallel", "arbitrary"),
        name=name,
    )(x, w, res, g, b)


def _mm_ln_cols_kernel(x_ref, w_ref, r_ref, g_ref, b_ref, of_ref, ob_ref, *, nj, bn, alpha):
    j = pl.program_id(1)
    y = jnp.dot(x_ref[...], w_ref[...], preferred_element_type=F32)
    for jj in range(nj):
        @pl.when(j == jj)
        def _(jj=jj):
            cs = slice(jj * bn, (jj + 1) * bn)
            of_ref[:, cs] = alpha * r_ref[:, cs] + y

    @pl.when(j == nj - 1)
    def _():
        out = _layer_norm(of_ref[...], g_ref[...], b_ref[...])
        of_ref[...] = out
        ob_ref[...] = out.astype(BF16)


def _mm_ln_cols(x, w, layer, res, g, b, *, bm, bn, alpha, name):
    m, k = x.shape
    d = w.shape[-1]
    nj = d // bn
    return pl.pallas_call(
        functools.partial(_mm_ln_cols_kernel, nj=nj, bn=bn, alpha=alpha),
        grid=(m // bm, nj),
        in_specs=[pl.BlockSpec((bm, k), lambda i, j: (i, 0)),
                  pl.BlockSpec((None, k, bn), lambda i, j: (layer, 0, j)),
                  pl.BlockSpec((bm, d), lambda i, j: (i, 0)),
                  pl.BlockSpec((1, d), lambda i, j: (0, 0)),
                  pl.BlockSpec((1, d), lambda i, j: (0, 0))],
        out_specs=[pl.BlockSpec((bm, d), lambda i, j: (i, 0)),
                   pl.BlockSpec((bm, d), lambda i, j: (i, 0))],
        out_shape=[jax.ShapeDtypeStruct((m, d), F32), jax.ShapeDtypeStruct((m, d), BF16)],
        compiler_params=_params("parallel", "arbitrary"),
        name=name,
    )(x, w, res, g, b)


def _tflash_kernel(nfull_ref, nkv_ref, *refs, n_parts, k_shared, hb, bk, mode, q_off):
    refs = list(refs)
    q_refs = [refs.pop(0) for _ in range(n_parts)]
    k_refs = [refs.pop(0) for _ in range(n_parts)]
    vt_ref = refs.pop(0)
    o_ref, acc_ref = refs
    qi = pl.program_id(2)
    bq = o_ref.shape[-1]
    qs = [jnp.concatenate([qr[h] for qr in q_refs], axis=0) if n_parts > 1 else q_refs[0][h]
          for h in range(hb)]
    acc_ref[...] = jnp.zeros(acc_ref.shape, F32)

    def block(j, carry, masked):
        ms, ls = carry
        off = pl.multiple_of(j * bk, bk)
        if masked:
            kpos = off + lax.broadcasted_iota(I32, (bk, bq), 0)
            qpos = q_off + qi * bq + lax.broadcasted_iota(I32, (bk, bq), 1)
            if mode == "chunk":
                mask = (kpos >> CHUNK_SHIFT) <= (qpos >> CHUNK_SHIFT)
            else:
                mask = kpos <= qpos
        new_m, new_l = [], []
        for h in range(hb):
            parts = []
            for p_i, kr in enumerate(k_refs):
                hs = 0 if k_shared[p_i] else h
                parts.append(kr[pl.ds(off, bk), hs * LANES:(hs + 1) * LANES])
            k = jnp.concatenate(parts, axis=1) if n_parts > 1 else parts[0]
            s = jnp.dot(k, qs[h], preferred_element_type=F32)
            if masked:
                s = jnp.where(mask, s, -jnp.inf)
            m_new = jnp.maximum(ms[h], jnp.max(s, axis=0, keepdims=True))
            p = jnp.exp2(s - m_new)
            alpha = jnp.exp2(ms[h] - m_new)
            new_l.append(alpha * ls[h] + jnp.sum(p, axis=0, keepdims=True))
            new_m.append(m_new)
            acc_ref[h] = alpha * acc_ref[h] + jnp.dot(vt_ref[h, j], p.astype(BF16), preferred_element_type=F32)
        return tuple(new_m), tuple(new_l)

    init = (tuple(jnp.full((1, bq), M_INIT, F32) for _ in range(hb)),
            tuple(jnp.zeros((1, bq), F32) for _ in range(hb)))
    carry = lax.fori_loop(0, nfull_ref[qi], functools.partial(block, masked=False), init)
    if mode != "none":
        carry = lax.fori_loop(nfull_ref[qi], nkv_ref[qi], functools.partial(block, masked=True), carry)
    for h in range(hb):
        o_ref[h] = (acc_ref[h] / carry[1][h]).astype(o_ref.dtype)


def _tflash(q_parts, q_head0, k_parts, k_shared, vt, nfull, nkv, *, hb, bq, bk, mode, q_off, name):
    nb, heads = vt.shape[0], vt.shape[1]
    s = vt.shape[2] * bk
    nq = nfull.shape[0]
    assert heads % hb == 0 and all(h0 % hb == 0 for h0 in q_head0)
    in_specs = []
    for h0 in q_head0:
        in_specs.append(pl.BlockSpec((None, hb, LANES, bq),
                                     lambda b, g, qi, nf, nk, h0=h0: (b, h0 // hb + g, 0, qi)))
    for shared in k_shared:
        if shared:
            in_specs.append(pl.BlockSpec((None, s, LANES), lambda b, g, qi, nf, nk: (b, 0, 0)))
        else:
            in_specs.append(pl.BlockSpec((None, s, hb * LANES), lambda b, g, qi, nf, nk: (b, 0, g)))
    in_specs.append(pl.BlockSpec((None, hb, s // bk, LANES, bk), lambda b, g, qi, nf, nk: (b, g, 0, 0, 0)))
    grid_spec = pltpu.PrefetchScalarGridSpec(
        num_scalar_prefetch=2,
        grid=(nb, heads // hb, nq),
        in_specs=in_specs,
        out_specs=pl.BlockSpec((None, hb, LANES, bq), lambda b, g, qi, nf, nk: (b, g, 0, qi)),
        scratch_shapes=[pltpu.VMEM((hb, LANES, bq), F32)],
    )
    return pl.pallas_call(
        functools.partial(_tflash_kernel, n_parts=len(q_parts), k_shared=tuple(k_shared), hb=hb, bk=bk,
                          mode=mode, q_off=q_off),
        grid_spec=grid_spec,
        out_shape=jax.ShapeDtypeStruct((nb, heads, LANES, nq * bq), BF16),
        compiler_params=_params("parallel", "parallel", "arbitrary"),
        name=name,
    )(nfull, nkv, *q_parts, *k_parts, vt)


def _sparse_kernel(nvis_ref, aq_ref, aqi_ref, aw_ref, ka_ref, kb_ref, k_ref, vt_ref,
                   o_ref, key_ref, bias_ref, acc_ref, j_ref, *, nq, kb, s_real, topk, index_bits, q_off):
    b = pl.program_id(0)
    qi = pl.program_id(1)
    nblk = nvis_ref[b * nq + qi]
    bq = aq_ref.shape[-1]
    cols = min(bq, SCORE_COLS)
    int_min = jnp.int32(INT_MIN)
    qchunk = (q_off + qi * bq + lax.broadcasted_iota(I32, (1, bq), 1)) >> CHUNK_SHIFT

    def score_tile(t, carry):
        off = pl.multiple_of(t * SCORE_ROWS, SCORE_ROWS)
        ka = ka_ref[pl.ds(off, SCORE_ROWS), :]
        kbb = kb_ref[pl.ds(off, SCORE_ROWS), :]
        kidx = off + lax.broadcasted_iota(I32, (SCORE_ROWS, cols), 0)
        kchunk = jnp.where(kidx < s_real, kidx >> CHUNK_SHIFT, jnp.int32(INT_MAX))
        for c in range(bq // cols):
            cs = slice(c * cols, (c + 1) * cols)
            acc = jnp.zeros((SCORE_ROWS, cols), F32)
            for hp in range(IDX_HEADS // 2):
                qp = aqi_ref[hp * LANES:(hp + 1) * LANES, cs]
                ze = jnp.dot(ka, qp, preferred_element_type=F32)
                zo = jnp.dot(kbb, qp, preferred_element_type=F32)
                acc = acc + jnp.maximum(ze, 0.0) * aw_ref[2 * hp:2 * hp + 1, cs]
                acc = acc + jnp.maximum(zo, 0.0) * aw_ref[2 * hp + 1:2 * hp + 2, cs]
            bits = pltpu.bitcast(acc, I32)
            key = bits ^ ((bits >> 31) & jnp.int32(INT_MAX))
            key_ref[pl.ds(off, SCORE_ROWS), cs] = jnp.where(kchunk <= qchunk[:, cs], key, int_min)
        return carry

    lax.fori_loop(0, nblk * (kb // SCORE_ROWS), score_tile, 0)

    def count(pred):
        def body(t, c):
            off = pl.multiple_of(t * COUNT_ROWS, COUNT_ROWS)
            kk = key_ref[pl.ds(off, COUNT_ROWS), :]
            idx = off + lax.broadcasted_iota(I32, (COUNT_ROWS, bq), 0)
            hit = jnp.where(pred(kk, idx), 1.0, 0.0)
            return c + jnp.sum(hit.reshape(COUNT_ROWS // SUBLANES, SUBLANES, bq), axis=0)
        c = lax.fori_loop(0, nblk * (kb // COUNT_ROWS), body, jnp.zeros((SUBLANES, bq), F32))
        return jnp.sum(c, axis=0, keepdims=True)

    def bit_step(i, carry):
        t_u, n_keep = carry
        cand_u = t_u | (jnp.int32(1) << (31 - i))
        cand_s = cand_u ^ int_min
        cnt = count(lambda kk, idx: kk >= cand_s)
        take = cnt >= topk
        return jnp.where(take, cand_u, t_u), jnp.where(take, cnt, n_keep)

    t_u, n_ge = lax.fori_loop(0, 32, bit_step, (jnp.zeros((1, bq), I32), jnp.zeros((1, bq), F32)))
    t_s = t_u ^ int_min
    need = jnp.logical_and(n_ge > topk, t_u != 0)
    j_ref[...] = jnp.full(j_ref.shape, INT_MAX, I32)

    @pl.when(jnp.max(jnp.where(need, 1.0, 0.0)) > 0.0)
    def _():
        want = topk - count(lambda kk, idx: kk > t_s)

        def idx_step(i, jc):
            cand = jc | (jnp.int32(1) << (index_bits - 1 - i))
            cnt = count(lambda kk, idx: jnp.logical_and(kk == t_s, idx < cand))
            return jnp.where(cnt < want, cand, jc)
        jc = lax.fori_loop(0, index_bits, idx_step, jnp.zeros((1, bq), I32))
        j_ref[...] = jnp.where(need, jc, jnp.int32(INT_MAX))

    jcut = j_ref[...]
    acc_ref[...] = jnp.zeros(acc_ref.shape, F32)

    def attend_block(j, carry):
        ms, ls = carry
        off = pl.multiple_of(j * kb, kb)
        kk = key_ref[pl.ds(off, kb), :]
        idx = off + lax.broadcasted_iota(I32, (kb, bq), 0)
        sel = jnp.logical_or(kk > t_s, jnp.logical_and(kk == t_s, idx <= jcut))
        sel = jnp.logical_and(sel, kk != int_min)
        bias_ref[...] = jnp.where(sel, 0.0, -jnp.inf)
        kblk = k_ref[pl.ds(off, kb), :]
        vt = vt_ref[j]
        new_m, new_l = [], []
        for h in range(A_HEADS):
            hs = slice(h * HEAD_DIM, (h + 1) * HEAD_DIM)
            s = jnp.dot(kblk, aq_ref[hs, :], preferred_element_type=F32) + bias_ref[...]
            m_new = jnp.maximum(ms[h], jnp.max(s, axis=0, keepdims=True))
            p = jnp.exp2(s - m_new)
            alpha = jnp.exp2(ms[h] - m_new)
            new_l.append(alpha * ls[h] + jnp.sum(p, axis=0, keepdims=True))
            new_m.append(m_new)
            acc_ref[hs, :] = alpha * acc_ref[hs, :] + jnp.dot(vt, p.astype(BF16), preferred_element_type=F32)
        return tuple(new_m), tuple(new_l)

    init = (tuple(jnp.full((1, bq), M_INIT, F32) for _ in range(A_HEADS)),
            tuple(jnp.zeros((1, bq), F32) for _ in range(A_HEADS)))
    _, ls = lax.fori_loop(0, nblk, attend_block, init)
    for h in range(A_HEADS):
        hs = slice(h * HEAD_DIM, (h + 1) * HEAD_DIM)
        o_ref[hs, :] = (acc_ref[hs, :] / ls[h]).astype(o_ref.dtype)


def _sparse_attention(aq_t, aqi_t, aw_t, ka, kb_, k, vt3, nvis, *, bq, kb, s_real, topk, q_off, name):
    nb, s = k.shape[0], k.shape[1]
    nq = nvis.shape[0] // nb
    assert s % kb == 0 and kb % COUNT_ROWS == 0 and kb % SCORE_ROWS == 0 and bq % min(bq, SCORE_COLS) == 0
    qspec = lambda rows: pl.BlockSpec((None, rows, bq), lambda b, qi, nv: (b, 0, qi))
    kspec = pl.BlockSpec((None, s, LANES), lambda b, qi, nv: (b, 0, 0))
    grid_spec = pltpu.PrefetchScalarGridSpec(
        num_scalar_prefetch=1,
        grid=(nb, nq),
        in_specs=[qspec(A_HEADS * HEAD_DIM), qspec(IDX_HEADS * IDX_DIM), qspec(IDX_HEADS),
                  kspec, kspec, kspec,
                  pl.BlockSpec((None, s // kb, LANES, kb), lambda b, qi, nv: (b, 0, 0, 0))],
        out_specs=qspec(A_HEADS * HEAD_DIM),
        scratch_shapes=[pltpu.VMEM((s, bq), I32), pltpu.VMEM((kb, bq), F32),
                        pltpu.VMEM((A_HEADS * HEAD_DIM, bq), F32), pltpu.VMEM((1, bq), I32)],
    )
    return pl.pallas_call(
        functools.partial(_sparse_kernel, nq=nq, kb=kb, s_real=s_real, topk=topk,
                          index_bits=max(1, (s - 1).bit_length()), q_off=q_off),
        grid_spec=grid_spec,
        out_shape=jax.ShapeDtypeStruct((nb, A_HEADS * HEAD_DIM, nq * bq), BF16),
        compiler_params=_params("parallel", "arbitrary"),
        name=name,
    )(nvis, aq_t, aqi_t, aw_t, ka, kb_, k, vt3)


def _router_kernel(x_ref, whi_ref, wlo_ref, bias_ref, ids_ref, g0_ref, g1_ref):
    x = x_ref[...]
    xh = x.astype(BF16)
    xl = (x - xh.astype(F32)).astype(BF16)
    whi, wlo = whi_ref[...], wlo_ref[...]
    lg = (jnp.dot(xh, whi, preferred_element_type=F32) + jnp.dot(xh, wlo, preferred_element_type=F32)
          + jnp.dot(xl, whi, preferred_element_type=F32)) + bias_ref[...]
    lane = lax.broadcasted_iota(I32, lg.shape, 1)
    neg = -jnp.inf
    is_group = lane < N_GROUPS
    gl = jnp.where(is_group, lg, neg)
    gmax = jnp.max(gl, axis=-1, keepdims=True)
    grp = jnp.min(jnp.where(gl == gmax, lane, LANES), axis=-1, keepdims=True)
    g1 = 1.0 / jnp.sum(jnp.where(is_group, jnp.exp(gl - gmax), 0.0), axis=-1, keepdims=True)
    lo = N_GROUPS + grp * EXPERTS_PER_GROUP
    el = jnp.where(jnp.logical_and(lane >= lo, lane < lo + EXPERTS_PER_GROUP), lg, neg)
    v1 = jnp.max(el, axis=-1, keepdims=True)
    i1 = jnp.min(jnp.where(el == v1, lane, LANES), axis=-1, keepdims=True)
    el2 = jnp.where(lane == i1, neg, el)
    v2 = jnp.max(el2, axis=-1, keepdims=True)
    i2 = jnp.min(jnp.where(el2 == v2, lane, LANES), axis=-1, keepdims=True)
    e21 = jnp.exp(v2 - v1)
    den = 1.0 + e21
    ids_ref[...] = jnp.where(lane == 0, i1 - N_GROUPS, jnp.where(lane == 1, i2 - N_GROUPS, 0))
    g0_ref[...] = jnp.broadcast_to(g1 * (1.0 / den), g0_ref.shape)
    g1_ref[...] = jnp.broadcast_to(g1 * (e21 / den), g1_ref.shape)


def _router(x, whi, wlo, bias, *, bm):
    n, d = x.shape
    row = pl.BlockSpec((bm, LANES), lambda i: (i, 0))
    return pl.pallas_call(
        _router_kernel,
        grid=(n // bm,),
        in_specs=[pl.BlockSpec((bm, d), lambda i: (i, 0)),
                  pl.BlockSpec((d, LANES), lambda i: (0, 0)),
                  pl.BlockSpec((d, LANES), lambda i: (0, 0)),
                  pl.BlockSpec((1, LANES), lambda i: (0, 0))],
        out_specs=[row, row, row],
        out_shape=[jax.ShapeDtypeStruct((n, LANES), I32), jax.ShapeDtypeStruct((n, LANES), F32),
                   jax.ShapeDtypeStruct((n, LANES), F32)],
        compiler_params=_params("parallel"),
        name="router",
    )(x, whi, wlo, bias)


def _dispatch_kernel(ppos_ref, x_ref, buf_hbm, xs_hbm, sems, *, bm):
    del buf_hbm
    base = pl.program_id(0) * bm * 2

    def row_copy(a):
        return pltpu.make_async_copy(x_ref.at[pl.ds(a >> 1, 1), :],
                                     xs_hbm.at[pl.ds(ppos_ref[base + a], 1), :], sems.at[a & 1])

    def start(a, c):
        row_copy(a).start()
        return c
    lax.fori_loop(0, 2 * bm, start, 0)

    def wait(a, c):
        row_copy(a).wait()
        return c
    lax.fori_loop(0, 2 * bm, wait, 0)


def _dispatch(x, ppos, buf, *, bm):
    n, d = x.shape
    assert n % bm == 0
    grid_spec = pltpu.PrefetchScalarGridSpec(
        num_scalar_prefetch=1,
        grid=(n // bm,),
        in_specs=[pl.BlockSpec((bm, d), lambda i, pp: (i, 0)), pl.BlockSpec(memory_space=pl.ANY)],
        out_specs=pl.BlockSpec(memory_space=pl.ANY),
        scratch_shapes=[pltpu.SemaphoreType.DMA((2,))],
    )
    return pl.pallas_call(
        functools.partial(_dispatch_kernel, bm=bm),
        grid_spec=grid_spec,
        out_shape=jax.ShapeDtypeStruct(buf.shape, buf.dtype),
        input_output_aliases={2: 0},
        compiler_params=_params("arbitrary"),
        name="dispatch",
    )(ppos, x, buf)


def _expert_kernel(te_ref, nv_ref, x_ref, wg_ref, wu_ref, wd_ref, y_ref):
    t = pl.program_id(0)

    @pl.when(nv_ref[t] > 0)
    def _():
        x = x_ref[...].astype(BF16)
        hg = jnp.dot(x, wg_ref[...], preferred_element_type=F32)
        hu = jnp.dot(x, wu_ref[...], preferred_element_type=F32)
        act = (hg * (1.0 / (1.0 + jnp.exp(-hg)))) * hu
        y_ref[...] = jnp.dot(act.astype(BF16), wd_ref[...], preferred_element_type=F32)

    @pl.when(nv_ref[t] == 0)
    def _():
        y_ref[...] = jnp.zeros(y_ref.shape, F32)


def _experts(xs, wg, wu, wd, layer, tile_e, tile_nvalid, *, bm):
    p, d = xs.shape
    f = wg.shape[-1]
    grid_spec = pltpu.PrefetchScalarGridSpec(
        num_scalar_prefetch=2,
        grid=(p // bm,),
        in_specs=[pl.BlockSpec((bm, d), lambda t, te, nv: (t, 0)),
                  pl.BlockSpec((None, None, d, f), lambda t, te, nv: (layer, te[t], 0, 0)),
                  pl.BlockSpec((None, None, d, f), lambda t, te, nv: (layer, te[t], 0, 0)),
                  pl.BlockSpec((None, None, f, d), lambda t, te, nv: (layer, te[t], 0, 0))],
        out_specs=pl.BlockSpec((bm, d), lambda t, te, nv: (t, 0)),
    )
    return pl.pallas_call(
        _expert_kernel,
        grid_spec=grid_spec,
        out_shape=jax.ShapeDtypeStruct((p, d), F32),
        compiler_params=_params("arbitrary"),
        name="experts",
    )(tile_e, tile_nvalid, xs, wg, wu, wd)


def _expert_plan(eid, *, bm):
    na = eid.shape[0] * 2
    flat_e = eid.reshape(na)
    onehot = (flat_e[:, None] == jnp.arange(N_EXPERTS, dtype=I32)[None, :]).astype(I32)
    csum = jnp.cumsum(onehot, axis=0)
    counts = csum[-1]
    padded = ((counts + bm - 1) // bm) * bm
    pend = jnp.cumsum(padded)
    pstart = pend - padded
    ppos = jnp.sum(onehot * (pstart[None, :] + csum - 1), axis=1).astype(I32)
    total = na + N_EXPERTS * bm
    tile_start = jnp.arange(total // bm, dtype=I32) * bm
    tile_e = jnp.minimum(jnp.sum((tile_start[:, None] >= pend[None, :]).astype(I32), axis=1), N_EXPERTS - 1)
    tile_nvalid = jnp.clip(pstart[tile_e] + counts[tile_e] - tile_start, 0, bm).astype(I32)
    return ppos, tile_e.astype(I32), tile_nvalid, total


def _combine_ln_kernel(ppos_ref, x_ref, g0_ref, g1_ref, y_hbm, g_ref, b_ref, of_ref, ob_ref, ybuf, sem,
                       *, bm, alpha):
    base = pl.program_id(0) * bm * 2
    d = x_ref.shape[-1]

    def row_copy(a):
        return pltpu.make_async_copy(y_hbm.at[pl.ds(ppos_ref[base + a], 1), :],
                                     ybuf.at[a & 1, pl.ds(a >> 1, 1), :], sem.at[a & 1])

    def start(a, c):
        row_copy(a).start()
        return c
    lax.fori_loop(0, 2 * bm, start, 0)

    def wait(a, c):
        row_copy(a).wait()
        return c
    lax.fori_loop(0, 2 * bm, wait, 0)

    reps = d // LANES
    y = (alpha * x_ref[...] + jnp.tile(g0_ref[...], (1, reps)) * ybuf[0]
         + jnp.tile(g1_ref[...], (1, reps)) * ybuf[1])
    out = _layer_norm(y, g_ref[...], b_ref[...])
    of_ref[...] = out
    ob_ref[...] = out.astype(BF16)


def _combine_ln(x, ys, ppos, g0, g1, g, b, *, bm, alpha):
    n, d = x.shape
    row = lambda w: pl.BlockSpec((bm, w), lambda i, pp: (i, 0))
    grid_spec = pltpu.PrefetchScalarGridSpec(
        num_scalar_prefetch=1,
        grid=(n // bm,),
        in_specs=[row(d), row(LANES), row(LANES), pl.BlockSpec(memory_space=pl.ANY),
                  pl.BlockSpec((1, d), lambda i, pp: (0, 0)), pl.BlockSpec((1, d), lambda i, pp: (0, 0))],
        out_specs=[row(d), row(d)],
        scratch_shapes=[pltpu.VMEM((2, bm, d), F32), pltpu.SemaphoreType.DMA((2,))],
    )
    return pl.pallas_call(
        functools.partial(_combine_ln_kernel, bm=bm, alpha=alpha),
        grid_spec=grid_spec,
        out_shape=[jax.ShapeDtypeStruct((n, d), F32), jax.ShapeDtypeStruct((n, d), BF16)],
        compiler_params=_params("arbitrary"),
        name="combine_ln",
    )(ppos, x, g0, g1, ys, g, b)


def _rope_tables(pos):
    posf = pos.astype(F32)[:, None]
    inv128 = ROPE_THETA ** (-jnp.arange(HEAD_DIM // 2, dtype=F32) * 2.0 / HEAD_DIM)
    ang = posf * inv128[None, :]
    cos, sin = jnp.cos(ang), jnp.sin(ang)
    a128 = jnp.concatenate([cos, cos], axis=-1)
    b128 = jnp.concatenate([-sin, sin], axis=-1)
    inv64 = ROPE_THETA ** (-jnp.arange(IDX_DIM // 2, dtype=F32) * 2.0 / IDX_DIM)
    ang = posf * inv64[None, :]
    cos, sin = jnp.cos(ang), jnp.sin(ang)
    zero = jnp.zeros_like(sin)
    a64 = jnp.concatenate([cos, cos, cos, cos], axis=-1)
    b64 = jnp.concatenate([-sin, zero, -sin, zero], axis=-1)
    c64 = jnp.concatenate([zero, sin, zero, sin], axis=-1)
    return (a128, b128), (a64, b64, c64)


def _z_weight(w_in):
    o = _IN_OFF
    d = w_in.shape[0]
    cols = [w_in[:, o[0]:o[4]],
            w_in[:, o[6]:o[9]],
            w_in[:, o[10]:o[12]],
            w_in[:, o[4]:o[5]],
            w_in[:, o[12]:o[13]],
            w_in[:, o[5]:o[6]],
            w_in[:, o[9]:o[10]],
            jnp.zeros((d, LANES - IDX_HEADS - B_HEADS), w_in.dtype)]
    return jnp.concatenate(cols, axis=1).astype(BF16)


def _uq_weight(w):
    r = w.shape[0]
    w3 = w.reshape(r, C_HEADS, C_NOPE + C_ROPE)
    nope = w3[:, :, :C_NOPE].reshape(r, C_HEADS * C_NOPE)
    rope = jnp.pad(w3[:, :, C_NOPE:], ((0, 0), (0, 0), (0, LANES - C_ROPE))).reshape(r, C_HEADS * LANES)
    return jnp.concatenate([nope, rope], axis=1).astype(BF16)


def _pad_lanes(x):
    return jnp.pad(x, [(0, 0)] * (x.ndim - 1) + [(0, LANES - x.shape[-1])])


def _pad_rows(x, rows):
    return jnp.pad(x, [(0, 0), (0, rows - x.shape[1])] + [(0, 0)] * (x.ndim - 2))


def _heads_t(x, heads):
    nb, t, _ = x.shape
    return jnp.transpose(x.reshape(nb, t, heads, LANES), (0, 2, 3, 1))


def _heads_back(o_t):
    nb, h, w, t = o_t.shape
    return jnp.transpose(o_t, (0, 3, 1, 2)).reshape(nb * t, h * w)


def _v_blocks(v, heads, bk):
    nb, s, _ = v.shape
    return jnp.transpose(v.reshape(nb, s // bk, bk, heads, LANES), (0, 3, 1, 4, 2))


def _bf16_head(c):
    bits = lax.bitcast_convert_type(c, I32) & jnp.int32(-65536)
    return lax.bitcast_convert_type(bits, F32)


def _split3(c):
    hi = _bf16_head(c)
    mid = _bf16_head(c - hi)
    lo = c - hi - mid
    return hi.astype(BF16), mid.astype(BF16), lo.astype(BF16)


def _fox_bias_parts(csum_q, csum_k):
    nb, t, h = csum_q.shape
    s = csum_k.shape[1]
    one_q = jnp.ones((nb, t, h), BF16)
    qrows = jnp.stack(list(_split3(csum_q * LOG2E)) + [one_q] * 3, axis=-1)
    qb = jnp.pad(qrows, ((0, 0), (0, 0), (0, 0), (0, LANES - 6)))
    one_k = jnp.ones((nb, s, h), BF16)
    krows = jnp.stack([one_k] * 3 + [-p for p in _split3(csum_k * LOG2E)], axis=-1)
    kbias = jnp.pad(krows, ((0, 0), (0, 0), (0, 0), (0, LANES - 6))).reshape(nb, s, h * LANES)
    return jnp.transpose(qb, (0, 2, 3, 1)), kbias


def _pick(n, *cands):
    for c in cands:
        if n % c == 0:
            return c
    return n


def kernel(x_prompt, x_sample, cache_a_k, cache_a_v, cache_a_kidx, cache_b_k, cache_b_v, cache_b_logf,
           cache_c_latent, cache_c_krope, cache_mem_k, cache_mem_v, mem_prompt,
           w_in, b_f, c_q_norm, w_c_uq, c_kv_norm, w_c_uk, w_c_uv, w_o, ln1_g, ln1_b,
           w_mq, w_mk, w_mv, w_mo, ln2_g, ln2_b, w_rg, b_rg, w_re, b_re,
           w_gate, w_up, w_down, ln3_g, ln3_b):
    depth = w_in.shape[0]
    bp, tp, d = x_prompt.shape
    bs, ts, _ = x_sample.shape
    past = cache_a_k.shape[2]
    n_mem = mem_prompt.shape[1]
    assert bp == 1
    n_p, n_s = bp * tp, bs * ts
    n = n_p + n_s
    alpha = (2 * depth) ** 0.25
    s_s = past + ts
    s_pad = -(-s_s // COUNT_ROWS) * COUNT_ROWS

    bm_tok = _pick(n, 1024, 512, 256, 128, 64)
    bm_post = _pick(n, 256, 128, 64)
    bm_ln = _pick(n, 256, 128, 64)
    bq_f = _pick(tp, 1024, 512, 256, 128)
    bq_a = _pick(tp, 512, 256, 128)
    kb_a = _pick(tp, 1024, 512, 256)
    bm_moe = 256

    x = jnp.concatenate([x_prompt.reshape(n_p, d), x_sample.reshape(n_s, d)], axis=0)
    xb = x.astype(BF16)
    pos_p = jnp.arange(tp, dtype=I32)
    pos_s = past + jnp.arange(ts, dtype=I32)
    tabs128, tabs64 = _rope_tables(jnp.concatenate([pos_p, jnp.tile(pos_s, bs)]))

    nq_f = tp // bq_f
    nfull_p = jnp.arange(nq_f, dtype=I32)
    nkv_p = nfull_p + 1
    one = jnp.ones((1,), I32)
    zero = jnp.zeros((1,), I32)
    nvis_p = (((jnp.arange(tp // bq_a, dtype=I32) + 1) * bq_a + kb_a - 1) // kb_a).astype(I32)
    nvis_s = jnp.ones((bs,), I32)

    xs_buf = jnp.zeros((2 * n + N_EXPERTS * bm_moe, d), F32)
    w_o_b = w_o.astype(BF16)
    w_gate_b, w_up_b, w_down_b = w_gate.astype(BF16), w_up.astype(BF16), w_down.astype(BF16)
    state_p = [[] for _ in range(8)]
    state_s = [[] for _ in range(8)]
    mem_k_out, mem_v_out = [], []

    for l in range(depth):
        z = _matmul(xb, _z_weight(w_in[l]), bm=bm_tok, bn=512, out_dtype=F32, name="in_proj")
        bf_row = jnp.zeros((1, LANES), F32).at[0, SM2_BF:SM2_BF + B_HEADS].set(b_f[l].astype(F32))
        aq, ak, aqi, bq_, cqn, clat, sm1, sm2, akb, avb, bkb, bvb = _post(
            z, tabs128, tabs64, bf_row, c_q_norm[l].reshape(1, -1).astype(F32),
            c_kv_norm[l].reshape(1, -1).astype(F32), bm=bm_post)
        cq = _matmul_rope(cqn, _uq_weight(w_c_uq[l]), tabs64, bm=bm_tok, bn=512,
                          first_rope_col=C_HEADS * C_NOPE, out_scale=(C_NOPE + C_ROPE) ** -0.5 * LOG2E,
                          name="c_q_up")
        av = z[:, Z_AV:Z_AV + HEAD_DIM]
        aki = sm1[:, :IDX_DIM]
        ckr = sm1[:, IDX_DIM:]
        aw = sm2[:, :IDX_HEADS]
        blogf = sm2[:, SM2_BF:SM2_BF + B_HEADS]
        bk_ = z[:, Z_BK:Z_BK + B_HEADS * HEAD_DIM]
        bv_ = z[:, Z_BV:Z_BV + B_HEADS * HEAD_DIM]
        rows = (ak, av, aki, bk_.reshape(n, B_HEADS, HEAD_DIM), bv_.reshape(n, B_HEADS, HEAD_DIM),
                blogf, clat, ckr)
        for i, r in enumerate(rows):
            state_p[i].append(r[:n_p].reshape((bp, tp) + r.shape[1:]))
            state_s[i].append(r[n_p:].reshape((bs, ts) + r.shape[1:]))
        grp_p = lambda a: a[:n_p].reshape(bp, tp, -1)
        grp_s = lambda a: a[n_p:].reshape(bs, ts, -1)
        cat_s = lambda c, r: jnp.concatenate([c.reshape(bs, past, -1).astype(r.dtype), grp_s(r)], axis=1)
        all_n = lambda a: a.reshape(1, n, -1)

        aw_scaled = aw * (IDX_DIM ** -0.5)

        def sparse(aq_g, aqi_g, aw_g, ki, k, v, nvis, *, bq, kb, s_real, q_off, name):
            ka = _pad_lanes(ki).astype(BF16)
            kb_ = jnp.concatenate([jnp.zeros_like(ki), ki], axis=-1).astype(BF16)
            out_t = _sparse_attention(
                jnp.swapaxes(aq_g, 1, 2), jnp.swapaxes(aqi_g, 1, 2), jnp.swapaxes(aw_g, 1, 2),
                ka, kb_, k, _v_blocks(v, 1, kb)[:, 0], nvis, bq=bq, kb=kb, s_real=s_real,
                topk=min(TOPK_MAX, s_real // 4), q_off=q_off, name=name)
            return jnp.swapaxes(out_t, 1, 2).reshape(-1, A_HEADS * HEAD_DIM)

        oa_p = sparse(all_n(aq), all_n(aqi), all_n(aw_scaled), grp_p(aki), grp_p(akb), grp_p(avb), nvis_p,
                      bq=bq_a, kb=kb_a, s_real=tp, q_off=0, name="sparse_prompt")
        oa_s = sparse(grp_s(aq), grp_s(aqi), grp_s(aw_scaled),
                      _pad_rows(cat_s(cache_a_kidx[l], aki), s_pad), _pad_rows(cat_s(cache_a_k[l], akb), s_pad),
                      _pad_rows(cat_s(cache_a_v[l], avb), s_pad), nvis_s,
                      bq=ts, kb=s_pad, s_real=s_s, q_off=past, name="sparse_sample")

        csum_p = jnp.cumsum(grp_p(blogf), axis=1)
        qb_p, kbias_p = _fox_bias_parts(csum_p, csum_p)
        ob_p = _tflash([_heads_t(all_n(bq_), B_HEADS), qb_p], (0, 0),
                       [all_n(bkb), kbias_p], (False, False),
                       _v_blocks(grp_p(bvb), B_HEADS, bq_f), nfull_p, nkv_p,
                       hb=2, bq=bq_f, bk=bq_f, mode="causal", q_off=0, name="fox_prompt")
        csum_s = jnp.cumsum(cat_s(cache_b_logf[l], blogf), axis=1)
        qb_s, kbias_s = _fox_bias_parts(csum_s[:, past:], csum_s)
        ob_s = _tflash([_heads_t(grp_s(bq_), B_HEADS), qb_s], (0, 0),
                       [cat_s(cache_b_k[l], bkb), kbias_s], (False, False),
                       _v_blocks(cat_s(cache_b_v[l], bvb), B_HEADS, s_s), zero, one,
                       hb=2, bq=ts, bk=s_s, mode="causal", q_off=past, name="fox_sample")

        w_ukv = jnp.concatenate([w_c_uk[l], w_c_uv[l]], axis=1).astype(BF16)
        kw = C_HEADS * C_NOPE
        kv_p = _matmul(clat[:n_p], w_ukv, bm=_pick(n_p, 1024, 512, 256), bn=w_ukv.shape[1], out_dtype=BF16,
                       name="c_kv_up_prompt").reshape(bp, tp, -1)
        cq_t_p = _heads_t(all_n(cq), 2 * C_HEADS)
        oc_p = _tflash([cq_t_p, cq_t_p], (0, C_HEADS),
                       [kv_p, _pad_lanes(grp_p(ckr)).astype(BF16)], (False, True),
                       _v_blocks(kv_p[..., kw:], C_HEADS, bq_f), nfull_p, nkv_p,
                       hb=2, bq=bq_f, bk=bq_f, mode="chunk", q_off=0, name="mla_prompt")
        lat_s = cat_s(cache_c_latent[l], clat)
        kv_s = _matmul(lat_s.reshape(bs * s_s, -1), w_ukv, bm=_pick(bs * s_s, 1024, 512, 256, 128, 64),
                       bn=w_ukv.shape[1], out_dtype=BF16, name="c_kv_up_sample").reshape(bs, s_s, -1)
        cq_t_s = _heads_t(grp_s(cq), 2 * C_HEADS)
        oc_s = _tflash([cq_t_s, cq_t_s], (0, C_HEADS),
                       [kv_s, _pad_lanes(cat_s(cache_c_krope[l], ckr)).astype(BF16)], (False, True),
                       _v_blocks(kv_s[..., kw:], C_HEADS, s_s), zero, one,
                       hb=2, bq=ts, bk=s_s, mode="chunk", q_off=past, name="mla_sample")

        mixed = jnp.concatenate([
            jnp.concatenate([oa_p, oa_s], axis=0),
            jnp.concatenate([_heads_back(ob_p), _heads_back(ob_s)], axis=0),
            jnp.concatenate([_heads_back(oc_p), _heads_back(oc_s)], axis=0)], axis=1)
        x1, x1b = _mm_ln_cols(mixed, w_o_b, l, x, ln1_g[l].reshape(1, d), ln1_b[l].reshape(1, d),
                              bm=bm_ln, bn=_pick(d, 1024, 512, 256, 128), alpha=alpha, name="out_proj_ln1")

        mw = MEM_HEADS * MEM_DIM
        w_mkv = jnp.concatenate([w_mk[l], w_mv[l]], axis=1).astype(BF16)
        mkv = _matmul(mem_prompt.reshape(bp * n_mem, d), w_mkv, bm=_pick(bp * n_mem, 256, 128, 64), bn=512,
                      out_dtype=F32, name="mem_kv")
        mem_k_out.append(mkv[:, :mw].reshape(bp, n_mem, MEM_HEADS, MEM_DIM))
        mem_v_out.append(mkv[:, mw:].reshape(bp, n_mem, MEM_HEADS, MEM_DIM))
        qm = _matmul(x1b, w_mq[l].astype(BF16), bm=bm_tok, bn=mw, out_dtype=BF16, name="mem_q",
                     out_scale=MEM_DIM ** -0.5 * LOG2E)
        mkv3 = mkv.reshape(bp, n_mem, 2 * mw).astype(BF16)
        om_p = _tflash([_heads_t(grp_p(qm), MEM_HEADS)], (0,), [mkv3[..., :mw]], (False,),
                       _v_blocks(mkv3[..., mw:], MEM_HEADS, n_mem), jnp.ones((nq_f,), I32), jnp.ones((nq_f,), I32),
                       hb=2, bq=bq_f, bk=n_mem, mode="none", q_off=0, name="mem_prompt")
        om_s = _tflash([_heads_t(grp_s(qm), MEM_HEADS)], (0,),
                       [cache_mem_k[l].reshape(bs, n_mem, mw).astype(BF16)], (False,),
                       _v_blocks(cache_mem_v[l].reshape(bs, n_mem, mw).astype(BF16), MEM_HEADS, n_mem), one, one,
                       hb=2, bq=ts, bk=n_mem, mode="none", q_off=0, name="mem_sample")
        om = jnp.concatenate([_heads_back(om_p), _heads_back(om_s)], axis=0)
        x2, _ = _mm_res_ln(om, w_mo[l].astype(BF16), x1, ln2_g[l].reshape(1, d), ln2_b[l].reshape(1, d),
                           bm=bm_ln, bk=mw, alpha=alpha, name="mem_out_ln2")

        w_r = jnp.concatenate([w_rg[l], jnp.transpose(w_re[l], (1, 0, 2)).reshape(d, N_EXPERTS)], axis=1).astype(F32)
        w_r = _pad_lanes(w_r)
        w_r_head = _bf16_head(w_r)
        w_r_hi = w_r_head.astype(BF16)
        w_r_lo = (w_r - w_r_head).astype(BF16)
        b_r = _pad_lanes(jnp.concatenate([b_rg[l], b_re[l].reshape(-1)]).astype(F32).reshape(1, -1))
        ids, g0, g1 = _router(x2, w_r_hi, w_r_lo, b_r, bm=_pick(n, 512, 256, 128, 64))
        ppos, tile_e, tile_nvalid, total = _expert_plan(ids[:, :2], bm=bm_moe)
        assert total == xs_buf.shape[0]
        xs_buf = _dispatch(x2, ppos, xs_buf, bm=_pick(n, 256, 128, 64))
        ys = _experts(xs_buf, w_gate_b, w_up_b, w_down_b, l, tile_e, tile_nvalid, bm=bm_moe)
        x, xb = _combine_ln(x2, ys, ppos, g0, g1, ln3_g[l].reshape(1, d), ln3_b[l].reshape(1, d),
                            bm=_pick(n, 256, 128, 64), alpha=alpha)

    outs = [x[:n_p].reshape(bp, tp, d), x[n_p:].reshape(bs, ts, d)]
    outs += [jnp.stack(c) for c in state_p]
    outs += [jnp.stack(mem_k_out), jnp.stack(mem_v_out)]
    outs += [jnp.stack(c) for c in state_s]
    return tuple(outs)
```

```python
import functools
import math

import jax
import jax.numpy as jnp
import numpy as np
from jax import lax
from jax.experimental import pallas as pl
from jax.experimental.pallas import tpu as pltpu

F32 = jnp.float32
BF16 = jnp.bfloat16
I32 = jnp.int32

LANES = 128
SUBLANES = 8
VMEM_LIMIT = 56 * 1024 * 1024

CHUNK_SHIFT = 6
HEAD_DIM = 128
ROPE_THETA = 10000.0
A_HEADS = 8
IDX_HEADS = 32
IDX_DIM = 64
TOPK_MAX = 256
B_HEADS = 8
C_HEADS = 16
C_Q_RANK = 768
C_KV_RANK = 256
C_NOPE = 128
C_ROPE = 64
C_V = 128
MEM_HEADS = 4
MEM_DIM = 128
N_GROUPS = 4
EXPERTS_PER_GROUP = 4
N_EXPERTS = N_GROUPS * EXPERTS_PER_GROUP
LN_EPS = 1e-5
RMS_EPS = 1e-6
LOG2E = math.log2(math.e)

_IN_SIZES = (A_HEADS * HEAD_DIM, HEAD_DIM, HEAD_DIM, IDX_HEADS * IDX_DIM, IDX_DIM, IDX_HEADS,
             B_HEADS * HEAD_DIM, B_HEADS * HEAD_DIM, B_HEADS * HEAD_DIM, B_HEADS,
             C_Q_RANK, C_KV_RANK, C_ROPE)
_IN_OFF = np.concatenate([[0], np.cumsum(_IN_SIZES)]).astype(int)

Z_AQ = 0
Z_AK = Z_AQ + A_HEADS * HEAD_DIM
Z_AV = Z_AK + HEAD_DIM
Z_AQI = Z_AV + HEAD_DIM
Z_BQ = Z_AQI + IDX_HEADS * IDX_DIM
Z_BK = Z_BQ + B_HEADS * HEAD_DIM
Z_BV = Z_BK + B_HEADS * HEAD_DIM
Z_CQ = Z_BV + B_HEADS * HEAD_DIM
Z_CKV = Z_CQ + C_Q_RANK
Z_SM1 = Z_CKV + C_KV_RANK
Z_SM2 = Z_SM1 + LANES
Z_WIDTH = Z_SM2 + LANES
SM2_BF = IDX_HEADS

M_INIT = -1e30
INT_MIN = -2 ** 31
INT_MAX = 2 ** 31 - 1
SCORE_ROWS = 128
SCORE_COLS = 256
COUNT_ROWS = 256
LN_ROWS = 64


def _params(*sem):
    return pltpu.CompilerParams(dimension_semantics=sem, vmem_limit_bytes=VMEM_LIMIT)


def _mm_kernel(x_ref, w_ref, o_ref, *, out_scale):
    x = x_ref[...].astype(BF16)
    acc = jnp.dot(x, w_ref[...], preferred_element_type=F32)
    if out_scale != 1.0:
        acc = acc * out_scale
    o_ref[...] = acc.astype(o_ref.dtype)


def _matmul(x, w, *, bm, bn, out_dtype, name, out_scale=1.0):
    m, k = x.shape
    n = w.shape[1]
    assert m % bm == 0 and n % bn == 0, (x.shape, w.shape, bm, bn)
    return pl.pallas_call(
        functools.partial(_mm_kernel, out_scale=out_scale),
        grid=(m // bm, n // bn),
        in_specs=[pl.BlockSpec((bm, k), lambda i, j: (i, 0)),
                  pl.BlockSpec((k, bn), lambda i, j: (0, j))],
        out_specs=pl.BlockSpec((bm, bn), lambda i, j: (i, j)),
        out_shape=jax.ShapeDtypeStruct((m, n), out_dtype),
        compiler_params=_params("parallel", "parallel"),
        name=name,
    )(x, w)


def _rope64(x, a, b, c):
    return x * a + pltpu.roll(x, 96, 1) * b + pltpu.roll(x, 32, 1) * c


def _rope128(x, a, b):
    return x * a + pltpu.roll(x, 64, 1) * b


def _mm_rope_kernel(x_ref, w_ref, a_ref, b_ref, c_ref, o_ref, *, first_rope_block, out_scale):
    j = pl.program_id(1)
    acc = jnp.dot(x_ref[...], w_ref[...], preferred_element_type=F32) * out_scale

    @pl.when(j < first_rope_block)
    def _():
        o_ref[...] = acc.astype(o_ref.dtype)

    @pl.when(j >= first_rope_block)
    def _():
        a, b, c = a_ref[...], b_ref[...], c_ref[...]
        for g in range(acc.shape[1] // LANES):
            sl = slice(g * LANES, (g + 1) * LANES)
            o_ref[:, sl] = _rope64(acc[:, sl], a, b, c).astype(o_ref.dtype)


def _matmul_rope(x, w, tabs, *, bm, bn, first_rope_col, out_scale, name):
    m, k = x.shape
    n = w.shape[1]
    assert m % bm == 0 and n % bn == 0 and first_rope_col % bn == 0
    tab_spec = pl.BlockSpec((bm, LANES), lambda i, j: (i, 0))
    return pl.pallas_call(
        functools.partial(_mm_rope_kernel, first_rope_block=first_rope_col // bn, out_scale=out_scale),
        grid=(m // bm, n // bn),
        in_specs=[pl.BlockSpec((bm, k), lambda i, j: (i, 0)),
                  pl.BlockSpec((k, bn), lambda i, j: (0, j)),
                  tab_spec, tab_spec, tab_spec],
        out_specs=pl.BlockSpec((bm, bn), lambda i, j: (i, j)),
        out_shape=jax.ShapeDtypeStruct((m, n), BF16),
        compiler_params=_params("parallel", "parallel"),
        name=name,
    )(x, w, *tabs)


def _post_kernel(z_ref, a128_ref, b128_ref, a64_ref, b64_ref, c64_ref, bf_ref, gq_ref, gkv_ref,
                 aq_ref, ak_ref, aqi_ref, bq_ref, cqn_ref, clat_ref, sm1_ref, sm2_ref,
                 akb_ref, avb_ref, bkb_ref, bvb_ref):
    a128, b128 = a128_ref[...], b128_ref[...]
    a64, b64, c64 = a64_ref[...], b64_ref[...], c64_ref[...]
    qscale = HEAD_DIM ** -0.5 * LOG2E
    for g in range(A_HEADS):
        src = slice(Z_AQ + g * LANES, Z_AQ + (g + 1) * LANES)
        aq_ref[:, g * LANES:(g + 1) * LANES] = (_rope128(z_ref[:, src], a128, b128) * qscale).astype(BF16)
    ak = _rope128(z_ref[:, Z_AK:Z_AK + LANES], a128, b128)
    ak_ref[...] = ak
    akb_ref[...] = ak.astype(BF16)
    avb_ref[...] = z_ref[:, Z_AV:Z_AV + LANES].astype(BF16)
    bkb_ref[...] = z_ref[:, Z_BK:Z_BK + B_HEADS * HEAD_DIM].astype(BF16)
    bvb_ref[...] = z_ref[:, Z_BV:Z_BV + B_HEADS * HEAD_DIM].astype(BF16)
    for g in range(IDX_HEADS * IDX_DIM // LANES):
        src = slice(Z_AQI + g * LANES, Z_AQI + (g + 1) * LANES)
        aqi_ref[:, g * LANES:(g + 1) * LANES] = _rope64(z_ref[:, src], a64, b64, c64).astype(BF16)
    bq_ref[...] = (z_ref[:, Z_BQ:Z_BQ + B_HEADS * HEAD_DIM] * qscale).astype(BF16)
    cq = z_ref[:, Z_CQ:Z_CQ + C_Q_RANK]
    cq = cq * lax.rsqrt(jnp.mean(cq * cq, axis=-1, keepdims=True) + RMS_EPS) * gq_ref[...]
    cqn_ref[...] = cq.astype(BF16)
    ckv = z_ref[:, Z_CKV:Z_CKV + C_KV_RANK]
    clat_ref[...] = ckv * lax.rsqrt(jnp.mean(ckv * ckv, axis=-1, keepdims=True) + RMS_EPS) * gkv_ref[...]
    sm1_ref[...] = _rope64(z_ref[:, Z_SM1:Z_SM1 + LANES], a64, b64, c64)
    s2 = z_ref[:, Z_SM2:Z_SM2 + LANES]
    lane = lax.broadcasted_iota(I32, s2.shape, 1)
    f = s2 + bf_ref[...]
    logf = jnp.minimum(f, 0.0) - jnp.log1p(jnp.exp(-jnp.abs(f)))
    sm2_ref[...] = jnp.where(lane < SM2_BF, s2 * (IDX_HEADS ** -0.5 * IDX_DIM ** -0.5),
                             jnp.where(lane < SM2_BF + B_HEADS, logf, 0.0))


def _post(z, tabs128, tabs64, bf_row, gq, gkv, *, bm):
    n = z.shape[0]
    row = lambda w: pl.BlockSpec((bm, w), lambda i: (i, 0))
    const = lambda w: pl.BlockSpec((1, w), lambda i: (0, 0))
    widths = (A_HEADS * HEAD_DIM, LANES, IDX_HEADS * IDX_DIM, B_HEADS * HEAD_DIM, C_Q_RANK, C_KV_RANK, LANES, LANES,
              LANES, LANES, B_HEADS * HEAD_DIM, B_HEADS * HEAD_DIM)
    dtypes = (BF16, F32, BF16, BF16, BF16, F32, F32, F32, BF16, BF16, BF16, BF16)
    return pl.pallas_call(
        _post_kernel,
        grid=(n // bm,),
        in_specs=[row(Z_WIDTH)] + [row(LANES)] * 5 + [const(LANES), const(C_Q_RANK), const(C_KV_RANK)],
        out_specs=[row(w) for w in widths],
        out_shape=[jax.ShapeDtypeStruct((n, w), d) for w, d in zip(widths, dtypes)],
        compiler_params=_params("parallel"),
        name="post_projection",
    )(z, *tabs128, *tabs64, bf_row, gq, gkv)


def _layer_norm(y, g, b):
    mu = jnp.mean(y, axis=-1, keepdims=True)
    yc = y - mu
    var = jnp.mean(yc * yc, axis=-1, keepdims=True)
    return yc * lax.rsqrt(var + LN_EPS) * g + b


def _mm_res_ln_kernel(x_ref, w_ref, r_ref, g_ref, b_ref, of_ref, ob_ref, acc_ref, *, nk, alpha):
    k = pl.program_id(1)

    @pl.when(k == 0)
    def _():
        acc_ref[...] = jnp.zeros_like(acc_ref)

    acc_ref[...] += jnp.dot(x_ref[...], w_ref[...], preferred_element_type=F32)

    @pl.when(k == nk - 1)
    def _():
        out = _layer_norm(alpha * r_ref[...] + acc_ref[...], g_ref[...], b_ref[...])
        of_ref[...] = out
        ob_ref[...] = out.astype(BF16)


def _mm_res_ln(x, w, res, g, b, *, bm, bk, alpha, name):
    m, k = x.shape
    d = w.shape[1]
    nk = k // bk
    return pl.pallas_call(
        functools.partial(_mm_res_ln_kernel, nk=nk, alpha=alpha),
        grid=(m // bm, nk),
        in_specs=[pl.BlockSpec((bm, bk), lambda i, kk: (i, kk)),
                  pl.BlockSpec((bk, d), lambda i, kk: (kk, 0)),
                  pl.BlockSpec((bm, d), lambda i, kk: (i, 0)),
                  pl.BlockSpec((1, d), lambda i, kk: (0, 0)),
                  pl.BlockSpec((1, d), lambda i, kk: (0, 0))],
        out_specs=[pl.BlockSpec((bm, d), lambda i, kk: (i, 0)),
                   pl.BlockSpec((bm, d), lambda i, kk: (i, 0))],
        out_shape=[jax.ShapeDtypeStruct((m, d), F32), jax.ShapeDtypeStruct((m, d), BF16)],
        scratch_shapes=[pltpu.VMEM((bm, d), F32)],
        compiler_params=_params("parallel", "arbitrary"),
        name=name,
    )(x, w, res, g, b)


def _mm_ln_cols_kernel(x_ref, w_ref, r_ref, g_ref, b_ref, of_ref, ob_ref, *, nj, bn, alpha):
    j = pl.program_id(1)
    y = alpha * r_ref[...] + jnp.dot(x_ref[...], w_ref[...], preferred_element_type=F32)
    for jj in range(nj):
        @pl.when(j == jj)
        def _(jj=jj):
            of_ref[:, jj * bn:(jj + 1) * bn] = y

    @pl.when(j == nj - 1)
    def _():
        for r0 in range(0, of_ref.shape[0], LN_ROWS):
            rs = slice(r0, r0 + LN_ROWS)
            out = _layer_norm(of_ref[rs, :], g_ref[...], b_ref[...])
            of_ref[rs, :] = out
            ob_ref[rs, :] = out.astype(BF16)


def _mm_ln_cols(x, w, layer, res, g, b, *, bm, bn, alpha, name):
    m, k = x.shape
    d = w.shape[-1]
    nj = d // bn
    return pl.pallas_call(
        functools.partial(_mm_ln_cols_kernel, nj=nj, bn=bn, alpha=alpha),
        grid=(m // bm, nj),
        in_specs=[pl.BlockSpec((bm, k), lambda i, j: (i, 0)),
                  pl.BlockSpec((None, k, bn), lambda i, j: (layer, 0, j)),
                  pl.BlockSpec((bm, bn), lambda i, j: (i, j)),
                  pl.BlockSpec((1, d), lambda i, j: (0, 0)),
                  pl.BlockSpec((1, d), lambda i, j: (0, 0))],
        out_specs=[pl.BlockSpec((bm, d), lambda i, j: (i, 0)),
                   pl.BlockSpec((bm, d), lambda i, j: (i, 0))],
        out_shape=[jax.ShapeDtypeStruct((m, d), F32), jax.ShapeDtypeStruct((m, d), BF16)],
        compiler_params=_params("parallel", "arbitrary"),
        name=name,
    )(x, w, res, g, b)


def _tflash_kernel(nfull_ref, nkv_ref, *refs, n_parts, k_shared, hb, bk, mode, q_off):
    refs = list(refs)
    q_refs = [refs.pop(0) for _ in range(n_parts)]
    k_refs = [refs.pop(0) for _ in range(n_parts)]
    v_ref = refs.pop(0)
    o_ref, acc_ref = refs
    qi = pl.program_id(2)
    bq = o_ref.shape[0]
    qs = []
    for h in range(hb):
        hs = slice(h * LANES, (h + 1) * LANES)
        qs.append(jnp.concatenate([qr[:, hs].astype(F32).T.astype(BF16) for qr in q_refs], axis=0))
    acc_ref[...] = jnp.zeros(acc_ref.shape, F32)

    def block(j, carry, masked):
        ms, ls = carry
        off = pl.multiple_of(j * bk, bk)
        if masked:
            kpos = off + lax.broadcasted_iota(I32, (bk, bq), 0)
            qpos = q_off + qi * bq + lax.broadcasted_iota(I32, (bk, bq), 1)
            if mode == "chunk":
                mask = (kpos >> CHUNK_SHIFT) <= (qpos >> CHUNK_SHIFT)
            else:
                mask = kpos <= qpos
        new_m, new_l = [], []
        for h in range(hb):
            parts = []
            for p_i, kr in enumerate(k_refs):
                hs = 0 if k_shared[p_i] else h
                parts.append(kr[pl.ds(off, bk), hs * LANES:(hs + 1) * LANES])
            k = jnp.concatenate(parts, axis=1) if n_parts > 1 else parts[0]
            s = jnp.dot(k, qs[h], preferred_element_type=F32)
            if masked:
                s = jnp.where(mask, s, -jnp.inf)
            m_new = jnp.maximum(ms[h], jnp.max(s, axis=0, keepdims=True))
            p = jnp.exp2(s - m_new)
            alpha = jnp.exp2(ms[h] - m_new)
            new_l.append(alpha * ls[h] + jnp.sum(p, axis=0, keepdims=True))
            new_m.append(m_new)
            v = v_ref[pl.ds(off, bk), h * LANES:(h + 1) * LANES]
            acc_ref[h] = alpha * acc_ref[h] + lax.dot_general(v, p.astype(BF16), (((0,), (0,)), ((), ())),
                                                              preferred_element_type=F32)
        return tuple(new_m), tuple(new_l)

    init = (tuple(jnp.full((1, bq), M_INIT, F32) for _ in range(hb)),
            tuple(jnp.zeros((1, bq), F32) for _ in range(hb)))
    carry = lax.fori_loop(0, nfull_ref[qi], functools.partial(block, masked=False), init)
    if mode != "none":
        carry = lax.fori_loop(nfull_ref[qi], nkv_ref[qi], functools.partial(block, masked=True), carry)
    for h in range(hb):
        o_ref[:, h * LANES:(h + 1) * LANES] = (acc_ref[h] / carry[1][h]).T.astype(o_ref.dtype)


def _tflash(q_parts, q_head0, k_parts, k_shared, v, v_head0, nfull, nkv, *, heads, s, hb, bq, bk, mode, q_off,
            name):
    nb = v.shape[0]
    nq = nfull.shape[0]
    assert heads % hb == 0 and v_head0 % hb == 0 and all(h0 % hb == 0 for h0 in q_head0) and s % bk == 0
    in_specs = []
    for h0 in q_head0:
        in_specs.append(pl.BlockSpec((None, bq, hb * LANES),
                                     lambda b, g, qi, nf, nk, h0=h0: (b, qi, h0 // hb + g)))
    for shared in k_shared:
        if shared:
            in_specs.append(pl.BlockSpec((None, s, LANES), lambda b, g, qi, nf, nk: (b, 0, 0)))
        else:
            in_specs.append(pl.BlockSpec((None, s, hb * LANES), lambda b, g, qi, nf, nk: (b, 0, g)))
    in_specs.append(pl.BlockSpec((None, s, hb * LANES), lambda b, g, qi, nf, nk: (b, 0, v_head0 // hb + g)))
    grid_spec = pltpu.PrefetchScalarGridSpec(
        num_scalar_prefetch=2,
        grid=(nb, heads // hb, nq),
        in_specs=in_specs,
        out_specs=pl.BlockSpec((None, bq, hb * LANES), lambda b, g, qi, nf, nk: (b, qi, g)),
        scratch_shapes=[pltpu.VMEM((hb, LANES, bq), F32)],
    )
    return pl.pallas_call(
        functools.partial(_tflash_kernel, n_parts=len(q_parts), k_shared=tuple(k_shared), hb=hb, bk=bk,
                          mode=mode, q_off=q_off),
        grid_spec=grid_spec,
        out_shape=jax.ShapeDtypeStruct((nb, nq * bq, heads * LANES), BF16),
        compiler_params=_params("parallel", "parallel", "arbitrary"),
        name=name,
    )(nfull, nkv, *q_parts, *k_parts, v)


def _sparse_kernel(nvis_ref, aq_in, aqi_in, aw_in, ka_ref, kb_ref, k_ref, v_ref,
                   o_ref, aq_ref, aqi_ref, aw_ref, key_ref, bias_ref, acc_ref, j_ref,
                   *, nq, kb, s_real, topk, index_bits, q_off):
    b = pl.program_id(0)
    qi = pl.program_id(1)
    nblk = nvis_ref[b * nq + qi]
    bq = aq_in.shape[0]
    cols = min(bq, SCORE_COLS)
    int_min = jnp.int32(INT_MIN)
    qchunk = (q_off + qi * bq + lax.broadcasted_iota(I32, (1, bq), 1)) >> CHUNK_SHIFT

    for g in range(aq_in.shape[1] // LANES):
        gs = slice(g * LANES, (g + 1) * LANES)
        aq_ref[gs, :] = aq_in[:, gs].astype(F32).T.astype(BF16)
    for g in range(aqi_in.shape[1] // LANES):
        gs = slice(g * LANES, (g + 1) * LANES)
        aqi_ref[gs, :] = aqi_in[:, gs].astype(F32).T.astype(BF16)
    aw_ref[...] = aw_in[...].T

    def score_tile(t, carry):
        off = pl.multiple_of(t * SCORE_ROWS, SCORE_ROWS)
        ka = ka_ref[pl.ds(off, SCORE_ROWS), :]
        kbb = kb_ref[pl.ds(off, SCORE_ROWS), :]
        kidx = off + lax.broadcasted_iota(I32, (SCORE_ROWS, cols), 0)
        kchunk = jnp.where(kidx < s_real, kidx >> CHUNK_SHIFT, jnp.int32(INT_MAX))
        for c in range(bq // cols):
            cs = slice(c * cols, (c + 1) * cols)
            acc = jnp.zeros((SCORE_ROWS, cols), F32)
            for hp in range(IDX_HEADS // 2):
                qp = aqi_ref[hp * LANES:(hp + 1) * LANES, cs]
                ze = jnp.dot(ka, qp, preferred_element_type=F32)
                zo = jnp.dot(kbb, qp, preferred_element_type=F32)
                acc = acc + jnp.maximum(ze, 0.0) * aw_ref[2 * hp:2 * hp + 1, cs]
                acc = acc + jnp.maximum(zo, 0.0) * aw_ref[2 * hp + 1:2 * hp + 2, cs]
            bits = pltpu.bitcast(acc, I32)
            key = bits ^ ((bits >> 31) & jnp.int32(INT_MAX))
            key_ref[pl.ds(off, SCORE_ROWS), cs] = jnp.where(kchunk <= qchunk[:, cs], key, int_min)
        return carry

    lax.fori_loop(0, nblk * (kb // SCORE_ROWS), score_tile, 0)

    def count(pred):
        def body(t, c):
            off = pl.multiple_of(t * COUNT_ROWS, COUNT_ROWS)
            kk = key_ref[pl.ds(off, COUNT_ROWS), :]
            idx = off + lax.broadcasted_iota(I32, (COUNT_ROWS, bq), 0)
            hit = jnp.where(pred(kk, idx), 1.0, 0.0)
            return c + jnp.sum(hit.reshape(COUNT_ROWS // SUBLANES, SUBLANES, bq), axis=0)
        c = lax.fori_loop(0, nblk * (kb // COUNT_ROWS), body, jnp.zeros((SUBLANES, bq), F32))
        return jnp.sum(c, axis=0, keepdims=True)

    def bit_step(i, carry):
        t_u, n_keep = carry
        cand_u = t_u | (jnp.int32(1) << (31 - i))
        cand_s = cand_u ^ int_min
        cnt = count(lambda kk, idx: kk >= cand_s)
        take = cnt >= topk
        return jnp.where(take, cand_u, t_u), jnp.where(take, cnt, n_keep)

    t_u, n_ge = lax.fori_loop(0, 32, bit_step, (jnp.zeros((1, bq), I32), jnp.zeros((1, bq), F32)))
    t_s = t_u ^ int_min
    need = jnp.logical_and(n_ge > topk, t_u != 0)
    j_ref[...] = jnp.full(j_ref.shape, INT_MAX, I32)

    @pl.when(jnp.max(jnp.where(need, 1.0, 0.0)) > 0.0)
    def _():
        want = topk - count(lambda kk, idx: kk > t_s)

        def idx_step(i, jc):
            cand = jc | (jnp.int32(1) << (index_bits - 1 - i))
            cnt = count(lambda kk, idx: jnp.logical_and(kk == t_s, idx < cand))
            return jnp.where(cnt < want, cand, jc)
        jc = lax.fori_loop(0, index_bits, idx_step, jnp.zeros((1, bq), I32))
        j_ref[...] = jnp.where(need, jc, jnp.int32(INT_MAX))

    jcut = j_ref[...]
    acc_ref[...] = jnp.zeros(acc_ref.shape, F32)

    def attend_block(j, carry):
        ms, ls = carry
        off = pl.multiple_of(j * kb, kb)
        kk = key_ref[pl.ds(off, kb), :]
        idx = off + lax.broadcasted_iota(I32, (kb, bq), 0)
        sel = jnp.logical_or(kk > t_s, jnp.logical_and(kk == t_s, idx <= jcut))
        sel = jnp.logical_and(sel, kk != int_min)
        bias_ref[...] = jnp.where(sel, 0.0, -jnp.inf)
        kblk = k_ref[pl.ds(off, kb), :]
        vblk = v_ref[pl.ds(off, kb), :]
        new_m, new_l = [], []
        for h in range(A_HEADS):
            hs = slice(h * HEAD_DIM, (h + 1) * HEAD_DIM)
            s = jnp.dot(kblk, aq_ref[hs, :], preferred_element_type=F32) + bias_ref[...]
            m_new = jnp.maximum(ms[h], jnp.max(s, axis=0, keepdims=True))
            p = jnp.exp2(s - m_new)
            alpha = jnp.exp2(ms[h] - m_new)
            new_l.append(alpha * ls[h] + jnp.sum(p, axis=0, keepdims=True))
            new_m.append(m_new)
            acc_ref[hs, :] = alpha * acc_ref[hs, :] + lax.dot_general(
                vblk, p.astype(BF16), (((0,), (0,)), ((), ())), preferred_element_type=F32)
        return tuple(new_m), tuple(new_l)

    init = (tuple(jnp.full((1, bq), M_INIT, F32) for _ in range(A_HEADS)),
            tuple(jnp.zeros((1, bq), F32) for _ in range(A_HEADS)))
    _, ls = lax.fori_loop(0, nblk, attend_block, init)
    for h in range(A_HEADS):
        hs = slice(h * HEAD_DIM, (h + 1) * HEAD_DIM)
        o_ref[:, hs] = (acc_ref[hs, :] / ls[h]).T.astype(o_ref.dtype)


def _sparse_attention(aq, aqi, aw, ka, kb_, k, v, nvis, *, bq, kb, s_real, topk, q_off, name):
    nb, s = k.shape[0], k.shape[1]
    nq = nvis.shape[0] // nb
    assert s % kb == 0 and kb % COUNT_ROWS == 0 and kb % SCORE_ROWS == 0 and bq % min(bq, SCORE_COLS) == 0
    qspec = lambda w: pl.BlockSpec((None, bq, w), lambda b, qi, nv: (b, qi, 0))
    kspec = pl.BlockSpec((None, s, LANES), lambda b, qi, nv: (b, 0, 0))
    aq_w, aqi_w = A_HEADS * HEAD_DIM, IDX_HEADS * IDX_DIM
    grid_spec = pltpu.PrefetchScalarGridSpec(
        num_scalar_prefetch=1,
        grid=(nb, nq),
        in_specs=[qspec(aq_w), qspec(aqi_w), qspec(LANES), kspec, kspec, kspec, kspec],
        out_specs=qspec(aq_w),
        scratch_shapes=[pltpu.VMEM((aq_w, bq), BF16), pltpu.VMEM((aqi_w, bq), BF16), pltpu.VMEM((LANES, bq), F32),
                        pltpu.VMEM((s, bq), I32), pltpu.VMEM((kb, bq), F32),
                        pltpu.VMEM((aq_w, bq), F32), pltpu.VMEM((1, bq), I32)],
    )
    return pl.pallas_call(
        functools.partial(_sparse_kernel, nq=nq, kb=kb, s_real=s_real, topk=topk,
                          index_bits=max(1, (s - 1).bit_length()), q_off=q_off),
        grid_spec=grid_spec,
        out_shape=jax.ShapeDtypeStruct((nb, nq * bq, aq_w), BF16),
        compiler_params=_params("parallel", "arbitrary"),
        name=name,
    )(nvis, aq, aqi, aw, ka, kb_, k, v)


def _router_kernel(x_ref, whi_ref, wlo_ref, bias_ref, ids_ref, g0_ref, g1_ref):
    x = x_ref[...]
    xh = x.astype(BF16)
    xl = (x - xh.astype(F32)).astype(BF16)
    whi, wlo = whi_ref[...], wlo_ref[...]
    lg = (jnp.dot(xh, whi, preferred_element_type=F32) + jnp.dot(xh, wlo, preferred_element_type=F32)
          + jnp.dot(xl, whi, preferred_element_type=F32)) + bias_ref[...]
    lane = lax.broadcasted_iota(I32, lg.shape, 1)
    neg = -jnp.inf
    is_group = lane < N_GROUPS
    gl = jnp.where(is_group, lg, neg)
    gmax = jnp.max(gl, axis=-1, keepdims=True)
    grp = jnp.min(jnp.where(gl == gmax, lane, LANES), axis=-1, keepdims=True)
    g1 = 1.0 / jnp.sum(jnp.where(is_group, jnp.exp(gl - gmax), 0.0), axis=-1, keepdims=True)
    lo = N_GROUPS + grp * EXPERTS_PER_GROUP
    el = jnp.where(jnp.logical_and(lane >= lo, lane < lo + EXPERTS_PER_GROUP), lg, neg)
    v1 = jnp.max(el, axis=-1, keepdims=True)
    i1 = jnp.min(jnp.where(el == v1, lane, LANES), axis=-1, keepdims=True)
    el2 = jnp.where(lane == i1, neg, el)
    v2 = jnp.max(el2, axis=-1, keepdims=True)
    i2 = jnp.min(jnp.where(el2 == v2, lane, LANES), axis=-1, keepdims=True)
    e21 = jnp.exp(v2 - v1)
    den = 1.0 + e21
    ids_ref[...] = jnp.where(lane == 0, i1 - N_GROUPS, jnp.where(lane == 1, i2 - N_GROUPS, 0))
    g0_ref[...] = jnp.broadcast_to(g1 * (1.0 / den), g0_ref.shape)
    g1_ref[...] = jnp.broadcast_to(g1 * (e21 / den), g1_ref.shape)


def _router(x, whi, wlo, bias, *, bm):
    n, d = x.shape
    row = pl.BlockSpec((bm, LANES), lambda i: (i, 0))
    return pl.pallas_call(
        _router_kernel,
        grid=(n // bm,),
        in_specs=[pl.BlockSpec((bm, d), lambda i: (i, 0)),
                  pl.BlockSpec((d, LANES), lambda i: (0, 0)),
                  pl.BlockSpec((d, LANES), lambda i: (0, 0)),
                  pl.BlockSpec((1, LANES), lambda i: (0, 0))],
        out_specs=[row, row, row],
        out_shape=[jax.ShapeDtypeStruct((n, LANES), I32), jax.ShapeDtypeStruct((n, LANES), F32),
                   jax.ShapeDtypeStruct((n, LANES), F32)],
        compiler_params=_params("parallel"),
        name="router",
    )(x, whi, wlo, bias)


def _dispatch_kernel(ppos_ref, x_ref, buf_hbm, xs_hbm, sems, *, bm):
    del buf_hbm
    base = pl.program_id(0) * bm * 2

    def row_copy(a):
        return pltpu.make_async_copy(x_ref.at[pl.ds(a >> 1, 1), :],
                                     xs_hbm.at[pl.ds(ppos_ref[base + a], 1), :], sems.at[a & 1])

    def start(a, c):
        row_copy(a).start()
        return c
    lax.fori_loop(0, 2 * bm, start, 0)

    def wait(a, c):
        row_copy(a).wait()
        return c
    lax.fori_loop(0, 2 * bm, wait, 0)


def _dispatch(x, ppos, buf, *, bm):
    n, d = x.shape
    assert n % bm == 0
    grid_spec = pltpu.PrefetchScalarGridSpec(
        num_scalar_prefetch=1,
        grid=(n // bm,),
        in_specs=[pl.BlockSpec((bm, d), lambda i, pp: (i, 0)), pl.BlockSpec(memory_space=pl.ANY)],
        out_specs=pl.BlockSpec(memory_space=pl.ANY),
        scratch_shapes=[pltpu.SemaphoreType.DMA((2,))],
    )
    return pl.pallas_call(
        functools.partial(_dispatch_kernel, bm=bm),
        grid_spec=grid_spec,
        out_shape=jax.ShapeDtypeStruct(buf.shape, buf.dtype),
        input_output_aliases={2: 0},
        compiler_params=_params("arbitrary"),
        name="dispatch",
    )(ppos, x, buf)


def _expert_kernel(te_ref, nv_ref, x_ref, wg_ref, wu_ref, wd_ref, y_ref):
    t = pl.program_id(0)

    @pl.when(nv_ref[t] > 0)
    def _():
        x = x_ref[...].astype(BF16)
        hg = jnp.dot(x, wg_ref[...], preferred_element_type=F32)
        hu = jnp.dot(x, wu_ref[...], preferred_element_type=F32)
        act = (hg * (1.0 / (1.0 + jnp.exp(-hg)))) * hu
        y_ref[...] = jnp.dot(act.astype(BF16), wd_ref[...], preferred_element_type=F32)

    @pl.when(nv_ref[t] == 0)
    def _():
        y_ref[...] = jnp.zeros(y_ref.shape, F32)


def _experts(xs, wg, wu, wd, layer, tile_e, tile_nvalid, *, bm):
    p, d = xs.shape
    f = wg.shape[-1]
    grid_spec = pltpu.PrefetchScalarGridSpec(
        num_scalar_prefetch=2,
        grid=(p // bm,),
        in_specs=[pl.BlockSpec((bm, d), lambda t, te, nv: (t, 0)),
                  pl.BlockSpec((None, None, d, f), lambda t, te, nv: (layer, te[t], 0, 0)),
                  pl.BlockSpec((None, None, d, f), lambda t, te, nv: (layer, te[t], 0, 0)),
                  pl.BlockSpec((None, None, f, d), lambda t, te, nv: (layer, te[t], 0, 0))],
        out_specs=pl.BlockSpec((bm, d), lambda t, te, nv: (t, 0)),
    )
    return pl.pallas_call(
        _expert_kernel,
        grid_spec=grid_spec,
        out_shape=jax.ShapeDtypeStruct((p, d), F32),
        compiler_params=_params("arbitrary"),
        name="experts",
    )(tile_e, tile_nvalid, xs, wg, wu, wd)


def _expert_plan(eid, *, bm):
    na = eid.shape[0] * 2
    flat_e = eid.reshape(na)
    onehot = (flat_e[:, None] == jnp.arange(N_EXPERTS, dtype=I32)[None, :]).astype(I32)
    csum = jnp.cumsum(onehot, axis=0)
    counts = csum[-1]
    padded = ((counts + bm - 1) // bm) * bm
    pend = jnp.cumsum(padded)
    pstart = pend - padded
    ppos = jnp.sum(onehot * (pstart[None, :] + csum - 1), axis=1).astype(I32)
    total = na + N_EXPERTS * bm
    tile_start = jnp.arange(total // bm, dtype=I32) * bm
    tile_e = jnp.minimum(jnp.sum((tile_start[:, None] >= pend[None, :]).astype(I32), axis=1), N_EXPERTS - 1)
    tile_nvalid = jnp.clip(pstart[tile_e] + counts[tile_e] - tile_start, 0, bm).astype(I32)
    return ppos, tile_e.astype(I32), tile_nvalid, total


def _combine_ln_kernel(ppos_ref, x_ref, g0_ref, g1_ref, y_hbm, g_ref, b_ref, of_ref, ob_ref, ybuf, sem,
                       *, bm, alpha):
    base = pl.program_id(0) * bm * 2
    d = x_ref.shape[-1]

    def row_copy(a):
        return pltpu.make_async_copy(y_hbm.at[pl.ds(ppos_ref[base + a], 1), :],
                                     ybuf.at[a & 1, pl.ds(a >> 1, 1), :], sem.at[a & 1])

    def start(a, c):
        row_copy(a).start()
        return c
    lax.fori_loop(0, 2 * bm, start, 0)

    def wait(a, c):
        row_copy(a).wait()
        return c
    lax.fori_loop(0, 2 * bm, wait, 0)

    reps = d // LANES
    y = (alpha * x_ref[...] + jnp.tile(g0_ref[...], (1, reps)) * ybuf[0]
         + jnp.tile(g1_ref[...], (1, reps)) * ybuf[1])
    out = _layer_norm(y, g_ref[...], b_ref[...])
    of_ref[...] = out
    ob_ref[...] = out.astype(BF16)


def _combine_ln(x, ys, ppos, g0, g1, g, b, *, bm, alpha):
    n, d = x.shape
    row = lambda w: pl.BlockSpec((bm, w), lambda i, pp: (i, 0))
    grid_spec = pltpu.PrefetchScalarGridSpec(
        num_scalar_prefetch=1,
        grid=(n // bm,),
        in_specs=[row(d), row(LANES), row(LANES), pl.BlockSpec(memory_space=pl.ANY),
                  pl.BlockSpec((1, d), lambda i, pp: (0, 0)), pl.BlockSpec((1, d), lambda i, pp: (0, 0))],
        out_specs=[row(d), row(d)],
        scratch_shapes=[pltpu.VMEM((2, bm, d), F32), pltpu.SemaphoreType.DMA((2,))],
    )
    return pl.pallas_call(
        functools.partial(_combine_ln_kernel, bm=bm, alpha=alpha),
        grid_spec=grid_spec,
        out_shape=[jax.ShapeDtypeStruct((n, d), F32), jax.ShapeDtypeStruct((n, d), BF16)],
        compiler_params=_params("arbitrary"),
        name="combine_ln",
    )(ppos, x, g0, g1, ys, g, b)


def _rope_tables(pos):
    posf = pos.astype(F32)[:, None]
    inv128 = ROPE_THETA ** (-jnp.arange(HEAD_DIM // 2, dtype=F32) * 2.0 / HEAD_DIM)
    ang = posf * inv128[None, :]
    cos, sin = jnp.cos(ang), jnp.sin(ang)
    a128 = jnp.concatenate([cos, cos], axis=-1)
    b128 = jnp.concatenate([-sin, sin], axis=-1)
    inv64 = ROPE_THETA ** (-jnp.arange(IDX_DIM // 2, dtype=F32) * 2.0 / IDX_DIM)
    ang = posf * inv64[None, :]
    cos, sin = jnp.cos(ang), jnp.sin(ang)
    zero = jnp.zeros_like(sin)
    a64 = jnp.concatenate([cos, cos, cos, cos], axis=-1)
    b64 = jnp.concatenate([-sin, zero, -sin, zero], axis=-1)
    c64 = jnp.concatenate([zero, sin, zero, sin], axis=-1)
    return (a128, b128), (a64, b64, c64)


def _z_weight(w_in):
    o = _IN_OFF
    d = w_in.shape[0]
    cols = [w_in[:, o[0]:o[4]],
            w_in[:, o[6]:o[9]],
            w_in[:, o[10]:o[12]],
            w_in[:, o[4]:o[5]],
            w_in[:, o[12]:o[13]],
            w_in[:, o[5]:o[6]],
            w_in[:, o[9]:o[10]],
            jnp.zeros((d, LANES - IDX_HEADS - B_HEADS), w_in.dtype)]
    return jnp.concatenate(cols, axis=1).astype(BF16)


def _uq_weight(w):
    r = w.shape[0]
    w3 = w.reshape(r, C_HEADS, C_NOPE + C_ROPE)
    nope = w3[:, :, :C_NOPE].reshape(r, C_HEADS * C_NOPE)
    rope = jnp.pad(w3[:, :, C_NOPE:], ((0, 0), (0, 0), (0, LANES - C_ROPE))).reshape(r, C_HEADS * LANES)
    return jnp.concatenate([nope, rope], axis=1).astype(BF16)


def _pad_lanes(x):
    return jnp.pad(x, [(0, 0)] * (x.ndim - 1) + [(0, LANES - x.shape[-1])])


def _pad_rows(x, rows):
    return jnp.pad(x, [(0, 0), (0, rows - x.shape[1])] + [(0, 0)] * (x.ndim - 2))


def _bf16_head(c):
    bits = lax.bitcast_convert_type(c, I32) & jnp.int32(-65536)
    return lax.bitcast_convert_type(bits, F32)


def _split3(c):
    hi = _bf16_head(c)
    mid = _bf16_head(c - hi)
    lo = c - hi - mid
    return hi.astype(BF16), mid.astype(BF16), lo.astype(BF16)


def _fox_bias_parts(csum_q, csum_k):
    nb, t, h = csum_q.shape
    s = csum_k.shape[1]
    one_q = jnp.ones((nb, t, h), BF16)
    qrows = jnp.stack(list(_split3(csum_q * LOG2E)) + [one_q] * 3, axis=-1)
    qbias = jnp.pad(qrows, ((0, 0), (0, 0), (0, 0), (0, LANES - 6))).reshape(nb, t, h * LANES)
    one_k = jnp.ones((nb, s, h), BF16)
    krows = jnp.stack([one_k] * 3 + [-p for p in _split3(csum_k * LOG2E)], axis=-1)
    kbias = jnp.pad(krows, ((0, 0), (0, 0), (0, 0), (0, LANES - 6))).reshape(nb, s, h * LANES)
    return qbias, kbias


def _pick(n, *cands):
    for c in cands:
        if n % c == 0:
            return c
    return n


def kernel(x_prompt, x_sample, cache_a_k, cache_a_v, cache_a_kidx, cache_b_k, cache_b_v, cache_b_logf,
           cache_c_latent, cache_c_krope, cache_mem_k, cache_mem_v, mem_prompt,
           w_in, b_f, c_q_norm, w_c_uq, c_kv_norm, w_c_uk, w_c_uv, w_o, ln1_g, ln1_b,
           w_mq, w_mk, w_mv, w_mo, ln2_g, ln2_b, w_rg, b_rg, w_re, b_re,
           w_gate, w_up, w_down, ln3_g, ln3_b):
    depth = w_in.shape[0]
    bp, tp, d = x_prompt.shape
    bs, ts, _ = x_sample.shape
    past = cache_a_k.shape[2]
    n_mem = mem_prompt.shape[1]
    assert bp == 1
    n_p, n_s = bp * tp, bs * ts
    n = n_p + n_s
    alpha = (2 * depth) ** 0.25
    s_s = past + ts
    s_pad = -(-s_s // COUNT_ROWS) * COUNT_ROWS

    bm_tok = _pick(n, 1024, 512, 256, 128, 64)
    bm_post = _pick(n, 256, 128, 64)
    bm_ln = _pick(n, 256, 128, 64)
    bq_f = _pick(tp, 1024, 512, 256, 128)
    bq_a = _pick(tp, 512, 256, 128)
    kb_a = _pick(tp, 1024, 512, 256)
    bm_moe = 256

    x = jnp.concatenate([x_prompt.reshape(n_p, d), x_sample.reshape(n_s, d)], axis=0)
    xb = x.astype(BF16)
    pos_p = jnp.arange(tp, dtype=I32)
    pos_s = past + jnp.arange(ts, dtype=I32)
    tabs128, tabs64 = _rope_tables(jnp.concatenate([pos_p, jnp.tile(pos_s, bs)]))

    nq_f = tp // bq_f
    nfull_p = jnp.arange(nq_f, dtype=I32)
    nkv_p = nfull_p + 1
    one = jnp.ones((1,), I32)
    zero = jnp.zeros((1,), I32)
    nvis_p = (((jnp.arange(tp // bq_a, dtype=I32) + 1) * bq_a + kb_a - 1) // kb_a).astype(I32)
    nvis_s = jnp.ones((bs,), I32)

    xs_buf = jnp.zeros((2 * n + N_EXPERTS * bm_moe, d), F32)
    w_o_b = w_o.astype(BF16)
    w_gate_b, w_up_b, w_down_b = w_gate.astype(BF16), w_up.astype(BF16), w_down.astype(BF16)
    state_p = [[] for _ in range(8)]
    state_s = [[] for _ in range(8)]
    mem_k_out, mem_v_out = [], []

    for l in range(depth):
        z = _matmul(xb, _z_weight(w_in[l]), bm=bm_tok, bn=512, out_dtype=F32, name="in_proj")
        bf_row = jnp.zeros((1, LANES), F32).at[0, SM2_BF:SM2_BF + B_HEADS].set(b_f[l].astype(F32))
        aq, ak, aqi, bq_, cqn, clat, sm1, sm2, akb, avb, bkb, bvb = _post(
            z, tabs128, tabs64, bf_row, c_q_norm[l].reshape(1, -1).astype(F32),
            c_kv_norm[l].reshape(1, -1).astype(F32), bm=bm_post)
        cq = _matmul_rope(cqn, _uq_weight(w_c_uq[l]), tabs64, bm=bm_tok, bn=512,
                          first_rope_col=C_HEADS * C_NOPE, out_scale=(C_NOPE + C_ROPE) ** -0.5 * LOG2E,
                          name="c_q_up")
        av = z[:, Z_AV:Z_AV + HEAD_DIM]
        aki = sm1[:, :IDX_DIM]
        ckr = sm1[:, IDX_DIM:]
        blogf = sm2[:, SM2_BF:SM2_BF + B_HEADS]
        bk_ = z[:, Z_BK:Z_BK + B_HEADS * HEAD_DIM]
        bv_ = z[:, Z_BV:Z_BV + B_HEADS * HEAD_DIM]
        rows = (ak, av, aki, bk_.reshape(n, B_HEADS, HEAD_DIM), bv_.reshape(n, B_HEADS, HEAD_DIM),
                blogf, clat, ckr)
        for i, r in enumerate(rows):
            state_p[i].append(r[:n_p].reshape((bp, tp) + r.shape[1:]))
            state_s[i].append(r[n_p:].reshape((bs, ts) + r.shape[1:]))
        grp_p = lambda a: a[:n_p].reshape(bp, tp, -1)
        grp_s = lambda a: a[n_p:].reshape(bs, ts, -1)
        cat_s = lambda c, r: jnp.concatenate([c.reshape(bs, past, -1).astype(r.dtype), grp_s(r)], axis=1)
        all_n = lambda a: a.reshape(1, n, -1)

        def sparse(aq_g, aqi_g, aw_g, ki, k, v, nvis, *, bq, kb, s_real, q_off, name):
            ka = _pad_lanes(ki).astype(BF16)
            kb_ = jnp.concatenate([jnp.zeros_like(ki), ki], axis=-1).astype(BF16)
            out = _sparse_attention(aq_g, aqi_g, aw_g, ka, kb_, k, v, nvis, bq=bq, kb=kb, s_real=s_real,
                                    topk=min(TOPK_MAX, s_real // 4), q_off=q_off, name=name)
            return out.reshape(-1, A_HEADS * HEAD_DIM)

        oa_p = sparse(all_n(aq), all_n(aqi), all_n(sm2), grp_p(aki), grp_p(akb), grp_p(avb), nvis_p,
                      bq=bq_a, kb=kb_a, s_real=tp, q_off=0, name="sparse_prompt")
        oa_s = sparse(grp_s(aq), grp_s(aqi), grp_s(sm2),
                      _pad_rows(cat_s(cache_a_kidx[l], aki), s_pad), _pad_rows(cat_s(cache_a_k[l], akb), s_pad),
                      _pad_rows(cat_s(cache_a_v[l], avb), s_pad), nvis_s,
                      bq=ts, kb=s_pad, s_real=s_s, q_off=past, name="sparse_sample")

        csum_p = jnp.cumsum(grp_p(blogf), axis=1)
        qb_p, kbias_p = _fox_bias_parts(csum_p, csum_p)
        ob_p = _tflash([all_n(bq_), qb_p], (0, 0), [all_n(bkb), kbias_p], (False, False), all_n(bvb), 0,
                       nfull_p, nkv_p, heads=B_HEADS, s=tp, hb=2, bq=bq_f, bk=bq_f, mode="causal", q_off=0,
                       name="fox_prompt")
        csum_s = jnp.cumsum(cat_s(cache_b_logf[l], blogf), axis=1)
        qb_s, kbias_s = _fox_bias_parts(csum_s[:, past:], csum_s)
        ob_s = _tflash([grp_s(bq_), qb_s], (0, 0), [cat_s(cache_b_k[l], bkb), kbias_s], (False, False),
                       cat_s(cache_b_v[l], bvb), 0, zero, one, heads=B_HEADS, s=s_s, hb=2, bq=ts, bk=s_s,
                       mode="causal", q_off=past, name="fox_sample")

        w_ukv = jnp.concatenate([w_c_uk[l], w_c_uv[l]], axis=1).astype(BF16)
        kv_p = _matmul(clat[:n_p], w_ukv, bm=_pick(n_p, 1024, 512, 256), bn=w_ukv.shape[1], out_dtype=BF16,
                       name="c_kv_up_prompt").reshape(bp, tp, -1)
        oc_p = _tflash([all_n(cq), all_n(cq)], (0, C_HEADS),
                       [kv_p, _pad_lanes(grp_p(ckr)).astype(BF16)], (False, True), kv_p, C_HEADS,
                       nfull_p, nkv_p, heads=C_HEADS, s=tp, hb=2, bq=bq_f, bk=bq_f, mode="chunk", q_off=0,
                       name="mla_prompt")
        lat_s = cat_s(cache_c_latent[l], clat)
        kv_s = _matmul(lat_s.reshape(bs * s_s, -1), w_ukv, bm=_pick(bs * s_s, 1024, 512, 256, 128, 64),
                       bn=w_ukv.shape[1], out_dtype=BF16, name="c_kv_up_sample").reshape(bs, s_s, -1)
        oc_s = _tflash([grp_s(cq), grp_s(cq)], (0, C_HEADS),
                       [kv_s, _pad_lanes(cat_s(cache_c_krope[l], ckr)).astype(BF16)], (False, True), kv_s, C_HEADS,
                       zero, one, heads=C_HEADS, s=s_s, hb=2, bq=ts, bk=s_s, mode="chunk", q_off=past,
                       name="mla_sample")

        mixed = jnp.concatenate([
            jnp.concatenate([oa_p, oa_s], axis=0),
            jnp.concatenate([ob_p.reshape(n_p, -1), ob_s.reshape(n_s, -1)], axis=0),
            jnp.concatenate([oc_p.reshape(n_p, -1), oc_s.reshape(n_s, -1)], axis=0)], axis=1)
        x1, x1b = _mm_ln_cols(mixed, w_o_b, l, x, ln1_g[l].reshape(1, d), ln1_b[l].reshape(1, d),
                              bm=_pick(n, 512, 256, 128, 64), bn=_pick(d, 512, 256, 128), alpha=alpha,
                              name="out_proj_ln1")

        mw = MEM_HEADS * MEM_DIM
        w_mkv = jnp.concatenate([w_mk[l], w_mv[l]], axis=1).astype(BF16)
        mkv = _matmul(mem_prompt.reshape(bp * n_mem, d), w_mkv, bm=_pick(bp * n_mem, 256, 128, 64), bn=512,
                      out_dtype=F32, name="mem_kv")
        mem_k_out.append(mkv[:, :mw].reshape(bp, n_mem, MEM_HEADS, MEM_DIM))
        mem_v_out.append(mkv[:, mw:].reshape(bp, n_mem, MEM_HEADS, MEM_DIM))
        qm = _matmul(x1b, w_mq[l].astype(BF16), bm=bm_tok, bn=mw, out_dtype=BF16, name="mem_q",
                     out_scale=MEM_DIM ** -0.5 * LOG2E)
        mkv3 = mkv.reshape(bp, n_mem, 2 * mw).astype(BF16)
        om_p = _tflash([all_n(qm)], (0,), [mkv3], (False,), mkv3, MEM_HEADS,
                       jnp.ones((nq_f,), I32), jnp.ones((nq_f,), I32), heads=MEM_HEADS, s=n_mem,
                       hb=2, bq=bq_f, bk=n_mem, mode="none", q_off=0, name="mem_prompt")
        om_s = _tflash([grp_s(qm)], (0,), [cache_mem_k[l].reshape(bs, n_mem, mw).astype(BF16)], (False,),
                       cache_mem_v[l].reshape(bs, n_mem, mw).astype(BF16), 0, one, one, heads=MEM_HEADS, s=n_mem,
                       hb=2, bq=ts, bk=n_mem, mode="none", q_off=0, name="mem_sample")
        om = jnp.concatenate([om_p.reshape(n_p, mw), om_s.reshape(n_s, mw)], axis=0)
        x2, _ = _mm_res_ln(om, w_mo[l].astype(BF16), x1, ln2_g[l].reshape(1, d), ln2_b[l].reshape(1, d),
                           bm=bm_ln, bk=mw, alpha=alpha, name="mem_out_ln2")

        w_r = jnp.concatenate([w_rg[l], jnp.transpose(w_re[l], (1, 0, 2)).reshape(d, N_EXPERTS)], axis=1).astype(F32)
        w_r = _pad_lanes(w_r)
        w_r_head = _bf16_head(w_r)
        w_r_hi = w_r_head.astype(BF16)
        w_r_lo = (w_r - w_r_head).astype(BF16)
        b_r = _pad_lanes(jnp.concatenate([b_rg[l], b_re[l].reshape(-1)]).astype(F32).reshape(1, -1))
        ids, g0, g1 = _router(x2, w_r_hi, w_r_lo, b_r, bm=_pick(n, 512, 256, 128, 64))
        ppos, tile_e, tile_nvalid, total = _expert_plan(ids[:, :2], bm=bm_moe)
        assert total == xs_buf.shape[0]
        xs_buf = _dispatch(x2, ppos, xs_buf, bm=_pick(n, 256, 128, 64))
        ys = _experts(xs_buf, w_gate_b, w_up_b, w_down_b, l, tile_e, tile_nvalid, bm=bm_moe)
        x, xb = _combine_ln(x2, ys, ppos, g0, g1, ln3_g[l].reshape(1, d), ln3_b[l].reshape(1, d),
                            bm=_pick(n, 256, 128, 64), alpha=alpha)

    outs = [x[:n_p].reshape(bp, tp, d), x[n_p:].reshape(bs, ts, d)]
    outs += [jnp.stack(c) for c in state_p]
    outs += [jnp.stack(mem_k_out), jnp.stack(mem_v_out)]
    outs += [jnp.stack(c) for c in state_s]
    return tuple(outs)
```

```python
import functools
import math

import jax
import jax.numpy as jnp
import numpy as np
from jax import lax
from jax.experimental import pallas as pl
from jax.experimental.pallas import tpu as pltpu

F32 = jnp.float32
BF16 = jnp.bfloat16
I32 = jnp.int32

LANES = 128
SUBLANES = 8
VMEM_LIMIT = 56 * 1024 * 1024

CHUNK_SHIFT = 6
HEAD_DIM = 128
ROPE_THETA = 10000.0
A_HEADS = 8
IDX_HEADS = 32
IDX_DIM = 64
TOPK_MAX = 256
B_HEADS = 8
C_HEADS = 16
C_Q_RANK = 768
C_KV_RANK = 256
C_NOPE = 128
C_ROPE = 64
C_V = 128
MEM_HEADS = 4
MEM_DIM = 128
N_GROUPS = 4
EXPERTS_PER_GROUP = 4
N_EXPERTS = N_GROUPS * EXPERTS_PER_GROUP
LN_EPS = 1e-5
RMS_EPS = 1e-6
LOG2E = math.log2(math.e)

_IN_SIZES = (A_HEADS * HEAD_DIM, HEAD_DIM, HEAD_DIM, IDX_HEADS * IDX_DIM, IDX_DIM, IDX_HEADS,
             B_HEADS * HEAD_DIM, B_HEADS * HEAD_DIM, B_HEADS * HEAD_DIM, B_HEADS,
             C_Q_RANK, C_KV_RANK, C_ROPE)
_IN_OFF = np.concatenate([[0], np.cumsum(_IN_SIZES)]).astype(int)

Z_AQ = 0
Z_AK = Z_AQ + A_HEADS * HEAD_DIM
Z_AV = Z_AK + HEAD_DIM
Z_AQI = Z_AV + HEAD_DIM
Z_BQ = Z_AQI + IDX_HEADS * IDX_DIM
Z_BK = Z_BQ + B_HEADS * HEAD_DIM
Z_BV = Z_BK + B_HEADS * HEAD_DIM
Z_CQ = Z_BV + B_HEADS * HEAD_DIM
Z_CKV = Z_CQ + C_Q_RANK
Z_SM1 = Z_CKV + C_KV_RANK
Z_SM2 = Z_SM1 + LANES
Z_WIDTH = Z_SM2 + LANES
SM2_BF = IDX_HEADS

M_INIT = -1e30
INT_MIN = -2 ** 31
INT_MAX = 2 ** 31 - 1
SCORE_ROWS = 128
SCORE_COLS = 256
COUNT_ROWS = 256
LN_ROWS = 64


def _params(*sem):
    return pltpu.CompilerParams(dimension_semantics=sem, vmem_limit_bytes=VMEM_LIMIT)


def _mm_kernel(x_ref, w_ref, o_ref, *, out_scale):
    x = x_ref[...].astype(BF16)
    acc = jnp.dot(x, w_ref[...], preferred_element_type=F32)
    if out_scale != 1.0:
        acc = acc * out_scale
    o_ref[...] = acc.astype(o_ref.dtype)


def _matmul(x, w, *, bm, bn, out_dtype, name, out_scale=1.0):
    m, k = x.shape
    n = w.shape[1]
    assert m % bm == 0 and n % bn == 0, (x.shape, w.shape, bm, bn)
    return pl.pallas_call(
        functools.partial(_mm_kernel, out_scale=out_scale),
        grid=(m // bm, n // bn),
        in_specs=[pl.BlockSpec((bm, k), lambda i, j: (i, 0)),
                  pl.BlockSpec((k, bn), lambda i, j: (0, j))],
        out_specs=pl.BlockSpec((bm, bn), lambda i, j: (i, j)),
        out_shape=jax.ShapeDtypeStruct((m, n), out_dtype),
        compiler_params=_params("parallel", "parallel"),
        name=name,
    )(x, w)


def _rope64(x, a, b, c):
    return x * a + pltpu.roll(x, 96, 1) * b + pltpu.roll(x, 32, 1) * c


def _rope128(x, a, b):
    return x * a + pltpu.roll(x, 64, 1) * b


def _mm_rope_kernel(x_ref, w_ref, a_ref, b_ref, c_ref, o_ref, *, first_rope_block, out_scale):
    j = pl.program_id(1)
    acc = jnp.dot(x_ref[...], w_ref[...], preferred_element_type=F32) * out_scale

    @pl.when(j < first_rope_block)
    def _():
        o_ref[...] = acc.astype(o_ref.dtype)

    @pl.when(j >= first_rope_block)
    def _():
        a, b, c = a_ref[...], b_ref[...], c_ref[...]
        for g in range(acc.shape[1] // LANES):
            sl = slice(g * LANES, (g + 1) * LANES)
            o_ref[:, sl] = _rope64(acc[:, sl], a, b, c).astype(o_ref.dtype)


def _matmul_rope(x, w, tabs, *, bm, bn, first_rope_col, out_scale, name):
    m, k = x.shape
    n = w.shape[1]
    assert m % bm == 0 and n % bn == 0 and first_rope_col % bn == 0
    tab_spec = pl.BlockSpec((bm, LANES), lambda i, j: (i, 0))
    return pl.pallas_call(
        functools.partial(_mm_rope_kernel, first_rope_block=first_rope_col // bn, out_scale=out_scale),
        grid=(m // bm, n // bn),
        in_specs=[pl.BlockSpec((bm, k), lambda i, j: (i, 0)),
                  pl.BlockSpec((k, bn), lambda i, j: (0, j)),
                  tab_spec, tab_spec, tab_spec],
        out_specs=pl.BlockSpec((bm, bn), lambda i, j: (i, j)),
        out_shape=jax.ShapeDtypeStruct((m, n), BF16),
        compiler_params=_params("parallel", "parallel"),
        name=name,
    )(x, w, *tabs)


def _post_kernel(z_ref, a128_ref, b128_ref, a64_ref, b64_ref, c64_ref, bf_ref, gq_ref, gkv_ref,
                 aq_ref, ak_ref, aqi_ref, bq_ref, cqn_ref, clat_ref, sm1_ref, sm2_ref,
                 akb_ref, avb_ref, bkb_ref, bvb_ref):
    a128, b128 = a128_ref[...], b128_ref[...]
    a64, b64, c64 = a64_ref[...], b64_ref[...], c64_ref[...]
    qscale = HEAD_DIM ** -0.5 * LOG2E
    for g in range(A_HEADS):
        src = slice(Z_AQ + g * LANES, Z_AQ + (g + 1) * LANES)
        aq_ref[:, g * LANES:(g + 1) * LANES] = (_rope128(z_ref[:, src], a128, b128) * qscale).astype(BF16)
    ak = _rope128(z_ref[:, Z_AK:Z_AK + LANES], a128, b128)
    ak_ref[...] = ak
    akb_ref[...] = ak.astype(BF16)
    avb_ref[...] = z_ref[:, Z_AV:Z_AV + LANES].astype(BF16)
    bkb_ref[...] = z_ref[:, Z_BK:Z_BK + B_HEADS * HEAD_DIM].astype(BF16)
    bvb_ref[...] = z_ref[:, Z_BV:Z_BV + B_HEADS * HEAD_DIM].astype(BF16)
    for g in range(IDX_HEADS * IDX_DIM // LANES):
        src = slice(Z_AQI + g * LANES, Z_AQI + (g + 1) * LANES)
        aqi_ref[:, g * LANES:(g + 1) * LANES] = _rope64(z_ref[:, src], a64, b64, c64).astype(BF16)
    bq_ref[...] = (z_ref[:, Z_BQ:Z_BQ + B_HEADS * HEAD_DIM] * qscale).astype(BF16)
    cq = z_ref[:, Z_CQ:Z_CQ + C_Q_RANK]
    cq = cq * lax.rsqrt(jnp.mean(cq * cq, axis=-1, keepdims=True) + RMS_EPS) * gq_ref[...]
    cqn_ref[...] = cq.astype(BF16)
    ckv = z_ref[:, Z_CKV:Z_CKV + C_KV_RANK]
    clat_ref[...] = ckv * lax.rsqrt(jnp.mean(ckv * ckv, axis=-1, keepdims=True) + RMS_EPS) * gkv_ref[...]
    sm1_ref[...] = _rope64(z_ref[:, Z_SM1:Z_SM1 + LANES], a64, b64, c64)
    s2 = z_ref[:, Z_SM2:Z_SM2 + LANES]
    lane = lax.broadcasted_iota(I32, s2.shape, 1)
    f = s2 + bf_ref[...]
    logf = jnp.minimum(f, 0.0) - jnp.log1p(jnp.exp(-jnp.abs(f)))
    sm2_ref[...] = jnp.where(lane < SM2_BF, s2 * (IDX_HEADS ** -0.5 * IDX_DIM ** -0.5),
                             jnp.where(lane < SM2_BF + B_HEADS, logf, 0.0))


def _post(z, tabs128, tabs64, bf_row, gq, gkv, *, bm):
    n = z.shape[0]
    row = lambda w: pl.BlockSpec((bm, w), lambda i: (i, 0))
    const = lambda w: pl.BlockSpec((1, w), lambda i: (0, 0))
    widths = (A_HEADS * HEAD_DIM, LANES, IDX_HEADS * IDX_DIM, B_HEADS * HEAD_DIM, C_Q_RANK, C_KV_RANK, LANES, LANES,
              LANES, LANES, B_HEADS * HEAD_DIM, B_HEADS * HEAD_DIM)
    dtypes = (BF16, F32, BF16, BF16, BF16, F32, F32, F32, BF16, BF16, BF16, BF16)
    return pl.pallas_call(
        _post_kernel,
        grid=(n // bm,),
        in_specs=[row(Z_WIDTH)] + [row(LANES)] * 5 + [const(LANES), const(C_Q_RANK), const(C_KV_RANK)],
        out_specs=[row(w) for w in widths],
        out_shape=[jax.ShapeDtypeStruct((n, w), d) for w, d in zip(widths, dtypes)],
        compiler_params=_params("parallel"),
        name="post_projection",
    )(z, *tabs128, *tabs64, bf_row, gq, gkv)


def _layer_norm(y, g, b):
    mu = jnp.mean(y, axis=-1, keepdims=True)
    yc = y - mu
    var = jnp.mean(yc * yc, axis=-1, keepdims=True)
    return yc * lax.rsqrt(var + LN_EPS) * g + b


def _mm_res_ln_kernel(x_ref, w_ref, r_ref, g_ref, b_ref, of_ref, ob_ref, acc_ref, *, nk, alpha):
    k = pl.program_id(1)

    @pl.when(k == 0)
    def _():
        acc_ref[...] = jnp.zeros_like(acc_ref)

    acc_ref[...] += jnp.dot(x_ref[...], w_ref[...], preferred_element_type=F32)

    @pl.when(k == nk - 1)
    def _():
        out = _layer_norm(alpha * r_ref[...] + acc_ref[...], g_ref[...], b_ref[...])
        of_ref[...] = out
        ob_ref[...] = out.astype(BF16)


def _mm_res_ln(x, w, res, g, b, *, bm, bk, alpha, name):
    m, k = x.shape
    d = w.shape[1]
    nk = k // bk
    return pl.pallas_call(
        functools.partial(_mm_res_ln_kernel, nk=nk, alpha=alpha),
        grid=(m // bm, nk),
        in_specs=[pl.BlockSpec((bm, bk), lambda i, kk: (i, kk)),
                  pl.BlockSpec((bk, d), lambda i, kk: (kk, 0)),
                  pl.BlockSpec((bm, d), lambda i, kk: (i, 0)),
                  pl.BlockSpec((1, d), lambda i, kk: (0, 0)),
                  pl.BlockSpec((1, d), lambda i, kk: (0, 0))],
        out_specs=[pl.BlockSpec((bm, d), lambda i, kk: (i, 0)),
                   pl.BlockSpec((bm, d), lambda i, kk: (i, 0))],
        out_shape=[jax.ShapeDtypeStruct((m, d), F32), jax.ShapeDtypeStruct((m, d), BF16)],
        scratch_shapes=[pltpu.VMEM((bm, d), F32)],
        compiler_params=_params("parallel", "arbitrary"),
        name=name,
    )(x, w, res, g, b)


def _mm_ln_cols_kernel(*refs, n_parts, nj, bn, alpha, first_rows):
    xa = refs[:n_parts]
    xb = refs[n_parts:2 * n_parts]
    w_ref, ra_ref, rb_ref, g_ref, b_ref, of_ref, ob_ref = refs[2 * n_parts:]
    i = pl.program_id(0)
    j = pl.program_id(1)

    def emit(x_refs, r_ref):
        y = alpha * r_ref[...]
        k0 = 0
        for xr in x_refs:
            y = y + jnp.dot(xr[...], w_ref[k0:k0 + xr.shape[1], :], preferred_element_type=F32)
            k0 += xr.shape[1]
        for jj in range(nj):
            @pl.when(j == jj)
            def _(jj=jj):
                of_ref[:, jj * bn:(jj + 1) * bn] = y

    @pl.when(i < first_rows)
    def _():
        emit(xa, ra_ref)

    @pl.when(i >= first_rows)
    def _():
        emit(xb, rb_ref)

    @pl.when(j == nj - 1)
    def _():
        for r0 in range(0, of_ref.shape[0], LN_ROWS):
            rs = slice(r0, r0 + LN_ROWS)
            out = _layer_norm(of_ref[rs, :], g_ref[...], b_ref[...])
            of_ref[rs, :] = out
            ob_ref[rs, :] = out.astype(BF16)


def _mm_ln_cols(x_first, x_second, w, layer, res_first, res_second, res_second_row0, g, b, *, bm, bn, alpha, name):
    m1, m2 = x_first[0].shape[0], x_second[0].shape[0]
    d = w.shape[-1]
    k = w.shape[-2]
    nj = d // bn
    assert m1 % bm == 0 and m2 % bm == 0 and res_second_row0 % bm == 0 and d % bn == 0
    nb1, nb2, r0b = m1 // bm, m2 // bm, res_second_row0 // bm
    first = lambda i: jnp.minimum(i, nb1 - 1)
    second = lambda i: jnp.maximum(i - nb1, 0)
    once = pl.Buffered(1)
    in_specs = [pl.BlockSpec((bm, xp.shape[1]), lambda i, j: (first(i), 0), pipeline_mode=once) for xp in x_first]
    in_specs += [pl.BlockSpec((bm, xp.shape[1]), lambda i, j: (second(i), 0), pipeline_mode=once)
                 for xp in x_second]
    in_specs += [pl.BlockSpec((None, k, bn), lambda i, j: (layer, 0, j)),
                 pl.BlockSpec((bm, bn), lambda i, j: (first(i), j)),
                 pl.BlockSpec((bm, bn), lambda i, j: (second(i) + r0b, j)),
                 pl.BlockSpec((1, d), lambda i, j: (0, 0)),
                 pl.BlockSpec((1, d), lambda i, j: (0, 0))]
    return pl.pallas_call(
        functools.partial(_mm_ln_cols_kernel, n_parts=len(x_first), nj=nj, bn=bn, alpha=alpha, first_rows=nb1),
        grid=(nb1 + nb2, nj),
        in_specs=in_specs,
        out_specs=[pl.BlockSpec((bm, d), lambda i, j: (i, 0)),
                   pl.BlockSpec((bm, d), lambda i, j: (i, 0))],
        out_shape=[jax.ShapeDtypeStruct((m1 + m2, d), F32), jax.ShapeDtypeStruct((m1 + m2, d), BF16)],
        compiler_params=_params("parallel", "arbitrary"),
        name=name,
    )(*x_first, *x_second, w, res_first, res_second, g, b)


def _tflash_kernel(nfull_ref, nkv_ref, *refs, n_parts, k_shared, hb, bk, mode, q_off):
    refs = list(refs)
    q_refs = [refs.pop(0) for _ in range(n_parts)]
    k_refs = [refs.pop(0) for _ in range(n_parts)]
    v_ref = refs.pop(0)
    o_ref, acc_ref = refs
    qi = pl.program_id(2)
    bq = o_ref.shape[0]
    qs = []
    for h in range(hb):
        hs = slice(h * LANES, (h + 1) * LANES)
        qs.append(jnp.concatenate([qr[:, hs].astype(F32).T.astype(BF16) for qr in q_refs], axis=0))
    acc_ref[...] = jnp.zeros(acc_ref.shape, F32)

    def block(j, carry, masked):
        ms, ls = carry
        off = pl.multiple_of(j * bk, bk)
        if masked:
            kpos = off + lax.broadcasted_iota(I32, (bk, bq), 0)
            qpos = q_off + qi * bq + lax.broadcasted_iota(I32, (bk, bq), 1)
            if mode == "chunk":
                mask = (kpos >> CHUNK_SHIFT) <= (qpos >> CHUNK_SHIFT)
            else:
                mask = kpos <= qpos
        new_m, new_l = [], []
        for h in range(hb):
            parts = []
            for p_i, kr in enumerate(k_refs):
                hs = 0 if k_shared[p_i] else h
                parts.append(kr[pl.ds(off, bk), hs * LANES:(hs + 1) * LANES])
            k = jnp.concatenate(parts, axis=1) if n_parts > 1 else parts[0]
            s = jnp.dot(k, qs[h], preferred_element_type=F32)
            if masked:
                s = jnp.where(mask, s, -jnp.inf)
            m_new = jnp.maximum(ms[h], jnp.max(s, axis=0, keepdims=True))
            p = jnp.exp2(s - m_new)
            alpha = jnp.exp2(ms[h] - m_new)
            new_l.append(alpha * ls[h] + jnp.sum(p, axis=0, keepdims=True))
            new_m.append(m_new)
            v = v_ref[pl.ds(off, bk), h * LANES:(h + 1) * LANES]
            acc_ref[h] = alpha * acc_ref[h] + lax.dot_general(v, p.astype(BF16), (((0,), (0,)), ((), ())),
                                                              preferred_element_type=F32)
        return tuple(new_m), tuple(new_l)

    init = (tuple(jnp.full((1, bq), M_INIT, F32) for _ in range(hb)),
            tuple(jnp.zeros((1, bq), F32) for _ in range(hb)))
    carry = lax.fori_loop(0, nfull_ref[qi], functools.partial(block, masked=False), init)
    if mode != "none":
        carry = lax.fori_loop(nfull_ref[qi], nkv_ref[qi], functools.partial(block, masked=True), carry)
    for h in range(hb):
        o_ref[:, h * LANES:(h + 1) * LANES] = (acc_ref[h] / carry[1][h]).T.astype(o_ref.dtype)


def _tflash(q_parts, q_head0, k_parts, k_shared, v, v_head0, nfull, nkv, *, heads, s, hb, bq, bk, mode, q_off,
            name):
    nb = v.shape[0]
    nq = nfull.shape[0]
    assert heads % hb == 0 and v_head0 % hb == 0 and all(h0 % hb == 0 for h0 in q_head0) and s % bk == 0
    in_specs = []
    for h0 in q_head0:
        in_specs.append(pl.BlockSpec((None, bq, hb * LANES),
                                     lambda b, g, qi, nf, nk, h0=h0: (b, qi, h0 // hb + g)))
    for shared in k_shared:
        if shared:
            in_specs.append(pl.BlockSpec((None, s, LANES), lambda b, g, qi, nf, nk: (b, 0, 0)))
        else:
            in_specs.append(pl.BlockSpec((None, s, hb * LANES), lambda b, g, qi, nf, nk: (b, 0, g)))
    in_specs.append(pl.BlockSpec((None, s, hb * LANES), lambda b, g, qi, nf, nk: (b, 0, v_head0 // hb + g)))
    grid_spec = pltpu.PrefetchScalarGridSpec(
        num_scalar_prefetch=2,
        grid=(nb, heads // hb, nq),
        in_specs=in_specs,
        out_specs=pl.BlockSpec((None, bq, hb * LANES), lambda b, g, qi, nf, nk: (b, qi, g)),
        scratch_shapes=[pltpu.VMEM((hb, LANES, bq), F32)],
    )
    return pl.pallas_call(
        functools.partial(_tflash_kernel, n_parts=len(q_parts), k_shared=tuple(k_shared), hb=hb, bk=bk,
                          mode=mode, q_off=q_off),
        grid_spec=grid_spec,
        out_shape=jax.ShapeDtypeStruct((nb, nq * bq, heads * LANES), BF16),
        compiler_params=_params("parallel", "parallel", "arbitrary"),
        name=name,
    )(nfull, nkv, *q_parts, *k_parts, v)


def _sparse_kernel(nvis_ref, aq_in, aqi_in, aw_in, ka_ref, kb_ref, k_ref, v_ref,
                   o_ref, aq_ref, aqi_ref, aw_ref, key_ref, bias_ref, acc_ref, j_ref,
                   *, nq, kb, s_real, topk, index_bits, q_off):
    b = pl.program_id(0)
    qi = pl.program_id(1)
    nblk = nvis_ref[b * nq + qi]
    bq = aq_in.shape[0]
    cols = min(bq, SCORE_COLS)
    int_min = jnp.int32(INT_MIN)
    qchunk = (q_off + qi * bq + lax.broadcasted_iota(I32, (1, bq), 1)) >> CHUNK_SHIFT

    for g in range(aq_in.shape[1] // LANES):
        gs = slice(g * LANES, (g + 1) * LANES)
        aq_ref[gs, :] = aq_in[:, gs].astype(F32).T.astype(BF16)
    for g in range(aqi_in.shape[1] // LANES):
        gs = slice(g * LANES, (g + 1) * LANES)
        aqi_ref[gs, :] = aqi_in[:, gs].astype(F32).T.astype(BF16)
    aw_ref[...] = aw_in[...].T

    def score_tile(t, carry):
        off = pl.multiple_of(t * SCORE_ROWS, SCORE_ROWS)
        ka = ka_ref[pl.ds(off, SCORE_ROWS), :]
        kbb = kb_ref[pl.ds(off, SCORE_ROWS), :]
        kidx = off + lax.broadcasted_iota(I32, (SCORE_ROWS, cols), 0)
        kchunk = jnp.where(kidx < s_real, kidx >> CHUNK_SHIFT, jnp.int32(INT_MAX))
        for c in range(bq // cols):
            cs = slice(c * cols, (c + 1) * cols)
            acc = jnp.zeros((SCORE_ROWS, cols), F32)
            for hp in range(IDX_HEADS // 2):
                qp = aqi_ref[hp * LANES:(hp + 1) * LANES, cs]
                ze = jnp.dot(ka, qp, preferred_element_type=F32)
                zo = jnp.dot(kbb, qp, preferred_element_type=F32)
                acc = acc + jnp.maximum(ze, 0.0) * aw_ref[2 * hp:2 * hp + 1, cs]
                acc = acc + jnp.maximum(zo, 0.0) * aw_ref[2 * hp + 1:2 * hp + 2, cs]
            bits = pltpu.bitcast(acc, I32)
            key = bits ^ ((bits >> 31) & jnp.int32(INT_MAX))
            key_ref[pl.ds(off, SCORE_ROWS), cs] = jnp.where(kchunk <= qchunk[:, cs], key, int_min)
        return carry

    lax.fori_loop(0, nblk * (kb // SCORE_ROWS), score_tile, 0)

    def count(pred):
        def body(t, c):
            off = pl.multiple_of(t * COUNT_ROWS, COUNT_ROWS)
            kk = key_ref[pl.ds(off, COUNT_ROWS), :]
            idx = off + lax.broadcasted_iota(I32, (COUNT_ROWS, bq), 0)
            hit = jnp.where(pred(kk, idx), 1.0, 0.0)
            return c + jnp.sum(hit.reshape(COUNT_ROWS // SUBLANES, SUBLANES, bq), axis=0)
        c = lax.fori_loop(0, nblk * (kb // COUNT_ROWS), body, jnp.zeros((SUBLANES, bq), F32))
        return jnp.sum(c, axis=0, keepdims=True)

    def bit_step(i, carry):
        t_u, n_keep = carry
        cand_u = t_u | (jnp.int32(1) << (31 - i))
        cand_s = cand_u ^ int_min
        cnt = count(lambda kk, idx: kk >= cand_s)
        take = cnt >= topk
        return jnp.where(take, cand_u, t_u), jnp.where(take, cnt, n_keep)

    t_u, n_ge = lax.fori_loop(0, 32, bit_step, (jnp.zeros((1, bq), I32), jnp.zeros((1, bq), F32)))
    t_s = t_u ^ int_min
    need = jnp.logical_and(n_ge > topk, t_u != 0)
    j_ref[...] = jnp.full(j_ref.shape, INT_MAX, I32)

    @pl.when(jnp.max(jnp.where(need, 1.0, 0.0)) > 0.0)
    def _():
        want = topk - count(lambda kk, idx: kk > t_s)

        def idx_step(i, jc):
            cand = jc | (jnp.int32(1) << (index_bits - 1 - i))
            cnt = count(lambda kk, idx: jnp.logical_and(kk == t_s, idx < cand))
            return jnp.where(cnt < want, cand, jc)
        jc = lax.fori_loop(0, index_bits, idx_step, jnp.zeros((1, bq), I32))
        j_ref[...] = jnp.where(need, jc, jnp.int32(INT_MAX))

    jcut = j_ref[...]
    acc_ref[...] = jnp.zeros(acc_ref.shape, F32)

    def attend_block(j, carry):
        ms, ls = carry
        off = pl.multiple_of(j * kb, kb)
        kk = key_ref[pl.ds(off, kb), :]
        idx = off + lax.broadcasted_iota(I32, (kb, bq), 0)
        sel = jnp.logical_or(kk > t_s, jnp.logical_and(kk == t_s, idx <= jcut))
        sel = jnp.logical_and(sel, kk != int_min)
        bias_ref[...] = jnp.where(sel, 0.0, -jnp.inf)
        kblk = k_ref[pl.ds(off, kb), :]
        vblk = v_ref[pl.ds(off, kb), :]
        new_m, new_l = [], []
        for h in range(A_HEADS):
            hs = slice(h * HEAD_DIM, (h + 1) * HEAD_DIM)
            s = jnp.dot(kblk, aq_ref[hs, :], preferred_element_type=F32) + bias_ref[...]
            m_new = jnp.maximum(ms[h], jnp.max(s, axis=0, keepdims=True))
            p = jnp.exp2(s - m_new)
            alpha = jnp.exp2(ms[h] - m_new)
            new_l.append(alpha * ls[h] + jnp.sum(p, axis=0, keepdims=True))
            new_m.append(m_new)
            acc_ref[hs, :] = alpha * acc_ref[hs, :] + lax.dot_general(
                vblk, p.astype(BF16), (((0,), (0,)), ((), ())), preferred_element_type=F32)
        return tuple(new_m), tuple(new_l)

    init = (tuple(jnp.full((1, bq), M_INIT, F32) for _ in range(A_HEADS)),
            tuple(jnp.zeros((1, bq), F32) for _ in range(A_HEADS)))
    _, ls = lax.fori_loop(0, nblk, attend_block, init)
    for h in range(A_HEADS):
        hs = slice(h * HEAD_DIM, (h + 1) * HEAD_DIM)
        o_ref[:, hs] = (acc_ref[hs, :] / ls[h]).T.astype(o_ref.dtype)


def _sparse_attention(aq, aqi, aw, ka, kb_, k, v, nvis, *, bq, kb, s_real, topk, q_off, name):
    nb, s = k.shape[0], k.shape[1]
    nq = nvis.shape[0] // nb
    assert s % kb == 0 and kb % COUNT_ROWS == 0 and kb % SCORE_ROWS == 0 and bq % min(bq, SCORE_COLS) == 0
    qspec = lambda w: pl.BlockSpec((None, bq, w), lambda b, qi, nv: (b, qi, 0))
    kspec = pl.BlockSpec((None, s, LANES), lambda b, qi, nv: (b, 0, 0))
    aq_w, aqi_w = A_HEADS * HEAD_DIM, IDX_HEADS * IDX_DIM
    grid_spec = pltpu.PrefetchScalarGridSpec(
        num_scalar_prefetch=1,
        grid=(nb, nq),
        in_specs=[qspec(aq_w), qspec(aqi_w), qspec(LANES), kspec, kspec, kspec, kspec],
        out_specs=qspec(aq_w),
        scratch_shapes=[pltpu.VMEM((aq_w, bq), BF16), pltpu.VMEM((aqi_w, bq), BF16), pltpu.VMEM((LANES, bq), F32),
                        pltpu.VMEM((s, bq), I32), pltpu.VMEM((kb, bq), F32),
                        pltpu.VMEM((aq_w, bq), F32), pltpu.VMEM((1, bq), I32)],
    )
    return pl.pallas_call(
        functools.partial(_sparse_kernel, nq=nq, kb=kb, s_real=s_real, topk=topk,
                          index_bits=max(1, (s - 1).bit_length()), q_off=q_off),
        grid_spec=grid_spec,
        out_shape=jax.ShapeDtypeStruct((nb, nq * bq, aq_w), BF16),
        compiler_params=_params("parallel", "arbitrary"),
        name=name,
    )(nvis, aq, aqi, aw, ka, kb_, k, v)


def _router_kernel(x_ref, whi_ref, wlo_ref, bias_ref, ids_ref, g0_ref, g1_ref):
    x = x_ref[...]
    xh = x.astype(BF16)
    xl = (x - xh.astype(F32)).astype(BF16)
    whi, wlo = whi_ref[...], wlo_ref[...]
    lg = (jnp.dot(xh, whi, preferred_element_type=F32) + jnp.dot(xh, wlo, preferred_element_type=F32)
          + jnp.dot(xl, whi, preferred_element_type=F32)) + bias_ref[...]
    lane = lax.broadcasted_iota(I32, lg.shape, 1)
    neg = -jnp.inf
    is_group = lane < N_GROUPS
    gl = jnp.where(is_group, lg, neg)
    gmax = jnp.max(gl, axis=-1, keepdims=True)
    grp = jnp.min(jnp.where(gl == gmax, lane, LANES), axis=-1, keepdims=True)
    g1 = 1.0 / jnp.sum(jnp.where(is_group, jnp.exp(gl - gmax), 0.0), axis=-1, keepdims=True)
    lo = N_GROUPS + grp * EXPERTS_PER_GROUP
    el = jnp.where(jnp.logical_and(lane >= lo, lane < lo + EXPERTS_PER_GROUP), lg, neg)
    v1 = jnp.max(el, axis=-1, keepdims=True)
    i1 = jnp.min(jnp.where(el == v1, lane, LANES), axis=-1, keepdims=True)
    el2 = jnp.where(lane == i1, neg, el)
    v2 = jnp.max(el2, axis=-1, keepdims=True)
    i2 = jnp.min(jnp.where(el2 == v2, lane, LANES), axis=-1, keepdims=True)
    e21 = jnp.exp(v2 - v1)
    den = 1.0 + e21
    ids_ref[...] = jnp.where(lane == 0, i1 - N_GROUPS, jnp.where(lane == 1, i2 - N_GROUPS, 0))
    g0_ref[...] = jnp.broadcast_to(g1 * (1.0 / den), g0_ref.shape)
    g1_ref[...] = jnp.broadcast_to(g1 * (e21 / den), g1_ref.shape)


def _router(x, whi, wlo, bias, *, bm):
    n, d = x.shape
    row = pl.BlockSpec((bm, LANES), lambda i: (i, 0))
    return pl.pallas_call(
        _router_kernel,
        grid=(n // bm,),
        in_specs=[pl.BlockSpec((bm, d), lambda i: (i, 0)),
                  pl.BlockSpec((d, LANES), lambda i: (0, 0)),
                  pl.BlockSpec((d, LANES), lambda i: (0, 0)),
                  pl.BlockSpec((1, LANES), lambda i: (0, 0))],
        out_specs=[row, row, row],
        out_shape=[jax.ShapeDtypeStruct((n, LANES), I32), jax.ShapeDtypeStruct((n, LANES), F32),
                   jax.ShapeDtypeStruct((n, LANES), F32)],
        compiler_params=_params("parallel"),
        name="router",
    )(x, whi, wlo, bias)


def _dispatch_kernel(ppos_ref, x_ref, buf_hbm, xs_hbm, sems, *, bm):
    del buf_hbm
    base = pl.program_id(0) * bm * 2

    def row_copy(a):
        return pltpu.make_async_copy(x_ref.at[pl.ds(a >> 1, 1), :],
                                     xs_hbm.at[pl.ds(ppos_ref[base + a], 1), :], sems.at[a & 1])

    def start(a, c):
        row_copy(a).start()
        return c
    lax.fori_loop(0, 2 * bm, start, 0)

    def wait(a, c):
        row_copy(a).wait()
        return c
    lax.fori_loop(0, 2 * bm, wait, 0)


def _dispatch(x, ppos, buf, *, bm):
    n, d = x.shape
    assert n % bm == 0
    grid_spec = pltpu.PrefetchScalarGridSpec(
        num_scalar_prefetch=1,
        grid=(n // bm,),
        in_specs=[pl.BlockSpec((bm, d), lambda i, pp: (i, 0)), pl.BlockSpec(memory_space=pl.ANY)],
        out_specs=pl.BlockSpec(memory_space=pl.ANY),
        scratch_shapes=[pltpu.SemaphoreType.DMA((2,))],
    )
    return pl.pallas_call(
        functools.partial(_dispatch_kernel, bm=bm),
        grid_spec=grid_spec,
        out_shape=jax.ShapeDtypeStruct(buf.shape, buf.dtype),
        input_output_aliases={2: 0},
        compiler_params=_params("arbitrary"),
        name="dispatch",
    )(ppos, x, buf)


def _expert_kernel(te_ref, nv_ref, x_ref, wg_ref, wu_ref, wd_ref, y_ref):
    t = pl.program_id(0)

    @pl.when(nv_ref[t] > 0)
    def _():
        x = x_ref[...].astype(BF16)
        hg = jnp.dot(x, wg_ref[...], preferred_element_type=F32)
        hu = jnp.dot(x, wu_ref[...], preferred_element_type=F32)
        act = (hg * (1.0 / (1.0 + jnp.exp(-hg)))) * hu
        y_ref[...] = jnp.dot(act.astype(BF16), wd_ref[...], preferred_element_type=F32)

    @pl.when(nv_ref[t] == 0)
    def _():
        y_ref[...] = jnp.zeros(y_ref.shape, F32)


def _experts(xs, wg, wu, wd, layer, tile_e, tile_nvalid, *, bm):
    p, d = xs.shape
    f = wg.shape[-1]
    grid_spec = pltpu.PrefetchScalarGridSpec(
        num_scalar_prefetch=2,
        grid=(p // bm,),
        in_specs=[pl.BlockSpec((bm, d), lambda t, te, nv: (t, 0)),
                  pl.BlockSpec((None, None, d, f), lambda t, te, nv: (layer, te[t], 0, 0)),
                  pl.BlockSpec((None, None, d, f), lambda t, te, nv: (layer, te[t], 0, 0)),
                  pl.BlockSpec((None, None, f, d), lambda t, te, nv: (layer, te[t], 0, 0))],
        out_specs=pl.BlockSpec((bm, d), lambda t, te, nv: (t, 0)),
    )
    return pl.pallas_call(
        _expert_kernel,
        grid_spec=grid_spec,
        out_shape=jax.ShapeDtypeStruct((p, d), F32),
        compiler_params=_params("arbitrary"),
        name="experts",
    )(tile_e, tile_nvalid, xs, wg, wu, wd)


def _expert_plan(eid, *, bm):
    na = eid.shape[0] * 2
    flat_e = eid.reshape(na)
    onehot = (flat_e[:, None] == jnp.arange(N_EXPERTS, dtype=I32)[None, :]).astype(I32)
    csum = jnp.cumsum(onehot, axis=0)
    counts = csum[-1]
    padded = ((counts + bm - 1) // bm) * bm
    pend = jnp.cumsum(padded)
    pstart = pend - padded
    ppos = jnp.sum(onehot * (pstart[None, :] + csum - 1), axis=1).astype(I32)
    total = na + N_EXPERTS * bm
    tile_start = jnp.arange(total // bm, dtype=I32) * bm
    tile_e = jnp.minimum(jnp.sum((tile_start[:, None] >= pend[None, :]).astype(I32), axis=1), N_EXPERTS - 1)
    tile_nvalid = jnp.clip(pstart[tile_e] + counts[tile_e] - tile_start, 0, bm).astype(I32)
    return ppos, tile_e.astype(I32), tile_nvalid, total


def _combine_ln_kernel(ppos_ref, x_ref, g0_ref, g1_ref, y_hbm, g_ref, b_ref, of_ref, ob_ref, ybuf, sem,
                       *, bm, alpha):
    base = pl.program_id(0) * bm * 2
    d = x_ref.shape[-1]

    def row_copy(a):
        return pltpu.make_async_copy(y_hbm.at[pl.ds(ppos_ref[base + a], 1), :],
                                     ybuf.at[a & 1, pl.ds(a >> 1, 1), :], sem.at[a & 1])

    def start(a, c):
        row_copy(a).start()
        return c
    lax.fori_loop(0, 2 * bm, start, 0)

    def wait(a, c):
        row_copy(a).wait()
        return c
    lax.fori_loop(0, 2 * bm, wait, 0)

    reps = d // LANES
    y = (alpha * x_ref[...] + jnp.tile(g0_ref[...], (1, reps)) * ybuf[0]
         + jnp.tile(g1_ref[...], (1, reps)) * ybuf[1])
    out = _layer_norm(y, g_ref[...], b_ref[...])
    of_ref[...] = out
    ob_ref[...] = out.astype(BF16)


def _combine_ln(x, ys, ppos, g0, g1, g, b, *, bm, alpha):
    n, d = x.shape
    row = lambda w: pl.BlockSpec((bm, w), lambda i, pp: (i, 0))
    grid_spec = pltpu.PrefetchScalarGridSpec(
        num_scalar_prefetch=1,
        grid=(n // bm,),
        in_specs=[row(d), row(LANES), row(LANES), pl.BlockSpec(memory_space=pl.ANY),
                  pl.BlockSpec((1, d), lambda i, pp: (0, 0)), pl.BlockSpec((1, d), lambda i, pp: (0, 0))],
        out_specs=[row(d), row(d)],
        scratch_shapes=[pltpu.VMEM((2, bm, d), F32), pltpu.SemaphoreType.DMA((2,))],
    )
    return pl.pallas_call(
        functools.partial(_combine_ln_kernel, bm=bm, alpha=alpha),
        grid_spec=grid_spec,
        out_shape=[jax.ShapeDtypeStruct((n, d), F32), jax.ShapeDtypeStruct((n, d), BF16)],
        compiler_params=_params("arbitrary"),
        name="combine_ln",
    )(ppos, x, g0, g1, ys, g, b)


def _rope_tables(pos):
    posf = pos.astype(F32)[:, None]
    inv128 = ROPE_THETA ** (-jnp.arange(HEAD_DIM // 2, dtype=F32) * 2.0 / HEAD_DIM)
    ang = posf * inv128[None, :]
    cos, sin = jnp.cos(ang), jnp.sin(ang)
    a128 = jnp.concatenate([cos, cos], axis=-1)
    b128 = jnp.concatenate([-sin, sin], axis=-1)
    inv64 = ROPE_THETA ** (-jnp.arange(IDX_DIM // 2, dtype=F32) * 2.0 / IDX_DIM)
    ang = posf * inv64[None, :]
    cos, sin = jnp.cos(ang), jnp.sin(ang)
    zero = jnp.zeros_like(sin)
    a64 = jnp.concatenate([cos, cos, cos, cos], axis=-1)
    b64 = jnp.concatenate([-sin, zero, -sin, zero], axis=-1)
    c64 = jnp.concatenate([zero, sin, zero, sin], axis=-1)
    return (a128, b128), (a64, b64, c64)


def _z_weight(w_in):
    o = _IN_OFF
    d = w_in.shape[0]
    cols = [w_in[:, o[0]:o[4]],
            w_in[:, o[6]:o[9]],
            w_in[:, o[10]:o[12]],
            w_in[:, o[4]:o[5]],
            w_in[:, o[12]:o[13]],
            w_in[:, o[5]:o[6]],
            w_in[:, o[9]:o[10]],
            jnp.zeros((d, LANES - IDX_HEADS - B_HEADS), w_in.dtype)]
    return jnp.concatenate(cols, axis=1).astype(BF16)


def _uq_weight(w):
    r = w.shape[0]
    w3 = w.reshape(r, C_HEADS, C_NOPE + C_ROPE)
    nope = w3[:, :, :C_NOPE].reshape(r, C_HEADS * C_NOPE)
    rope = jnp.pad(w3[:, :, C_NOPE:], ((0, 0), (0, 0), (0, LANES - C_ROPE))).reshape(r, C_HEADS * LANES)
    return jnp.concatenate([nope, rope], axis=1).astype(BF16)


def _pad_lanes(x):
    return jnp.pad(x, [(0, 0)] * (x.ndim - 1) + [(0, LANES - x.shape[-1])])


def _pad_rows(x, rows):
    return jnp.pad(x, [(0, 0), (0, rows - x.shape[1])] + [(0, 0)] * (x.ndim - 2))


def _bf16_head(c):
    bits = lax.bitcast_convert_type(c, I32) & jnp.int32(-65536)
    return lax.bitcast_convert_type(bits, F32)


def _split3(c):
    hi = _bf16_head(c)
    mid = _bf16_head(c - hi)
    lo = c - hi - mid
    return hi.astype(BF16), mid.astype(BF16), lo.astype(BF16)


def _fox_bias_parts(csum_q, csum_k):
    nb, t, h = csum_q.shape
    s = csum_k.shape[1]
    one_q = jnp.ones((nb, t, h), BF16)
    qrows = jnp.stack(list(_split3(csum_q * LOG2E)) + [one_q] * 3, axis=-1)
    qbias = jnp.pad(qrows, ((0, 0), (0, 0), (0, 0), (0, LANES - 6))).reshape(nb, t, h * LANES)
    one_k = jnp.ones((nb, s, h), BF16)
    krows = jnp.stack([one_k] * 3 + [-p for p in _split3(csum_k * LOG2E)], axis=-1)
    kbias = jnp.pad(krows, ((0, 0), (0, 0), (0, 0), (0, LANES - 6))).reshape(nb, s, h * LANES)
    return qbias, kbias


def _pick(n, *cands):
    for c in cands:
        if n % c == 0:
            return c
    return n


def kernel(x_prompt, x_sample, cache_a_k, cache_a_v, cache_a_kidx, cache_b_k, cache_b_v, cache_b_logf,
           cache_c_latent, cache_c_krope, cache_mem_k, cache_mem_v, mem_prompt,
           w_in, b_f, c_q_norm, w_c_uq, c_kv_norm, w_c_uk, w_c_uv, w_o, ln1_g, ln1_b,
           w_mq, w_mk, w_mv, w_mo, ln2_g, ln2_b, w_rg, b_rg, w_re, b_re,
           w_gate, w_up, w_down, ln3_g, ln3_b):
    depth = w_in.shape[0]
    bp, tp, d = x_prompt.shape
    bs, ts, _ = x_sample.shape
    past = cache_a_k.shape[2]
    n_mem = mem_prompt.shape[1]
    assert bp == 1
    n_p, n_s = bp * tp, bs * ts
    n = n_p + n_s
    alpha = (2 * depth) ** 0.25
    s_s = past + ts
    s_pad = -(-s_s // COUNT_ROWS) * COUNT_ROWS

    bm_tok = _pick(n, 1024, 512, 256, 128, 64)
    bm_post = _pick(n, 256, 128, 64)
    bm_ln = _pick(n, 256, 128, 64)
    bq_f = _pick(tp, 1024, 512, 256, 128)
    bq_a = _pick(tp, 512, 256, 128)
    kb_a = _pick(tp, 1024, 512, 256)
    bm_moe = 256

    res_first, res_second, res_second_row0 = x_prompt.reshape(n_p, d), x_sample.reshape(n_s, d), 0
    xb = jnp.concatenate([res_first.astype(BF16), res_second.astype(BF16)], axis=0)
    pos_p = jnp.arange(tp, dtype=I32)
    pos_s = past + jnp.arange(ts, dtype=I32)
    tabs128, tabs64 = _rope_tables(jnp.concatenate([pos_p, jnp.tile(pos_s, bs)]))

    nq_f = tp // bq_f
    nfull_p = jnp.arange(nq_f, dtype=I32)
    nkv_p = nfull_p + 1
    one = jnp.ones((1,), I32)
    zero = jnp.zeros((1,), I32)
    nvis_p = (((jnp.arange(tp // bq_a, dtype=I32) + 1) * bq_a + kb_a - 1) // kb_a).astype(I32)
    nvis_s = jnp.ones((bs,), I32)

    xs_buf = jnp.zeros((2 * n + N_EXPERTS * bm_moe, d), F32)
    w_o_b = w_o.astype(BF16)
    w_gate_b, w_up_b, w_down_b = w_gate.astype(BF16), w_up.astype(BF16), w_down.astype(BF16)
    state_p = [[] for _ in range(8)]
    state_s = [[] for _ in range(8)]
    mem_k_out, mem_v_out = [], []

    for l in range(depth):
        z = _matmul(xb, _z_weight(w_in[l]), bm=bm_tok, bn=512, out_dtype=F32, name="in_proj")
        bf_row = jnp.zeros((1, LANES), F32).at[0, SM2_BF:SM2_BF + B_HEADS].set(b_f[l].astype(F32))
        aq, ak, aqi, bq_, cqn, clat, sm1, sm2, akb, avb, bkb, bvb = _post(
            z, tabs128, tabs64, bf_row, c_q_norm[l].reshape(1, -1).astype(F32),
            c_kv_norm[l].reshape(1, -1).astype(F32), bm=bm_post)
        cq = _matmul_rope(cqn, _uq_weight(w_c_uq[l]), tabs64, bm=bm_tok, bn=512,
                          first_rope_col=C_HEADS * C_NOPE, out_scale=(C_NOPE + C_ROPE) ** -0.5 * LOG2E,
                          name="c_q_up")
        av = z[:, Z_AV:Z_AV + HEAD_DIM]
        aki = sm1[:, :IDX_DIM]
        ckr = sm1[:, IDX_DIM:]
        blogf = sm2[:, SM2_BF:SM2_BF + B_HEADS]
        bk_ = z[:, Z_BK:Z_BK + B_HEADS * HEAD_DIM]
        bv_ = z[:, Z_BV:Z_BV + B_HEADS * HEAD_DIM]
        rows = (ak, av, aki, bk_.reshape(n, B_HEADS, HEAD_DIM), bv_.reshape(n, B_HEADS, HEAD_DIM),
                blogf, clat, ckr)
        for i, r in enumerate(rows):
            state_p[i].append(r[:n_p].reshape((bp, tp) + r.shape[1:]))
            state_s[i].append(r[n_p:].reshape((bs, ts) + r.shape[1:]))
        grp_p = lambda a: a[:n_p].reshape(bp, tp, -1)
        grp_s = lambda a: a[n_p:].reshape(bs, ts, -1)
        cat_s = lambda c, r: jnp.concatenate([c.reshape(bs, past, -1).astype(r.dtype), grp_s(r)], axis=1)
        all_n = lambda a: a.reshape(1, n, -1)

        def sparse(aq_g, aqi_g, aw_g, ki, k, v, nvis, *, bq, kb, s_real, q_off, name):
            ka = _pad_lanes(ki).astype(BF16)
            kb_ = jnp.concatenate([jnp.zeros_like(ki), ki], axis=-1).astype(BF16)
            out = _sparse_attention(aq_g, aqi_g, aw_g, ka, kb_, k, v, nvis, bq=bq, kb=kb, s_real=s_real,
                                    topk=min(TOPK_MAX, s_real // 4), q_off=q_off, name=name)
            return out.reshape(-1, A_HEADS * HEAD_DIM)

        oa_p = sparse(all_n(aq), all_n(aqi), all_n(sm2), grp_p(aki), grp_p(akb), grp_p(avb), nvis_p,
                      bq=bq_a, kb=kb_a, s_real=tp, q_off=0, name="sparse_prompt")
        oa_s = sparse(grp_s(aq), grp_s(aqi), grp_s(sm2),
                      _pad_rows(cat_s(cache_a_kidx[l], aki), s_pad), _pad_rows(cat_s(cache_a_k[l], akb), s_pad),
                      _pad_rows(cat_s(cache_a_v[l], avb), s_pad), nvis_s,
                      bq=ts, kb=s_pad, s_real=s_s, q_off=past, name="sparse_sample")

        csum_p = jnp.cumsum(grp_p(blogf), axis=1)
        qb_p, kbias_p = _fox_bias_parts(csum_p, csum_p)
        ob_p = _tflash([all_n(bq_), qb_p], (0, 0), [all_n(bkb), kbias_p], (False, False), all_n(bvb), 0,
                       nfull_p, nkv_p, heads=B_HEADS, s=tp, hb=2, bq=bq_f, bk=bq_f, mode="causal", q_off=0,
                       name="fox_prompt")
        csum_s = jnp.cumsum(cat_s(cache_b_logf[l], blogf), axis=1)
        qb_s, kbias_s = _fox_bias_parts(csum_s[:, past:], csum_s)
        ob_s = _tflash([grp_s(bq_), qb_s], (0, 0), [cat_s(cache_b_k[l], bkb), kbias_s], (False, False),
                       cat_s(cache_b_v[l], bvb), 0, zero, one, heads=B_HEADS, s=s_s, hb=2, bq=ts, bk=s_s,
                       mode="causal", q_off=past, name="fox_sample")

        w_ukv = jnp.concatenate([w_c_uk[l], w_c_uv[l]], axis=1).astype(BF16)
        kv_p = _matmul(clat[:n_p], w_ukv, bm=_pick(n_p, 1024, 512, 256), bn=w_ukv.shape[1], out_dtype=BF16,
                       name="c_kv_up_prompt").reshape(bp, tp, -1)
        oc_p = _tflash([all_n(cq), all_n(cq)], (0, C_HEADS),
                       [kv_p, _pad_lanes(grp_p(ckr)).astype(BF16)], (False, True), kv_p, C_HEADS,
                       nfull_p, nkv_p, heads=C_HEADS, s=tp, hb=2, bq=bq_f, bk=bq_f, mode="chunk", q_off=0,
                       name="mla_prompt")
        lat_s = cat_s(cache_c_latent[l], clat)
        kv_s = _matmul(lat_s.reshape(bs * s_s, -1), w_ukv, bm=_pick(bs * s_s, 1024, 512, 256, 128, 64),
                       bn=w_ukv.shape[1], out_dtype=BF16, name="c_kv_up_sample").reshape(bs, s_s, -1)
        oc_s = _tflash([grp_s(cq), grp_s(cq)], (0, C_HEADS),
                       [kv_s, _pad_lanes(cat_s(cache_c_krope[l], ckr)).astype(BF16)], (False, True), kv_s, C_HEADS,
                       zero, one, heads=C_HEADS, s=s_s, hb=2, bq=ts, bk=s_s, mode="chunk", q_off=past,
                       name="mla_sample")

        x1, x1b = _mm_ln_cols([oa_p, ob_p.reshape(n_p, -1), oc_p.reshape(n_p, -1)],
                              [oa_s, ob_s.reshape(n_s, -1), oc_s.reshape(n_s, -1)],
                              w_o_b, l, res_first, res_second, res_second_row0,
                              ln1_g[l].reshape(1, d), ln1_b[l].reshape(1, d),
                              bm=_pick(math.gcd(n_p, n_s), 512, 256, 128, 64), bn=_pick(d, 512, 256, 128),
                              alpha=alpha, name="out_proj_ln1")

        mw = MEM_HEADS * MEM_DIM
        w_mkv = jnp.concatenate([w_mk[l], w_mv[l]], axis=1).astype(BF16)
        mkv = _matmul(mem_prompt.reshape(bp * n_mem, d), w_mkv, bm=_pick(bp * n_mem, 256, 128, 64), bn=512,
                      out_dtype=F32, name="mem_kv")
        mem_k_out.append(mkv[:, :mw].reshape(bp, n_mem, MEM_HEADS, MEM_DIM))
        mem_v_out.append(mkv[:, mw:].reshape(bp, n_mem, MEM_HEADS, MEM_DIM))
        qm = _matmul(x1b, w_mq[l].astype(BF16), bm=bm_tok, bn=mw, out_dtype=BF16, name="mem_q",
                     out_scale=MEM_DIM ** -0.5 * LOG2E)
        mkv3 = mkv.reshape(bp, n_mem, 2 * mw).astype(BF16)
        om_p = _tflash([all_n(qm)], (0,), [mkv3], (False,), mkv3, MEM_HEADS,
                       jnp.ones((nq_f,), I32), jnp.ones((nq_f,), I32), heads=MEM_HEADS, s=n_mem,
                       hb=2, bq=bq_f, bk=n_mem, mode="none", q_off=0, name="mem_prompt")
        om_s = _tflash([grp_s(qm)], (0,), [cache_mem_k[l].reshape(bs, n_mem, mw).astype(BF16)], (False,),
                       cache_mem_v[l].reshape(bs, n_mem, mw).astype(BF16), 0, one, one, heads=MEM_HEADS, s=n_mem,
                       hb=2, bq=ts, bk=n_mem, mode="none", q_off=0, name="mem_sample")
        om = jnp.concatenate([om_p.reshape(n_p, mw), om_s.reshape(n_s, mw)], axis=0)
        x2, _ = _mm_res_ln(om, w_mo[l].astype(BF16), x1, ln2_g[l].reshape(1, d), ln2_b[l].reshape(1, d),
                           bm=bm_ln, bk=mw, alpha=alpha, name="mem_out_ln2")

        w_r = jnp.concatenate([w_rg[l], jnp.transpose(w_re[l], (1, 0, 2)).reshape(d, N_EXPERTS)], axis=1).astype(F32)
        w_r = _pad_lanes(w_r)
        w_r_head = _bf16_head(w_r)
        w_r_hi = w_r_head.astype(BF16)
        w_r_lo = (w_r - w_r_head).astype(BF16)
        b_r = _pad_lanes(jnp.concatenate([b_rg[l], b_re[l].reshape(-1)]).astype(F32).reshape(1, -1))
        ids, g0, g1 = _router(x2, w_r_hi, w_r_lo, b_r, bm=_pick(n, 512, 256, 128, 64))
        ppos, tile_e, tile_nvalid, total = _expert_plan(ids[:, :2], bm=bm_moe)
        assert total == xs_buf.shape[0]
        xs_buf = _dispatch(x2, ppos, xs_buf, bm=_pick(n, 256, 128, 64))
        ys = _experts(xs_buf, w_gate_b, w_up_b, w_down_b, l, tile_e, tile_nvalid, bm=bm_moe)
        x, xb = _combine_ln(x2, ys, ppos, g0, g1, ln3_g[l].reshape(1, d), ln3_b[l].reshape(1, d),
                            bm=_pick(n, 256, 128, 64), alpha=alpha)
        res_first, res_second, res_second_row0 = x, x, n_p

    outs = [x[:n_p].reshape(bp, tp, d), x[n_p:].reshape(bs, ts, d)]
    outs += [jnp.stack(c) for c in state_p]
    outs += [jnp.stack(mem_k_out), jnp.stack(mem_v_out)]
    outs += [jnp.stack(c) for c in state_s]
    return tuple(outs)
```

```python
import functools
import math

import jax
import jax.numpy as jnp
import numpy as np
from jax import lax
from jax.experimental import pallas as pl
from jax.experimental.pallas import tpu as pltpu

F32 = jnp.float32
BF16 = jnp.bfloat16
I32 = jnp.int32

LANES = 128
SUBLANES = 8
VMEM_LIMIT = 56 * 1024 * 1024

CHUNK_SHIFT = 6
HEAD_DIM = 128
ROPE_THETA = 10000.0
A_HEADS = 8
IDX_HEADS = 32
IDX_DIM = 64
TOPK_MAX = 256
B_HEADS = 8
C_HEADS = 16
C_Q_RANK = 768
C_KV_RANK = 256
C_NOPE = 128
C_ROPE = 64
C_V = 128
MEM_HEADS = 4
MEM_DIM = 128
N_GROUPS = 4
EXPERTS_PER_GROUP = 4
N_EXPERTS = N_GROUPS * EXPERTS_PER_GROUP
LN_EPS = 1e-5
RMS_EPS = 1e-6
LOG2E = math.log2(math.e)

_IN_SIZES = (A_HEADS * HEAD_DIM, HEAD_DIM, HEAD_DIM, IDX_HEADS * IDX_DIM, IDX_DIM, IDX_HEADS,
             B_HEADS * HEAD_DIM, B_HEADS * HEAD_DIM, B_HEADS * HEAD_DIM, B_HEADS,
             C_Q_RANK, C_KV_RANK, C_ROPE)
_IN_OFF = np.concatenate([[0], np.cumsum(_IN_SIZES)]).astype(int)

Z_AQ = 0
Z_AK = Z_AQ + A_HEADS * HEAD_DIM
Z_AV = Z_AK + HEAD_DIM
Z_AQI = Z_AV + HEAD_DIM
Z_BQ = Z_AQI + IDX_HEADS * IDX_DIM
Z_BK = Z_BQ + B_HEADS * HEAD_DIM
Z_BV = Z_BK + B_HEADS * HEAD_DIM
Z_CQ = Z_BV + B_HEADS * HEAD_DIM
Z_CKV = Z_CQ + C_Q_RANK
Z_SM1 = Z_CKV + C_KV_RANK
Z_SM2 = Z_SM1 + LANES
Z_WIDTH = Z_SM2 + LANES
SM2_BF = IDX_HEADS

M_INIT = -1e30
INT_MIN = -2 ** 31
INT_MAX = 2 ** 31 - 1
SCORE_ROWS = 128
SCORE_COLS = 256
COUNT_ROWS = 256
LN_ROWS = 64


def _params(*sem):
    return pltpu.CompilerParams(dimension_semantics=sem, vmem_limit_bytes=VMEM_LIMIT)


def _mm_kernel(x_ref, w_ref, o_ref, *, out_scale):
    x = x_ref[...].astype(BF16)
    acc = jnp.dot(x, w_ref[...], preferred_element_type=F32)
    if out_scale != 1.0:
        acc = acc * out_scale
    o_ref[...] = acc.astype(o_ref.dtype)


def _matmul(x, w, *, bm, bn, out_dtype, name, out_scale=1.0):
    m, k = x.shape
    n = w.shape[1]
    assert m % bm == 0 and n % bn == 0, (x.shape, w.shape, bm, bn)
    return pl.pallas_call(
        functools.partial(_mm_kernel, out_scale=out_scale),
        grid=(m // bm, n // bn),
        in_specs=[pl.BlockSpec((bm, k), lambda i, j: (i, 0)),
                  pl.BlockSpec((k, bn), lambda i, j: (0, j))],
        out_specs=pl.BlockSpec((bm, bn), lambda i, j: (i, j)),
        out_shape=jax.ShapeDtypeStruct((m, n), out_dtype),
        compiler_params=_params("parallel", "parallel"),
        name=name,
    )(x, w)


def _rope64(x, a, b, c):
    return x * a + pltpu.roll(x, 96, 1) * b + pltpu.roll(x, 32, 1) * c


def _rope128(x, a, b):
    return x * a + pltpu.roll(x, 64, 1) * b


def _mm_rope_kernel(x_ref, w_ref, a_ref, b_ref, c_ref, o_ref, *, first_rope_block, out_scale):
    j = pl.program_id(1)
    acc = jnp.dot(x_ref[...], w_ref[...], preferred_element_type=F32) * out_scale

    @pl.when(j < first_rope_block)
    def _():
        o_ref[...] = acc.astype(o_ref.dtype)

    @pl.when(j >= first_rope_block)
    def _():
        a, b, c = a_ref[...], b_ref[...], c_ref[...]
        for g in range(acc.shape[1] // LANES):
            sl = slice(g * LANES, (g + 1) * LANES)
            o_ref[:, sl] = _rope64(acc[:, sl], a, b, c).astype(o_ref.dtype)


def _matmul_rope(x, w, tabs, *, bm, bn, first_rope_col, out_scale, name):
    m, k = x.shape
    n = w.shape[1]
    assert m % bm == 0 and n % bn == 0 and first_rope_col % bn == 0
    tab_spec = pl.BlockSpec((bm, LANES), lambda i, j: (i, 0))
    return pl.pallas_call(
        functools.partial(_mm_rope_kernel, first_rope_block=first_rope_col // bn, out_scale=out_scale),
        grid=(m // bm, n // bn),
        in_specs=[pl.BlockSpec((bm, k), lambda i, j: (i, 0)),
                  pl.BlockSpec((k, bn), lambda i, j: (0, j)),
                  tab_spec, tab_spec, tab_spec],
        out_specs=pl.BlockSpec((bm, bn), lambda i, j: (i, j)),
        out_shape=jax.ShapeDtypeStruct((m, n), BF16),
        compiler_params=_params("parallel", "parallel"),
        name=name,
    )(x, w, *tabs)


def _post_kernel(z_ref, a128_ref, b128_ref, a64_ref, b64_ref, c64_ref, bf_ref, gq_ref, gkv_ref,
                 aq_ref, ak_ref, aqi_ref, bq_ref, cqn_ref, clat_ref, sm1_ref, sm2_ref,
                 akb_ref, avb_ref, bkb_ref, bvb_ref):
    a128, b128 = a128_ref[...], b128_ref[...]
    a64, b64, c64 = a64_ref[...], b64_ref[...], c64_ref[...]
    qscale = HEAD_DIM ** -0.5 * LOG2E
    for g in range(A_HEADS):
        src = slice(Z_AQ + g * LANES, Z_AQ + (g + 1) * LANES)
        aq_ref[:, g * LANES:(g + 1) * LANES] = (_rope128(z_ref[:, src], a128, b128) * qscale).astype(BF16)
    ak = _rope128(z_ref[:, Z_AK:Z_AK + LANES], a128, b128)
    ak_ref[...] = ak
    akb_ref[...] = ak.astype(BF16)
    avb_ref[...] = z_ref[:, Z_AV:Z_AV + LANES].astype(BF16)
    bkb_ref[...] = z_ref[:, Z_BK:Z_BK + B_HEADS * HEAD_DIM].astype(BF16)
    bvb_ref[...] = z_ref[:, Z_BV:Z_BV + B_HEADS * HEAD_DIM].astype(BF16)
    for g in range(IDX_HEADS * IDX_DIM // LANES):
        src = slice(Z_AQI + g * LANES, Z_AQI + (g + 1) * LANES)
        aqi_ref[:, g * LANES:(g + 1) * LANES] = _rope64(z_ref[:, src], a64, b64, c64).astype(BF16)
    bq_ref[...] = (z_ref[:, Z_BQ:Z_BQ + B_HEADS * HEAD_DIM] * qscale).astype(BF16)
    cq = z_ref[:, Z_CQ:Z_CQ + C_Q_RANK]
    cq = cq * lax.rsqrt(jnp.mean(cq * cq, axis=-1, keepdims=True) + RMS_EPS) * gq_ref[...]
    cqn_ref[...] = cq.astype(BF16)
    ckv = z_ref[:, Z_CKV:Z_CKV + C_KV_RANK]
    clat_ref[...] = ckv * lax.rsqrt(jnp.mean(ckv * ckv, axis=-1, keepdims=True) + RMS_EPS) * gkv_ref[...]
    sm1_ref[...] = _rope64(z_ref[:, Z_SM1:Z_SM1 + LANES], a64, b64, c64)
    s2 = z_ref[:, Z_SM2:Z_SM2 + LANES]
    lane = lax.broadcasted_iota(I32, s2.shape, 1)
    f = s2 + bf_ref[...]
    logf = jnp.minimum(f, 0.0) - jnp.log1p(jnp.exp(-jnp.abs(f)))
    sm2_ref[...] = jnp.where(lane < SM2_BF, s2 * (IDX_HEADS ** -0.5 * IDX_DIM ** -0.5),
                             jnp.where(lane < SM2_BF + B_HEADS, logf, 0.0))


def _post(z, tabs128, tabs64, bf_row, gq, gkv, *, bm):
    n = z.shape[0]
    row = lambda w: pl.BlockSpec((bm, w), lambda i: (i, 0))
    const = lambda w: pl.BlockSpec((1, w), lambda i: (0, 0))
    widths = (A_HEADS * HEAD_DIM, LANES, IDX_HEADS * IDX_DIM, B_HEADS * HEAD_DIM, C_Q_RANK, C_KV_RANK, LANES, LANES,
              LANES, LANES, B_HEADS * HEAD_DIM, B_HEADS * HEAD_DIM)
    dtypes = (BF16, F32, BF16, BF16, BF16, F32, F32, F32, BF16, BF16, BF16, BF16)
    return pl.pallas_call(
        _post_kernel,
        grid=(n // bm,),
        in_specs=[row(Z_WIDTH)] + [row(LANES)] * 5 + [const(LANES), const(C_Q_RANK), const(C_KV_RANK)],
        out_specs=[row(w) for w in widths],
        out_shape=[jax.ShapeDtypeStruct((n, w), d) for w, d in zip(widths, dtypes)],
        compiler_params=_params("parallel"),
        name="post_projection",
    )(z, *tabs128, *tabs64, bf_row, gq, gkv)


def _layer_norm(y, g, b):
    mu = jnp.mean(y, axis=-1, keepdims=True)
    yc = y - mu
    var = jnp.mean(yc * yc, axis=-1, keepdims=True)
    return yc * lax.rsqrt(var + LN_EPS) * g + b


def _mm_res_ln_kernel(x_ref, w_ref, r_ref, g_ref, b_ref, of_ref, ob_ref, acc_ref, *, nk, alpha):
    k = pl.program_id(1)

    @pl.when(k == 0)
    def _():
        acc_ref[...] = jnp.zeros_like(acc_ref)

    acc_ref[...] += jnp.dot(x_ref[...], w_ref[...], preferred_element_type=F32)

    @pl.when(k == nk - 1)
    def _():
        out = _layer_norm(alpha * r_ref[...] + acc_ref[...], g_ref[...], b_ref[...])
        of_ref[...] = out
        ob_ref[...] = out.astype(BF16)


def _mm_res_ln(x, w, res, g, b, *, bm, bk, alpha, name):
    m, k = x.shape
    d = w.shape[1]
    nk = k // bk
    return pl.pallas_call(
        functools.partial(_mm_res_ln_kernel, nk=nk, alpha=alpha),
        grid=(m // bm, nk),
        in_specs=[pl.BlockSpec((bm, bk), lambda i, kk: (i, kk)),
                  pl.BlockSpec((bk, d), lambda i, kk: (kk, 0)),
                  pl.BlockSpec((bm, d), lambda i, kk: (i, 0)),
                  pl.BlockSpec((1, d), lambda i, kk: (0, 0)),
                  pl.BlockSpec((1, d), lambda i, kk: (0, 0))],
        out_specs=[pl.BlockSpec((bm, d), lambda i, kk: (i, 0)),
                   pl.BlockSpec((bm, d), lambda i, kk: (i, 0))],
        out_shape=[jax.ShapeDtypeStruct((m, d), F32), jax.ShapeDtypeStruct((m, d), BF16)],
        scratch_shapes=[pltpu.VMEM((bm, d), F32)],
        compiler_params=_params("parallel", "arbitrary"),
        name=name,
    )(x, w, res, g, b)


def _mm_ln_cols_kernel(*refs, n_parts, nj, bn, alpha, first_rows):
    xa = refs[:n_parts]
    xb = refs[n_parts:2 * n_parts]
    w_ref, ra_ref, rb_ref, g_ref, b_ref, of_ref, ob_ref = refs[2 * n_parts:]
    i = pl.program_id(0)
    j = pl.program_id(1)

    def emit(x_refs, r_ref):
        x = jnp.concatenate([xr[...] for xr in x_refs], axis=1)
        y = alpha * r_ref[...] + jnp.dot(x, w_ref[...], preferred_element_type=F32)
        for jj in range(nj):
            @pl.when(j == jj)
            def _(jj=jj):
                of_ref[:, jj * bn:(jj + 1) * bn] = y

    @pl.when(i < first_rows)
    def _():
        emit(xa, ra_ref)

    @pl.when(i >= first_rows)
    def _():
        emit(xb, rb_ref)

    @pl.when(j == nj - 1)
    def _():
        for r0 in range(0, of_ref.shape[0], LN_ROWS):
            rs = slice(r0, r0 + LN_ROWS)
            out = _layer_norm(of_ref[rs, :], g_ref[...], b_ref[...])
            of_ref[rs, :] = out
            ob_ref[rs, :] = out.astype(BF16)


def _mm_ln_cols(x_first, x_second, w, layer, res_first, res_second, res_second_row0, g, b, *, bm, bn, alpha, name):
    m1, m2 = x_first[0].shape[0], x_second[0].shape[0]
    d = w.shape[-1]
    k = w.shape[-2]
    nj = d // bn
    assert m1 % bm == 0 and m2 % bm == 0 and res_second_row0 % bm == 0 and d % bn == 0
    nb1, nb2, r0b = m1 // bm, m2 // bm, res_second_row0 // bm
    first = lambda i: jnp.minimum(i, nb1 - 1)
    second = lambda i: jnp.maximum(i - nb1, 0)
    once = pl.Buffered(1)
    in_specs = [pl.BlockSpec((bm, xp.shape[1]), lambda i, j: (first(i), 0), pipeline_mode=once) for xp in x_first]
    in_specs += [pl.BlockSpec((bm, xp.shape[1]), lambda i, j: (second(i), 0), pipeline_mode=once)
                 for xp in x_second]
    in_specs += [pl.BlockSpec((None, k, bn), lambda i, j: (layer, 0, j)),
                 pl.BlockSpec((bm, bn), lambda i, j: (first(i), j)),
                 pl.BlockSpec((bm, bn), lambda i, j: (second(i) + r0b, j)),
                 pl.BlockSpec((1, d), lambda i, j: (0, 0)),
                 pl.BlockSpec((1, d), lambda i, j: (0, 0))]
    return pl.pallas_call(
        functools.partial(_mm_ln_cols_kernel, n_parts=len(x_first), nj=nj, bn=bn, alpha=alpha, first_rows=nb1),
        grid=(nb1 + nb2, nj),
        in_specs=in_specs,
        out_specs=[pl.BlockSpec((bm, d), lambda i, j: (i, 0)),
                   pl.BlockSpec((bm, d), lambda i, j: (i, 0))],
        out_shape=[jax.ShapeDtypeStruct((m1 + m2, d), F32), jax.ShapeDtypeStruct((m1 + m2, d), BF16)],
        compiler_params=_params("parallel", "arbitrary"),
        name=name,
    )(*x_first, *x_second, w, res_first, res_second, g, b)


def _tflash_kernel(nfull_ref, nkv_ref, *refs, n_parts, k_shared, hb, bk, mode, q_off):
    refs = list(refs)
    q_refs = [refs.pop(0) for _ in range(n_parts)]
    k_refs = [refs.pop(0) for _ in range(n_parts)]
    v_ref = refs.pop(0)
    o_ref, acc_ref = refs
    qi = pl.program_id(2)
    bq = o_ref.shape[0]
    qs = []
    for h in range(hb):
        hs = slice(h * LANES, (h + 1) * LANES)
        qs.append(jnp.concatenate([qr[:, hs].astype(F32).T.astype(BF16) for qr in q_refs], axis=0))
    acc_ref[...] = jnp.zeros(acc_ref.shape, F32)

    def block(j, carry, masked):
        ms, ls = carry
        off = pl.multiple_of(j * bk, bk)
        if masked:
            kpos = off + lax.broadcasted_iota(I32, (bk, bq), 0)
            qpos = q_off + qi * bq + lax.broadcasted_iota(I32, (bk, bq), 1)
            if mode == "chunk":
                mask = (kpos >> CHUNK_SHIFT) <= (qpos >> CHUNK_SHIFT)
            else:
                mask = kpos <= qpos
        new_m, new_l = [], []
        for h in range(hb):
            parts = []
            for p_i, kr in enumerate(k_refs):
                hs = 0 if k_shared[p_i] else h
                parts.append(kr[pl.ds(off, bk), hs * LANES:(hs + 1) * LANES])
            k = jnp.concatenate(parts, axis=1) if n_parts > 1 else parts[0]
            s = jnp.dot(k, qs[h], preferred_element_type=F32)
            if masked:
                s = jnp.where(mask, s, -jnp.inf)
            m_new = jnp.maximum(ms[h], jnp.max(s, axis=0, keepdims=True))
            p = jnp.exp2(s - m_new)
            alpha = jnp.exp2(ms[h] - m_new)
            new_l.append(alpha * ls[h] + jnp.sum(p, axis=0, keepdims=True))
            new_m.append(m_new)
            v = v_ref[pl.ds(off, bk), h * LANES:(h + 1) * LANES]
            acc_ref[h] = alpha * acc_ref[h] + lax.dot_general(v, p.astype(BF16), (((0,), (0,)), ((), ())),
                                                              preferred_element_type=F32)
        return tuple(new_m), tuple(new_l)

    init = (tuple(jnp.full((1, bq), M_INIT, F32) for _ in range(hb)),
            tuple(jnp.zeros((1, bq), F32) for _ in range(hb)))
    carry = lax.fori_loop(0, nfull_ref[qi], functools.partial(block, masked=False), init)
    if mode != "none":
        carry = lax.fori_loop(nfull_ref[qi], nkv_ref[qi], functools.partial(block, masked=True), carry)
    for h in range(hb):
        o_ref[:, h * LANES:(h + 1) * LANES] = (acc_ref[h] / carry[1][h]).T.astype(o_ref.dtype)


def _tflash(q_parts, q_head0, k_parts, k_shared, v, v_head0, nfull, nkv, *, heads, s, hb, bq, bk, mode, q_off,
            name):
    nb = v.shape[0]
    nq = nfull.shape[0]
    assert heads % hb == 0 and v_head0 % hb == 0 and all(h0 % hb == 0 for h0 in q_head0) and s % bk == 0
    in_specs = []
    for h0 in q_head0:
        in_specs.append(pl.BlockSpec((None, bq, hb * LANES),
                                     lambda b, g, qi, nf, nk, h0=h0: (b, qi, h0 // hb + g)))
    for shared in k_shared:
        if shared:
            in_specs.append(pl.BlockSpec((None, s, LANES), lambda b, g, qi, nf, nk: (b, 0, 0)))
        else:
            in_specs.append(pl.BlockSpec((None, s, hb * LANES), lambda b, g, qi, nf, nk: (b, 0, g)))
    in_specs.append(pl.BlockSpec((None, s, hb * LANES), lambda b, g, qi, nf, nk: (b, 0, v_head0 // hb + g)))
    grid_spec = pltpu.PrefetchScalarGridSpec(
        num_scalar_prefetch=2,
        grid=(nb, heads // hb, nq),
        in_specs=in_specs,
        out_specs=pl.BlockSpec((None, bq, hb * LANES), lambda b, g, qi, nf, nk: (b, qi, g)),
        scratch_shapes=[pltpu.VMEM((hb, LANES, bq), F32)],
    )
    return pl.pallas_call(
        functools.partial(_tflash_kernel, n_parts=len(q_parts), k_shared=tuple(k_shared), hb=hb, bk=bk,
                          mode=mode, q_off=q_off),
        grid_spec=grid_spec,
        out_shape=jax.ShapeDtypeStruct((nb, nq * bq, heads * LANES), BF16),
        compiler_params=_params("parallel", "parallel", "arbitrary"),
        name=name,
    )(nfull, nkv, *q_parts, *k_parts, v)


def _sparse_kernel(nvis_ref, aq_in, aqi_in, aw_in, ka_ref, kb_ref, k_ref, v_ref,
                   o_ref, aq_ref, aqi_ref, aw_ref, key_ref, bias_ref, acc_ref, j_ref,
                   *, nq, kb, s_real, topk, index_bits, q_off):
    b = pl.program_id(0)
    qi = pl.program_id(1)
    nblk = nvis_ref[b * nq + qi]
    bq = aq_in.shape[0]
    cols = min(bq, SCORE_COLS)
    int_min = jnp.int32(INT_MIN)
    qchunk = (q_off + qi * bq + lax.broadcasted_iota(I32, (1, bq), 1)) >> CHUNK_SHIFT

    for g in range(aq_in.shape[1] // LANES):
        gs = slice(g * LANES, (g + 1) * LANES)
        aq_ref[gs, :] = aq_in[:, gs].astype(F32).T.astype(BF16)
    for g in range(aqi_in.shape[1] // LANES):
        gs = slice(g * LANES, (g + 1) * LANES)
        aqi_ref[gs, :] = aqi_in[:, gs].astype(F32).T.astype(BF16)
    aw_ref[...] = aw_in[...].T

    def score_tile(t, carry):
        off = pl.multiple_of(t * SCORE_ROWS, SCORE_ROWS)
        ka = ka_ref[pl.ds(off, SCORE_ROWS), :]
        kbb = kb_ref[pl.ds(off, SCORE_ROWS), :]
        kidx = off + lax.broadcasted_iota(I32, (SCORE_ROWS, cols), 0)
        kchunk = jnp.where(kidx < s_real, kidx >> CHUNK_SHIFT, jnp.int32(INT_MAX))
        for c in range(bq // cols):
            cs = slice(c * cols, (c + 1) * cols)
            acc = jnp.zeros((SCORE_ROWS, cols), F32)
            for hp in range(IDX_HEADS // 2):
                qp = aqi_ref[hp * LANES:(hp + 1) * LANES, cs]
                ze = jnp.dot(ka, qp, preferred_element_type=F32)
                zo = jnp.dot(kbb, qp, preferred_element_type=F32)
                acc = acc + jnp.maximum(ze, 0.0) * aw_ref[2 * hp:2 * hp + 1, cs]
                acc = acc + jnp.maximum(zo, 0.0) * aw_ref[2 * hp + 1:2 * hp + 2, cs]
            bits = pltpu.bitcast(acc, I32)
            key = bits ^ ((bits >> 31) & jnp.int32(INT_MAX))
            key_ref[pl.ds(off, SCORE_ROWS), cs] = jnp.where(kchunk <= qchunk[:, cs], key, int_min)
        return carry

    lax.fori_loop(0, nblk * (kb // SCORE_ROWS), score_tile, 0)

    def count(pred):
        def body(t, c):
            off = pl.multiple_of(t * COUNT_ROWS, COUNT_ROWS)
            kk = key_ref[pl.ds(off, COUNT_ROWS), :]
            idx = off + lax.broadcasted_iota(I32, (COUNT_ROWS, bq), 0)
            hit = jnp.where(pred(kk, idx), 1.0, 0.0)
            return c + jnp.sum(hit.reshape(COUNT_ROWS // SUBLANES, SUBLANES, bq), axis=0)
        c = lax.fori_loop(0, nblk * (kb // COUNT_ROWS), body, jnp.zeros((SUBLANES, bq), F32))
        return jnp.sum(c, axis=0, keepdims=True)

    def bit_step(i, carry):
        t_u, n_keep = carry
        cand_u = t_u | (jnp.int32(1) << (31 - i))
        cand_s = cand_u ^ int_min
        cnt = count(lambda kk, idx: kk >= cand_s)
        take = cnt >= topk
        return jnp.where(take, cand_u, t_u), jnp.where(take, cnt, n_keep)

    t_u, n_ge = lax.fori_loop(0, 32, bit_step, (jnp.zeros((1, bq), I32), jnp.zeros((1, bq), F32)))
    t_s = t_u ^ int_min
    need = jnp.logical_and(n_ge > topk, t_u != 0)
    j_ref[...] = jnp.full(j_ref.shape, INT_MAX, I32)

    @pl.when(jnp.max(jnp.where(need, 1.0, 0.0)) > 0.0)
    def _():
        want = topk - count(lambda kk, idx: kk > t_s)

        def idx_step(i, jc):
            cand = jc | (jnp.int32(1) << (index_bits - 1 - i))
            cnt = count(lambda kk, idx: jnp.logical_and(kk == t_s, idx < cand))
            return jnp.where(cnt < want, cand, jc)
        jc = lax.fori_loop(0, index_bits, idx_step, jnp.zeros((1, bq), I32))
        j_ref[...] = jnp.where(need, jc, jnp.int32(INT_MAX))

    jcut = j_ref[...]
    acc_ref[...] = jnp.zeros(acc_ref.shape, F32)

    def attend_block(j, carry):
        ms, ls = carry
        off = pl.multiple_of(j * kb, kb)
        kk = key_ref[pl.ds(off, kb), :]
        idx = off + lax.broadcasted_iota(I32, (kb, bq), 0)
        sel = jnp.logical_or(kk > t_s, jnp.logical_and(kk == t_s, idx <= jcut))
        sel = jnp.logical_and(sel, kk != int_min)
        bias_ref[...] = jnp.where(sel, 0.0, -jnp.inf)
        kblk = k_ref[pl.ds(off, kb), :]
        vblk = v_ref[pl.ds(off, kb), :]
        new_m, new_l = [], []
        for h in range(A_HEADS):
            hs = slice(h * HEAD_DIM, (h + 1) * HEAD_DIM)
            s = jnp.dot(kblk, aq_ref[hs, :], preferred_element_type=F32) + bias_ref[...]
            m_new = jnp.maximum(ms[h], jnp.max(s, axis=0, keepdims=True))
            p = jnp.exp2(s - m_new)
            alpha = jnp.exp2(ms[h] - m_new)
            new_l.append(alpha * ls[h] + jnp.sum(p, axis=0, keepdims=True))
            new_m.append(m_new)
            acc_ref[hs, :] = alpha * acc_ref[hs, :] + lax.dot_general(
                vblk, p.astype(BF16), (((0,), (0,)), ((), ())), preferred_element_type=F32)
        return tuple(new_m), tuple(new_l)

    init = (tuple(jnp.full((1, bq), M_INIT, F32) for _ in range(A_HEADS)),
            tuple(jnp.zeros((1, bq), F32) for _ in range(A_HEADS)))
    _, ls = lax.fori_loop(0, nblk, attend_block, init)
    for h in range(A_HEADS):
        hs = slice(h * HEAD_DIM, (h + 1) * HEAD_DIM)
        o_ref[:, hs] = (acc_ref[hs, :] / ls[h]).T.astype(o_ref.dtype)


def _sparse_attention(aq, aqi, aw, ka, kb_, k, v, nvis, *, bq, kb, s_real, topk, q_off, name):
    nb, s = k.shape[0], k.shape[1]
    nq = nvis.shape[0] // nb
    assert s % kb == 0 and kb % COUNT_ROWS == 0 and kb % SCORE_ROWS == 0 and bq % min(bq, SCORE_COLS) == 0
    qspec = lambda w: pl.BlockSpec((None, bq, w), lambda b, qi, nv: (b, qi, 0))
    kspec = pl.BlockSpec((None, s, LANES), lambda b, qi, nv: (b, 0, 0))
    aq_w, aqi_w = A_HEADS * HEAD_DIM, IDX_HEADS * IDX_DIM
    grid_spec = pltpu.PrefetchScalarGridSpec(
        num_scalar_prefetch=1,
        grid=(nb, nq),
        in_specs=[qspec(aq_w), qspec(aqi_w), qspec(LANES), kspec, kspec, kspec, kspec],
        out_specs=qspec(aq_w),
        scratch_shapes=[pltpu.VMEM((aq_w, bq), BF16), pltpu.VMEM((aqi_w, bq), BF16), pltpu.VMEM((LANES, bq), F32),
                        pltpu.VMEM((s, bq), I32), pltpu.VMEM((kb, bq), F32),
                        pltpu.VMEM((aq_w, bq), F32), pltpu.VMEM((1, bq), I32)],
    )
    return pl.pallas_call(
        functools.partial(_sparse_kernel, nq=nq, kb=kb, s_real=s_real, topk=topk,
                          index_bits=max(1, (s - 1).bit_length()), q_off=q_off),
        grid_spec=grid_spec,
        out_shape=jax.ShapeDtypeStruct((nb, nq * bq, aq_w), BF16),
        compiler_params=_params("parallel", "arbitrary"),
        name=name,
    )(nvis, aq, aqi, aw, ka, kb_, k, v)


def _router_kernel(x_ref, whi_ref, wlo_ref, bias_ref, ids_ref, g0_ref, g1_ref):
    x = x_ref[...]
    xh = x.astype(BF16)
    xl = (x - xh.astype(F32)).astype(BF16)
    whi, wlo = whi_ref[...], wlo_ref[...]
    lg = (jnp.dot(xh, whi, preferred_element_type=F32) + jnp.dot(xh, wlo, preferred_element_type=F32)
          + jnp.dot(xl, whi, preferred_element_type=F32)) + bias_ref[...]
    lane = lax.broadcasted_iota(I32, lg.shape, 1)
    neg = -jnp.inf
    is_group = lane < N_GROUPS
    gl = jnp.where(is_group, lg, neg)
    gmax = jnp.max(gl, axis=-1, keepdims=True)
    grp = jnp.min(jnp.where(gl == gmax, lane, LANES), axis=-1, keepdims=True)
    g1 = 1.0 / jnp.sum(jnp.where(is_group, jnp.exp(gl - gmax), 0.0), axis=-1, keepdims=True)
    lo = N_GROUPS + grp * EXPERTS_PER_GROUP
    el = jnp.where(jnp.logical_and(lane >= lo, lane < lo + EXPERTS_PER_GROUP), lg, neg)
    v1 = jnp.max(el, axis=-1, keepdims=True)
    i1 = jnp.min(jnp.where(el == v1, lane, LANES), axis=-1, keepdims=True)
    el2 = jnp.where(lane == i1, neg, el)
    v2 = jnp.max(el2, axis=-1, keepdims=True)
    i2 = jnp.min(jnp.where(el2 == v2, lane, LANES), axis=-1, keepdims=True)
    e21 = jnp.exp(v2 - v1)
    den = 1.0 + e21
    ids_ref[...] = jnp.where(lane == 0, i1 - N_GROUPS, jnp.where(lane == 1, i2 - N_GROUPS, 0))
    g0_ref[...] = jnp.broadcast_to(g1 * (1.0 / den), g0_ref.shape)
    g1_ref[...] = jnp.broadcast_to(g1 * (e21 / den), g1_ref.shape)


def _router(x, whi, wlo, bias, *, bm):
    n, d = x.shape
    row = pl.BlockSpec((bm, LANES), lambda i: (i, 0))
    return pl.pallas_call(
        _router_kernel,
        grid=(n // bm,),
        in_specs=[pl.BlockSpec((bm, d), lambda i: (i, 0)),
                  pl.BlockSpec((d, LANES), lambda i: (0, 0)),
                  pl.BlockSpec((d, LANES), lambda i: (0, 0)),
                  pl.BlockSpec((1, LANES), lambda i: (0, 0))],
        out_specs=[row, row, row],
        out_shape=[jax.ShapeDtypeStruct((n, LANES), I32), jax.ShapeDtypeStruct((n, LANES), F32),
                   jax.ShapeDtypeStruct((n, LANES), F32)],
        compiler_params=_params("parallel"),
        name="router",
    )(x, whi, wlo, bias)


def _dispatch_kernel(ppos_ref, x_ref, buf_hbm, xs_hbm, sems, *, bm):
    del buf_hbm
    base = pl.program_id(0) * bm * 2

    def row_copy(a):
        return pltpu.make_async_copy(x_ref.at[pl.ds(a >> 1, 1), :],
                                     xs_hbm.at[pl.ds(ppos_ref[base + a], 1), :], sems.at[a & 1])

    def start(a, c):
        row_copy(a).start()
        return c
    lax.fori_loop(0, 2 * bm, start, 0, unroll=4)

    def wait(a, c):
        row_copy(a).wait()
        return c
    lax.fori_loop(0, 2 * bm, wait, 0, unroll=4)


def _dispatch(x, ppos, buf, *, bm):
    n, d = x.shape
    assert n % bm == 0
    grid_spec = pltpu.PrefetchScalarGridSpec(
        num_scalar_prefetch=1,
        grid=(n // bm,),
        in_specs=[pl.BlockSpec((bm, d), lambda i, pp: (i, 0)), pl.BlockSpec(memory_space=pl.ANY)],
        out_specs=pl.BlockSpec(memory_space=pl.ANY),
        scratch_shapes=[pltpu.SemaphoreType.DMA((2,))],
    )
    return pl.pallas_call(
        functools.partial(_dispatch_kernel, bm=bm),
        grid_spec=grid_spec,
        out_shape=jax.ShapeDtypeStruct(buf.shape, buf.dtype),
        input_output_aliases={2: 0},
        compiler_params=_params("arbitrary"),
        name="dispatch",
    )(ppos, x, buf)


def _expert_kernel(te_ref, nv_ref, x_ref, wg_ref, wu_ref, wd_ref, y_ref):
    t = pl.program_id(0)

    @pl.when(nv_ref[t] > 0)
    def _():
        x = x_ref[...].astype(BF16)
        hg = jnp.dot(x, wg_ref[...], preferred_element_type=F32)
        hu = jnp.dot(x, wu_ref[...], preferred_element_type=F32)
        act = (hg * (1.0 / (1.0 + jnp.exp(-hg)))) * hu
        y_ref[...] = jnp.dot(act.astype(BF16), wd_ref[...], preferred_element_type=F32)

    @pl.when(nv_ref[t] == 0)
    def _():
        y_ref[...] = jnp.zeros(y_ref.shape, F32)


def _experts(xs, wg, wu, wd, layer, tile_e, tile_nvalid, *, bm):
    p, d = xs.shape
    f = wg.shape[-1]
    grid_spec = pltpu.PrefetchScalarGridSpec(
        num_scalar_prefetch=2,
        grid=(p // bm,),
        in_specs=[pl.BlockSpec((bm, d), lambda t, te, nv: (t, 0)),
                  pl.BlockSpec((None, None, d, f), lambda t, te, nv: (layer, te[t], 0, 0)),
                  pl.BlockSpec((None, None, d, f), lambda t, te, nv: (layer, te[t], 0, 0)),
                  pl.BlockSpec((None, None, f, d), lambda t, te, nv: (layer, te[t], 0, 0))],
        out_specs=pl.BlockSpec((bm, d), lambda t, te, nv: (t, 0)),
    )
    return pl.pallas_call(
        _expert_kernel,
        grid_spec=grid_spec,
        out_shape=jax.ShapeDtypeStruct((p, d), F32),
        compiler_params=_params("arbitrary"),
        name="experts",
    )(tile_e, tile_nvalid, xs, wg, wu, wd)


def _expert_plan(eid, *, bm):
    na = eid.shape[0] * 2
    flat_e = eid.reshape(na)
    onehot = (flat_e[:, None] == jnp.arange(N_EXPERTS, dtype=I32)[None, :]).astype(I32)
    csum = jnp.cumsum(onehot, axis=0)
    counts = csum[-1]
    padded = ((counts + bm - 1) // bm) * bm
    pend = jnp.cumsum(padded)
    pstart = pend - padded
    ppos = jnp.sum(onehot * (pstart[None, :] + csum - 1), axis=1).astype(I32)
    total = na + N_EXPERTS * bm
    tile_start = jnp.arange(total // bm, dtype=I32) * bm
    tile_e = jnp.minimum(jnp.sum((tile_start[:, None] >= pend[None, :]).astype(I32), axis=1), N_EXPERTS - 1)
    tile_nvalid = jnp.clip(pstart[tile_e] + counts[tile_e] - tile_start, 0, bm).astype(I32)
    return ppos, tile_e.astype(I32), tile_nvalid, total


def _combine_ln_kernel(ppos_ref, x_ref, g0_ref, g1_ref, y_hbm, g_ref, b_ref, of_ref, ob_ref, ybuf, sem,
                       *, bm, alpha):
    i = pl.program_id(0)
    d = x_ref.shape[-1]

    def row_copy(blk, a):
        par = blk & 1
        return pltpu.make_async_copy(y_hbm.at[pl.ds(ppos_ref[blk * (2 * bm) + a], 1), :],
                                     ybuf.at[par, a & 1, pl.ds(a >> 1, 1), :], sem.at[par, a & 1])

    def start_block(blk):
        def start(a, c):
            row_copy(blk, a).start()
            return c
        lax.fori_loop(0, 2 * bm, start, 0, unroll=4)

    @pl.when(i == 0)
    def _():
        start_block(i)

    @pl.when(i + 1 < pl.num_programs(0))
    def _():
        start_block(i + 1)

    def wait(a, c):
        row_copy(i, a).wait()
        return c
    lax.fori_loop(0, 2 * bm, wait, 0, unroll=4)

    reps = d // LANES
    par = i & 1
    y = (alpha * x_ref[...] + jnp.tile(g0_ref[...], (1, reps)) * ybuf[par, 0]
         + jnp.tile(g1_ref[...], (1, reps)) * ybuf[par, 1])
    out = _layer_norm(y, g_ref[...], b_ref[...])
    of_ref[...] = out
    ob_ref[...] = out.astype(BF16)


def _combine_ln(x, ys, ppos, g0, g1, g, b, *, bm, alpha):
    n, d = x.shape
    row = lambda w: pl.BlockSpec((bm, w), lambda i, pp: (i, 0))
    grid_spec = pltpu.PrefetchScalarGridSpec(
        num_scalar_prefetch=1,
        grid=(n // bm,),
        in_specs=[row(d), row(LANES), row(LANES), pl.BlockSpec(memory_space=pl.ANY),
                  pl.BlockSpec((1, d), lambda i, pp: (0, 0)), pl.BlockSpec((1, d), lambda i, pp: (0, 0))],
        out_specs=[row(d), row(d)],
        scratch_shapes=[pltpu.VMEM((2, 2, bm, d), F32), pltpu.SemaphoreType.DMA((2, 2))],
    )
    return pl.pallas_call(
        functools.partial(_combine_ln_kernel, bm=bm, alpha=alpha),
        grid_spec=grid_spec,
        out_shape=[jax.ShapeDtypeStruct((n, d), F32), jax.ShapeDtypeStruct((n, d), BF16)],
        compiler_params=_params("arbitrary"),
        name="combine_ln",
    )(ppos, x, g0, g1, ys, g, b)


def _rope_tables(pos):
    posf = pos.astype(F32)[:, None]
    inv128 = ROPE_THETA ** (-jnp.arange(HEAD_DIM // 2, dtype=F32) * 2.0 / HEAD_DIM)
    ang = posf * inv128[None, :]
    cos, sin = jnp.cos(ang), jnp.sin(ang)
    a128 = jnp.concatenate([cos, cos], axis=-1)
    b128 = jnp.concatenate([-sin, sin], axis=-1)
    inv64 = ROPE_THETA ** (-jnp.arange(IDX_DIM // 2, dtype=F32) * 2.0 / IDX_DIM)
    ang = posf * inv64[None, :]
    cos, sin = jnp.cos(ang), jnp.sin(ang)
    zero = jnp.zeros_like(sin)
    a64 = jnp.concatenate([cos, cos, cos, cos], axis=-1)
    b64 = jnp.concatenate([-sin, zero, -sin, zero], axis=-1)
    c64 = jnp.concatenate([zero, sin, zero, sin], axis=-1)
    return (a128, b128), (a64, b64, c64)


def _z_weight(w_in):
    o = _IN_OFF
    d = w_in.shape[0]
    cols = [w_in[:, o[0]:o[4]],
            w_in[:, o[6]:o[9]],
            w_in[:, o[10]:o[12]],
            w_in[:, o[4]:o[5]],
            w_in[:, o[12]:o[13]],
            w_in[:, o[5]:o[6]],
            w_in[:, o[9]:o[10]],
            jnp.zeros((d, LANES - IDX_HEADS - B_HEADS), w_in.dtype)]
    return jnp.concatenate(cols, axis=1).astype(BF16)


def _uq_weight(w):
    r = w.shape[0]
    w3 = w.reshape(r, C_HEADS, C_NOPE + C_ROPE)
    nope = w3[:, :, :C_NOPE].reshape(r, C_HEADS * C_NOPE)
    rope = jnp.pad(w3[:, :, C_NOPE:], ((0, 0), (0, 0), (0, LANES - C_ROPE))).reshape(r, C_HEADS * LANES)
    return jnp.concatenate([nope, rope], axis=1).astype(BF16)


def _pad_lanes(x):
    return jnp.pad(x, [(0, 0)] * (x.ndim - 1) + [(0, LANES - x.shape[-1])])


def _pad_rows(x, rows):
    return jnp.pad(x, [(0, 0), (0, rows - x.shape[1])] + [(0, 0)] * (x.ndim - 2))


def _bf16_head(c):
    bits = lax.bitcast_convert_type(c, I32) & jnp.int32(-65536)
    return lax.bitcast_convert_type(bits, F32)


def _split3(c):
    hi = _bf16_head(c)
    mid = _bf16_head(c - hi)
    lo = c - hi - mid
    return hi.astype(BF16), mid.astype(BF16), lo.astype(BF16)


def _fox_bias_parts(csum_q, csum_k):
    nb, t, h = csum_q.shape
    s = csum_k.shape[1]
    one_q = jnp.ones((nb, t, h), BF16)
    qrows = jnp.stack(list(_split3(csum_q * LOG2E)) + [one_q] * 3, axis=-1)
    qbias = jnp.pad(qrows, ((0, 0), (0, 0), (0, 0), (0, LANES - 6))).reshape(nb, t, h * LANES)
    one_k = jnp.ones((nb, s, h), BF16)
    krows = jnp.stack([one_k] * 3 + [-p for p in _split3(csum_k * LOG2E)], axis=-1)
    kbias = jnp.pad(krows, ((0, 0), (0, 0), (0, 0), (0, LANES - 6))).reshape(nb, s, h * LANES)
    return qbias, kbias


def _pick(n, *cands):
    for c in cands:
        if n % c == 0:
            return c
    return n


def kernel(x_prompt, x_sample, cache_a_k, cache_a_v, cache_a_kidx, cache_b_k, cache_b_v, cache_b_logf,
           cache_c_latent, cache_c_krope, cache_mem_k, cache_mem_v, mem_prompt,
           w_in, b_f, c_q_norm, w_c_uq, c_kv_norm, w_c_uk, w_c_uv, w_o, ln1_g, ln1_b,
           w_mq, w_mk, w_mv, w_mo, ln2_g, ln2_b, w_rg, b_rg, w_re, b_re,
           w_gate, w_up, w_down, ln3_g, ln3_b):
    depth = w_in.shape[0]
    bp, tp, d = x_prompt.shape
    bs, ts, _ = x_sample.shape
    past = cache_a_k.shape[2]
    n_mem = mem_prompt.shape[1]
    assert bp == 1
    n_p, n_s = bp * tp, bs * ts
    n = n_p + n_s
    alpha = (2 * depth) ** 0.25
    s_s = past + ts
    s_pad = -(-s_s // COUNT_ROWS) * COUNT_ROWS

    bm_tok = _pick(n, 1024, 512, 256, 128, 64)
    bm_post = _pick(n, 256, 128, 64)
    bm_ln = _pick(n, 256, 128, 64)
    bq_f = _pick(tp, 1024, 512, 256, 128)
    bq_a = _pick(tp, 512, 256, 128)
    kb_a = _pick(tp, 1024, 512, 256)
    bm_moe = 256

    res_first, res_second, res_second_row0 = x_prompt.reshape(n_p, d), x_sample.reshape(n_s, d), 0
    xb = jnp.concatenate([res_first.astype(BF16), res_second.astype(BF16)], axis=0)
    pos_p = jnp.arange(tp, dtype=I32)
    pos_s = past + jnp.arange(ts, dtype=I32)
    tabs128, tabs64 = _rope_tables(jnp.concatenate([pos_p, jnp.tile(pos_s, bs)]))

    nq_f = tp // bq_f
    nfull_p = jnp.arange(nq_f, dtype=I32)
    nkv_p = nfull_p + 1
    one = jnp.ones((1,), I32)
    zero = jnp.zeros((1,), I32)
    nvis_p = (((jnp.arange(tp // bq_a, dtype=I32) + 1) * bq_a + kb_a - 1) // kb_a).astype(I32)
    nvis_s = jnp.ones((bs,), I32)

    xs_buf = jnp.zeros((2 * n + N_EXPERTS * bm_moe, d), F32)
    w_o_b = w_o.astype(BF16)
    w_gate_b, w_up_b, w_down_b = w_gate.astype(BF16), w_up.astype(BF16), w_down.astype(BF16)
    state_p = [[] for _ in range(8)]
    state_s = [[] for _ in range(8)]
    mem_k_out, mem_v_out = [], []

    for l in range(depth):
        z = _matmul(xb, _z_weight(w_in[l]), bm=bm_tok, bn=512, out_dtype=F32, name="in_proj")
        bf_row = jnp.zeros((1, LANES), F32).at[0, SM2_BF:SM2_BF + B_HEADS].set(b_f[l].astype(F32))
        aq, ak, aqi, bq_, cqn, clat, sm1, sm2, akb, avb, bkb, bvb = _post(
            z, tabs128, tabs64, bf_row, c_q_norm[l].reshape(1, -1).astype(F32),
            c_kv_norm[l].reshape(1, -1).astype(F32), bm=bm_post)
        cq = _matmul_rope(cqn, _uq_weight(w_c_uq[l]), tabs64, bm=bm_tok, bn=512,
                          first_rope_col=C_HEADS * C_NOPE, out_scale=(C_NOPE + C_ROPE) ** -0.5 * LOG2E,
                          name="c_q_up")
        av = z[:, Z_AV:Z_AV + HEAD_DIM]
        aki = sm1[:, :IDX_DIM]
        ckr = sm1[:, IDX_DIM:]
        blogf = sm2[:, SM2_BF:SM2_BF + B_HEADS]
        bk_ = z[:, Z_BK:Z_BK + B_HEADS * HEAD_DIM]
        bv_ = z[:, Z_BV:Z_BV + B_HEADS * HEAD_DIM]
        rows = (ak, av, aki, bk_.reshape(n, B_HEADS, HEAD_DIM), bv_.reshape(n, B_HEADS, HEAD_DIM),
                blogf, clat, ckr)
        for i, r in enumerate(rows):
            state_p[i].append(r[:n_p].reshape((bp, tp) + r.shape[1:]))
            state_s[i].append(r[n_p:].reshape((bs, ts) + r.shape[1:]))
        grp_p = lambda a: a[:n_p].reshape(bp, tp, -1)
        grp_s = lambda a: a[n_p:].reshape(bs, ts, -1)
        cat_s = lambda c, r: jnp.concatenate([c.reshape(bs, past, -1).astype(r.dtype), grp_s(r)], axis=1)
        all_n = lambda a: a.reshape(1, n, -1)

        def sparse(aq_g, aqi_g, aw_g, ki, k, v, nvis, *, bq, kb, s_real, q_off, name):
            ka = _pad_lanes(ki).astype(BF16)
            kb_ = jnp.concatenate([jnp.zeros_like(ki), ki], axis=-1).astype(BF16)
            out = _sparse_attention(aq_g, aqi_g, aw_g, ka, kb_, k, v, nvis, bq=bq, kb=kb, s_real=s_real,
                                    topk=min(TOPK_MAX, s_real // 4), q_off=q_off, name=name)
            return out.reshape(-1, A_HEADS * HEAD_DIM)

        oa_p = sparse(all_n(aq), all_n(aqi), all_n(sm2), grp_p(aki), grp_p(akb), grp_p(avb), nvis_p,
                      bq=bq_a, kb=kb_a, s_real=tp, q_off=0, name="sparse_prompt")
        oa_s = sparse(grp_s(aq), grp_s(aqi), grp_s(sm2),
                      _pad_rows(cat_s(cache_a_kidx[l], aki), s_pad), _pad_rows(cat_s(cache_a_k[l], akb), s_pad),
                      _pad_rows(cat_s(cache_a_v[l], avb), s_pad), nvis_s,
                      bq=ts, kb=s_pad, s_real=s_s, q_off=past, name="sparse_sample")

        csum_p = jnp.cumsum(grp_p(blogf), axis=1)
        qb_p, kbias_p = _fox_bias_parts(csum_p, csum_p)
        ob_p = _tflash([all_n(bq_), qb_p], (0, 0), [all_n(bkb), kbias_p], (False, False), all_n(bvb), 0,
                       nfull_p, nkv_p, heads=B_HEADS, s=tp, hb=2, bq=bq_f, bk=bq_f, mode="causal", q_off=0,
                       name="fox_prompt")
        csum_s = jnp.cumsum(cat_s(cache_b_logf[l], blogf), axis=1)
        qb_s, kbias_s = _fox_bias_parts(csum_s[:, past:], csum_s)
        ob_s = _tflash([grp_s(bq_), qb_s], (0, 0), [cat_s(cache_b_k[l], bkb), kbias_s], (False, False),
                       cat_s(cache_b_v[l], bvb), 0, zero, one, heads=B_HEADS, s=s_s, hb=2, bq=ts, bk=s_s,
                       mode="causal", q_off=past, name="fox_sample")

        w_ukv = jnp.concatenate([w_c_uk[l], w_c_uv[l]], axis=1).astype(BF16)
        kv_p = _matmul(clat[:n_p], w_ukv, bm=_pick(n_p, 1024, 512, 256), bn=w_ukv.shape[1], out_dtype=BF16,
                       name="c_kv_up_prompt").reshape(bp, tp, -1)
        oc_p = _tflash([all_n(cq), all_n(cq)], (0, C_HEADS),
                       [kv_p, _pad_lanes(grp_p(ckr)).astype(BF16)], (False, True), kv_p, C_HEADS,
                       nfull_p, nkv_p, heads=C_HEADS, s=tp, hb=2, bq=bq_f, bk=bq_f, mode="chunk", q_off=0,
                       name="mla_prompt")
        lat_s = cat_s(cache_c_latent[l], clat)
        kv_s = _matmul(lat_s.reshape(bs * s_s, -1), w_ukv, bm=_pick(bs * s_s, 1024, 512, 256, 128, 64),
                       bn=w_ukv.shape[1], out_dtype=BF16, name="c_kv_up_sample").reshape(bs, s_s, -1)
        oc_s = _tflash([grp_s(cq), grp_s(cq)], (0, C_HEADS),
                       [kv_s, _pad_lanes(cat_s(cache_c_krope[l], ckr)).astype(BF16)], (False, True), kv_s, C_HEADS,
                       zero, one, heads=C_HEADS, s=s_s, hb=2, bq=ts, bk=s_s, mode="chunk", q_off=past,
                       name="mla_sample")

        x1, x1b = _mm_ln_cols([oa_p, ob_p.reshape(n_p, -1), oc_p.reshape(n_p, -1)],
                              [oa_s, ob_s.reshape(n_s, -1), oc_s.reshape(n_s, -1)],
                              w_o_b, l, res_first, res_second, res_second_row0,
                              ln1_g[l].reshape(1, d), ln1_b[l].reshape(1, d),
                              bm=_pick(math.gcd(n_p, n_s), 512, 256, 128, 64), bn=_pick(d, 512, 256, 128),
                              alpha=alpha, name="out_proj_ln1")

        mw = MEM_HEADS * MEM_DIM
        w_mkv = jnp.concatenate([w_mk[l], w_mv[l]], axis=1).astype(BF16)
        mkv = _matmul(mem_prompt.reshape(bp * n_mem, d), w_mkv, bm=_pick(bp * n_mem, 256, 128, 64), bn=512,
                      out_dtype=F32, name="mem_kv")
        mem_k_out.append(mkv[:, :mw].reshape(bp, n_mem, MEM_HEADS, MEM_DIM))
        mem_v_out.append(mkv[:, mw:].reshape(bp, n_mem, MEM_HEADS, MEM_DIM))
        qm = _matmul(x1b, w_mq[l].astype(BF16), bm=bm_tok, bn=mw, out_dtype=BF16, name="mem_q",
                     out_scale=MEM_DIM ** -0.5 * LOG2E)
        mkv3 = mkv.reshape(bp, n_mem, 2 * mw).astype(BF16)
        om_p = _tflash([all_n(qm)], (0,), [mkv3], (False,), mkv3, MEM_HEADS,
                       jnp.ones((nq_f,), I32), jnp.ones((nq_f,), I32), heads=MEM_HEADS, s=n_mem,
                       hb=2, bq=bq_f, bk=n_mem, mode="none", q_off=0, name="mem_prompt")
        om_s = _tflash([grp_s(qm)], (0,), [cache_mem_k[l].reshape(bs, n_mem, mw).astype(BF16)], (False,),
                       cache_mem_v[l].reshape(bs, n_mem, mw).astype(BF16), 0, one, one, heads=MEM_HEADS, s=n_mem,
                       hb=2, bq=ts, bk=n_mem, mode="none", q_off=0, name="mem_sample")
        om = jnp.concatenate([om_p.reshape(n_p, mw), om_s.reshape(n_s, mw)], axis=0)
        x2, _ = _mm_res_ln(om, w_mo[l].astype(BF16), x1, ln2_g[l].reshape(1, d), ln2_b[l].reshape(1, d),
                           bm=bm_ln, bk=mw, alpha=alpha, name="mem_out_ln2")

        w_r = jnp.concatenate([w_rg[l], jnp.transpose(w_re[l], (1, 0, 2)).reshape(d, N_EXPERTS)], axis=1).astype(F32)
        w_r = _pad_lanes(w_r)
        w_r_head = _bf16_head(w_r)
        w_r_hi = w_r_head.astype(BF16)
        w_r_lo = (w_r - w_r_head).astype(BF16)
        b_r = _pad_lanes(jnp.concatenate([b_rg[l], b_re[l].reshape(-1)]).astype(F32).reshape(1, -1))
        ids, g0, g1 = _router(x2, w_r_hi, w_r_lo, b_r, bm=_pick(n, 512, 256, 128, 64))
        ppos, tile_e, tile_nvalid, total = _expert_plan(ids[:, :2], bm=bm_moe)
        assert total == xs_buf.shape[0]
        xs_buf = _dispatch(x2, ppos, xs_buf, bm=_pick(n, 256, 128, 64))
        ys = _experts(xs_buf, w_gate_b, w_up_b, w_down_b, l, tile_e, tile_nvalid, bm=bm_moe)
        x, xb = _combine_ln(x2, ys, ppos, g0, g1, ln3_g[l].reshape(1, d), ln3_b[l].reshape(1, d),
                            bm=_pick(n, 256, 128, 64), alpha=alpha)
        res_first, res_second, res_second_row0 = x, x, n_p

    outs = [x[:n_p].reshape(bp, tp, d), x[n_p:].reshape(bs, ts, d)]
    outs += [jnp.stack(c) for c in state_p]
    outs += [jnp.stack(mem_k_out), jnp.stack(mem_v_out)]
    outs += [jnp.stack(c) for c in state_s]
    return tuple(outs)
```

```python
import functools
import math

import jax
import jax.numpy as jnp
import numpy as np
from jax import lax
from jax.experimental import pallas as pl
from jax.experimental.pallas import tpu as pltpu

F32 = jnp.float32
BF16 = jnp.bfloat16
I32 = jnp.int32

LANES = 128
SUBLANES = 8
VMEM_LIMIT = 56 * 1024 * 1024

CHUNK_SHIFT = 6
HEAD_DIM = 128
ROPE_THETA = 10000.0
A_HEADS = 8
IDX_HEADS = 32
IDX_DIM = 64
TOPK_MAX = 256
B_HEADS = 8
C_HEADS = 16
C_Q_RANK = 768
C_KV_RANK = 256
C_NOPE = 128
C_ROPE = 64
C_V = 128
MEM_HEADS = 4
MEM_DIM = 128
N_GROUPS = 4
EXPERTS_PER_GROUP = 4
N_EXPERTS = N_GROUPS * EXPERTS_PER_GROUP
LN_EPS = 1e-5
RMS_EPS = 1e-6
LOG2E = math.log2(math.e)

_IN_SIZES = (A_HEADS * HEAD_DIM, HEAD_DIM, HEAD_DIM, IDX_HEADS * IDX_DIM, IDX_DIM, IDX_HEADS,
             B_HEADS * HEAD_DIM, B_HEADS * HEAD_DIM, B_HEADS * HEAD_DIM, B_HEADS,
             C_Q_RANK, C_KV_RANK, C_ROPE)
_IN_OFF = np.concatenate([[0], np.cumsum(_IN_SIZES)]).astype(int)

Z_AQ = 0
Z_AK = Z_AQ + A_HEADS * HEAD_DIM
Z_AV = Z_AK + HEAD_DIM
Z_AQI = Z_AV + HEAD_DIM
Z_BQ = Z_AQI + IDX_HEADS * IDX_DIM
Z_BK = Z_BQ + B_HEADS * HEAD_DIM
Z_BV = Z_BK + B_HEADS * HEAD_DIM
Z_CQ = Z_BV + B_HEADS * HEAD_DIM
Z_CKV = Z_CQ + C_Q_RANK
Z_SM1 = Z_CKV + C_KV_RANK
Z_SM2 = Z_SM1 + LANES
Z_WIDTH = Z_SM2 + LANES
SM2_BF = IDX_HEADS

M_INIT = -1e30
INT_MIN = -2 ** 31
INT_MAX = 2 ** 31 - 1
SCORE_ROWS = 128
SCORE_COLS = 256
COUNT_ROWS = 256
LN_ROWS = 64


def _params(*sem):
    return pltpu.CompilerParams(dimension_semantics=sem, vmem_limit_bytes=VMEM_LIMIT)


def _mm_kernel(x_ref, w_ref, o_ref, *, out_scale):
    x = x_ref[...].astype(BF16)
    acc = jnp.dot(x, w_ref[...], preferred_element_type=F32)
    if out_scale != 1.0:
        acc = acc * out_scale
    o_ref[...] = acc.astype(o_ref.dtype)


def _matmul(x, w, *, bm, bn, out_dtype, name, out_scale=1.0):
    m, k = x.shape
    n = w.shape[1]
    assert m % bm == 0 and n % bn == 0, (x.shape, w.shape, bm, bn)
    return pl.pallas_call(
        functools.partial(_mm_kernel, out_scale=out_scale),
        grid=(m // bm, n // bn),
        in_specs=[pl.BlockSpec((bm, k), lambda i, j: (i, 0)),
                  pl.BlockSpec((k, bn), lambda i, j: (0, j))],
        out_specs=pl.BlockSpec((bm, bn), lambda i, j: (i, j)),
        out_shape=jax.ShapeDtypeStruct((m, n), out_dtype),
        compiler_params=_params("parallel", "parallel"),
        name=name,
    )(x, w)


def _rope64(x, a, b, c):
    return x * a + pltpu.roll(x, 96, 1) * b + pltpu.roll(x, 32, 1) * c


def _rope128(x, a, b):
    return x * a + pltpu.roll(x, 64, 1) * b


def _mm_rope_kernel(x_ref, w_ref, a_ref, b_ref, c_ref, o_ref, *, first_rope_block, out_scale):
    j = pl.program_id(1)
    acc = jnp.dot(x_ref[...], w_ref[...], preferred_element_type=F32) * out_scale

    @pl.when(j < first_rope_block)
    def _():
        o_ref[...] = acc.astype(o_ref.dtype)

    @pl.when(j >= first_rope_block)
    def _():
        a, b, c = a_ref[...], b_ref[...], c_ref[...]
        for g in range(acc.shape[1] // LANES):
            sl = slice(g * LANES, (g + 1) * LANES)
            o_ref[:, sl] = _rope64(acc[:, sl], a, b, c).astype(o_ref.dtype)


def _matmul_rope(x, w, tabs, *, bm, bn, first_rope_col, out_scale, name):
    m, k = x.shape
    n = w.shape[1]
    assert m % bm == 0 and n % bn == 0 and first_rope_col % bn == 0
    tab_spec = pl.BlockSpec((bm, LANES), lambda i, j: (i, 0))
    return pl.pallas_call(
        functools.partial(_mm_rope_kernel, first_rope_block=first_rope_col // bn, out_scale=out_scale),
        grid=(m // bm, n // bn),
        in_specs=[pl.BlockSpec((bm, k), lambda i, j: (i, 0)),
                  pl.BlockSpec((k, bn), lambda i, j: (0, j)),
                  tab_spec, tab_spec, tab_spec],
        out_specs=pl.BlockSpec((bm, bn), lambda i, j: (i, j)),
        out_shape=jax.ShapeDtypeStruct((m, n), BF16),
        compiler_params=_params("parallel", "parallel"),
        name=name,
    )(x, w, *tabs)


def _post_kernel(z_ref, a128_ref, b128_ref, a64_ref, b64_ref, c64_ref, bf_ref, gq_ref, gkv_ref,
                 aq_ref, ak_ref, aqi_ref, bq_ref, cqn_ref, clat_ref, sm1_ref, sm2_ref,
                 akb_ref, avb_ref, bkb_ref, bvb_ref):
    a128, b128 = a128_ref[...], b128_ref[...]
    a64, b64, c64 = a64_ref[...], b64_ref[...], c64_ref[...]
    qscale = HEAD_DIM ** -0.5 * LOG2E
    for g in range(A_HEADS):
        src = slice(Z_AQ + g * LANES, Z_AQ + (g + 1) * LANES)
        aq_ref[:, g * LANES:(g + 1) * LANES] = (_rope128(z_ref[:, src], a128, b128) * qscale).astype(BF16)
    ak = _rope128(z_ref[:, Z_AK:Z_AK + LANES], a128, b128)
    ak_ref[...] = ak
    akb_ref[...] = ak.astype(BF16)
    avb_ref[...] = z_ref[:, Z_AV:Z_AV + LANES].astype(BF16)
    bkb_ref[...] = z_ref[:, Z_BK:Z_BK + B_HEADS * HEAD_DIM].astype(BF16)
    bvb_ref[...] = z_ref[:, Z_BV:Z_BV + B_HEADS * HEAD_DIM].astype(BF16)
    for g in range(IDX_HEADS * IDX_DIM // LANES):
        src = slice(Z_AQI + g * LANES, Z_AQI + (g + 1) * LANES)
        aqi_ref[:, g * LANES:(g + 1) * LANES] = _rope64(z_ref[:, src], a64, b64, c64).astype(BF16)
    bq_ref[...] = (z_ref[:, Z_BQ:Z_BQ + B_HEADS * HEAD_DIM] * qscale).astype(BF16)
    cq = z_ref[:, Z_CQ:Z_CQ + C_Q_RANK]
    cq = cq * lax.rsqrt(jnp.mean(cq * cq, axis=-1, keepdims=True) + RMS_EPS) * gq_ref[...]
    cqn_ref[...] = cq.astype(BF16)
    ckv = z_ref[:, Z_CKV:Z_CKV + C_KV_RANK]
    clat_ref[...] = ckv * lax.rsqrt(jnp.mean(ckv * ckv, axis=-1, keepdims=True) + RMS_EPS) * gkv_ref[...]
    sm1_ref[...] = _rope64(z_ref[:, Z_SM1:Z_SM1 + LANES], a64, b64, c64)
    s2 = z_ref[:, Z_SM2:Z_SM2 + LANES]
    lane = lax.broadcasted_iota(I32, s2.shape, 1)
    f = s2 + bf_ref[...]
    logf = jnp.minimum(f, 0.0) - jnp.log1p(jnp.exp(-jnp.abs(f)))
    sm2_ref[...] = jnp.where(lane < SM2_BF, s2 * (IDX_HEADS ** -0.5 * IDX_DIM ** -0.5),
                             jnp.where(lane < SM2_BF + B_HEADS, logf, 0.0))


def _post(z, tabs128, tabs64, bf_row, gq, gkv, *, bm):
    n = z.shape[0]
    row = lambda w: pl.BlockSpec((bm, w), lambda i: (i, 0))
    const = lambda w: pl.BlockSpec((1, w), lambda i: (0, 0))
    widths = (A_HEADS * HEAD_DIM, LANES, IDX_HEADS * IDX_DIM, B_HEADS * HEAD_DIM, C_Q_RANK, C_KV_RANK, LANES, LANES,
              LANES, LANES, B_HEADS * HEAD_DIM, B_HEADS * HEAD_DIM)
    dtypes = (BF16, F32, BF16, BF16, BF16, F32, F32, F32, BF16, BF16, BF16, BF16)
    return pl.pallas_call(
        _post_kernel,
        grid=(n // bm,),
        in_specs=[row(Z_WIDTH)] + [row(LANES)] * 5 + [const(LANES), const(C_Q_RANK), const(C_KV_RANK)],
        out_specs=[row(w) for w in widths],
        out_shape=[jax.ShapeDtypeStruct((n, w), d) for w, d in zip(widths, dtypes)],
        compiler_params=_params("parallel"),
        name="post_projection",
    )(z, *tabs128, *tabs64, bf_row, gq, gkv)


def _layer_norm(y, g, b):
    mu = jnp.mean(y, axis=-1, keepdims=True)
    yc = y - mu
    var = jnp.mean(yc * yc, axis=-1, keepdims=True)
    return yc * lax.rsqrt(var + LN_EPS) * g + b


def _mm_res_ln_kernel(x_ref, w_ref, r_ref, g_ref, b_ref, of_ref, ob_ref, acc_ref, *, nk, alpha):
    k = pl.program_id(1)

    @pl.when(k == 0)
    def _():
        acc_ref[...] = jnp.zeros_like(acc_ref)

    acc_ref[...] += jnp.dot(x_ref[...], w_ref[...], preferred_element_type=F32)

    @pl.when(k == nk - 1)
    def _():
        out = _layer_norm(alpha * r_ref[...] + acc_ref[...], g_ref[...], b_ref[...])
        of_ref[...] = out
        ob_ref[...] = out.astype(BF16)


def _mm_res_ln(x, w, res, g, b, *, bm, bk, alpha, name):
    m, k = x.shape
    d = w.shape[1]
    nk = k // bk
    return pl.pallas_call(
        functools.partial(_mm_res_ln_kernel, nk=nk, alpha=alpha),
        grid=(m // bm, nk),
        in_specs=[pl.BlockSpec((bm, bk), lambda i, kk: (i, kk)),
                  pl.BlockSpec((bk, d), lambda i, kk: (kk, 0)),
                  pl.BlockSpec((bm, d), lambda i, kk: (i, 0)),
                  pl.BlockSpec((1, d), lambda i, kk: (0, 0)),
                  pl.BlockSpec((1, d), lambda i, kk: (0, 0))],
        out_specs=[pl.BlockSpec((bm, d), lambda i, kk: (i, 0)),
                   pl.BlockSpec((bm, d), lambda i, kk: (i, 0))],
        out_shape=[jax.ShapeDtypeStruct((m, d), F32), jax.ShapeDtypeStruct((m, d), BF16)],
        scratch_shapes=[pltpu.VMEM((bm, d), F32)],
        compiler_params=_params("parallel", "arbitrary"),
        name=name,
    )(x, w, res, g, b)


def _mm_ln_cols_kernel(*refs, n_parts, nj, bn, alpha, first_rows):
    xa = refs[:n_parts]
    xb = refs[n_parts:2 * n_parts]
    w_ref, ra_ref, rb_ref, g_ref, b_ref, of_ref, ob_ref = refs[2 * n_parts:]
    i = pl.program_id(0)
    j = pl.program_id(1)

    def emit(x_refs, r_ref):
        x = jnp.concatenate([xr[...] for xr in x_refs], axis=1)
        y = alpha * r_ref[...] + jnp.dot(x, w_ref[...], preferred_element_type=F32)
        for jj in range(nj):
            @pl.when(j == jj)
            def _(jj=jj):
                of_ref[:, jj * bn:(jj + 1) * bn] = y

    @pl.when(i < first_rows)
    def _():
        emit(xa, ra_ref)

    @pl.when(i >= first_rows)
    def _():
        emit(xb, rb_ref)

    @pl.when(j == nj - 1)
    def _():
        for r0 in range(0, of_ref.shape[0], LN_ROWS):
            rs = slice(r0, r0 + LN_ROWS)
            out = _layer_norm(of_ref[rs, :], g_ref[...], b_ref[...])
            of_ref[rs, :] = out
            ob_ref[rs, :] = out.astype(BF16)


def _mm_ln_cols(x_first, x_second, w, layer, res_first, res_second, res_second_row0, g, b, *, bm, bn, alpha, name):
    m1, m2 = x_first[0].shape[0], x_second[0].shape[0]
    d = w.shape[-1]
    k = w.shape[-2]
    nj = d // bn
    assert m1 % bm == 0 and m2 % bm == 0 and res_second_row0 % bm == 0 and d % bn == 0
    nb1, nb2, r0b = m1 // bm, m2 // bm, res_second_row0 // bm
    first = lambda i: jnp.minimum(i, nb1 - 1)
    second = lambda i: jnp.maximum(i - nb1, 0)
    once = pl.Buffered(1)
    in_specs = [pl.BlockSpec((bm, xp.shape[1]), lambda i, j: (first(i), 0), pipeline_mode=once) for xp in x_first]
    in_specs += [pl.BlockSpec((bm, xp.shape[1]), lambda i, j: (second(i), 0), pipeline_mode=once)
                 for xp in x_second]
    in_specs += [pl.BlockSpec((None, k, bn), lambda i, j: (layer, 0, j)),
                 pl.BlockSpec((bm, bn), lambda i, j: (first(i), j)),
                 pl.BlockSpec((bm, bn), lambda i, j: (second(i) + r0b, j)),
                 pl.BlockSpec((1, d), lambda i, j: (0, 0)),
                 pl.BlockSpec((1, d), lambda i, j: (0, 0))]
    return pl.pallas_call(
        functools.partial(_mm_ln_cols_kernel, n_parts=len(x_first), nj=nj, bn=bn, alpha=alpha, first_rows=nb1),
        grid=(nb1 + nb2, nj),
        in_specs=in_specs,
        out_specs=[pl.BlockSpec((bm, d), lambda i, j: (i, 0)),
                   pl.BlockSpec((bm, d), lambda i, j: (i, 0))],
        out_shape=[jax.ShapeDtypeStruct((m1 + m2, d), F32), jax.ShapeDtypeStruct((m1 + m2, d), BF16)],
        compiler_params=_params("parallel", "arbitrary"),
        name=name,
    )(*x_first, *x_second, w, res_first, res_second, g, b)


def _tflash_kernel(nfull_ref, nkv_ref, *refs, n_parts, k_shared, hb, bk, mode, q_off):
    refs = list(refs)
    q_refs = [refs.pop(0) for _ in range(n_parts)]
    k_refs = [refs.pop(0) for _ in range(n_parts)]
    v_ref = refs.pop(0)
    o_ref, acc_ref = refs
    qi = pl.program_id(2)
    bq = o_ref.shape[0]
    qs = []
    for h in range(hb):
        hs = slice(h * LANES, (h + 1) * LANES)
        qs.append(jnp.concatenate([qr[:, hs].astype(F32).T.astype(BF16) for qr in q_refs], axis=0))
    acc_ref[...] = jnp.zeros(acc_ref.shape, F32)

    def block(j, carry, masked):
        ms, ls = carry
        off = pl.multiple_of(j * bk, bk)
        if masked:
            kpos = off + lax.broadcasted_iota(I32, (bk, bq), 0)
            qpos = q_off + qi * bq + lax.broadcasted_iota(I32, (bk, bq), 1)
            if mode == "chunk":
                mask = (kpos >> CHUNK_SHIFT) <= (qpos >> CHUNK_SHIFT)
            else:
                mask = kpos <= qpos
        new_m, new_l = [], []
        for h in range(hb):
            parts = []
            for p_i, kr in enumerate(k_refs):
                hs = 0 if k_shared[p_i] else h
                parts.append(kr[pl.ds(off, bk), hs * LANES:(hs + 1) * LANES])
            k = jnp.concatenate(parts, axis=1) if n_parts > 1 else parts[0]
            s = jnp.dot(k, qs[h], preferred_element_type=F32)
            if masked:
                s = jnp.where(mask, s, -jnp.inf)
            m_new = jnp.maximum(ms[h], jnp.max(s, axis=0, keepdims=True))
            p = jnp.exp2(s - m_new)
            alpha = jnp.exp2(ms[h] - m_new)
            new_l.append(alpha * ls[h] + jnp.sum(p, axis=0, keepdims=True))
            new_m.append(m_new)
            v = v_ref[pl.ds(off, bk), h * LANES:(h + 1) * LANES]
            acc_ref[h] = alpha * acc_ref[h] + lax.dot_general(v, p.astype(BF16), (((0,), (0,)), ((), ())),
                                                              preferred_element_type=F32)
        return tuple(new_m), tuple(new_l)

    init = (tuple(jnp.full((1, bq), M_INIT, F32) for _ in range(hb)),
            tuple(jnp.zeros((1, bq), F32) for _ in range(hb)))
    carry = lax.fori_loop(0, nfull_ref[qi], functools.partial(block, masked=False), init)
    if mode != "none":
        carry = lax.fori_loop(nfull_ref[qi], nkv_ref[qi], functools.partial(block, masked=True), carry)
    for h in range(hb):
        o_ref[:, h * LANES:(h + 1) * LANES] = (acc_ref[h] / carry[1][h]).T.astype(o_ref.dtype)


def _tflash(q_parts, q_head0, k_parts, k_shared, v, v_head0, nfull, nkv, *, heads, s, hb, bq, bk, mode, q_off,
            name):
    nb = v.shape[0]
    nq = nfull.shape[0]
    assert heads % hb == 0 and v_head0 % hb == 0 and all(h0 % hb == 0 for h0 in q_head0) and s % bk == 0
    in_specs = []
    for h0 in q_head0:
        in_specs.append(pl.BlockSpec((None, bq, hb * LANES),
                                     lambda b, g, qi, nf, nk, h0=h0: (b, qi, h0 // hb + g)))
    for shared in k_shared:
        if shared:
            in_specs.append(pl.BlockSpec((None, s, LANES), lambda b, g, qi, nf, nk: (b, 0, 0)))
        else:
            in_specs.append(pl.BlockSpec((None, s, hb * LANES), lambda b, g, qi, nf, nk: (b, 0, g)))
    in_specs.append(pl.BlockSpec((None, s, hb * LANES), lambda b, g, qi, nf, nk: (b, 0, v_head0 // hb + g)))
    grid_spec = pltpu.PrefetchScalarGridSpec(
        num_scalar_prefetch=2,
        grid=(nb, heads // hb, nq),
        in_specs=in_specs,
        out_specs=pl.BlockSpec((None, bq, hb * LANES), lambda b, g, qi, nf, nk: (b, qi, g)),
        scratch_shapes=[pltpu.VMEM((hb, LANES, bq), F32)],
    )
    return pl.pallas_call(
        functools.partial(_tflash_kernel, n_parts=len(q_parts), k_shared=tuple(k_shared), hb=hb, bk=bk,
                          mode=mode, q_off=q_off),
        grid_spec=grid_spec,
        out_shape=jax.ShapeDtypeStruct((nb, nq * bq, heads * LANES), BF16),
        compiler_params=_params("parallel", "parallel", "arbitrary"),
        name=name,
    )(nfull, nkv, *q_parts, *k_parts, v)


def _sparse_kernel(nvis_ref, aq_in, aqi_in, aw_in, ka_ref, kb_ref, k_ref, v_ref,
                   o_ref, aq_ref, aqi_ref, aw_ref, key_ref, bias_ref, acc_ref, j_ref,
                   *, nq, kb, s_real, topk, index_bits, q_off):
    b = pl.program_id(0)
    qi = pl.program_id(1)
    nblk = nvis_ref[b * nq + qi]
    bq = aq_in.shape[0]
    cols = min(bq, SCORE_COLS)
    int_min = jnp.int32(INT_MIN)
    qchunk = (q_off + qi * bq + lax.broadcasted_iota(I32, (1, bq), 1)) >> CHUNK_SHIFT

    for g in range(aq_in.shape[1] // LANES):
        gs = slice(g * LANES, (g + 1) * LANES)
        aq_ref[gs, :] = aq_in[:, gs].astype(F32).T.astype(BF16)
    for g in range(aqi_in.shape[1] // LANES):
        gs = slice(g * LANES, (g + 1) * LANES)
        aqi_ref[gs, :] = aqi_in[:, gs].astype(F32).T.astype(BF16)
    aw_ref[...] = aw_in[...].T

    def score_tile(t, carry):
        off = pl.multiple_of(t * SCORE_ROWS, SCORE_ROWS)
        ka = ka_ref[pl.ds(off, SCORE_ROWS), :]
        kbb = kb_ref[pl.ds(off, SCORE_ROWS), :]
        kidx = off + lax.broadcasted_iota(I32, (SCORE_ROWS, cols), 0)
        kchunk = jnp.where(kidx < s_real, kidx >> CHUNK_SHIFT, jnp.int32(INT_MAX))
        for c in range(bq // cols):
            cs = slice(c * cols, (c + 1) * cols)
            acc = jnp.zeros((SCORE_ROWS, cols), F32)
            for hp in range(IDX_HEADS // 2):
                qp = aqi_ref[hp * LANES:(hp + 1) * LANES, cs]
                ze = jnp.dot(ka, qp, preferred_element_type=F32)
                zo = jnp.dot(kbb, qp, preferred_element_type=F32)
                acc = acc + jnp.maximum(ze, 0.0) * aw_ref[2 * hp:2 * hp + 1, cs]
                acc = acc + jnp.maximum(zo, 0.0) * aw_ref[2 * hp + 1:2 * hp + 2, cs]
            bits = pltpu.bitcast(acc, I32)
            key = bits ^ ((bits >> 31) & jnp.int32(INT_MAX))
            key_ref[pl.ds(off, SCORE_ROWS), cs] = jnp.where(kchunk <= qchunk[:, cs], key, int_min)
        return carry

    lax.fori_loop(0, nblk * (kb // SCORE_ROWS), score_tile, 0)

    def count(pred):
        def body(t, c):
            off = pl.multiple_of(t * COUNT_ROWS, COUNT_ROWS)
            kk = key_ref[pl.ds(off, COUNT_ROWS), :]
            idx = off + lax.broadcasted_iota(I32, (COUNT_ROWS, bq), 0)
            hit = jnp.where(pred(kk, idx), 1.0, 0.0)
            return c + jnp.sum(hit.reshape(COUNT_ROWS // SUBLANES, SUBLANES, bq), axis=0)
        c = lax.fori_loop(0, nblk * (kb // COUNT_ROWS), body, jnp.zeros((SUBLANES, bq), F32))
        return jnp.sum(c, axis=0, keepdims=True)

    def bit_step(i, carry):
        t_u, n_keep = carry
        cand_u = t_u | (jnp.int32(1) << (31 - i))
        cand_s = cand_u ^ int_min
        cnt = count(lambda kk, idx: kk >= cand_s)
        take = cnt >= topk
        return jnp.where(take, cand_u, t_u), jnp.where(take, cnt, n_keep)

    t_u, n_ge = lax.fori_loop(0, 32, bit_step, (jnp.zeros((1, bq), I32), jnp.zeros((1, bq), F32)))
    t_s = t_u ^ int_min
    need = jnp.logical_and(n_ge > topk, t_u != 0)
    j_ref[...] = jnp.full(j_ref.shape, INT_MAX, I32)

    @pl.when(jnp.max(jnp.where(need, 1.0, 0.0)) > 0.0)
    def _():
        want = topk - count(lambda kk, idx: kk > t_s)

        def idx_step(i, jc):
            cand = jc | (jnp.int32(1) << (index_bits - 1 - i))
            cnt = count(lambda kk, idx: jnp.logical_and(kk == t_s, idx < cand))
            return jnp.where(cnt < want, cand, jc)
        jc = lax.fori_loop(0, index_bits, idx_step, jnp.zeros((1, bq), I32))
        j_ref[...] = jnp.where(need, jc, jnp.int32(INT_MAX))

    jcut = j_ref[...]
    acc_ref[...] = jnp.zeros(acc_ref.shape, F32)

    def attend_block(j, carry):
        ms, ls = carry
        off = pl.multiple_of(j * kb, kb)
        kk = key_ref[pl.ds(off, kb), :]
        idx = off + lax.broadcasted_iota(I32, (kb, bq), 0)
        sel = jnp.logical_or(kk > t_s, jnp.logical_and(kk == t_s, idx <= jcut))
        sel = jnp.logical_and(sel, kk != int_min)
        bias_ref[...] = jnp.where(sel, 0.0, -jnp.inf)
        kblk = k_ref[pl.ds(off, kb), :]
        vblk = v_ref[pl.ds(off, kb), :]
        new_m, new_l = [], []
        for h in range(A_HEADS):
            hs = slice(h * HEAD_DIM, (h + 1) * HEAD_DIM)
            s = jnp.dot(kblk, aq_ref[hs, :], preferred_element_type=F32) + bias_ref[...]
            m_new = jnp.maximum(ms[h], jnp.max(s, axis=0, keepdims=True))
            p = jnp.exp2(s - m_new)
            alpha = jnp.exp2(ms[h] - m_new)
            new_l.append(alpha * ls[h] + jnp.sum(p, axis=0, keepdims=True))
            new_m.append(m_new)
            acc_ref[hs, :] = alpha * acc_ref[hs, :] + lax.dot_general(
                vblk, p.astype(BF16), (((0,), (0,)), ((), ())), preferred_element_type=F32)
        return tuple(new_m), tuple(new_l)

    init = (tuple(jnp.full((1, bq), M_INIT, F32) for _ in range(A_HEADS)),
            tuple(jnp.zeros((1, bq), F32) for _ in range(A_HEADS)))
    _, ls = lax.fori_loop(0, nblk, attend_block, init)
    for h in range(A_HEADS):
        hs = slice(h * HEAD_DIM, (h + 1) * HEAD_DIM)
        o_ref[:, hs] = (acc_ref[hs, :] / ls[h]).T.astype(o_ref.dtype)


def _sparse_attention(aq, aqi, aw, ka, kb_, k, v, nvis, *, bq, kb, s_real, topk, q_off, name):
    nb, s = k.shape[0], k.shape[1]
    nq = nvis.shape[0] // nb
    assert s % kb == 0 and kb % COUNT_ROWS == 0 and kb % SCORE_ROWS == 0 and bq % min(bq, SCORE_COLS) == 0
    qspec = lambda w: pl.BlockSpec((None, bq, w), lambda b, qi, nv: (b, qi, 0))
    kspec = pl.BlockSpec((None, s, LANES), lambda b, qi, nv: (b, 0, 0))
    aq_w, aqi_w = A_HEADS * HEAD_DIM, IDX_HEADS * IDX_DIM
    grid_spec = pltpu.PrefetchScalarGridSpec(
        num_scalar_prefetch=1,
        grid=(nb, nq),
        in_specs=[qspec(aq_w), qspec(aqi_w), qspec(LANES), kspec, kspec, kspec, kspec],
        out_specs=qspec(aq_w),
        scratch_shapes=[pltpu.VMEM((aq_w, bq), BF16), pltpu.VMEM((aqi_w, bq), BF16), pltpu.VMEM((LANES, bq), F32),
                        pltpu.VMEM((s, bq), I32), pltpu.VMEM((kb, bq), F32),
                        pltpu.VMEM((aq_w, bq), F32), pltpu.VMEM((1, bq), I32)],
    )
    return pl.pallas_call(
        functools.partial(_sparse_kernel, nq=nq, kb=kb, s_real=s_real, topk=topk,
                          index_bits=max(1, (s - 1).bit_length()), q_off=q_off),
        grid_spec=grid_spec,
        out_shape=jax.ShapeDtypeStruct((nb, nq * bq, aq_w), BF16),
        compiler_params=_params("parallel", "arbitrary"),
        name=name,
    )(nvis, aq, aqi, aw, ka, kb_, k, v)


def _router_kernel(x_ref, whi_ref, wlo_ref, bias_ref, ids_ref, g0_ref, g1_ref):
    x = x_ref[...]
    xh = x.astype(BF16)
    xl = (x - xh.astype(F32)).astype(BF16)
    whi, wlo = whi_ref[...], wlo_ref[...]
    lg = (jnp.dot(xh, whi, preferred_element_type=F32) + jnp.dot(xh, wlo, preferred_element_type=F32)
          + jnp.dot(xl, whi, preferred_element_type=F32)) + bias_ref[...]
    lane = lax.broadcasted_iota(I32, lg.shape, 1)
    neg = -jnp.inf
    is_group = lane < N_GROUPS
    gl = jnp.where(is_group, lg, neg)
    gmax = jnp.max(gl, axis=-1, keepdims=True)
    grp = jnp.min(jnp.where(gl == gmax, lane, LANES), axis=-1, keepdims=True)
    g1 = 1.0 / jnp.sum(jnp.where(is_group, jnp.exp(gl - gmax), 0.0), axis=-1, keepdims=True)
    lo = N_GROUPS + grp * EXPERTS_PER_GROUP
    el = jnp.where(jnp.logical_and(lane >= lo, lane < lo + EXPERTS_PER_GROUP), lg, neg)
    v1 = jnp.max(el, axis=-1, keepdims=True)
    i1 = jnp.min(jnp.where(el == v1, lane, LANES), axis=-1, keepdims=True)
    el2 = jnp.where(lane == i1, neg, el)
    v2 = jnp.max(el2, axis=-1, keepdims=True)
    i2 = jnp.min(jnp.where(el2 == v2, lane, LANES), axis=-1, keepdims=True)
    e21 = jnp.exp(v2 - v1)
    den = 1.0 + e21
    ids_ref[...] = jnp.where(lane == 0, i1 - N_GROUPS, jnp.where(lane == 1, i2 - N_GROUPS, 0))
    g0_ref[...] = jnp.broadcast_to(g1 * (1.0 / den), g0_ref.shape)
    g1_ref[...] = jnp.broadcast_to(g1 * (e21 / den), g1_ref.shape)


def _router(x, whi, wlo, bias, *, bm):
    n, d = x.shape
    row = pl.BlockSpec((bm, LANES), lambda i: (i, 0))
    return pl.pallas_call(
        _router_kernel,
        grid=(n // bm,),
        in_specs=[pl.BlockSpec((bm, d), lambda i: (i, 0)),
                  pl.BlockSpec((d, LANES), lambda i: (0, 0)),
                  pl.BlockSpec((d, LANES), lambda i: (0, 0)),
                  pl.BlockSpec((1, LANES), lambda i: (0, 0))],
        out_specs=[row, row, row],
        out_shape=[jax.ShapeDtypeStruct((n, LANES), I32), jax.ShapeDtypeStruct((n, LANES), F32),
                   jax.ShapeDtypeStruct((n, LANES), F32)],
        compiler_params=_params("parallel"),
        name="router",
    )(x, whi, wlo, bias)


def _dispatch_kernel(ppos_ref, x_ref, buf_hbm, xs_hbm, sems, *, bm):
    del buf_hbm
    base = pl.program_id(0) * bm * 2

    def row_copy(a):
        return pltpu.make_async_copy(x_ref.at[pl.ds(a >> 1, 1), :],
                                     xs_hbm.at[pl.ds(ppos_ref[base + a], 1), :], sems.at[a & 1])

    def start(a, c):
        row_copy(a).start()
        return c
    lax.fori_loop(0, 2 * bm, start, 0, unroll=4)

    def wait(a, c):
        row_copy(a).wait()
        return c
    lax.fori_loop(0, 2 * bm, wait, 0, unroll=4)


def _dispatch(x, ppos, buf, *, bm):
    n, d = x.shape
    assert n % bm == 0
    grid_spec = pltpu.PrefetchScalarGridSpec(
        num_scalar_prefetch=1,
        grid=(n // bm,),
        in_specs=[pl.BlockSpec((bm, d), lambda i, pp: (i, 0)), pl.BlockSpec(memory_space=pl.ANY)],
        out_specs=pl.BlockSpec(memory_space=pl.ANY),
        scratch_shapes=[pltpu.SemaphoreType.DMA((2,))],
    )
    return pl.pallas_call(
        functools.partial(_dispatch_kernel, bm=bm),
        grid_spec=grid_spec,
        out_shape=jax.ShapeDtypeStruct(buf.shape, buf.dtype),
        input_output_aliases={2: 0},
        compiler_params=_params("arbitrary"),
        name="dispatch",
    )(ppos, x, buf)


def _expert_kernel(te_ref, nv_ref, x_ref, wg_ref, wu_ref, wd_ref, y_ref):
    t = pl.program_id(0)

    @pl.when(nv_ref[t] > 0)
    def _():
        x = x_ref[...].astype(BF16)
        hg = jnp.dot(x, wg_ref[...], preferred_element_type=F32)
        hu = jnp.dot(x, wu_ref[...], preferred_element_type=F32)
        act = (hg * (1.0 / (1.0 + jnp.exp(-hg)))) * hu
        y_ref[...] = jnp.dot(act.astype(BF16), wd_ref[...], preferred_element_type=F32)

    @pl.when(nv_ref[t] == 0)
    def _():
        y_ref[...] = jnp.zeros(y_ref.shape, F32)


def _experts(xs, wg, wu, wd, layer, tile_e, tile_nvalid, *, bm):
    p, d = xs.shape
    f = wg.shape[-1]
    grid_spec = pltpu.PrefetchScalarGridSpec(
        num_scalar_prefetch=2,
        grid=(p // bm,),
        in_specs=[pl.BlockSpec((bm, d), lambda t, te, nv: (t, 0)),
                  pl.BlockSpec((None, None, d, f), lambda t, te, nv: (layer, te[t], 0, 0)),
                  pl.BlockSpec((None, None, d, f), lambda t, te, nv: (layer, te[t], 0, 0)),
                  pl.BlockSpec((None, None, f, d), lambda t, te, nv: (layer, te[t], 0, 0))],
        out_specs=pl.BlockSpec((bm, d), lambda t, te, nv: (t, 0)),
    )
    return pl.pallas_call(
        _expert_kernel,
        grid_spec=grid_spec,
        out_shape=jax.ShapeDtypeStruct((p, d), F32),
        compiler_params=_params("arbitrary"),
        name="experts",
    )(tile_e, tile_nvalid, xs, wg, wu, wd)


def _expert_plan(eid, *, bm):
    na = eid.shape[0] * 2
    flat_e = eid.reshape(na)
    onehot = (flat_e[:, None] == jnp.arange(N_EXPERTS, dtype=I32)[None, :]).astype(I32)
    csum = jnp.cumsum(onehot, axis=0)
    counts = csum[-1]
    padded = ((counts + bm - 1) // bm) * bm
    pend = jnp.cumsum(padded)
    pstart = pend - padded
    ppos = jnp.sum(onehot * (pstart[None, :] + csum - 1), axis=1).astype(I32)
    total = na + N_EXPERTS * bm
    tile_start = jnp.arange(total // bm, dtype=I32) * bm
    tile_e = jnp.minimum(jnp.sum((tile_start[:, None] >= pend[None, :]).astype(I32), axis=1), N_EXPERTS - 1)
    tile_nvalid = jnp.clip(pstart[tile_e] + counts[tile_e] - tile_start, 0, bm).astype(I32)
    return ppos, tile_e.astype(I32), tile_nvalid, total


def _combine_ln_kernel(ppos_ref, x_ref, g0_ref, g1_ref, y_hbm, g_ref, b_ref, of_ref, ob_ref, ybuf, sem,
                       *, bm, alpha):
    i = pl.program_id(0)
    d = x_ref.shape[-1]

    def row_copy(blk, a):
        par = blk & 1
        return pltpu.make_async_copy(y_hbm.at[pl.ds(ppos_ref[blk * (2 * bm) + a], 1), :],
                                     ybuf.at[par, a & 1, pl.ds(a >> 1, 1), :], sem.at[par, a & 1])

    def start_block(blk):
        def start(a, c):
            row_copy(blk, a).start()
            return c
        lax.fori_loop(0, 2 * bm, start, 0, unroll=4)

    @pl.when(i == 0)
    def _():
        start_block(i)

    @pl.when(i + 1 < pl.num_programs(0))
    def _():
        start_block(i + 1)

    def wait(a, c):
        row_copy(i, a).wait()
        return c
    lax.fori_loop(0, 2 * bm, wait, 0, unroll=4)

    reps = d // LANES
    par = i & 1
    y = (alpha * x_ref[...] + jnp.tile(g0_ref[...], (1, reps)) * ybuf[par, 0]
         + jnp.tile(g1_ref[...], (1, reps)) * ybuf[par, 1])
    out = _layer_norm(y, g_ref[...], b_ref[...])
    of_ref[...] = out
    ob_ref[...] = out.astype(BF16)


def _combine_ln(x, ys, ppos, g0, g1, g, b, *, bm, alpha):
    n, d = x.shape
    row = lambda w: pl.BlockSpec((bm, w), lambda i, pp: (i, 0))
    grid_spec = pltpu.PrefetchScalarGridSpec(
        num_scalar_prefetch=1,
        grid=(n // bm,),
        in_specs=[row(d), row(LANES), row(LANES), pl.BlockSpec(memory_space=pl.ANY),
                  pl.BlockSpec((1, d), lambda i, pp: (0, 0)), pl.BlockSpec((1, d), lambda i, pp: (0, 0))],
        out_specs=[row(d), row(d)],
        scratch_shapes=[pltpu.VMEM((2, 2, bm, d), F32), pltpu.SemaphoreType.DMA((2, 2))],
    )
    return pl.pallas_call(
        functools.partial(_combine_ln_kernel, bm=bm, alpha=alpha),
        grid_spec=grid_spec,
        out_shape=[jax.ShapeDtypeStruct((n, d), F32), jax.ShapeDtypeStruct((n, d), BF16)],
        compiler_params=_params("arbitrary"),
        name="combine_ln",
    )(ppos, x, g0, g1, ys, g, b)


def _rope_tables(pos):
    posf = pos.astype(F32)[:, None]
    inv128 = ROPE_THETA ** (-jnp.arange(HEAD_DIM // 2, dtype=F32) * 2.0 / HEAD_DIM)
    ang = posf * inv128[None, :]
    cos, sin = jnp.cos(ang), jnp.sin(ang)
    a128 = jnp.concatenate([cos, cos], axis=-1)
    b128 = jnp.concatenate([-sin, sin], axis=-1)
    inv64 = ROPE_THETA ** (-jnp.arange(IDX_DIM // 2, dtype=F32) * 2.0 / IDX_DIM)
    ang = posf * inv64[None, :]
    cos, sin = jnp.cos(ang), jnp.sin(ang)
    zero = jnp.zeros_like(sin)
    a64 = jnp.concatenate([cos, cos, cos, cos], axis=-1)
    b64 = jnp.concatenate([-sin, zero, -sin, zero], axis=-1)
    c64 = jnp.concatenate([zero, sin, zero, sin], axis=-1)
    return (a128, b128), (a64, b64, c64)


def _z_weight(w_in):
    o = _IN_OFF
    d = w_in.shape[0]
    cols = [w_in[:, o[0]:o[4]],
            w_in[:, o[6]:o[9]],
            w_in[:, o[10]:o[12]],
            w_in[:, o[4]:o[5]],
            w_in[:, o[12]:o[13]],
            w_in[:, o[5]:o[6]],
            w_in[:, o[9]:o[10]],
            jnp.zeros((d, LANES - IDX_HEADS - B_HEADS), w_in.dtype)]
    return jnp.concatenate(cols, axis=1).astype(BF16)


def _uq_weight(w):
    r = w.shape[0]
    w3 = w.reshape(r, C_HEADS, C_NOPE + C_ROPE)
    nope = w3[:, :, :C_NOPE].reshape(r, C_HEADS * C_NOPE)
    rope = jnp.pad(w3[:, :, C_NOPE:], ((0, 0), (0, 0), (0, LANES - C_ROPE))).reshape(r, C_HEADS * LANES)
    return jnp.concatenate([nope, rope], axis=1).astype(BF16)


def _pad_lanes(x):
    return jnp.pad(x, [(0, 0)] * (x.ndim - 1) + [(0, LANES - x.shape[-1])])


def _pad_rows(x, rows):
    return jnp.pad(x, [(0, 0), (0, rows - x.shape[1])] + [(0, 0)] * (x.ndim - 2))


def _bf16_head(c):
    bits = lax.bitcast_convert_type(c, I32) & jnp.int32(-65536)
    return lax.bitcast_convert_type(bits, F32)


def _split3(c):
    hi = _bf16_head(c)
    mid = _bf16_head(c - hi)
    lo = c - hi - mid
    return hi.astype(BF16), mid.astype(BF16), lo.astype(BF16)


def _fox_bias_parts(csum_q, csum_k):
    nb, t, h = csum_q.shape
    s = csum_k.shape[1]
    one_q = jnp.ones((nb, t, h), BF16)
    qrows = jnp.stack(list(_split3(csum_q * LOG2E)) + [one_q] * 3, axis=-1)
    qbias = jnp.pad(qrows, ((0, 0), (0, 0), (0, 0), (0, LANES - 6))).reshape(nb, t, h * LANES)
    one_k = jnp.ones((nb, s, h), BF16)
    krows = jnp.stack([one_k] * 3 + [-p for p in _split3(csum_k * LOG2E)], axis=-1)
    kbias = jnp.pad(krows, ((0, 0), (0, 0), (0, 0), (0, LANES - 6))).reshape(nb, s, h * LANES)
    return qbias, kbias


def _pick(n, *cands):
    for c in cands:
        if n % c == 0:
            return c
    return n


def kernel(x_prompt, x_sample, cache_a_k, cache_a_v, cache_a_kidx, cache_b_k, cache_b_v, cache_b_logf,
           cache_c_latent, cache_c_krope, cache_mem_k, cache_mem_v, mem_prompt,
           w_in, b_f, c_q_norm, w_c_uq, c_kv_norm, w_c_uk, w_c_uv, w_o, ln1_g, ln1_b,
           w_mq, w_mk, w_mv, w_mo, ln2_g, ln2_b, w_rg, b_rg, w_re, b_re,
           w_gate, w_up, w_down, ln3_g, ln3_b):
    depth = w_in.shape[0]
    bp, tp, d = x_prompt.shape
    bs, ts, _ = x_sample.shape
    past = cache_a_k.shape[2]
    n_mem = mem_prompt.shape[1]
    assert bp == 1
    n_p, n_s = bp * tp, bs * ts
    n = n_p + n_s
    alpha = (2 * depth) ** 0.25
    s_s = past + ts
    s_pad = -(-s_s // COUNT_ROWS) * COUNT_ROWS

    bm_tok = _pick(n, 1024, 512, 256, 128, 64)
    bm_post = _pick(n, 256, 128, 64)
    bm_ln = _pick(n, 256, 128, 64)
    bq_f = _pick(tp, 1024, 512, 256, 128)
    bq_a = _pick(tp, 512, 256, 128)
    kb_a = _pick(tp, 1024, 512, 256)
    bm_moe = 256

    res_first, res_second, res_second_row0 = x_prompt.reshape(n_p, d), x_sample.reshape(n_s, d), 0
    xb = jnp.concatenate([res_first.astype(BF16), res_second.astype(BF16)], axis=0)
    pos_p = jnp.arange(tp, dtype=I32)
    pos_s = past + jnp.arange(ts, dtype=I32)
    tabs128, tabs64 = _rope_tables(jnp.concatenate([pos_p, jnp.tile(pos_s, bs)]))

    nq_f = tp // bq_f
    nfull_p = jnp.arange(nq_f, dtype=I32)
    nkv_p = nfull_p + 1
    one = jnp.ones((1,), I32)
    zero = jnp.zeros((1,), I32)
    nvis_p = (((jnp.arange(tp // bq_a, dtype=I32) + 1) * bq_a + kb_a - 1) // kb_a).astype(I32)
    nvis_s = jnp.ones((bs,), I32)

    xs_buf = jnp.zeros((2 * n + N_EXPERTS * bm_moe, d), F32)
    w_o_b = w_o.astype(BF16)
    w_gate_b, w_up_b, w_down_b = w_gate.astype(BF16), w_up.astype(BF16), w_down.astype(BF16)
    state_p = [[] for _ in range(8)]
    state_s = [[] for _ in range(8)]
    mem_k_out, mem_v_out = [], []

    for l in range(depth):
        z = _matmul(xb, _z_weight(w_in[l]), bm=bm_tok, bn=512, out_dtype=F32, name="in_proj")
        bf_row = jnp.zeros((1, LANES), F32).at[0, SM2_BF:SM2_BF + B_HEADS].set(b_f[l].astype(F32))
        aq, ak, aqi, bq_, cqn, clat, sm1, sm2, akb, avb, bkb, bvb = _post(
            z, tabs128, tabs64, bf_row, c_q_norm[l].reshape(1, -1).astype(F32),
            c_kv_norm[l].reshape(1, -1).astype(F32), bm=bm_post)
        cq = _matmul_rope(cqn, _uq_weight(w_c_uq[l]), tabs64, bm=bm_tok, bn=512,
                          first_rope_col=C_HEADS * C_NOPE, out_scale=(C_NOPE + C_ROPE) ** -0.5 * LOG2E,
                          name="c_q_up")
        av = z[:, Z_AV:Z_AV + HEAD_DIM]
        aki = sm1[:, :IDX_DIM]
        ckr = sm1[:, IDX_DIM:]
        blogf = sm2[:, SM2_BF:SM2_BF + B_HEADS]
        bk_ = z[:, Z_BK:Z_BK + B_HEADS * HEAD_DIM]
        bv_ = z[:, Z_BV:Z_BV + B_HEADS * HEAD_DIM]
        rows = (ak, av, aki, bk_.reshape(n, B_HEADS, HEAD_DIM), bv_.reshape(n, B_HEADS, HEAD_DIM),
                blogf, clat, ckr)
        for i, r in enumerate(rows):
            state_p[i].append(r[:n_p].reshape((bp, tp) + r.shape[1:]))
            state_s[i].append(r[n_p:].reshape((bs, ts) + r.shape[1:]))
        grp_p = lambda a: a[:n_p].reshape(bp, tp, -1)
        grp_s = lambda a: a[n_p:].reshape(bs, ts, -1)
        cat_s = lambda c, r: jnp.concatenate([c.reshape(bs, past, -1).astype(r.dtype), grp_s(r)], axis=1)
        all_n = lambda a: a.reshape(1, n, -1)

        def sparse(aq_g, aqi_g, aw_g, ki, k, v, nvis, *, bq, kb, s_real, q_off, name):
            ka = _pad_lanes(ki).astype(BF16)
            kb_ = jnp.concatenate([jnp.zeros_like(ki), ki], axis=-1).astype(BF16)
            out = _sparse_attention(aq_g, aqi_g, aw_g, ka, kb_, k, v, nvis, bq=bq, kb=kb, s_real=s_real,
                                    topk=min(TOPK_MAX, s_real // 4), q_off=q_off, name=name)
            return out.reshape(-1, A_HEADS * HEAD_DIM)

        oa_p = sparse(all_n(aq), all_n(aqi), all_n(sm2), grp_p(aki), grp_p(akb), grp_p(avb), nvis_p,
                      bq=bq_a, kb=kb_a, s_real=tp, q_off=0, name="sparse_prompt")
        oa_s = sparse(grp_s(aq), grp_s(aqi), grp_s(sm2),
                      _pad_rows(cat_s(cache_a_kidx[l], aki), s_pad), _pad_rows(cat_s(cache_a_k[l], akb), s_pad),
                      _pad_rows(cat_s(cache_a_v[l], avb), s_pad), nvis_s,
                      bq=ts, kb=s_pad, s_real=s_s, q_off=past, name="sparse_sample")

        csum_p = jnp.cumsum(grp_p(blogf), axis=1)
        qb_p, kbias_p = _fox_bias_parts(csum_p, csum_p)
        ob_p = _tflash([all_n(bq_), qb_p], (0, 0), [all_n(bkb), kbias_p], (False, False), all_n(bvb), 0,
                       nfull_p, nkv_p, heads=B_HEADS, s=tp, hb=2, bq=bq_f, bk=bq_f, mode="causal", q_off=0,
                       name="fox_prompt")
        csum_s = jnp.cumsum(cat_s(cache_b_logf[l], blogf), axis=1)
        qb_s, kbias_s = _fox_bias_parts(csum_s[:, past:], csum_s)
        ob_s = _tflash([grp_s(bq_), qb_s], (0, 0), [cat_s(cache_b_k[l], bkb), kbias_s], (False, False),
                       cat_s(cache_b_v[l], bvb), 0, zero, one, heads=B_HEADS, s=s_s, hb=B_HEADS, bq=ts, bk=s_s,
                       mode="causal", q_off=past, name="fox_sample")

        w_ukv = jnp.concatenate([w_c_uk[l], w_c_uv[l]], axis=1).astype(BF16)
        kv_p = _matmul(clat[:n_p], w_ukv, bm=_pick(n_p, 1024, 512, 256), bn=w_ukv.shape[1], out_dtype=BF16,
                       name="c_kv_up_prompt").reshape(bp, tp, -1)
        oc_p = _tflash([all_n(cq), all_n(cq)], (0, C_HEADS),
                       [kv_p, _pad_lanes(grp_p(ckr)).astype(BF16)], (False, True), kv_p, C_HEADS,
                       nfull_p, nkv_p, heads=C_HEADS, s=tp, hb=2, bq=bq_f, bk=bq_f, mode="chunk", q_off=0,
                       name="mla_prompt")
        lat_s = cat_s(cache_c_latent[l], clat)
        kv_s = _matmul(lat_s.reshape(bs * s_s, -1), w_ukv, bm=_pick(bs * s_s, 1024, 512, 256, 128, 64),
                       bn=w_ukv.shape[1], out_dtype=BF16, name="c_kv_up_sample").reshape(bs, s_s, -1)
        oc_s = _tflash([grp_s(cq), grp_s(cq)], (0, C_HEADS),
                       [kv_s, _pad_lanes(cat_s(cache_c_krope[l], ckr)).astype(BF16)], (False, True), kv_s, C_HEADS,
                       zero, one, heads=C_HEADS, s=s_s, hb=C_HEADS // 2, bq=ts, bk=s_s, mode="chunk", q_off=past,
                       name="mla_sample")

        x1, x1b = _mm_ln_cols([oa_p, ob_p.reshape(n_p, -1), oc_p.reshape(n_p, -1)],
                              [oa_s, ob_s.reshape(n_s, -1), oc_s.reshape(n_s, -1)],
                              w_o_b, l, res_first, res_second, res_second_row0,
                              ln1_g[l].reshape(1, d), ln1_b[l].reshape(1, d),
                              bm=_pick(math.gcd(n_p, n_s), 512, 256, 128, 64), bn=_pick(d, 512, 256, 128),
                              alpha=alpha, name="out_proj_ln1")

        mw = MEM_HEADS * MEM_DIM
        w_mkv = jnp.concatenate([w_mk[l], w_mv[l]], axis=1).astype(BF16)
        mkv = _matmul(mem_prompt.reshape(bp * n_mem, d), w_mkv, bm=_pick(bp * n_mem, 256, 128, 64), bn=512,
                      out_dtype=F32, name="mem_kv")
        mem_k_out.append(mkv[:, :mw].reshape(bp, n_mem, MEM_HEADS, MEM_DIM))
        mem_v_out.append(mkv[:, mw:].reshape(bp, n_mem, MEM_HEADS, MEM_DIM))
        qm = _matmul(x1b, w_mq[l].astype(BF16), bm=bm_tok, bn=mw, out_dtype=BF16, name="mem_q",
                     out_scale=MEM_DIM ** -0.5 * LOG2E)
        mkv3 = mkv.reshape(bp, n_mem, 2 * mw).astype(BF16)
        om_p = _tflash([all_n(qm)], (0,), [mkv3], (False,), mkv3, MEM_HEADS,
                       jnp.ones((nq_f,), I32), jnp.ones((nq_f,), I32), heads=MEM_HEADS, s=n_mem,
                       hb=2, bq=bq_f, bk=n_mem, mode="none", q_off=0, name="mem_prompt")
        om_s = _tflash([grp_s(qm)], (0,), [cache_mem_k[l].reshape(bs, n_mem, mw).astype(BF16)], (False,),
                       cache_mem_v[l].reshape(bs, n_mem, mw).astype(BF16), 0, one, one, heads=MEM_HEADS, s=n_mem,
                       hb=MEM_HEADS, bq=ts, bk=n_mem, mode="none", q_off=0, name="mem_sample")
        om = jnp.concatenate([om_p.reshape(n_p, mw), om_s.reshape(n_s, mw)], axis=0)
        x2, _ = _mm_res_ln(om, w_mo[l].astype(BF16), x1, ln2_g[l].reshape(1, d), ln2_b[l].reshape(1, d),
                           bm=bm_ln, bk=mw, alpha=alpha, name="mem_out_ln2")

        w_r = jnp.concatenate([w_rg[l], jnp.transpose(w_re[l], (1, 0, 2)).reshape(d, N_EXPERTS)], axis=1).astype(F32)
        w_r = _pad_lanes(w_r)
        w_r_head = _bf16_head(w_r)
        w_r_hi = w_r_head.astype(BF16)
        w_r_lo = (w_r - w_r_head).astype(BF16)
        b_r = _pad_lanes(jnp.concatenate([b_rg[l], b_re[l].reshape(-1)]).astype(F32).reshape(1, -1))
        ids, g0, g1 = _router(x2, w_r_hi, w_r_lo, b_r, bm=_pick(n, 512, 256, 128, 64))
        ppos, tile_e, tile_nvalid, total = _expert_plan(ids[:, :2], bm=bm_moe)
        assert total == xs_buf.shape[0]
        xs_buf = _dispatch(x2, ppos, xs_buf, bm=_pick(n, 256, 128, 64))
        ys = _experts(xs_buf, w_gate_b, w_up_b, w_down_b, l, tile_e, tile_nvalid, bm=bm_moe)
        x, xb = _combine_ln(x2, ys, ppos, g0, g1, ln3_g[l].reshape(1, d), ln3_b[l].reshape(1, d),
                            bm=_pick(n, 256, 128, 64), alpha=alpha)
        res_first, res_second, res_second_row0 = x, x, n_p

    outs = [x[:n_p].reshape(bp, tp, d), x[n_p:].reshape(bs, ts, d)]
    outs += [jnp.stack(c) for c in state_p]
    outs += [jnp.stack(mem_k_out), jnp.stack(mem_v_out)]
    outs += [jnp.stack(c) for c in state_s]
    return tuple(outs)
```

```python
import functools
import math

import jax
import jax.numpy as jnp
import numpy as np
from jax import lax
from jax.experimental import pallas as pl
from jax.experimental.pallas import tpu as pltpu

F32 = jnp.float32
BF16 = jnp.bfloat16
I32 = jnp.int32

LANES = 128
SUBLANES = 8
VMEM_LIMIT = 56 * 1024 * 1024

CHUNK_SHIFT = 6
HEAD_DIM = 128
ROPE_THETA = 10000.0
A_HEADS = 8
IDX_HEADS = 32
IDX_DIM = 64
TOPK_MAX = 256
B_HEADS = 8
C_HEADS = 16
C_Q_RANK = 768
C_KV_RANK = 256
C_NOPE = 128
C_ROPE = 64
C_V = 128
MEM_HEADS = 4
MEM_DIM = 128
N_GROUPS = 4
EXPERTS_PER_GROUP = 4
N_EXPERTS = N_GROUPS * EXPERTS_PER_GROUP
LN_EPS = 1e-5
RMS_EPS = 1e-6
LOG2E = math.log2(math.e)

_IN_SIZES = (A_HEADS * HEAD_DIM, HEAD_DIM, HEAD_DIM, IDX_HEADS * IDX_DIM, IDX_DIM, IDX_HEADS,
             B_HEADS * HEAD_DIM, B_HEADS * HEAD_DIM, B_HEADS * HEAD_DIM, B_HEADS,
             C_Q_RANK, C_KV_RANK, C_ROPE)
_IN_OFF = np.concatenate([[0], np.cumsum(_IN_SIZES)]).astype(int)

Z_AQ = 0
Z_AK = Z_AQ + A_HEADS * HEAD_DIM
Z_AV = Z_AK + HEAD_DIM
Z_AQI = Z_AV + HEAD_DIM
Z_BQ = Z_AQI + IDX_HEADS * IDX_DIM
Z_BK = Z_BQ + B_HEADS * HEAD_DIM
Z_BV = Z_BK + B_HEADS * HEAD_DIM
Z_CQ = Z_BV + B_HEADS * HEAD_DIM
Z_CKV = Z_CQ + C_Q_RANK
Z_SM1 = Z_CKV + C_KV_RANK
Z_SM2 = Z_SM1 + LANES
Z_WIDTH = Z_SM2 + LANES
SM2_BF = IDX_HEADS

M_INIT = -1e30
INT_MIN = -2 ** 31
INT_MAX = 2 ** 31 - 1
SCORE_ROWS = 128
SCORE_COLS = 256
COUNT_ROWS = 256
LN_ROWS = 64


def _params(*sem):
    return pltpu.CompilerParams(dimension_semantics=sem, vmem_limit_bytes=VMEM_LIMIT)


def _mm_kernel(x_ref, w_ref, o_ref, *, out_scale):
    x = x_ref[...].astype(BF16)
    acc = jnp.dot(x, w_ref[...], preferred_element_type=F32)
    if out_scale != 1.0:
        acc = acc * out_scale
    o_ref[...] = acc.astype(o_ref.dtype)


def _matmul(x, w, *, bm, bn, out_dtype, name, out_scale=1.0):
    m, k = x.shape
    n = w.shape[1]
    assert m % bm == 0 and n % bn == 0, (x.shape, w.shape, bm, bn)
    return pl.pallas_call(
        functools.partial(_mm_kernel, out_scale=out_scale),
        grid=(m // bm, n // bn),
        in_specs=[pl.BlockSpec((bm, k), lambda i, j: (i, 0)),
                  pl.BlockSpec((k, bn), lambda i, j: (0, j))],
        out_specs=pl.BlockSpec((bm, bn), lambda i, j: (i, j)),
        out_shape=jax.ShapeDtypeStruct((m, n), out_dtype),
        compiler_params=_params("parallel", "parallel"),
        name=name,
    )(x, w)


def _rope64(x, a, b, c):
    return x * a + pltpu.roll(x, 96, 1) * b + pltpu.roll(x, 32, 1) * c


def _rope128(x, a, b):
    return x * a + pltpu.roll(x, 64, 1) * b


def _mm_rope_kernel(x_ref, w_ref, a_ref, b_ref, c_ref, o_ref, *, first_rope_block, out_scale):
    j = pl.program_id(1)
    acc = jnp.dot(x_ref[...], w_ref[...], preferred_element_type=F32) * out_scale

    @pl.when(j < first_rope_block)
    def _():
        o_ref[...] = acc.astype(o_ref.dtype)

    @pl.when(j >= first_rope_block)
    def _():
        a, b, c = a_ref[...], b_ref[...], c_ref[...]
        for g in range(acc.shape[1] // LANES):
            sl = slice(g * LANES, (g + 1) * LANES)
            o_ref[:, sl] = _rope64(acc[:, sl], a, b, c).astype(o_ref.dtype)


def _matmul_rope(x, w, tabs, *, bm, bn, first_rope_col, out_scale, name):
    m, k = x.shape
    n = w.shape[1]
    assert m % bm == 0 and n % bn == 0 and first_rope_col % bn == 0
    tab_spec = pl.BlockSpec((bm, LANES), lambda i, j: (i, 0))
    return pl.pallas_call(
        functools.partial(_mm_rope_kernel, first_rope_block=first_rope_col // bn, out_scale=out_scale),
        grid=(m // bm, n // bn),
        in_specs=[pl.BlockSpec((bm, k), lambda i, j: (i, 0)),
                  pl.BlockSpec((k, bn), lambda i, j: (0, j)),
                  tab_spec, tab_spec, tab_spec],
        out_specs=pl.BlockSpec((bm, bn), lambda i, j: (i, j)),
        out_shape=jax.ShapeDtypeStruct((m, n), BF16),
        compiler_params=_params("parallel", "parallel"),
        name=name,
    )(x, w, *tabs)


def _post_kernel(z_ref, a128_ref, b128_ref, a64_ref, b64_ref, c64_ref, bf_ref, gq_ref, gkv_ref,
                 aq_ref, ak_ref, aqi_ref, bq_ref, cqn_ref, clat_ref, sm1_ref, sm2_ref,
                 akb_ref, avb_ref, bkb_ref, bvb_ref):
    a128, b128 = a128_ref[...], b128_ref[...]
    a64, b64, c64 = a64_ref[...], b64_ref[...], c64_ref[...]
    qscale = HEAD_DIM ** -0.5 * LOG2E
    for g in range(A_HEADS):
        src = slice(Z_AQ + g * LANES, Z_AQ + (g + 1) * LANES)
        aq_ref[:, g * LANES:(g + 1) * LANES] = (_rope128(z_ref[:, src], a128, b128) * qscale).astype(BF16)
    ak = _rope128(z_ref[:, Z_AK:Z_AK + LANES], a128, b128)
    ak_ref[...] = ak
    akb_ref[...] = ak.astype(BF16)
    avb_ref[...] = z_ref[:, Z_AV:Z_AV + LANES].astype(BF16)
    bkb_ref[...] = z_ref[:, Z_BK:Z_BK + B_HEADS * HEAD_DIM].astype(BF16)
    bvb_ref[...] = z_ref[:, Z_BV:Z_BV + B_HEADS * HEAD_DIM].astype(BF16)
    for g in range(IDX_HEADS * IDX_DIM // LANES):
        src = slice(Z_AQI + g * LANES, Z_AQI + (g + 1) * LANES)
        aqi_ref[:, g * LANES:(g + 1) * LANES] = _rope64(z_ref[:, src], a64, b64, c64).astype(BF16)
    bq_ref[...] = (z_ref[:, Z_BQ:Z_BQ + B_HEADS * HEAD_DIM] * qscale).astype(BF16)
    cq = z_ref[:, Z_CQ:Z_CQ + C_Q_RANK]
    cq = cq * lax.rsqrt(jnp.mean(cq * cq, axis=-1, keepdims=True) + RMS_EPS) * gq_ref[...]
    cqn_ref[...] = cq.astype(BF16)
    ckv = z_ref[:, Z_CKV:Z_CKV + C_KV_RANK]
    clat_ref[...] = ckv * lax.rsqrt(jnp.mean(ckv * ckv, axis=-1, keepdims=True) + RMS_EPS) * gkv_ref[...]
    sm1_ref[...] = _rope64(z_ref[:, Z_SM1:Z_SM1 + LANES], a64, b64, c64)
    s2 = z_ref[:, Z_SM2:Z_SM2 + LANES]
    lane = lax.broadcasted_iota(I32, s2.shape, 1)
    f = s2 + bf_ref[...]
    logf = jnp.minimum(f, 0.0) - jnp.log1p(jnp.exp(-jnp.abs(f)))
    sm2_ref[...] = jnp.where(lane < SM2_BF, s2 * (IDX_HEADS ** -0.5 * IDX_DIM ** -0.5),
                             jnp.where(lane < SM2_BF + B_HEADS, logf, 0.0))


def _post(z, tabs128, tabs64, bf_row, gq, gkv, *, bm):
    n = z.shape[0]
    row = lambda w: pl.BlockSpec((bm, w), lambda i: (i, 0))
    const = lambda w: pl.BlockSpec((1, w), lambda i: (0, 0))
    widths = (A_HEADS * HEAD_DIM, LANES, IDX_HEADS * IDX_DIM, B_HEADS * HEAD_DIM, C_Q_RANK, C_KV_RANK, LANES, LANES,
              LANES, LANES, B_HEADS * HEAD_DIM, B_HEADS * HEAD_DIM)
    dtypes = (BF16, F32, BF16, BF16, BF16, F32, F32, F32, BF16, BF16, BF16, BF16)
    return pl.pallas_call(
        _post_kernel,
        grid=(n // bm,),
        in_specs=[row(Z_WIDTH)] + [row(LANES)] * 5 + [const(LANES), const(C_Q_RANK), const(C_KV_RANK)],
        out_specs=[row(w) for w in widths],
        out_shape=[jax.ShapeDtypeStruct((n, w), d) for w, d in zip(widths, dtypes)],
        compiler_params=_params("parallel"),
        name="post_projection",
    )(z, *tabs128, *tabs64, bf_row, gq, gkv)


def _layer_norm(y, g, b):
    mu = jnp.mean(y, axis=-1, keepdims=True)
    yc = y - mu
    var = jnp.mean(yc * yc, axis=-1, keepdims=True)
    return yc * lax.rsqrt(var + LN_EPS) * g + b


def _mm_res_ln_kernel(x_ref, w_ref, r_ref, g_ref, b_ref, of_ref, ob_ref, acc_ref, *, nk, alpha):
    k = pl.program_id(1)

    @pl.when(k == 0)
    def _():
        acc_ref[...] = jnp.zeros_like(acc_ref)

    acc_ref[...] += jnp.dot(x_ref[...], w_ref[...], preferred_element_type=F32)

    @pl.when(k == nk - 1)
    def _():
        out = _layer_norm(alpha * r_ref[...] + acc_ref[...], g_ref[...], b_ref[...])
        of_ref[...] = out
        ob_ref[...] = out.astype(BF16)


def _mm_res_ln(x, w, res, g, b, *, bm, bk, alpha, name):
    m, k = x.shape
    d = w.shape[1]
    nk = k // bk
    return pl.pallas_call(
        functools.partial(_mm_res_ln_kernel, nk=nk, alpha=alpha),
        grid=(m // bm, nk),
        in_specs=[pl.BlockSpec((bm, bk), lambda i, kk: (i, kk)),
                  pl.BlockSpec((bk, d), lambda i, kk: (kk, 0)),
                  pl.BlockSpec((bm, d), lambda i, kk: (i, 0)),
                  pl.BlockSpec((1, d), lambda i, kk: (0, 0)),
                  pl.BlockSpec((1, d), lambda i, kk: (0, 0))],
        out_specs=[pl.BlockSpec((bm, d), lambda i, kk: (i, 0)),
                   pl.BlockSpec((bm, d), lambda i, kk: (i, 0))],
        out_shape=[jax.ShapeDtypeStruct((m, d), F32), jax.ShapeDtypeStruct((m, d), BF16)],
        scratch_shapes=[pltpu.VMEM((bm, d), F32)],
        compiler_params=_params("parallel", "arbitrary"),
        name=name,
    )(x, w, res, g, b)


def _mm_ln_cols_kernel(*refs, n_parts, nj, bn, alpha, first_rows):
    xa = refs[:n_parts]
    xb = refs[n_parts:2 * n_parts]
    w_ref, ra_ref, rb_ref, g_ref, b_ref, of_ref, ob_ref = refs[2 * n_parts:]
    i = pl.program_id(0)
    j = pl.program_id(1)

    def emit(x_refs, r_ref):
        x = jnp.concatenate([xr[...] for xr in x_refs], axis=1)
        y = alpha * r_ref[...] + jnp.dot(x, w_ref[...], preferred_element_type=F32)
        for jj in range(nj):
            @pl.when(j == jj)
            def _(jj=jj):
                of_ref[:, jj * bn:(jj + 1) * bn] = y

    @pl.when(i < first_rows)
    def _():
        emit(xa, ra_ref)

    @pl.when(i >= first_rows)
    def _():
        emit(xb, rb_ref)

    @pl.when(j == nj - 1)
    def _():
        for r0 in range(0, of_ref.shape[0], LN_ROWS):
            rs = slice(r0, r0 + LN_ROWS)
            out = _layer_norm(of_ref[rs, :], g_ref[...], b_ref[...])
            of_ref[rs, :] = out
            ob_ref[rs, :] = out.astype(BF16)


def _mm_ln_cols(x_first, x_second, w, layer, res_first, res_second, res_second_row0, g, b, *, bm, bn, alpha, name):
    m1, m2 = x_first[0].shape[0], x_second[0].shape[0]
    d = w.shape[-1]
    k = w.shape[-2]
    nj = d // bn
    assert m1 % bm == 0 and m2 % bm == 0 and res_second_row0 % bm == 0 and d % bn == 0
    nb1, nb2, r0b = m1 // bm, m2 // bm, res_second_row0 // bm
    first = lambda i: jnp.minimum(i, nb1 - 1)
    second = lambda i: jnp.maximum(i - nb1, 0)
    once = pl.Buffered(1)
    in_specs = [pl.BlockSpec((bm, xp.shape[1]), lambda i, j: (first(i), 0), pipeline_mode=once) for xp in x_first]
    in_specs += [pl.BlockSpec((bm, xp.shape[1]), lambda i, j: (second(i), 0), pipeline_mode=once)
                 for xp in x_second]
    in_specs += [pl.BlockSpec((None, k, bn), lambda i, j: (layer, 0, j)),
                 pl.BlockSpec((bm, bn), lambda i, j: (first(i), j)),
                 pl.BlockSpec((bm, bn), lambda i, j: (second(i) + r0b, j)),
                 pl.BlockSpec((1, d), lambda i, j: (0, 0)),
                 pl.BlockSpec((1, d), lambda i, j: (0, 0))]
    return pl.pallas_call(
        functools.partial(_mm_ln_cols_kernel, n_parts=len(x_first), nj=nj, bn=bn, alpha=alpha, first_rows=nb1),
        grid=(nb1 + nb2, nj),
        in_specs=in_specs,
        out_specs=[pl.BlockSpec((bm, d), lambda i, j: (i, 0)),
                   pl.BlockSpec((bm, d), lambda i, j: (i, 0))],
        out_shape=[jax.ShapeDtypeStruct((m1 + m2, d), F32), jax.ShapeDtypeStruct((m1 + m2, d), BF16)],
        compiler_params=_params("parallel", "arbitrary"),
        name=name,
    )(*x_first, *x_second, w, res_first, res_second, g, b)


def _tflash_kernel(nfull_ref, nkv_ref, *refs, n_parts, k_shared, hb, bk, mode, q_off):
    refs = list(refs)
    q_refs = [refs.pop(0) for _ in range(n_parts)]
    k_refs = [refs.pop(0) for _ in range(n_parts)]
    v_ref = refs.pop(0)
    o_ref, acc_ref = refs
    qi = pl.program_id(2)
    bq = o_ref.shape[0]
    qs = []
    for h in range(hb):
        hs = slice(h * LANES, (h + 1) * LANES)
        qs.append(jnp.concatenate([qr[:, hs].astype(F32).T.astype(BF16) for qr in q_refs], axis=0))
    acc_ref[...] = jnp.zeros(acc_ref.shape, F32)

    def block(j, carry, masked):
        ms, ls = carry
        off = pl.multiple_of(j * bk, bk)
        if masked:
            kpos = off + lax.broadcasted_iota(I32, (bk, bq), 0)
            qpos = q_off + qi * bq + lax.broadcasted_iota(I32, (bk, bq), 1)
            if mode == "chunk":
                mask = (kpos >> CHUNK_SHIFT) <= (qpos >> CHUNK_SHIFT)
            else:
                mask = kpos <= qpos
        new_m, new_l = [], []
        for h in range(hb):
            parts = []
            for p_i, kr in enumerate(k_refs):
                hs = 0 if k_shared[p_i] else h
                parts.append(kr[pl.ds(off, bk), hs * LANES:(hs + 1) * LANES])
            k = jnp.concatenate(parts, axis=1) if n_parts > 1 else parts[0]
            s = jnp.dot(k, qs[h], preferred_element_type=F32)
            if masked:
                s = jnp.where(mask, s, -jnp.inf)
            m_new = jnp.maximum(ms[h], jnp.max(s, axis=0, keepdims=True))
            p = jnp.exp2(s - m_new)
            alpha = jnp.exp2(ms[h] - m_new)
            new_l.append(alpha * ls[h] + jnp.sum(p, axis=0, keepdims=True))
            new_m.append(m_new)
            v = v_ref[pl.ds(off, bk), h * LANES:(h + 1) * LANES]
            acc_ref[h] = alpha * acc_ref[h] + lax.dot_general(v, p.astype(BF16), (((0,), (0,)), ((), ())),
                                                              preferred_element_type=F32)
        return tuple(new_m), tuple(new_l)

    init = (tuple(jnp.full((1, bq), M_INIT, F32) for _ in range(hb)),
            tuple(jnp.zeros((1, bq), F32) for _ in range(hb)))
    carry = lax.fori_loop(0, nfull_ref[qi], functools.partial(block, masked=False), init)
    if mode != "none":
        carry = lax.fori_loop(nfull_ref[qi], nkv_ref[qi], functools.partial(block, masked=True), carry)
    for h in range(hb):
        o_ref[:, h * LANES:(h + 1) * LANES] = (acc_ref[h] / carry[1][h]).T.astype(o_ref.dtype)


def _tflash(q_parts, q_head0, k_parts, k_shared, v, v_head0, nfull, nkv, *, heads, s, hb, bq, bk, mode, q_off,
            name):
    nb = v.shape[0]
    nq = nfull.shape[0]
    assert heads % hb == 0 and v_head0 % hb == 0 and all(h0 % hb == 0 for h0 in q_head0) and s % bk == 0
    in_specs = []
    for h0 in q_head0:
        in_specs.append(pl.BlockSpec((None, bq, hb * LANES),
                                     lambda b, g, qi, nf, nk, h0=h0: (b, qi, h0 // hb + g)))
    for shared in k_shared:
        if shared:
            in_specs.append(pl.BlockSpec((None, s, LANES), lambda b, g, qi, nf, nk: (b, 0, 0)))
        else:
            in_specs.append(pl.BlockSpec((None, s, hb * LANES), lambda b, g, qi, nf, nk: (b, 0, g)))
    in_specs.append(pl.BlockSpec((None, s, hb * LANES), lambda b, g, qi, nf, nk: (b, 0, v_head0 // hb + g)))
    grid_spec = pltpu.PrefetchScalarGridSpec(
        num_scalar_prefetch=2,
        grid=(nb, heads // hb, nq),
        in_specs=in_specs,
        out_specs=pl.BlockSpec((None, bq, hb * LANES), lambda b, g, qi, nf, nk: (b, qi, g)),
        scratch_shapes=[pltpu.VMEM((hb, LANES, bq), F32)],
    )
    return pl.pallas_call(
        functools.partial(_tflash_kernel, n_parts=len(q_parts), k_shared=tuple(k_shared), hb=hb, bk=bk,
                          mode=mode, q_off=q_off),
        grid_spec=grid_spec,
        out_shape=jax.ShapeDtypeStruct((nb, nq * bq, heads * LANES), BF16),
        compiler_params=_params("parallel", "parallel", "arbitrary"),
        name=name,
    )(nfull, nkv, *q_parts, *k_parts, v)


def _sparse_kernel(nvis_ref, aq_in, aqi_in, aw_in, ka_ref, kb_ref, k_ref, v_ref,
                   o_ref, aq_ref, aqi_ref, aw_ref, key_ref, bias_ref, acc_ref, j_ref,
                   *, nq, kb, s_real, topk, index_bits, q_off):
    b = pl.program_id(0)
    qi = pl.program_id(1)
    nblk = nvis_ref[b * nq + qi]
    bq = aq_in.shape[0]
    cols = min(bq, SCORE_COLS)
    int_min = jnp.int32(INT_MIN)
    qchunk = (q_off + qi * bq + lax.broadcasted_iota(I32, (1, bq), 1)) >> CHUNK_SHIFT

    for g in range(aq_in.shape[1] // LANES):
        gs = slice(g * LANES, (g + 1) * LANES)
        aq_ref[gs, :] = aq_in[:, gs].astype(F32).T.astype(BF16)
    for g in range(aqi_in.shape[1] // LANES):
        gs = slice(g * LANES, (g + 1) * LANES)
        aqi_ref[gs, :] = aqi_in[:, gs].astype(F32).T.astype(BF16)
    aw_ref[...] = aw_in[...].T

    def score_tile(t, carry):
        off = pl.multiple_of(t * SCORE_ROWS, SCORE_ROWS)
        ka = ka_ref[pl.ds(off, SCORE_ROWS), :]
        kbb = kb_ref[pl.ds(off, SCORE_ROWS), :]
        kidx = off + lax.broadcasted_iota(I32, (SCORE_ROWS, cols), 0)
        kchunk = jnp.where(kidx < s_real, kidx >> CHUNK_SHIFT, jnp.int32(INT_MAX))
        for c in range(bq // cols):
            cs = slice(c * cols, (c + 1) * cols)
            acc = jnp.zeros((SCORE_ROWS, cols), F32)
            for hp in range(IDX_HEADS // 2):
                qp = aqi_ref[hp * LANES:(hp + 1) * LANES, cs]
                ze = jnp.dot(ka, qp, preferred_element_type=F32)
                zo = jnp.dot(kbb, qp, preferred_element_type=F32)
                acc = acc + jnp.maximum(ze, 0.0) * aw_ref[2 * hp:2 * hp + 1, cs]
                acc = acc + jnp.maximum(zo, 0.0) * aw_ref[2 * hp + 1:2 * hp + 2, cs]
            bits = pltpu.bitcast(acc, I32)
            key = bits ^ ((bits >> 31) & jnp.int32(INT_MAX))
            key_ref[pl.ds(off, SCORE_ROWS), cs] = jnp.where(kchunk <= qchunk[:, cs], key, int_min)
        return carry

    lax.fori_loop(0, nblk * (kb // SCORE_ROWS), score_tile, 0)

    def count(pred):
        def body(t, c):
            off = pl.multiple_of(t * COUNT_ROWS, COUNT_ROWS)
            kk = key_ref[pl.ds(off, COUNT_ROWS), :]
            idx = off + lax.broadcasted_iota(I32, (COUNT_ROWS, bq), 0)
            hit = jnp.where(pred(kk, idx), 1.0, 0.0)
            return c + jnp.sum(hit.reshape(COUNT_ROWS // SUBLANES, SUBLANES, bq), axis=0)
        c = lax.fori_loop(0, nblk * (kb // COUNT_ROWS), body, jnp.zeros((SUBLANES, bq), F32))
        return jnp.sum(c, axis=0, keepdims=True)

    def bit_step(i, carry):
        t_u, n_keep = carry
        cand_u = t_u | (jnp.int32(1) << (31 - i))
        cand_s = cand_u ^ int_min
        cnt = count(lambda kk, idx: kk >= cand_s)
        take = cnt >= topk
        return jnp.where(take, cand_u, t_u), jnp.where(take, cnt, n_keep)

    t_u, n_ge = lax.fori_loop(0, 32, bit_step, (jnp.zeros((1, bq), I32), jnp.zeros((1, bq), F32)))
    t_s = t_u ^ int_min
    need = jnp.logical_and(n_ge > topk, t_u != 0)
    j_ref[...] = jnp.full(j_ref.shape, INT_MAX, I32)

    @pl.when(jnp.max(jnp.where(need, 1.0, 0.0)) > 0.0)
    def _():
        want = topk - count(lambda kk, idx: kk > t_s)

        def idx_step(i, jc):
            cand = jc | (jnp.int32(1) << (index_bits - 1 - i))
            cnt = count(lambda kk, idx: jnp.logical_and(kk == t_s, idx < cand))
            return jnp.where(cnt < want, cand, jc)
        jc = lax.fori_loop(0, index_bits, idx_step, jnp.zeros((1, bq), I32))
        j_ref[...] = jnp.where(need, jc, jnp.int32(INT_MAX))

    jcut = j_ref[...]
    acc_ref[...] = jnp.zeros(acc_ref.shape, F32)

    def attend_block(j, carry):
        ms, ls = carry
        off = pl.multiple_of(j * kb, kb)
        kk = key_ref[pl.ds(off, kb), :]
        idx = off + lax.broadcasted_iota(I32, (kb, bq), 0)
        sel = jnp.logical_or(kk > t_s, jnp.logical_and(kk == t_s, idx <= jcut))
        sel = jnp.logical_and(sel, kk != int_min)
        bias_ref[...] = jnp.where(sel, 0.0, -jnp.inf)
        kblk = k_ref[pl.ds(off, kb), :]
        vblk = v_ref[pl.ds(off, kb), :]
        new_m, new_l = [], []
        for h in range(A_HEADS):
            hs = slice(h * HEAD_DIM, (h + 1) * HEAD_DIM)
            s = jnp.dot(kblk, aq_ref[hs, :], preferred_element_type=F32) + bias_ref[...]
            m_new = jnp.maximum(ms[h], jnp.max(s, axis=0, keepdims=True))
            p = jnp.exp2(s - m_new)
            alpha = jnp.exp2(ms[h] - m_new)
            new_l.append(alpha * ls[h] + jnp.sum(p, axis=0, keepdims=True))
            new_m.append(m_new)
            acc_ref[hs, :] = alpha * acc_ref[hs, :] + lax.dot_general(
                vblk, p.astype(BF16), (((0,), (0,)), ((), ())), preferred_element_type=F32)
        return tuple(new_m), tuple(new_l)

    init = (tuple(jnp.full((1, bq), M_INIT, F32) for _ in range(A_HEADS)),
            tuple(jnp.zeros((1, bq), F32) for _ in range(A_HEADS)))
    _, ls = lax.fori_loop(0, nblk, attend_block, init)
    for h in range(A_HEADS):
        hs = slice(h * HEAD_DIM, (h + 1) * HEAD_DIM)
        o_ref[:, hs] = (acc_ref[hs, :] / ls[h]).T.astype(o_ref.dtype)


def _sparse_attention(aq, aqi, aw, ka, kb_, k, v, nvis, *, bq, kb, s_real, topk, q_off, name):
    nb, s = k.shape[0], k.shape[1]
    nq = nvis.shape[0] // nb
    assert s % kb == 0 and kb % COUNT_ROWS == 0 and kb % SCORE_ROWS == 0 and bq % min(bq, SCORE_COLS) == 0
    qspec = lambda w: pl.BlockSpec((None, bq, w), lambda b, qi, nv: (b, qi, 0))
    kspec = pl.BlockSpec((None, s, LANES), lambda b, qi, nv: (b, 0, 0))
    aq_w, aqi_w = A_HEADS * HEAD_DIM, IDX_HEADS * IDX_DIM
    grid_spec = pltpu.PrefetchScalarGridSpec(
        num_scalar_prefetch=1,
        grid=(nb, nq),
        in_specs=[qspec(aq_w), qspec(aqi_w), qspec(LANES), kspec, kspec, kspec, kspec],
        out_specs=qspec(aq_w),
        scratch_shapes=[pltpu.VMEM((aq_w, bq), BF16), pltpu.VMEM((aqi_w, bq), BF16), pltpu.VMEM((LANES, bq), F32),
                        pltpu.VMEM((s, bq), I32), pltpu.VMEM((kb, bq), F32),
                        pltpu.VMEM((aq_w, bq), F32), pltpu.VMEM((1, bq), I32)],
    )
    return pl.pallas_call(
        functools.partial(_sparse_kernel, nq=nq, kb=kb, s_real=s_real, topk=topk,
                          index_bits=max(1, (s - 1).bit_length()), q_off=q_off),
        grid_spec=grid_spec,
        out_shape=jax.ShapeDtypeStruct((nb, nq * bq, aq_w), BF16),
        compiler_params=_params("parallel", "arbitrary"),
        name=name,
    )(nvis, aq, aqi, aw, ka, kb_, k, v)


def _router_kernel(x_ref, whi_ref, wlo_ref, bias_ref, ids_ref, g0_ref, g1_ref):
    x = x_ref[...]
    xh = x.astype(BF16)
    xl = (x - xh.astype(F32)).astype(BF16)
    whi, wlo = whi_ref[...], wlo_ref[...]
    lg = (jnp.dot(xh, whi, preferred_element_type=F32) + jnp.dot(xh, wlo, preferred_element_type=F32)
          + jnp.dot(xl, whi, preferred_element_type=F32)) + bias_ref[...]
    lane = lax.broadcasted_iota(I32, lg.shape, 1)
    neg = -jnp.inf
    is_group = lane < N_GROUPS
    gl = jnp.where(is_group, lg, neg)
    gmax = jnp.max(gl, axis=-1, keepdims=True)
    grp = jnp.min(jnp.where(gl == gmax, lane, LANES), axis=-1, keepdims=True)
    g1 = 1.0 / jnp.sum(jnp.where(is_group, jnp.exp(gl - gmax), 0.0), axis=-1, keepdims=True)
    lo = N_GROUPS + grp * EXPERTS_PER_GROUP
    el = jnp.where(jnp.logical_and(lane >= lo, lane < lo + EXPERTS_PER_GROUP), lg, neg)
    v1 = jnp.max(el, axis=-1, keepdims=True)
    i1 = jnp.min(jnp.where(el == v1, lane, LANES), axis=-1, keepdims=True)
    el2 = jnp.where(lane == i1, neg, el)
    v2 = jnp.max(el2, axis=-1, keepdims=True)
    i2 = jnp.min(jnp.where(el2 == v2, lane, LANES), axis=-1, keepdims=True)
    e21 = jnp.exp(v2 - v1)
    den = 1.0 + e21
    ids_ref[...] = jnp.where(lane == 0, i1 - N_GROUPS, jnp.where(lane == 1, i2 - N_GROUPS, 0))
    g0_ref[...] = jnp.broadcast_to(g1 * (1.0 / den), g0_ref.shape)
    g1_ref[...] = jnp.broadcast_to(g1 * (e21 / den), g1_ref.shape)


def _router(x, whi, wlo, bias, *, bm):
    n, d = x.shape
    row = pl.BlockSpec((bm, LANES), lambda i: (i, 0))
    return pl.pallas_call(
        _router_kernel,
        grid=(n // bm,),
        in_specs=[pl.BlockSpec((bm, d), lambda i: (i, 0)),
                  pl.BlockSpec((d, LANES), lambda i: (0, 0)),
                  pl.BlockSpec((d, LANES), lambda i: (0, 0)),
                  pl.BlockSpec((1, LANES), lambda i: (0, 0))],
        out_specs=[row, row, row],
        out_shape=[jax.ShapeDtypeStruct((n, LANES), I32), jax.ShapeDtypeStruct((n, LANES), F32),
                   jax.ShapeDtypeStruct((n, LANES), F32)],
        compiler_params=_params("parallel"),
        name="router",
    )(x, whi, wlo, bias)


def _dispatch_kernel(ppos_ref, x_ref, buf_hbm, xs_hbm, sems, *, bm):
    del buf_hbm
    base = pl.program_id(0) * bm * 2

    def row_copy(a):
        return pltpu.make_async_copy(x_ref.at[pl.ds(a >> 1, 1), :],
                                     xs_hbm.at[pl.ds(ppos_ref[base + a], 1), :], sems.at[a & 1])

    def start(a, c):
        row_copy(a).start()
        return c
    lax.fori_loop(0, 2 * bm, start, 0, unroll=4)

    def wait(a, c):
        row_copy(a).wait()
        return c
    lax.fori_loop(0, 2 * bm, wait, 0, unroll=4)


def _dispatch(x, ppos, buf, *, bm):
    n, d = x.shape
    assert n % bm == 0
    grid_spec = pltpu.PrefetchScalarGridSpec(
        num_scalar_prefetch=1,
        grid=(n // bm,),
        in_specs=[pl.BlockSpec((bm, d), lambda i, pp: (i, 0)), pl.BlockSpec(memory_space=pl.ANY)],
        out_specs=pl.BlockSpec(memory_space=pl.ANY),
        scratch_shapes=[pltpu.SemaphoreType.DMA((2,))],
    )
    return pl.pallas_call(
        functools.partial(_dispatch_kernel, bm=bm),
        grid_spec=grid_spec,
        out_shape=jax.ShapeDtypeStruct(buf.shape, buf.dtype),
        input_output_aliases={2: 0},
        compiler_params=_params("arbitrary"),
        name="dispatch",
    )(ppos, x, buf)


def _expert_kernel(te_ref, nv_ref, x_ref, wg_ref, wu_ref, wd_ref, y_ref):
    t = pl.program_id(0)

    @pl.when(nv_ref[t] > 0)
    def _():
        x = x_ref[...].astype(BF16)
        hg = jnp.dot(x, wg_ref[...], preferred_element_type=F32)
        hu = jnp.dot(x, wu_ref[...], preferred_element_type=F32)
        act = (hg * (1.0 / (1.0 + jnp.exp(-hg)))) * hu
        y_ref[...] = jnp.dot(act.astype(BF16), wd_ref[...], preferred_element_type=F32)

    @pl.when(nv_ref[t] == 0)
    def _():
        y_ref[...] = jnp.zeros(y_ref.shape, F32)


def _experts(xs, wg, wu, wd, layer, tile_e, tile_nvalid, *, bm):
    p, d = xs.shape
    f = wg.shape[-1]
    grid_spec = pltpu.PrefetchScalarGridSpec(
        num_scalar_prefetch=2,
        grid=(p // bm,),
        in_specs=[pl.BlockSpec((bm, d), lambda t, te, nv: (t, 0)),
                  pl.BlockSpec((None, None, d, f), lambda t, te, nv: (layer, te[t], 0, 0)),
                  pl.BlockSpec((None, None, d, f), lambda t, te, nv: (layer, te[t], 0, 0)),
                  pl.BlockSpec((None, None, f, d), lambda t, te, nv: (layer, te[t], 0, 0))],
        out_specs=pl.BlockSpec((bm, d), lambda t, te, nv: (t, 0)),
    )
    return pl.pallas_call(
        _expert_kernel,
        grid_spec=grid_spec,
        out_shape=jax.ShapeDtypeStruct((p, d), F32),
        compiler_params=_params("arbitrary"),
        name="experts",
    )(tile_e, tile_nvalid, xs, wg, wu, wd)


def _expert_plan(eid, *, bm):
    na = eid.shape[0] * 2
    flat_e = eid.reshape(na)
    onehot = (flat_e[:, None] == jnp.arange(N_EXPERTS, dtype=I32)[None, :]).astype(I32)
    csum = jnp.cumsum(onehot, axis=0)
    counts = csum[-1]
    padded = ((counts + bm - 1) // bm) * bm
    pend = jnp.cumsum(padded)
    pstart = pend - padded
    ppos = jnp.sum(onehot * (pstart[None, :] + csum - 1), axis=1).astype(I32)
    total = na + N_EXPERTS * bm
    tile_start = jnp.arange(total // bm, dtype=I32) * bm
    tile_e = jnp.minimum(jnp.sum((tile_start[:, None] >= pend[None, :]).astype(I32), axis=1), N_EXPERTS - 1)
    tile_nvalid = jnp.clip(pstart[tile_e] + counts[tile_e] - tile_start, 0, bm).astype(I32)
    return ppos, tile_e.astype(I32), tile_nvalid, total


def _combine_ln_kernel(ppos_ref, x_ref, g0_ref, g1_ref, y_hbm, g_ref, b_ref, of_ref, ob_ref, ybuf, sem,
                       *, bm, alpha):
    i = pl.program_id(0)
    d = x_ref.shape[-1]

    def row_copy(blk, a):
        par = blk & 1
        return pltpu.make_async_copy(y_hbm.at[pl.ds(ppos_ref[blk * (2 * bm) + a], 1), :],
                                     ybuf.at[par, a & 1, pl.ds(a >> 1, 1), :], sem.at[par, a & 1])

    def start_block(blk):
        def start(a, c):
            row_copy(blk, a).start()
            return c
        lax.fori_loop(0, 2 * bm, start, 0, unroll=4)

    @pl.when(i == 0)
    def _():
        start_block(i)

    @pl.when(i + 1 < pl.num_programs(0))
    def _():
        start_block(i + 1)

    def wait(a, c):
        row_copy(i, a).wait()
        return c
    lax.fori_loop(0, 2 * bm, wait, 0, unroll=4)

    reps = d // LANES
    par = i & 1
    y = (alpha * x_ref[...] + jnp.tile(g0_ref[...], (1, reps)) * ybuf[par, 0]
         + jnp.tile(g1_ref[...], (1, reps)) * ybuf[par, 1])
    out = _layer_norm(y, g_ref[...], b_ref[...])
    of_ref[...] = out
    ob_ref[...] = out.astype(BF16)


def _combine_ln(x, ys, ppos, g0, g1, g, b, *, bm, alpha):
    n, d = x.shape
    row = lambda w: pl.BlockSpec((bm, w), lambda i, pp: (i, 0))
    grid_spec = pltpu.PrefetchScalarGridSpec(
        num_scalar_prefetch=1,
        grid=(n // bm,),
        in_specs=[row(d), row(LANES), row(LANES), pl.BlockSpec(memory_space=pl.ANY),
                  pl.BlockSpec((1, d), lambda i, pp: (0, 0)), pl.BlockSpec((1, d), lambda i, pp: (0, 0))],
        out_specs=[row(d), row(d)],
        scratch_shapes=[pltpu.VMEM((2, 2, bm, d), F32), pltpu.SemaphoreType.DMA((2, 2))],
    )
    return pl.pallas_call(
        functools.partial(_combine_ln_kernel, bm=bm, alpha=alpha),
        grid_spec=grid_spec,
        out_shape=[jax.ShapeDtypeStruct((n, d), F32), jax.ShapeDtypeStruct((n, d), BF16)],
        compiler_params=_params("arbitrary"),
        name="combine_ln",
    )(ppos, x, g0, g1, ys, g, b)


def _rope_tables(pos):
    posf = pos.astype(F32)[:, None]
    inv128 = ROPE_THETA ** (-jnp.arange(HEAD_DIM // 2, dtype=F32) * 2.0 / HEAD_DIM)
    ang = posf * inv128[None, :]
    cos, sin = jnp.cos(ang), jnp.sin(ang)
    a128 = jnp.concatenate([cos, cos], axis=-1)
    b128 = jnp.concatenate([-sin, sin], axis=-1)
    inv64 = ROPE_THETA ** (-jnp.arange(IDX_DIM // 2, dtype=F32) * 2.0 / IDX_DIM)
    ang = posf * inv64[None, :]
    cos, sin = jnp.cos(ang), jnp.sin(ang)
    zero = jnp.zeros_like(sin)
    a64 = jnp.concatenate([cos, cos, cos, cos], axis=-1)
    b64 = jnp.concatenate([-sin, zero, -sin, zero], axis=-1)
    c64 = jnp.concatenate([zero, sin, zero, sin], axis=-1)
    return (a128, b128), (a64, b64, c64)


def _z_weight(w_in):
    o = _IN_OFF
    d = w_in.shape[0]
    cols = [w_in[:, o[0]:o[4]],
            w_in[:, o[6]:o[9]],
            w_in[:, o[10]:o[12]],
            w_in[:, o[4]:o[5]],
            w_in[:, o[12]:o[13]],
            w_in[:, o[5]:o[6]],
            w_in[:, o[9]:o[10]],
            jnp.zeros((d, LANES - IDX_HEADS - B_HEADS), w_in.dtype)]
    return jnp.concatenate(cols, axis=1).astype(BF16)


def _uq_weight(w):
    r = w.shape[0]
    w3 = w.reshape(r, C_HEADS, C_NOPE + C_ROPE)
    nope = w3[:, :, :C_NOPE].reshape(r, C_HEADS * C_NOPE)
    rope = jnp.pad(w3[:, :, C_NOPE:], ((0, 0), (0, 0), (0, LANES - C_ROPE))).reshape(r, C_HEADS * LANES)
    return jnp.concatenate([nope, rope], axis=1).astype(BF16)


def _pad_lanes(x):
    return jnp.pad(x, [(0, 0)] * (x.ndim - 1) + [(0, LANES - x.shape[-1])])


def _pad_rows(x, rows):
    return jnp.pad(x, [(0, 0), (0, rows - x.shape[1])] + [(0, 0)] * (x.ndim - 2))


def _bf16_head(c):
    bits = lax.bitcast_convert_type(c, I32) & jnp.int32(-65536)
    return lax.bitcast_convert_type(bits, F32)


def _fox_bias_kernel(c_ref, q_ref, k_ref):
    c = c_ref[...] * LOG2E
    lane = lax.broadcasted_iota(I32, c.shape, 1)
    head_mask = jnp.int32(-65536)
    for h in range(B_HEADS):
        ch = jnp.sum(jnp.where(lane == h, c, 0.0), axis=1, keepdims=True)
        hi = pltpu.bitcast(pltpu.bitcast(ch, I32) & head_mask, F32)
        r = ch - hi
        mid = pltpu.bitcast(pltpu.bitcast(r, I32) & head_mask, F32)
        lo = r - mid
        parts = jnp.where(lane == 0, hi, jnp.where(lane == 1, mid, jnp.where(lane == 2, lo, 0.0)))
        ones = jnp.where(jnp.logical_and(lane >= 3, lane < 6), 1.0, 0.0)
        q_ref[:, h * LANES:(h + 1) * LANES] = (parts + ones).astype(BF16)
        k_ref[:, h * LANES:(h + 1) * LANES] = (pltpu.roll(ones, LANES - 3, 1) - pltpu.roll(parts, 3, 1)).astype(BF16)


def _fox_bias_parts(csum):
    nb, s, h = csum.shape
    rows = nb * s
    bm = _pick(rows, 512, 256, 128, 64, 32, 16)
    spec = lambda w: pl.BlockSpec((bm, w), lambda i: (i, 0))
    qb, kb = pl.pallas_call(
        _fox_bias_kernel,
        grid=(rows // bm,),
        in_specs=[spec(LANES)],
        out_specs=[spec(h * LANES), spec(h * LANES)],
        out_shape=[jax.ShapeDtypeStruct((rows, h * LANES), BF16)] * 2,
        compiler_params=_params("parallel"),
        name="fox_bias",
    )(_pad_lanes(csum.reshape(rows, h)))
    return qb.reshape(nb, s, -1), kb.reshape(nb, s, -1)


def _pick(n, *cands):
    for c in cands:
        if n % c == 0:
            return c
    return n


def kernel(x_prompt, x_sample, cache_a_k, cache_a_v, cache_a_kidx, cache_b_k, cache_b_v, cache_b_logf,
           cache_c_latent, cache_c_krope, cache_mem_k, cache_mem_v, mem_prompt,
           w_in, b_f, c_q_norm, w_c_uq, c_kv_norm, w_c_uk, w_c_uv, w_o, ln1_g, ln1_b,
           w_mq, w_mk, w_mv, w_mo, ln2_g, ln2_b, w_rg, b_rg, w_re, b_re,
           w_gate, w_up, w_down, ln3_g, ln3_b):
    depth = w_in.shape[0]
    bp, tp, d = x_prompt.shape
    bs, ts, _ = x_sample.shape
    past = cache_a_k.shape[2]
    n_mem = mem_prompt.shape[1]
    assert bp == 1
    n_p, n_s = bp * tp, bs * ts
    n = n_p + n_s
    alpha = (2 * depth) ** 0.25
    s_s = past + ts
    s_pad = -(-s_s // COUNT_ROWS) * COUNT_ROWS

    bm_tok = _pick(n, 1024, 512, 256, 128, 64)
    bm_post = _pick(n, 256, 128, 64)
    bm_ln = _pick(n, 256, 128, 64)
    bq_f = _pick(tp, 1024, 512, 256, 128)
    bq_a = _pick(tp, 512, 256, 128)
    kb_a = _pick(tp, 1024, 512, 256)
    bm_moe = 256

    res_first, res_second, res_second_row0 = x_prompt.reshape(n_p, d), x_sample.reshape(n_s, d), 0
    xb = jnp.concatenate([res_first.astype(BF16), res_second.astype(BF16)], axis=0)
    pos_p = jnp.arange(tp, dtype=I32)
    pos_s = past + jnp.arange(ts, dtype=I32)
    tabs128, tabs64 = _rope_tables(jnp.concatenate([pos_p, jnp.tile(pos_s, bs)]))

    nq_f = tp // bq_f
    nfull_p = jnp.arange(nq_f, dtype=I32)
    nkv_p = nfull_p + 1
    one = jnp.ones((1,), I32)
    zero = jnp.zeros((1,), I32)
    nvis_p = (((jnp.arange(tp // bq_a, dtype=I32) + 1) * bq_a + kb_a - 1) // kb_a).astype(I32)
    nvis_s = jnp.ones((bs,), I32)

    xs_buf = jnp.zeros((2 * n + N_EXPERTS * bm_moe, d), F32)
    w_o_b = w_o.astype(BF16)
    w_gate_b, w_up_b, w_down_b = w_gate.astype(BF16), w_up.astype(BF16), w_down.astype(BF16)
    state_p = [[] for _ in range(8)]
    state_s = [[] for _ in range(8)]
    mem_k_out, mem_v_out = [], []

    for l in range(depth):
        z = _matmul(xb, _z_weight(w_in[l]), bm=bm_tok, bn=512, out_dtype=F32, name="in_proj")
        bf_row = jnp.zeros((1, LANES), F32).at[0, SM2_BF:SM2_BF + B_HEADS].set(b_f[l].astype(F32))
        aq, ak, aqi, bq_, cqn, clat, sm1, sm2, akb, avb, bkb, bvb = _post(
            z, tabs128, tabs64, bf_row, c_q_norm[l].reshape(1, -1).astype(F32),
            c_kv_norm[l].reshape(1, -1).astype(F32), bm=bm_post)
        cq = _matmul_rope(cqn, _uq_weight(w_c_uq[l]), tabs64, bm=bm_tok, bn=512,
                          first_rope_col=C_HEADS * C_NOPE, out_scale=(C_NOPE + C_ROPE) ** -0.5 * LOG2E,
                          name="c_q_up")
        av = z[:, Z_AV:Z_AV + HEAD_DIM]
        aki = sm1[:, :IDX_DIM]
        ckr = sm1[:, IDX_DIM:]
        blogf = sm2[:, SM2_BF:SM2_BF + B_HEADS]
        bk_ = z[:, Z_BK:Z_BK + B_HEADS * HEAD_DIM]
        bv_ = z[:, Z_BV:Z_BV + B_HEADS * HEAD_DIM]
        rows = (ak, av, aki, bk_.reshape(n, B_HEADS, HEAD_DIM), bv_.reshape(n, B_HEADS, HEAD_DIM),
                blogf, clat, ckr)
        for i, r in enumerate(rows):
            state_p[i].append(r[:n_p].reshape((bp, tp) + r.shape[1:]))
            state_s[i].append(r[n_p:].reshape((bs, ts) + r.shape[1:]))
        grp_p = lambda a: a[:n_p].reshape(bp, tp, -1)
        grp_s = lambda a: a[n_p:].reshape(bs, ts, -1)
        cat_s = lambda c, r: jnp.concatenate([c.reshape(bs, past, -1).astype(r.dtype), grp_s(r)], axis=1)
        all_n = lambda a: a.reshape(1, n, -1)

        def sparse(aq_g, aqi_g, aw_g, ki, k, v, nvis, *, bq, kb, s_real, q_off, name):
            ka = _pad_lanes(ki).astype(BF16)
            kb_ = jnp.concatenate([jnp.zeros_like(ki), ki], axis=-1).astype(BF16)
            out = _sparse_attention(aq_g, aqi_g, aw_g, ka, kb_, k, v, nvis, bq=bq, kb=kb, s_real=s_real,
                                    topk=min(TOPK_MAX, s_real // 4), q_off=q_off, name=name)
            return out.reshape(-1, A_HEADS * HEAD_DIM)

        oa_p = sparse(all_n(aq), all_n(aqi), all_n(sm2), grp_p(aki), grp_p(akb), grp_p(avb), nvis_p,
                      bq=bq_a, kb=kb_a, s_real=tp, q_off=0, name="sparse_prompt")
        oa_s = sparse(grp_s(aq), grp_s(aqi), grp_s(sm2),
                      _pad_rows(cat_s(cache_a_kidx[l], aki), s_pad), _pad_rows(cat_s(cache_a_k[l], akb), s_pad),
                      _pad_rows(cat_s(cache_a_v[l], avb), s_pad), nvis_s,
                      bq=ts, kb=s_pad, s_real=s_s, q_off=past, name="sparse_sample")

        csum_p = jnp.cumsum(grp_p(blogf), axis=1)
        qb_p, kbias_p = _fox_bias_parts(csum_p)
        ob_p = _tflash([all_n(bq_), qb_p], (0, 0), [all_n(bkb), kbias_p], (False, False), all_n(bvb), 0,
                       nfull_p, nkv_p, heads=B_HEADS, s=tp, hb=2, bq=bq_f, bk=bq_f, mode="causal", q_off=0,
                       name="fox_prompt")
        csum_s = jnp.cumsum(cat_s(cache_b_logf[l], blogf), axis=1)
        qb_all_s, kbias_s = _fox_bias_parts(csum_s)
        qb_s = qb_all_s[:, past:]
        ob_s = _tflash([grp_s(bq_), qb_s], (0, 0), [cat_s(cache_b_k[l], bkb), kbias_s], (False, False),
                       cat_s(cache_b_v[l], bvb), 0, zero, one, heads=B_HEADS, s=s_s, hb=B_HEADS, bq=ts, bk=s_s,
                       mode="causal", q_off=past, name="fox_sample")

        w_ukv = jnp.concatenate([w_c_uk[l], w_c_uv[l]], axis=1).astype(BF16)
        kv_p = _matmul(clat[:n_p], w_ukv, bm=_pick(n_p, 1024, 512, 256), bn=w_ukv.shape[1], out_dtype=BF16,
                       name="c_kv_up_prompt").reshape(bp, tp, -1)
        oc_p = _tflash([all_n(cq), all_n(cq)], (0, C_HEADS),
                       [kv_p, _pad_lanes(grp_p(ckr)).astype(BF16)], (False, True), kv_p, C_HEADS,
                       nfull_p, nkv_p, heads=C_HEADS, s=tp, hb=2, bq=bq_f, bk=bq_f, mode="chunk", q_off=0,
                       name="mla_prompt")
        lat_s = cat_s(cache_c_latent[l], clat)
        kv_s = _matmul(lat_s.reshape(bs * s_s, -1), w_ukv, bm=_pick(bs * s_s, 1024, 512, 256, 128, 64),
                       bn=w_ukv.shape[1], out_dtype=BF16, name="c_kv_up_sample").reshape(bs, s_s, -1)
        oc_s = _tflash([grp_s(cq), grp_s(cq)], (0, C_HEADS),
                       [kv_s, _pad_lanes(cat_s(cache_c_krope[l], ckr)).astype(BF16)], (False, True), kv_s, C_HEADS,
                       zero, one, heads=C_HEADS, s=s_s, hb=C_HEADS // 2, bq=ts, bk=s_s, mode="chunk", q_off=past,
                       name="mla_sample")

        x1, x1b = _mm_ln_cols([oa_p, ob_p.reshape(n_p, -1), oc_p.reshape(n_p, -1)],
                              [oa_s, ob_s.reshape(n_s, -1), oc_s.reshape(n_s, -1)],
                              w_o_b, l, res_first, res_second, res_second_row0,
                              ln1_g[l].reshape(1, d), ln1_b[l].reshape(1, d),
                              bm=_pick(math.gcd(n_p, n_s), 512, 256, 128, 64), bn=_pick(d, 512, 256, 128),
                              alpha=alpha, name="out_proj_ln1")

        mw = MEM_HEADS * MEM_DIM
        w_mkv = jnp.concatenate([w_mk[l], w_mv[l]], axis=1).astype(BF16)
        mkv = _matmul(mem_prompt.reshape(bp * n_mem, d), w_mkv, bm=_pick(bp * n_mem, 256, 128, 64), bn=512,
                      out_dtype=F32, name="mem_kv")
        mem_k_out.append(mkv[:, :mw].reshape(bp, n_mem, MEM_HEADS, MEM_DIM))
        mem_v_out.append(mkv[:, mw:].reshape(bp, n_mem, MEM_HEADS, MEM_DIM))
        qm = _matmul(x1b, w_mq[l].astype(BF16), bm=bm_tok, bn=mw, out_dtype=BF16, name="mem_q",
                     out_scale=MEM_DIM ** -0.5 * LOG2E)
        mkv3 = mkv.reshape(bp, n_mem, 2 * mw).astype(BF16)
        om_p = _tflash([all_n(qm)], (0,), [mkv3], (False,), mkv3, MEM_HEADS,
                       jnp.ones((nq_f,), I32), jnp.ones((nq_f,), I32), heads=MEM_HEADS, s=n_mem,
                       hb=2, bq=bq_f, bk=n_mem, mode="none", q_off=0, name="mem_prompt")
        om_s = _tflash([grp_s(qm)], (0,), [cache_mem_k[l].reshape(bs, n_mem, mw).astype(BF16)], (False,),
                       cache_mem_v[l].reshape(bs, n_mem, mw).astype(BF16), 0, one, one, heads=MEM_HEADS, s=n_mem,
                       hb=MEM_HEADS, bq=ts, bk=n_mem, mode="none", q_off=0, name="mem_sample")
        om = jnp.concatenate([om_p.reshape(n_p, mw), om_s.reshape(n_s, mw)], axis=0)
        x2, _ = _mm_res_ln(om, w_mo[l].astype(BF16), x1, ln2_g[l].reshape(1, d), ln2_b[l].reshape(1, d),
                           bm=bm_ln, bk=mw, alpha=alpha, name="mem_out_ln2")

        w_r = jnp.concatenate([w_rg[l], jnp.transpose(w_re[l], (1, 0, 2)).reshape(d, N_EXPERTS)], axis=1).astype(F32)
        w_r = _pad_lanes(w_r)
        w_r_head = _bf16_head(w_r)
        w_r_hi = w_r_head.astype(BF16)
        w_r_lo = (w_r - w_r_head).astype(BF16)
        b_r = _pad_lanes(jnp.concatenate([b_rg[l], b_re[l].reshape(-1)]).astype(F32).reshape(1, -1))
        ids, g0, g1 = _router(x2, w_r_hi, w_r_lo, b_r, bm=_pick(n, 512, 256, 128, 64))
        ppos, tile_e, tile_nvalid, total = _expert_plan(ids[:, :2], bm=bm_moe)
        assert total == xs_buf.shape[0]
        xs_buf = _dispatch(x2, ppos, xs_buf, bm=_pick(n, 256, 128, 64))
        ys = _experts(xs_buf, w_gate_b, w_up_b, w_down_b, l, tile_e, tile_nvalid, bm=bm_moe)
        x, xb = _combine_ln(x2, ys, ppos, g0, g1, ln3_g[l].reshape(1, d), ln3_b[l].reshape(1, d),
                            bm=_pick(n, 256, 128, 64), alpha=alpha)
        res_first, res_second, res_second_row0 = x, x, n_p

    outs = [x[:n_p].reshape(bp, tp, d), x[n_p:].reshape(bs, ts, d)]
    outs += [jnp.stack(c) for c in state_p]
    outs += [jnp.stack(mem_k_out), jnp.stack(mem_v_out)]
    outs += [jnp.stack(c) for c in state_s]
    return tuple(outs)
```

```python
import functools
import math

import jax
import jax.numpy as jnp
import numpy as np
from jax import lax
from jax.experimental import pallas as pl
from jax.experimental.pallas import tpu as pltpu

F32 = jnp.float32
BF16 = jnp.bfloat16
I32 = jnp.int32

LANES = 128
SUBLANES = 8
VMEM_LIMIT = 56 * 1024 * 1024

CHUNK_SHIFT = 6
HEAD_DIM = 128
ROPE_THETA = 10000.0
A_HEADS = 8
IDX_HEADS = 32
IDX_DIM = 64
TOPK_MAX = 256
B_HEADS = 8
C_HEADS = 16
C_Q_RANK = 768
C_KV_RANK = 256
C_NOPE = 128
C_ROPE = 64
C_V = 128
MEM_HEADS = 4
MEM_DIM = 128
N_GROUPS = 4
EXPERTS_PER_GROUP = 4
N_EXPERTS = N_GROUPS * EXPERTS_PER_GROUP
LN_EPS = 1e-5
RMS_EPS = 1e-6
LOG2E = math.log2(math.e)

_IN_SIZES = (A_HEADS * HEAD_DIM, HEAD_DIM, HEAD_DIM, IDX_HEADS * IDX_DIM, IDX_DIM, IDX_HEADS,
             B_HEADS * HEAD_DIM, B_HEADS * HEAD_DIM, B_HEADS * HEAD_DIM, B_HEADS,
             C_Q_RANK, C_KV_RANK, C_ROPE)
_IN_OFF = np.concatenate([[0], np.cumsum(_IN_SIZES)]).astype(int)

Z_AQ = 0
Z_AK = Z_AQ + A_HEADS * HEAD_DIM
Z_AV = Z_AK + HEAD_DIM
Z_AQI = Z_AV + HEAD_DIM
Z_BQ = Z_AQI + IDX_HEADS * IDX_DIM
Z_BK = Z_BQ + B_HEADS * HEAD_DIM
Z_BV = Z_BK + B_HEADS * HEAD_DIM
Z_CQ = Z_BV + B_HEADS * HEAD_DIM
Z_CKV = Z_CQ + C_Q_RANK
Z_SM1 = Z_CKV + C_KV_RANK
Z_SM2 = Z_SM1 + LANES
Z_WIDTH = Z_SM2 + LANES
SM2_BF = IDX_HEADS

M_INIT = -1e30
INT_MIN = -2 ** 31
INT_MAX = 2 ** 31 - 1
SCORE_ROWS = 128
SCORE_COLS = 256
COUNT_ROWS = 256
LN_ROWS = 64


def _params(*sem):
    return pltpu.CompilerParams(dimension_semantics=sem, vmem_limit_bytes=VMEM_LIMIT)


def _mm_kernel(x_ref, w_ref, o_ref, *, out_scale):
    x = x_ref[...].astype(BF16)
    acc = jnp.dot(x, w_ref[...], preferred_element_type=F32)
    if out_scale != 1.0:
        acc = acc * out_scale
    o_ref[...] = acc.astype(o_ref.dtype)


def _matmul(x, w, *, bm, bn, out_dtype, name, out_scale=1.0):
    m, k = x.shape
    n = w.shape[1]
    assert m % bm == 0 and n % bn == 0, (x.shape, w.shape, bm, bn)
    return pl.pallas_call(
        functools.partial(_mm_kernel, out_scale=out_scale),
        grid=(m // bm, n // bn),
        in_specs=[pl.BlockSpec((bm, k), lambda i, j: (i, 0)),
                  pl.BlockSpec((k, bn), lambda i, j: (0, j))],
        out_specs=pl.BlockSpec((bm, bn), lambda i, j: (i, j)),
        out_shape=jax.ShapeDtypeStruct((m, n), out_dtype),
        compiler_params=_params("parallel", "parallel"),
        name=name,
    )(x, w)


def _rope64(x, a, b, c):
    return x * a + pltpu.roll(x, 96, 1) * b + pltpu.roll(x, 32, 1) * c


def _rope128(x, a, b):
    return x * a + pltpu.roll(x, 64, 1) * b


def _mm_rope_kernel(x_ref, w_ref, a_ref, b_ref, c_ref, o_ref, *, first_rope_block, out_scale):
    j = pl.program_id(1)
    acc = jnp.dot(x_ref[...], w_ref[...], preferred_element_type=F32) * out_scale

    @pl.when(j < first_rope_block)
    def _():
        o_ref[...] = acc.astype(o_ref.dtype)

    @pl.when(j >= first_rope_block)
    def _():
        a, b, c = a_ref[...], b_ref[...], c_ref[...]
        for g in range(acc.shape[1] // LANES):
            sl = slice(g * LANES, (g + 1) * LANES)
            o_ref[:, sl] = _rope64(acc[:, sl], a, b, c).astype(o_ref.dtype)


def _matmul_rope(x, w, tabs, *, bm, bn, first_rope_col, out_scale, name):
    m, k = x.shape
    n = w.shape[1]
    assert m % bm == 0 and n % bn == 0 and first_rope_col % bn == 0
    tab_spec = pl.BlockSpec((bm, LANES), lambda i, j: (i, 0))
    return pl.pallas_call(
        functools.partial(_mm_rope_kernel, first_rope_block=first_rope_col // bn, out_scale=out_scale),
        grid=(m // bm, n // bn),
        in_specs=[pl.BlockSpec((bm, k), lambda i, j: (i, 0)),
                  pl.BlockSpec((k, bn), lambda i, j: (0, j)),
                  tab_spec, tab_spec, tab_spec],
        out_specs=pl.BlockSpec((bm, bn), lambda i, j: (i, j)),
        out_shape=jax.ShapeDtypeStruct((m, n), BF16),
        compiler_params=_params("parallel", "parallel"),
        name=name,
    )(x, w, *tabs)


def _post_kernel(z_ref, a128_ref, b128_ref, a64_ref, b64_ref, c64_ref, bf_ref, gq_ref, gkv_ref,
                 aq_ref, ak_ref, aqi_ref, bq_ref, cqn_ref, clat_ref, sm1_ref, sm2_ref,
                 akb_ref, avb_ref, bkb_ref, bvb_ref):
    a128, b128 = a128_ref[...], b128_ref[...]
    a64, b64, c64 = a64_ref[...], b64_ref[...], c64_ref[...]
    qscale = HEAD_DIM ** -0.5 * LOG2E
    for g in range(A_HEADS):
        src = slice(Z_AQ + g * LANES, Z_AQ + (g + 1) * LANES)
        aq_ref[:, g * LANES:(g + 1) * LANES] = (_rope128(z_ref[:, src], a128, b128) * qscale).astype(BF16)
    ak = _rope128(z_ref[:, Z_AK:Z_AK + LANES], a128, b128)
    ak_ref[...] = ak
    akb_ref[...] = ak.astype(BF16)
    avb_ref[...] = z_ref[:, Z_AV:Z_AV + LANES].astype(BF16)
    bkb_ref[...] = z_ref[:, Z_BK:Z_BK + B_HEADS * HEAD_DIM].astype(BF16)
    bvb_ref[...] = z_ref[:, Z_BV:Z_BV + B_HEADS * HEAD_DIM].astype(BF16)
    for g in range(IDX_HEADS * IDX_DIM // LANES):
        src = slice(Z_AQI + g * LANES, Z_AQI + (g + 1) * LANES)
        aqi_ref[:, g * LANES:(g + 1) * LANES] = _rope64(z_ref[:, src], a64, b64, c64).astype(BF16)
    bq_ref[...] = (z_ref[:, Z_BQ:Z_BQ + B_HEADS * HEAD_DIM] * qscale).astype(BF16)
    cq = z_ref[:, Z_CQ:Z_CQ + C_Q_RANK]
    cq = cq * lax.rsqrt(jnp.mean(cq * cq, axis=-1, keepdims=True) + RMS_EPS) * gq_ref[...]
    cqn_ref[...] = cq.astype(BF16)
    ckv = z_ref[:, Z_CKV:Z_CKV + C_KV_RANK]
    clat_ref[...] = ckv * lax.rsqrt(jnp.mean(ckv * ckv, axis=-1, keepdims=True) + RMS_EPS) * gkv_ref[...]
    sm1_ref[...] = _rope64(z_ref[:, Z_SM1:Z_SM1 + LANES], a64, b64, c64)
    s2 = z_ref[:, Z_SM2:Z_SM2 + LANES]
    lane = lax.broadcasted_iota(I32, s2.shape, 1)
    f = s2 + bf_ref[...]
    logf = jnp.minimum(f, 0.0) - jnp.log1p(jnp.exp(-jnp.abs(f)))
    sm2_ref[...] = jnp.where(lane < SM2_BF, s2 * (IDX_HEADS ** -0.5 * IDX_DIM ** -0.5),
                             jnp.where(lane < SM2_BF + B_HEADS, logf, 0.0))


def _post(z, tabs128, tabs64, bf_row, gq, gkv, *, bm):
    n = z.shape[0]
    row = lambda w: pl.BlockSpec((bm, w), lambda i: (i, 0))
    const = lambda w: pl.BlockSpec((1, w), lambda i: (0, 0))
    widths = (A_HEADS * HEAD_DIM, LANES, IDX_HEADS * IDX_DIM, B_HEADS * HEAD_DIM, C_Q_RANK, C_KV_RANK, LANES, LANES,
              LANES, LANES, B_HEADS * HEAD_DIM, B_HEADS * HEAD_DIM)
    dtypes = (BF16, F32, BF16, BF16, BF16, F32, F32, F32, BF16, BF16, BF16, BF16)
    return pl.pallas_call(
        _post_kernel,
        grid=(n // bm,),
        in_specs=[row(Z_WIDTH)] + [row(LANES)] * 5 + [const(LANES), const(C_Q_RANK), const(C_KV_RANK)],
        out_specs=[row(w) for w in widths],
        out_shape=[jax.ShapeDtypeStruct((n, w), d) for w, d in zip(widths, dtypes)],
        compiler_params=_params("parallel"),
        name="post_projection",
    )(z, *tabs128, *tabs64, bf_row, gq, gkv)


def _layer_norm(y, g, b):
    mu = jnp.mean(y, axis=-1, keepdims=True)
    yc = y - mu
    var = jnp.mean(yc * yc, axis=-1, keepdims=True)
    return yc * lax.rsqrt(var + LN_EPS) * g + b


def _mm_ln_cols_kernel(*refs, n_parts, nj, bn, alpha, first_rows):
    xa = refs[:n_parts]
    xb = refs[n_parts:2 * n_parts]
    w_ref, ra_ref, rb_ref, g_ref, b_ref, of_ref, ob_ref = refs[2 * n_parts:]
    i = pl.program_id(0)
    j = pl.program_id(1)

    def emit(x_refs, r_ref):
        x = jnp.concatenate([xr[...] for xr in x_refs], axis=1)
        y = alpha * r_ref[...] + jnp.dot(x, w_ref[...], preferred_element_type=F32)
        for jj in range(nj):
            @pl.when(j == jj)
            def _(jj=jj):
                of_ref[:, jj * bn:(jj + 1) * bn] = y

    @pl.when(i < first_rows)
    def _():
        emit(xa, ra_ref)

    @pl.when(i >= first_rows)
    def _():
        emit(xb, rb_ref)

    @pl.when(j == nj - 1)
    def _():
        for r0 in range(0, of_ref.shape[0], LN_ROWS):
            rs = slice(r0, r0 + LN_ROWS)
            out = _layer_norm(of_ref[rs, :], g_ref[...], b_ref[...])
            of_ref[rs, :] = out
            ob_ref[rs, :] = out.astype(BF16)


def _mm_ln_cols(x_first, x_second, w, layer, res_first, res_second, res_second_row0, g, b, *, bm, bn, alpha, name):
    m1, m2 = x_first[0].shape[0], x_second[0].shape[0]
    d = w.shape[-1]
    k = w.shape[-2]
    nj = d // bn
    assert m1 % bm == 0 and m2 % bm == 0 and res_second_row0 % bm == 0 and d % bn == 0
    nb1, nb2, r0b = m1 // bm, m2 // bm, res_second_row0 // bm
    first = lambda i: jnp.minimum(i, nb1 - 1)
    second = lambda i: jnp.maximum(i - nb1, 0)
    once = pl.Buffered(1)
    in_specs = [pl.BlockSpec((bm, xp.shape[1]), lambda i, j: (first(i), 0), pipeline_mode=once) for xp in x_first]
    in_specs += [pl.BlockSpec((bm, xp.shape[1]), lambda i, j: (second(i), 0), pipeline_mode=once)
                 for xp in x_second]
    in_specs += [pl.BlockSpec((None, k, bn), lambda i, j: (layer, 0, j)),
                 pl.BlockSpec((bm, bn), lambda i, j: (first(i), j)),
                 pl.BlockSpec((bm, bn), lambda i, j: (second(i) + r0b, j)),
                 pl.BlockSpec((1, d), lambda i, j: (0, 0)),
                 pl.BlockSpec((1, d), lambda i, j: (0, 0))]
    return pl.pallas_call(
        functools.partial(_mm_ln_cols_kernel, n_parts=len(x_first), nj=nj, bn=bn, alpha=alpha, first_rows=nb1),
        grid=(nb1 + nb2, nj),
        in_specs=in_specs,
        out_specs=[pl.BlockSpec((bm, d), lambda i, j: (i, 0)),
                   pl.BlockSpec((bm, d), lambda i, j: (i, 0))],
        out_shape=[jax.ShapeDtypeStruct((m1 + m2, d), F32), jax.ShapeDtypeStruct((m1 + m2, d), BF16)],
        compiler_params=_params("parallel", "arbitrary"),
        name=name,
    )(*x_first, *x_second, w, res_first, res_second, g, b)


def _tflash_kernel(nfull_ref, nkv_ref, *refs, n_parts, k_shared, hb, bk, mode, q_off):
    refs = list(refs)
    q_refs = [refs.pop(0) for _ in range(n_parts)]
    k_refs = [refs.pop(0) for _ in range(n_parts)]
    v_ref = refs.pop(0)
    o_ref, acc_ref = refs
    qi = pl.program_id(2)
    bq = o_ref.shape[0]
    qs = []
    for h in range(hb):
        hs = slice(h * LANES, (h + 1) * LANES)
        qs.append(jnp.concatenate([qr[:, hs].astype(F32).T.astype(BF16) for qr in q_refs], axis=0))
    acc_ref[...] = jnp.zeros(acc_ref.shape, F32)

    def block(j, carry, masked):
        ms, ls = carry
        off = pl.multiple_of(j * bk, bk)
        if masked:
            kpos = off + lax.broadcasted_iota(I32, (bk, bq), 0)
            qpos = q_off + qi * bq + lax.broadcasted_iota(I32, (bk, bq), 1)
            if mode == "chunk":
                mask = (kpos >> CHUNK_SHIFT) <= (qpos >> CHUNK_SHIFT)
            else:
                mask = kpos <= qpos
        new_m, new_l = [], []
        for h in range(hb):
            parts = []
            for p_i, kr in enumerate(k_refs):
                hs = 0 if k_shared[p_i] else h
                parts.append(kr[pl.ds(off, bk), hs * LANES:(hs + 1) * LANES])
            k = jnp.concatenate(parts, axis=1) if n_parts > 1 else parts[0]
            s = jnp.dot(k, qs[h], preferred_element_type=F32)
            if masked:
                s = jnp.where(mask, s, -jnp.inf)
            m_new = jnp.maximum(ms[h], jnp.max(s, axis=0, keepdims=True))
            p = jnp.exp2(s - m_new)
            alpha = jnp.exp2(ms[h] - m_new)
            new_l.append(alpha * ls[h] + jnp.sum(p, axis=0, keepdims=True))
            new_m.append(m_new)
            v = v_ref[pl.ds(off, bk), h * LANES:(h + 1) * LANES]
            acc_ref[h] = alpha * acc_ref[h] + lax.dot_general(v, p.astype(BF16), (((0,), (0,)), ((), ())),
                                                              preferred_element_type=F32)
        return tuple(new_m), tuple(new_l)

    init = (tuple(jnp.full((1, bq), M_INIT, F32) for _ in range(hb)),
            tuple(jnp.zeros((1, bq), F32) for _ in range(hb)))
    carry = lax.fori_loop(0, nfull_ref[qi], functools.partial(block, masked=False), init)
    if mode != "none":
        carry = lax.fori_loop(nfull_ref[qi], nkv_ref[qi], functools.partial(block, masked=True), carry)
    for h in range(hb):
        o_ref[:, h * LANES:(h + 1) * LANES] = (acc_ref[h] / carry[1][h]).T.astype(o_ref.dtype)


def _tflash(q_parts, q_head0, k_parts, k_shared, v, v_head0, nfull, nkv, *, heads, s, hb, bq, bk, mode, q_off,
            name):
    nb = v.shape[0]
    nq = nfull.shape[0]
    assert heads % hb == 0 and v_head0 % hb == 0 and all(h0 % hb == 0 for h0 in q_head0) and s % bk == 0
    in_specs = []
    for h0 in q_head0:
        in_specs.append(pl.BlockSpec((None, bq, hb * LANES),
                                     lambda b, g, qi, nf, nk, h0=h0: (b, qi, h0 // hb + g)))
    for shared in k_shared:
        if shared:
            in_specs.append(pl.BlockSpec((None, s, LANES), lambda b, g, qi, nf, nk: (b, 0, 0)))
        else:
            in_specs.append(pl.BlockSpec((None, s, hb * LANES), lambda b, g, qi, nf, nk: (b, 0, g)))
    in_specs.append(pl.BlockSpec((None, s, hb * LANES), lambda b, g, qi, nf, nk: (b, 0, v_head0 // hb + g)))
    grid_spec = pltpu.PrefetchScalarGridSpec(
        num_scalar_prefetch=2,
        grid=(nb, heads // hb, nq),
        in_specs=in_specs,
        out_specs=pl.BlockSpec((None, bq, hb * LANES), lambda b, g, qi, nf, nk: (b, qi, g)),
        scratch_shapes=[pltpu.VMEM((hb, LANES, bq), F32)],
    )
    return pl.pallas_call(
        functools.partial(_tflash_kernel, n_parts=len(q_parts), k_shared=tuple(k_shared), hb=hb, bk=bk,
                          mode=mode, q_off=q_off),
        grid_spec=grid_spec,
        out_shape=jax.ShapeDtypeStruct((nb, nq * bq, heads * LANES), BF16),
        compiler_params=_params("parallel", "parallel", "arbitrary"),
        name=name,
    )(nfull, nkv, *q_parts, *k_parts, v)


def _sparse_kernel(nvis_ref, aq_in, aqi_in, aw_in, ka_ref, kb_ref, k_ref, v_ref,
                   o_ref, aq_ref, aqi_ref, aw_ref, key_ref, bias_ref, acc_ref, j_ref,
                   *, nq, kb, s_real, topk, index_bits, q_off):
    b = pl.program_id(0)
    qi = pl.program_id(1)
    nblk = nvis_ref[b * nq + qi]
    bq = aq_in.shape[0]
    cols = min(bq, SCORE_COLS)
    int_min = jnp.int32(INT_MIN)
    qchunk = (q_off + qi * bq + lax.broadcasted_iota(I32, (1, bq), 1)) >> CHUNK_SHIFT

    for g in range(aq_in.shape[1] // LANES):
        gs = slice(g * LANES, (g + 1) * LANES)
        aq_ref[gs, :] = aq_in[:, gs].astype(F32).T.astype(BF16)
    for g in range(aqi_in.shape[1] // LANES):
        gs = slice(g * LANES, (g + 1) * LANES)
        aqi_ref[gs, :] = aqi_in[:, gs].astype(F32).T.astype(BF16)
    aw_ref[...] = aw_in[...].T

    def score_tile(t, carry):
        off = pl.multiple_of(t * SCORE_ROWS, SCORE_ROWS)
        ka = ka_ref[pl.ds(off, SCORE_ROWS), :]
        kbb = kb_ref[pl.ds(off, SCORE_ROWS), :]
        kidx = off + lax.broadcasted_iota(I32, (SCORE_ROWS, cols), 0)
        kchunk = jnp.where(kidx < s_real, kidx >> CHUNK_SHIFT, jnp.int32(INT_MAX))
        for c in range(bq // cols):
            cs = slice(c * cols, (c + 1) * cols)
            acc = jnp.zeros((SCORE_ROWS, cols), F32)
            for hp in range(IDX_HEADS // 2):
                qp = aqi_ref[hp * LANES:(hp + 1) * LANES, cs]
                ze = jnp.dot(ka, qp, preferred_element_type=F32)
                zo = jnp.dot(kbb, qp, preferred_element_type=F32)
                acc = acc + jnp.maximum(ze, 0.0) * aw_ref[2 * hp:2 * hp + 1, cs]
                acc = acc + jnp.maximum(zo, 0.0) * aw_ref[2 * hp + 1:2 * hp + 2, cs]
            bits = pltpu.bitcast(acc, I32)
            key = bits ^ ((bits >> 31) & jnp.int32(INT_MAX))
            key_ref[pl.ds(off, SCORE_ROWS), cs] = jnp.where(kchunk <= qchunk[:, cs], key, int_min)
        return carry

    lax.fori_loop(0, nblk * (kb // SCORE_ROWS), score_tile, 0)

    def count(pred):
        def body(t, c):
            off = pl.multiple_of(t * COUNT_ROWS, COUNT_ROWS)
            kk = key_ref[pl.ds(off, COUNT_ROWS), :]
            idx = off + lax.broadcasted_iota(I32, (COUNT_ROWS, bq), 0)
            hit = jnp.where(pred(kk, idx), 1.0, 0.0)
            return c + jnp.sum(hit.reshape(COUNT_ROWS // SUBLANES, SUBLANES, bq), axis=0)
        c = lax.fori_loop(0, nblk * (kb // COUNT_ROWS), body, jnp.zeros((SUBLANES, bq), F32))
        return jnp.sum(c, axis=0, keepdims=True)

    def bit_step(i, carry):
        t_u, n_keep = carry
        cand_u = t_u | (jnp.int32(1) << (31 - i))
        cand_s = cand_u ^ int_min
        cnt = count(lambda kk, idx: kk >= cand_s)
        take = cnt >= topk
        return jnp.where(take, cand_u, t_u), jnp.where(take, cnt, n_keep)

    t_u, n_ge = lax.fori_loop(0, 32, bit_step, (jnp.zeros((1, bq), I32), jnp.zeros((1, bq), F32)))
    t_s = t_u ^ int_min
    need = jnp.logical_and(n_ge > topk, t_u != 0)
    j_ref[...] = jnp.full(j_ref.shape, INT_MAX, I32)

    @pl.when(jnp.max(jnp.where(need, 1.0, 0.0)) > 0.0)
    def _():
        want = topk - count(lambda kk, idx: kk > t_s)

        def idx_step(i, jc):
            cand = jc | (jnp.int32(1) << (index_bits - 1 - i))
            cnt = count(lambda kk, idx: jnp.logical_and(kk == t_s, idx < cand))
            return jnp.where(cnt < want, cand, jc)
        jc = lax.fori_loop(0, index_bits, idx_step, jnp.zeros((1, bq), I32))
        j_ref[...] = jnp.where(need, jc, jnp.int32(INT_MAX))

    jcut = j_ref[...]
    acc_ref[...] = jnp.zeros(acc_ref.shape, F32)

    def attend_block(j, carry):
        ms, ls = carry
        off = pl.multiple_of(j * kb, kb)
        kk = key_ref[pl.ds(off, kb), :]
        idx = off + lax.broadcasted_iota(I32, (kb, bq), 0)
        sel = jnp.logical_or(kk > t_s, jnp.logical_and(kk == t_s, idx <= jcut))
        sel = jnp.logical_and(sel, kk != int_min)
        bias_ref[...] = jnp.where(sel, 0.0, -jnp.inf)
        kblk = k_ref[pl.ds(off, kb), :]
        vblk = v_ref[pl.ds(off, kb), :]
        new_m, new_l = [], []
        for h in range(A_HEADS):
            hs = slice(h * HEAD_DIM, (h + 1) * HEAD_DIM)
            s = jnp.dot(kblk, aq_ref[hs, :], preferred_element_type=F32) + bias_ref[...]
            m_new = jnp.maximum(ms[h], jnp.max(s, axis=0, keepdims=True))
            p = jnp.exp2(s - m_new)
            alpha = jnp.exp2(ms[h] - m_new)
            new_l.append(alpha * ls[h] + jnp.sum(p, axis=0, keepdims=True))
            new_m.append(m_new)
            acc_ref[hs, :] = alpha * acc_ref[hs, :] + lax.dot_general(
                vblk, p.astype(BF16), (((0,), (0,)), ((), ())), preferred_element_type=F32)
        return tuple(new_m), tuple(new_l)

    init = (tuple(jnp.full((1, bq), M_INIT, F32) for _ in range(A_HEADS)),
            tuple(jnp.zeros((1, bq), F32) for _ in range(A_HEADS)))
    _, ls = lax.fori_loop(0, nblk, attend_block, init)
    for h in range(A_HEADS):
        hs = slice(h * HEAD_DIM, (h + 1) * HEAD_DIM)
        o_ref[:, hs] = (acc_ref[hs, :] / ls[h]).T.astype(o_ref.dtype)


def _sparse_attention(aq, aqi, aw, ka, kb_, k, v, nvis, *, bq, kb, s_real, topk, q_off, name):
    nb, s = k.shape[0], k.shape[1]
    nq = nvis.shape[0] // nb
    assert s % kb == 0 and kb % COUNT_ROWS == 0 and kb % SCORE_ROWS == 0 and bq % min(bq, SCORE_COLS) == 0
    qspec = lambda w: pl.BlockSpec((None, bq, w), lambda b, qi, nv: (b, qi, 0))
    kspec = pl.BlockSpec((None, s, LANES), lambda b, qi, nv: (b, 0, 0))
    aq_w, aqi_w = A_HEADS * HEAD_DIM, IDX_HEADS * IDX_DIM
    grid_spec = pltpu.PrefetchScalarGridSpec(
        num_scalar_prefetch=1,
        grid=(nb, nq),
        in_specs=[qspec(aq_w), qspec(aqi_w), qspec(LANES), kspec, kspec, kspec, kspec],
        out_specs=qspec(aq_w),
        scratch_shapes=[pltpu.VMEM((aq_w, bq), BF16), pltpu.VMEM((aqi_w, bq), BF16), pltpu.VMEM((LANES, bq), F32),
                        pltpu.VMEM((s, bq), I32), pltpu.VMEM((kb, bq), F32),
                        pltpu.VMEM((aq_w, bq), F32), pltpu.VMEM((1, bq), I32)],
    )
    return pl.pallas_call(
        functools.partial(_sparse_kernel, nq=nq, kb=kb, s_real=s_real, topk=topk,
                          index_bits=max(1, (s - 1).bit_length()), q_off=q_off),
        grid_spec=grid_spec,
        out_shape=jax.ShapeDtypeStruct((nb, nq * bq, aq_w), BF16),
        compiler_params=_params("parallel", "arbitrary"),
        name=name,
    )(nvis, aq, aqi, aw, ka, kb_, k, v)


def _router_kernel(x_ref, whi_ref, wlo_ref, bias_ref, ids_ref, g0_ref, g1_ref):
    x = x_ref[...]
    xh = x.astype(BF16)
    xl = (x - xh.astype(F32)).astype(BF16)
    whi, wlo = whi_ref[...], wlo_ref[...]
    lg = (jnp.dot(xh, whi, preferred_element_type=F32) + jnp.dot(xh, wlo, preferred_element_type=F32)
          + jnp.dot(xl, whi, preferred_element_type=F32)) + bias_ref[...]
    lane = lax.broadcasted_iota(I32, lg.shape, 1)
    neg = -jnp.inf
    is_group = lane < N_GROUPS
    gl = jnp.where(is_group, lg, neg)
    gmax = jnp.max(gl, axis=-1, keepdims=True)
    grp = jnp.min(jnp.where(gl == gmax, lane, LANES), axis=-1, keepdims=True)
    g1 = 1.0 / jnp.sum(jnp.where(is_group, jnp.exp(gl - gmax), 0.0), axis=-1, keepdims=True)
    lo = N_GROUPS + grp * EXPERTS_PER_GROUP
    el = jnp.where(jnp.logical_and(lane >= lo, lane < lo + EXPERTS_PER_GROUP), lg, neg)
    v1 = jnp.max(el, axis=-1, keepdims=True)
    i1 = jnp.min(jnp.where(el == v1, lane, LANES), axis=-1, keepdims=True)
    el2 = jnp.where(lane == i1, neg, el)
    v2 = jnp.max(el2, axis=-1, keepdims=True)
    i2 = jnp.min(jnp.where(el2 == v2, lane, LANES), axis=-1, keepdims=True)
    e21 = jnp.exp(v2 - v1)
    den = 1.0 + e21
    ids_ref[...] = jnp.where(lane == 0, i1 - N_GROUPS, jnp.where(lane == 1, i2 - N_GROUPS, 0))
    g0_ref[...] = jnp.broadcast_to(g1 * (1.0 / den), g0_ref.shape)
    g1_ref[...] = jnp.broadcast_to(g1 * (e21 / den), g1_ref.shape)


def _router(x, whi, wlo, bias, *, bm):
    n, d = x.shape
    row = pl.BlockSpec((bm, LANES), lambda i: (i, 0))
    return pl.pallas_call(
        _router_kernel,
        grid=(n // bm,),
        in_specs=[pl.BlockSpec((bm, d), lambda i: (i, 0)),
                  pl.BlockSpec((d, LANES), lambda i: (0, 0)),
                  pl.BlockSpec((d, LANES), lambda i: (0, 0)),
                  pl.BlockSpec((1, LANES), lambda i: (0, 0))],
        out_specs=[row, row, row],
        out_shape=[jax.ShapeDtypeStruct((n, LANES), I32), jax.ShapeDtypeStruct((n, LANES), F32),
                   jax.ShapeDtypeStruct((n, LANES), F32)],
        compiler_params=_params("parallel"),
        name="router",
    )(x, whi, wlo, bias)


def _dispatch_kernel(ppos_ref, x_ref, buf_hbm, xs_hbm, sems, *, bm):
    del buf_hbm
    base = pl.program_id(0) * bm * 2

    def row_copy(a):
        return pltpu.make_async_copy(x_ref.at[pl.ds(a >> 1, 1), :],
                                     xs_hbm.at[pl.ds(ppos_ref[base + a], 1), :], sems.at[a & 1])

    def start(a, c):
        row_copy(a).start()
        return c
    lax.fori_loop(0, 2 * bm, start, 0, unroll=4)

    def wait(a, c):
        row_copy(a).wait()
        return c
    lax.fori_loop(0, 2 * bm, wait, 0, unroll=4)


def _dispatch(x, ppos, buf, *, bm):
    n, d = x.shape
    assert n % bm == 0
    grid_spec = pltpu.PrefetchScalarGridSpec(
        num_scalar_prefetch=1,
        grid=(n // bm,),
        in_specs=[pl.BlockSpec((bm, d), lambda i, pp: (i, 0)), pl.BlockSpec(memory_space=pl.ANY)],
        out_specs=pl.BlockSpec(memory_space=pl.ANY),
        scratch_shapes=[pltpu.SemaphoreType.DMA((2,))],
    )
    return pl.pallas_call(
        functools.partial(_dispatch_kernel, bm=bm),
        grid_spec=grid_spec,
        out_shape=jax.ShapeDtypeStruct(buf.shape, buf.dtype),
        input_output_aliases={2: 0},
        compiler_params=_params("arbitrary"),
        name="dispatch",
    )(ppos, x, buf)


def _expert_kernel(te_ref, nv_ref, x_ref, wg_ref, wu_ref, wd_ref, y_ref):
    t = pl.program_id(0)

    @pl.when(nv_ref[t] > 0)
    def _():
        x = x_ref[...].astype(BF16)
        hg = jnp.dot(x, wg_ref[...], preferred_element_type=F32)
        hu = jnp.dot(x, wu_ref[...], preferred_element_type=F32)
        act = (hg * (1.0 / (1.0 + jnp.exp(-hg)))) * hu
        y_ref[...] = jnp.dot(act.astype(BF16), wd_ref[...], preferred_element_type=F32)

    @pl.when(nv_ref[t] == 0)
    def _():
        y_ref[...] = jnp.zeros(y_ref.shape, F32)


def _experts(xs, wg, wu, wd, layer, tile_e, tile_nvalid, *, bm):
    p, d = xs.shape
    f = wg.shape[-1]
    grid_spec = pltpu.PrefetchScalarGridSpec(
        num_scalar_prefetch=2,
        grid=(p // bm,),
        in_specs=[pl.BlockSpec((bm, d), lambda t, te, nv: (t, 0)),
                  pl.BlockSpec((None, None, d, f), lambda t, te, nv: (layer, te[t], 0, 0)),
                  pl.BlockSpec((None, None, d, f), lambda t, te, nv: (layer, te[t], 0, 0)),
                  pl.BlockSpec((None, None, f, d), lambda t, te, nv: (layer, te[t], 0, 0))],
        out_specs=pl.BlockSpec((bm, d), lambda t, te, nv: (t, 0)),
    )
    return pl.pallas_call(
        _expert_kernel,
        grid_spec=grid_spec,
        out_shape=jax.ShapeDtypeStruct((p, d), F32),
        compiler_params=_params("arbitrary"),
        name="experts",
    )(tile_e, tile_nvalid, xs, wg, wu, wd)


def _expert_plan(eid, *, bm):
    na = eid.shape[0] * 2
    flat_e = eid.reshape(na)
    onehot = (flat_e[:, None] == jnp.arange(N_EXPERTS, dtype=I32)[None, :]).astype(I32)
    csum = jnp.cumsum(onehot, axis=0)
    counts = csum[-1]
    padded = ((counts + bm - 1) // bm) * bm
    pend = jnp.cumsum(padded)
    pstart = pend - padded
    ppos = jnp.sum(onehot * (pstart[None, :] + csum - 1), axis=1).astype(I32)
    total = na + N_EXPERTS * bm
    tile_start = jnp.arange(total // bm, dtype=I32) * bm
    tile_e = jnp.minimum(jnp.sum((tile_start[:, None] >= pend[None, :]).astype(I32), axis=1), N_EXPERTS - 1)
    tile_nvalid = jnp.clip(pstart[tile_e] + counts[tile_e] - tile_start, 0, bm).astype(I32)
    return ppos, tile_e.astype(I32), tile_nvalid, total


def _combine_ln_kernel(ppos_ref, x_ref, g0_ref, g1_ref, y_hbm, g_ref, b_ref, of_ref, ob_ref, ybuf, sem,
                       *, bm, alpha):
    i = pl.program_id(0)
    d = x_ref.shape[-1]

    def row_copy(blk, a):
        par = blk & 1
        return pltpu.make_async_copy(y_hbm.at[pl.ds(ppos_ref[blk * (2 * bm) + a], 1), :],
                                     ybuf.at[par, a & 1, pl.ds(a >> 1, 1), :], sem.at[par, a & 1])

    def start_block(blk):
        def start(a, c):
            row_copy(blk, a).start()
            return c
        lax.fori_loop(0, 2 * bm, start, 0, unroll=4)

    @pl.when(i == 0)
    def _():
        start_block(i)

    @pl.when(i + 1 < pl.num_programs(0))
    def _():
        start_block(i + 1)

    def wait(a, c):
        row_copy(i, a).wait()
        return c
    lax.fori_loop(0, 2 * bm, wait, 0, unroll=4)

    reps = d // LANES
    par = i & 1
    y = (alpha * x_ref[...] + jnp.tile(g0_ref[...], (1, reps)) * ybuf[par, 0]
         + jnp.tile(g1_ref[...], (1, reps)) * ybuf[par, 1])
    out = _layer_norm(y, g_ref[...], b_ref[...])
    of_ref[...] = out
    ob_ref[...] = out.astype(BF16)


def _combine_ln(x, ys, ppos, g0, g1, g, b, *, bm, alpha):
    n, d = x.shape
    row = lambda w: pl.BlockSpec((bm, w), lambda i, pp: (i, 0))
    grid_spec = pltpu.PrefetchScalarGridSpec(
        num_scalar_prefetch=1,
        grid=(n // bm,),
        in_specs=[row(d), row(LANES), row(LANES), pl.BlockSpec(memory_space=pl.ANY),
                  pl.BlockSpec((1, d), lambda i, pp: (0, 0)), pl.BlockSpec((1, d), lambda i, pp: (0, 0))],
        out_specs=[row(d), row(d)],
        scratch_shapes=[pltpu.VMEM((2, 2, bm, d), F32), pltpu.SemaphoreType.DMA((2, 2))],
    )
    return pl.pallas_call(
        functools.partial(_combine_ln_kernel, bm=bm, alpha=alpha),
        grid_spec=grid_spec,
        out_shape=[jax.ShapeDtypeStruct((n, d), F32), jax.ShapeDtypeStruct((n, d), BF16)],
        compiler_params=_params("arbitrary"),
        name="combine_ln",
    )(ppos, x, g0, g1, ys, g, b)


def _rope_tables(pos):
    posf = pos.astype(F32)[:, None]
    inv128 = ROPE_THETA ** (-jnp.arange(HEAD_DIM // 2, dtype=F32) * 2.0 / HEAD_DIM)
    ang = posf * inv128[None, :]
    cos, sin = jnp.cos(ang), jnp.sin(ang)
    a128 = jnp.concatenate([cos, cos], axis=-1)
    b128 = jnp.concatenate([-sin, sin], axis=-1)
    inv64 = ROPE_THETA ** (-jnp.arange(IDX_DIM // 2, dtype=F32) * 2.0 / IDX_DIM)
    ang = posf * inv64[None, :]
    cos, sin = jnp.cos(ang), jnp.sin(ang)
    zero = jnp.zeros_like(sin)
    a64 = jnp.concatenate([cos, cos, cos, cos], axis=-1)
    b64 = jnp.concatenate([-sin, zero, -sin, zero], axis=-1)
    c64 = jnp.concatenate([zero, sin, zero, sin], axis=-1)
    return (a128, b128), (a64, b64, c64)


def _z_weight(w_in):
    o = _IN_OFF
    d = w_in.shape[0]
    cols = [w_in[:, o[0]:o[4]],
            w_in[:, o[6]:o[9]],
            w_in[:, o[10]:o[12]],
            w_in[:, o[4]:o[5]],
            w_in[:, o[12]:o[13]],
            w_in[:, o[5]:o[6]],
            w_in[:, o[9]:o[10]],
            jnp.zeros((d, LANES - IDX_HEADS - B_HEADS), w_in.dtype)]
    return jnp.concatenate(cols, axis=1).astype(BF16)


def _uq_weight(w):
    r = w.shape[0]
    w3 = w.reshape(r, C_HEADS, C_NOPE + C_ROPE)
    nope = w3[:, :, :C_NOPE].reshape(r, C_HEADS * C_NOPE)
    rope = jnp.pad(w3[:, :, C_NOPE:], ((0, 0), (0, 0), (0, LANES - C_ROPE))).reshape(r, C_HEADS * LANES)
    return jnp.concatenate([nope, rope], axis=1).astype(BF16)


def _pad_lanes(x):
    return jnp.pad(x, [(0, 0)] * (x.ndim - 1) + [(0, LANES - x.shape[-1])])


def _pad_rows(x, rows):
    return jnp.pad(x, [(0, 0), (0, rows - x.shape[1])] + [(0, 0)] * (x.ndim - 2))


def _bf16_head(c):
    bits = lax.bitcast_convert_type(c, I32) & jnp.int32(-65536)
    return lax.bitcast_convert_type(bits, F32)


def _fox_bias_kernel(c_ref, q_ref, k_ref):
    c = c_ref[...] * LOG2E
    lane = lax.broadcasted_iota(I32, c.shape, 1)
    head_mask = jnp.int32(-65536)
    for h in range(B_HEADS):
        ch = jnp.sum(jnp.where(lane == h, c, 0.0), axis=1, keepdims=True)
        hi = pltpu.bitcast(pltpu.bitcast(ch, I32) & head_mask, F32)
        r = ch - hi
        mid = pltpu.bitcast(pltpu.bitcast(r, I32) & head_mask, F32)
        lo = r - mid
        parts = jnp.where(lane == 0, hi, jnp.where(lane == 1, mid, jnp.where(lane == 2, lo, 0.0)))
        ones = jnp.where(jnp.logical_and(lane >= 3, lane < 6), 1.0, 0.0)
        q_ref[:, h * LANES:(h + 1) * LANES] = (parts + ones).astype(BF16)
        k_ref[:, h * LANES:(h + 1) * LANES] = (pltpu.roll(ones, LANES - 3, 1) - pltpu.roll(parts, 3, 1)).astype(BF16)


def _fox_bias_parts(csum):
    nb, s, h = csum.shape
    rows = nb * s
    bm = _pick(rows, 512, 256, 128, 64, 32, 16)
    spec = lambda w: pl.BlockSpec((bm, w), lambda i: (i, 0))
    qb, kb = pl.pallas_call(
        _fox_bias_kernel,
        grid=(rows // bm,),
        in_specs=[spec(LANES)],
        out_specs=[spec(h * LANES), spec(h * LANES)],
        out_shape=[jax.ShapeDtypeStruct((rows, h * LANES), BF16)] * 2,
        compiler_params=_params("parallel"),
        name="fox_bias",
    )(_pad_lanes(csum.reshape(rows, h)))
    return qb.reshape(nb, s, -1), kb.reshape(nb, s, -1)


def _pick(n, *cands):
    for c in cands:
        if n % c == 0:
            return c
    return n


def kernel(x_prompt, x_sample, cache_a_k, cache_a_v, cache_a_kidx, cache_b_k, cache_b_v, cache_b_logf,
           cache_c_latent, cache_c_krope, cache_mem_k, cache_mem_v, mem_prompt,
           w_in, b_f, c_q_norm, w_c_uq, c_kv_norm, w_c_uk, w_c_uv, w_o, ln1_g, ln1_b,
           w_mq, w_mk, w_mv, w_mo, ln2_g, ln2_b, w_rg, b_rg, w_re, b_re,
           w_gate, w_up, w_down, ln3_g, ln3_b):
    depth = w_in.shape[0]
    bp, tp, d = x_prompt.shape
    bs, ts, _ = x_sample.shape
    past = cache_a_k.shape[2]
    n_mem = mem_prompt.shape[1]
    assert bp == 1
    n_p, n_s = bp * tp, bs * ts
    n = n_p + n_s
    alpha = (2 * depth) ** 0.25
    s_s = past + ts
    s_pad = -(-s_s // COUNT_ROWS) * COUNT_ROWS

    bm_tok = _pick(n, 1024, 512, 256, 128, 64)
    bm_post = _pick(n, 256, 128, 64)
    bq_f = _pick(tp, 1024, 512, 256, 128)
    bq_a = _pick(tp, 512, 256, 128)
    kb_a = _pick(tp, 1024, 512, 256)
    bm_moe = 256

    res_first, res_second, res_second_row0 = x_prompt.reshape(n_p, d), x_sample.reshape(n_s, d), 0
    xb = jnp.concatenate([res_first.astype(BF16), res_second.astype(BF16)], axis=0)
    pos_p = jnp.arange(tp, dtype=I32)
    pos_s = past + jnp.arange(ts, dtype=I32)
    tabs128, tabs64 = _rope_tables(jnp.concatenate([pos_p, jnp.tile(pos_s, bs)]))

    nq_f = tp // bq_f
    nfull_p = jnp.arange(nq_f, dtype=I32)
    nkv_p = nfull_p + 1
    one = jnp.ones((1,), I32)
    zero = jnp.zeros((1,), I32)
    nvis_p = (((jnp.arange(tp // bq_a, dtype=I32) + 1) * bq_a + kb_a - 1) // kb_a).astype(I32)
    nvis_s = jnp.ones((bs,), I32)

    xs_buf = jnp.zeros((2 * n + N_EXPERTS * bm_moe, d), F32)
    w_o_b, w_mo_b = w_o.astype(BF16), w_mo.astype(BF16)
    w_gate_b, w_up_b, w_down_b = w_gate.astype(BF16), w_up.astype(BF16), w_down.astype(BF16)
    state_p = [[] for _ in range(8)]
    state_s = [[] for _ in range(8)]
    mem_k_out, mem_v_out = [], []

    for l in range(depth):
        z = _matmul(xb, _z_weight(w_in[l]), bm=bm_tok, bn=512, out_dtype=F32, name="in_proj")
        bf_row = jnp.zeros((1, LANES), F32).at[0, SM2_BF:SM2_BF + B_HEADS].set(b_f[l].astype(F32))
        aq, ak, aqi, bq_, cqn, clat, sm1, sm2, akb, avb, bkb, bvb = _post(
            z, tabs128, tabs64, bf_row, c_q_norm[l].reshape(1, -1).astype(F32),
            c_kv_norm[l].reshape(1, -1).astype(F32), bm=bm_post)
        cq = _matmul_rope(cqn, _uq_weight(w_c_uq[l]), tabs64, bm=bm_tok, bn=512,
                          first_rope_col=C_HEADS * C_NOPE, out_scale=(C_NOPE + C_ROPE) ** -0.5 * LOG2E,
                          name="c_q_up")
        av = z[:, Z_AV:Z_AV + HEAD_DIM]
        aki = sm1[:, :IDX_DIM]
        ckr = sm1[:, IDX_DIM:]
        blogf = sm2[:, SM2_BF:SM2_BF + B_HEADS]
        bk_ = z[:, Z_BK:Z_BK + B_HEADS * HEAD_DIM]
        bv_ = z[:, Z_BV:Z_BV + B_HEADS * HEAD_DIM]
        rows = (ak, av, aki, bk_.reshape(n, B_HEADS, HEAD_DIM), bv_.reshape(n, B_HEADS, HEAD_DIM),
                blogf, clat, ckr)
        for i, r in enumerate(rows):
            state_p[i].append(r[:n_p].reshape((bp, tp) + r.shape[1:]))
            state_s[i].append(r[n_p:].reshape((bs, ts) + r.shape[1:]))
        grp_p = lambda a: a[:n_p].reshape(bp, tp, -1)
        grp_s = lambda a: a[n_p:].reshape(bs, ts, -1)
        cat_s = lambda c, r: jnp.concatenate([c.reshape(bs, past, -1).astype(r.dtype), grp_s(r)], axis=1)
        all_n = lambda a: a.reshape(1, n, -1)

        def sparse(aq_g, aqi_g, aw_g, ki, k, v, nvis, *, bq, kb, s_real, q_off, name):
            ka = _pad_lanes(ki).astype(BF16)
            kb_ = jnp.concatenate([jnp.zeros_like(ki), ki], axis=-1).astype(BF16)
            out = _sparse_attention(aq_g, aqi_g, aw_g, ka, kb_, k, v, nvis, bq=bq, kb=kb, s_real=s_real,
                                    topk=min(TOPK_MAX, s_real // 4), q_off=q_off, name=name)
            return out.reshape(-1, A_HEADS * HEAD_DIM)

        oa_p = sparse(all_n(aq), all_n(aqi), all_n(sm2), grp_p(aki), grp_p(akb), grp_p(avb), nvis_p,
                      bq=bq_a, kb=kb_a, s_real=tp, q_off=0, name="sparse_prompt")
        oa_s = sparse(grp_s(aq), grp_s(aqi), grp_s(sm2),
                      _pad_rows(cat_s(cache_a_kidx[l], aki), s_pad), _pad_rows(cat_s(cache_a_k[l], akb), s_pad),
                      _pad_rows(cat_s(cache_a_v[l], avb), s_pad), nvis_s,
                      bq=ts, kb=s_pad, s_real=s_s, q_off=past, name="sparse_sample")

        csum_p = jnp.cumsum(grp_p(blogf), axis=1)
        qb_p, kbias_p = _fox_bias_parts(csum_p)
        ob_p = _tflash([all_n(bq_), qb_p], (0, 0), [all_n(bkb), kbias_p], (False, False), all_n(bvb), 0,
                       nfull_p, nkv_p, heads=B_HEADS, s=tp, hb=2, bq=bq_f, bk=bq_f, mode="causal", q_off=0,
                       name="fox_prompt")
        csum_s = jnp.cumsum(cat_s(cache_b_logf[l], blogf), axis=1)
        qb_all_s, kbias_s = _fox_bias_parts(csum_s)
        qb_s = qb_all_s[:, past:]
        ob_s = _tflash([grp_s(bq_), qb_s], (0, 0), [cat_s(cache_b_k[l], bkb), kbias_s], (False, False),
                       cat_s(cache_b_v[l], bvb), 0, zero, one, heads=B_HEADS, s=s_s, hb=B_HEADS, bq=ts, bk=s_s,
                       mode="causal", q_off=past, name="fox_sample")

        w_ukv = jnp.concatenate([w_c_uk[l], w_c_uv[l]], axis=1).astype(BF16)
        kv_p = _matmul(clat[:n_p], w_ukv, bm=_pick(n_p, 1024, 512, 256), bn=w_ukv.shape[1], out_dtype=BF16,
                       name="c_kv_up_prompt").reshape(bp, tp, -1)
        oc_p = _tflash([all_n(cq), all_n(cq)], (0, C_HEADS),
                       [kv_p, _pad_lanes(grp_p(ckr)).astype(BF16)], (False, True), kv_p, C_HEADS,
                       nfull_p, nkv_p, heads=C_HEADS, s=tp, hb=2, bq=bq_f, bk=bq_f, mode="chunk", q_off=0,
                       name="mla_prompt")
        lat_s = cat_s(cache_c_latent[l], clat)
        kv_s = _matmul(lat_s.reshape(bs * s_s, -1), w_ukv, bm=_pick(bs * s_s, 1024, 512, 256, 128, 64),
                       bn=w_ukv.shape[1], out_dtype=BF16, name="c_kv_up_sample").reshape(bs, s_s, -1)
        oc_s = _tflash([grp_s(cq), grp_s(cq)], (0, C_HEADS),
                       [kv_s, _pad_lanes(cat_s(cache_c_krope[l], ckr)).astype(BF16)], (False, True), kv_s, C_HEADS,
                       zero, one, heads=C_HEADS, s=s_s, hb=C_HEADS // 2, bq=ts, bk=s_s, mode="chunk", q_off=past,
                       name="mla_sample")

        x1, x1b = _mm_ln_cols([oa_p, ob_p.reshape(n_p, -1), oc_p.reshape(n_p, -1)],
                              [oa_s, ob_s.reshape(n_s, -1), oc_s.reshape(n_s, -1)],
                              w_o_b, l, res_first, res_second, res_second_row0,
                              ln1_g[l].reshape(1, d), ln1_b[l].reshape(1, d),
                              bm=_pick(math.gcd(n_p, n_s), 512, 256, 128, 64), bn=_pick(d, 512, 256, 128),
                              alpha=alpha, name="out_proj_ln1")

        mw = MEM_HEADS * MEM_DIM
        w_mkv = jnp.concatenate([w_mk[l], w_mv[l]], axis=1).astype(BF16)
        mkv = _matmul(mem_prompt.reshape(bp * n_mem, d), w_mkv, bm=_pick(bp * n_mem, 256, 128, 64), bn=512,
                      out_dtype=F32, name="mem_kv")
        mem_k_out.append(mkv[:, :mw].reshape(bp, n_mem, MEM_HEADS, MEM_DIM))
        mem_v_out.append(mkv[:, mw:].reshape(bp, n_mem, MEM_HEADS, MEM_DIM))
        qm = _matmul(x1b, w_mq[l].astype(BF16), bm=bm_tok, bn=mw, out_dtype=BF16, name="mem_q",
                     out_scale=MEM_DIM ** -0.5 * LOG2E)
        mkv3 = mkv.reshape(bp, n_mem, 2 * mw).astype(BF16)
        om_p = _tflash([all_n(qm)], (0,), [mkv3], (False,), mkv3, MEM_HEADS,
                       jnp.ones((nq_f,), I32), jnp.ones((nq_f,), I32), heads=MEM_HEADS, s=n_mem,
                       hb=2, bq=bq_f, bk=n_mem, mode="none", q_off=0, name="mem_prompt")
        om_s = _tflash([grp_s(qm)], (0,), [cache_mem_k[l].reshape(bs, n_mem, mw).astype(BF16)], (False,),
                       cache_mem_v[l].reshape(bs, n_mem, mw).astype(BF16), 0, one, one, heads=MEM_HEADS, s=n_mem,
                       hb=MEM_HEADS, bq=ts, bk=n_mem, mode="none", q_off=0, name="mem_sample")
        x2, _ = _mm_ln_cols([om_p.reshape(n_p, mw)], [om_s.reshape(n_s, mw)], w_mo_b, l, x1, x1, n_p,
                            ln2_g[l].reshape(1, d), ln2_b[l].reshape(1, d),
                            bm=_pick(math.gcd(n_p, n_s), 512, 256, 128, 64), bn=_pick(d, 1024, 512, 256, 128),
                            alpha=alpha, name="mem_out_ln2")

        w_r = jnp.concatenate([w_rg[l], jnp.transpose(w_re[l], (1, 0, 2)).reshape(d, N_EXPERTS)], axis=1).astype(F32)
        w_r = _pad_lanes(w_r)
        w_r_head = _bf16_head(w_r)
        w_r_hi = w_r_head.astype(BF16)
        w_r_lo = (w_r - w_r_head).astype(BF16)
        b_r = _pad_lanes(jnp.concatenate([b_rg[l], b_re[l].reshape(-1)]).astype(F32).reshape(1, -1))
        ids, g0, g1 = _router(x2, w_r_hi, w_r_lo, b_r, bm=_pick(n, 512, 256, 128, 64))
        ppos, tile_e, tile_nvalid, total = _expert_plan(ids[:, :2], bm=bm_moe)
        assert total == xs_buf.shape[0]
        xs_buf = _dispatch(x2, ppos, xs_buf, bm=_pick(n, 256, 128, 64))
        ys = _experts(xs_buf, w_gate_b, w_up_b, w_down_b, l, tile_e, tile_nvalid, bm=bm_moe)
        x, xb = _combine_ln(x2, ys, ppos, g0, g1, ln3_g[l].reshape(1, d), ln3_b[l].reshape(1, d),
                            bm=_pick(n, 256, 128, 64), alpha=alpha)
        res_first, res_second, res_second_row0 = x, x, n_p

    outs = [x[:n_p].reshape(bp, tp, d), x[n_p:].reshape(bs, ts, d)]
    outs += [jnp.stack(c) for c in state_p]
    outs += [jnp.stack(mem_k_out), jnp.stack(mem_v_out)]
    outs += [jnp.stack(c) for c in state_s]
    return tuple(outs)
```

```python
import functools
import math

import jax
import jax.numpy as jnp
import numpy as np
from jax import lax
from jax.experimental import pallas as pl
from jax.experimental.pallas import tpu as pltpu

F32 = jnp.float32
BF16 = jnp.bfloat16
I32 = jnp.int32

LANES = 128
SUBLANES = 8
VMEM_LIMIT = 56 * 1024 * 1024

CHUNK_SHIFT = 6
HEAD_DIM = 128
ROPE_THETA = 10000.0
A_HEADS = 8
IDX_HEADS = 32
IDX_DIM = 64
TOPK_MAX = 256
B_HEADS = 8
C_HEADS = 16
C_Q_RANK = 768
C_KV_RANK = 256
C_NOPE = 128
C_ROPE = 64
C_V = 128
MEM_HEADS = 4
MEM_DIM = 128
N_GROUPS = 4
EXPERTS_PER_GROUP = 4
N_EXPERTS = N_GROUPS * EXPERTS_PER_GROUP
LN_EPS = 1e-5
RMS_EPS = 1e-6
LOG2E = math.log2(math.e)

_IN_SIZES = (A_HEADS * HEAD_DIM, HEAD_DIM, HEAD_DIM, IDX_HEADS * IDX_DIM, IDX_DIM, IDX_HEADS,
             B_HEADS * HEAD_DIM, B_HEADS * HEAD_DIM, B_HEADS * HEAD_DIM, B_HEADS,
             C_Q_RANK, C_KV_RANK, C_ROPE)
_IN_OFF = np.concatenate([[0], np.cumsum(_IN_SIZES)]).astype(int)

Z_AQ = 0
Z_AK = Z_AQ + A_HEADS * HEAD_DIM
Z_AV = Z_AK + HEAD_DIM
Z_AQI = Z_AV + HEAD_DIM
Z_BQ = Z_AQI + IDX_HEADS * IDX_DIM
Z_BK = Z_BQ + B_HEADS * HEAD_DIM
Z_BV = Z_BK + B_HEADS * HEAD_DIM
Z_CQ = Z_BV + B_HEADS * HEAD_DIM
Z_CKV = Z_CQ + C_Q_RANK
Z_SM1 = Z_CKV + C_KV_RANK
Z_SM2 = Z_SM1 + LANES
Z_WIDTH = Z_SM2 + LANES
SM2_BF = IDX_HEADS

M_INIT = -1e30
INT_MIN = -2 ** 31
INT_MAX = 2 ** 31 - 1
SCORE_ROWS = 128
SCORE_COLS = 256
COUNT_ROWS = 256
LN_ROWS = 64


def _params(*sem):
    return pltpu.CompilerParams(dimension_semantics=sem, vmem_limit_bytes=VMEM_LIMIT)


def _mm_kernel(x_ref, w_ref, o_ref, *, out_scale):
    x = x_ref[...].astype(BF16)
    acc = jnp.dot(x, w_ref[...], preferred_element_type=F32)
    if out_scale != 1.0:
        acc = acc * out_scale
    o_ref[...] = acc.astype(o_ref.dtype)


def _matmul(x, w, *, bm, bn, out_dtype, name, out_scale=1.0):
    m, k = x.shape
    n = w.shape[1]
    assert m % bm == 0 and n % bn == 0, (x.shape, w.shape, bm, bn)
    return pl.pallas_call(
        functools.partial(_mm_kernel, out_scale=out_scale),
        grid=(m // bm, n // bn),
        in_specs=[pl.BlockSpec((bm, k), lambda i, j: (i, 0)),
                  pl.BlockSpec((k, bn), lambda i, j: (0, j))],
        out_specs=pl.BlockSpec((bm, bn), lambda i, j: (i, j)),
        out_shape=jax.ShapeDtypeStruct((m, n), out_dtype),
        compiler_params=_params("parallel", "parallel"),
        name=name,
    )(x, w)


def _rope64(x, a, b, c):
    return x * a + pltpu.roll(x, 96, 1) * b + pltpu.roll(x, 32, 1) * c


def _rope128(x, a, b):
    return x * a + pltpu.roll(x, 64, 1) * b


def _mm_rope_kernel(x_ref, w_ref, a_ref, b_ref, c_ref, o_ref, *, first_rope_block, out_scale):
    j = pl.program_id(1)
    acc = jnp.dot(x_ref[...], w_ref[...], preferred_element_type=F32) * out_scale

    @pl.when(j < first_rope_block)
    def _():
        o_ref[...] = acc.astype(o_ref.dtype)

    @pl.when(j >= first_rope_block)
    def _():
        a, b, c = a_ref[...], b_ref[...], c_ref[...]
        for g in range(acc.shape[1] // LANES):
            sl = slice(g * LANES, (g + 1) * LANES)
            o_ref[:, sl] = _rope64(acc[:, sl], a, b, c).astype(o_ref.dtype)


def _matmul_rope(x, w, tabs, *, bm, bn, first_rope_col, out_scale, name):
    m, k = x.shape
    n = w.shape[1]
    assert m % bm == 0 and n % bn == 0 and first_rope_col % bn == 0
    tab_spec = pl.BlockSpec((bm, LANES), lambda i, j: (i, 0))
    return pl.pallas_call(
        functools.partial(_mm_rope_kernel, first_rope_block=first_rope_col // bn, out_scale=out_scale),
        grid=(m // bm, n // bn),
        in_specs=[pl.BlockSpec((bm, k), lambda i, j: (i, 0)),
                  pl.BlockSpec((k, bn), lambda i, j: (0, j)),
                  tab_spec, tab_spec, tab_spec],
        out_specs=pl.BlockSpec((bm, bn), lambda i, j: (i, j)),
        out_shape=jax.ShapeDtypeStruct((m, n), BF16),
        compiler_params=_params("parallel", "parallel"),
        name=name,
    )(x, w, *tabs)


def _post_kernel(z_ref, a128_ref, b128_ref, a64_ref, b64_ref, c64_ref, bf_ref, gq_ref, gkv_ref,
                 aq_ref, ak_ref, aqi_ref, bq_ref, cqn_ref, clat_ref, sm1_ref, sm2_ref,
                 akb_ref, avb_ref, bkb_ref, bvb_ref):
    a128, b128 = a128_ref[...], b128_ref[...]
    a64, b64, c64 = a64_ref[...], b64_ref[...], c64_ref[...]
    qscale = HEAD_DIM ** -0.5 * LOG2E
    for g in range(A_HEADS):
        src = slice(Z_AQ + g * LANES, Z_AQ + (g + 1) * LANES)
        aq_ref[:, g * LANES:(g + 1) * LANES] = (_rope128(z_ref[:, src], a128, b128) * qscale).astype(BF16)
    ak = _rope128(z_ref[:, Z_AK:Z_AK + LANES], a128, b128)
    ak_ref[...] = ak
    akb_ref[...] = ak.astype(BF16)
    avb_ref[...] = z_ref[:, Z_AV:Z_AV + LANES].astype(BF16)
    bkb_ref[...] = z_ref[:, Z_BK:Z_BK + B_HEADS * HEAD_DIM].astype(BF16)
    bvb_ref[...] = z_ref[:, Z_BV:Z_BV + B_HEADS * HEAD_DIM].astype(BF16)
    for g in range(IDX_HEADS * IDX_DIM // LANES):
        src = slice(Z_AQI + g * LANES, Z_AQI + (g + 1) * LANES)
        aqi_ref[:, g * LANES:(g + 1) * LANES] = _rope64(z_ref[:, src], a64, b64, c64).astype(BF16)
    bq_ref[...] = (z_ref[:, Z_BQ:Z_BQ + B_HEADS * HEAD_DIM] * qscale).astype(BF16)
    cq = z_ref[:, Z_CQ:Z_CQ + C_Q_RANK]
    cq = cq * lax.rsqrt(jnp.mean(cq * cq, axis=-1, keepdims=True) + RMS_EPS) * gq_ref[...]
    cqn_ref[...] = cq.astype(BF16)
    ckv = z_ref[:, Z_CKV:Z_CKV + C_KV_RANK]
    clat_ref[...] = ckv * lax.rsqrt(jnp.mean(ckv * ckv, axis=-1, keepdims=True) + RMS_EPS) * gkv_ref[...]
    sm1_ref[...] = _rope64(z_ref[:, Z_SM1:Z_SM1 + LANES], a64, b64, c64)
    s2 = z_ref[:, Z_SM2:Z_SM2 + LANES]
    lane = lax.broadcasted_iota(I32, s2.shape, 1)
    f = s2 + bf_ref[...]
    logf = jnp.minimum(f, 0.0) - jnp.log1p(jnp.exp(-jnp.abs(f)))
    sm2_ref[...] = jnp.where(lane < SM2_BF, s2 * (IDX_HEADS ** -0.5 * IDX_DIM ** -0.5),
                             jnp.where(lane < SM2_BF + B_HEADS, logf, 0.0))


def _post(z, tabs128, tabs64, bf_row, gq, gkv, *, bm):
    n = z.shape[0]
    row = lambda w: pl.BlockSpec((bm, w), lambda i: (i, 0))
    const = lambda w: pl.BlockSpec((1, w), lambda i: (0, 0))
    widths = (A_HEADS * HEAD_DIM, LANES, IDX_HEADS * IDX_DIM, B_HEADS * HEAD_DIM, C_Q_RANK, C_KV_RANK, LANES, LANES,
              LANES, LANES, B_HEADS * HEAD_DIM, B_HEADS * HEAD_DIM)
    dtypes = (BF16, F32, BF16, BF16, BF16, F32, F32, F32, BF16, BF16, BF16, BF16)
    return pl.pallas_call(
        _post_kernel,
        grid=(n // bm,),
        in_specs=[row(Z_WIDTH)] + [row(LANES)] * 5 + [const(LANES), const(C_Q_RANK), const(C_KV_RANK)],
        out_specs=[row(w) for w in widths],
        out_shape=[jax.ShapeDtypeStruct((n, w), d) for w, d in zip(widths, dtypes)],
        compiler_params=_params("parallel"),
        name="post_projection",
    )(z, *tabs128, *tabs64, bf_row, gq, gkv)


def _layer_norm(y, g, b):
    mu = jnp.mean(y, axis=-1, keepdims=True)
    yc = y - mu
    var = jnp.mean(yc * yc, axis=-1, keepdims=True)
    return yc * lax.rsqrt(var + LN_EPS) * g + b


def _mm_res_ln_kernel(x_ref, w_ref, r_ref, g_ref, b_ref, of_ref, ob_ref, acc_ref, *, nk, alpha):
    k = pl.program_id(1)

    @pl.when(k == 0)
    def _():
        acc_ref[...] = jnp.zeros_like(acc_ref)

    acc_ref[...] += jnp.dot(x_ref[...], w_ref[...], preferred_element_type=F32)

    @pl.when(k == nk - 1)
    def _():
        out = _layer_norm(alpha * r_ref[...] + acc_ref[...], g_ref[...], b_ref[...])
        of_ref[...] = out
        ob_ref[...] = out.astype(BF16)


def _mm_res_ln(x, w, res, g, b, *, bm, bk, alpha, name):
    m, k = x.shape
    d = w.shape[1]
    nk = k // bk
    return pl.pallas_call(
        functools.partial(_mm_res_ln_kernel, nk=nk, alpha=alpha),
        grid=(m // bm, nk),
        in_specs=[pl.BlockSpec((bm, bk), lambda i, kk: (i, kk)),
                  pl.BlockSpec((bk, d), lambda i, kk: (kk, 0)),
                  pl.BlockSpec((bm, d), lambda i, kk: (i, 0)),
                  pl.BlockSpec((1, d), lambda i, kk: (0, 0)),
                  pl.BlockSpec((1, d), lambda i, kk: (0, 0))],
        out_specs=[pl.BlockSpec((bm, d), lambda i, kk: (i, 0)),
                   pl.BlockSpec((bm, d), lambda i, kk: (i, 0))],
        out_shape=[jax.ShapeDtypeStruct((m, d), F32), jax.ShapeDtypeStruct((m, d), BF16)],
        scratch_shapes=[pltpu.VMEM((bm, d), F32)],
        compiler_params=_params("parallel", "arbitrary"),
        name=name,
    )(x, w, res, g, b)


def _mm_ln_cols_kernel(*refs, n_parts, nj, bn, alpha, first_rows):
    xa = refs[:n_parts]
    xb = refs[n_parts:2 * n_parts]
    w_ref, ra_ref, rb_ref, g_ref, b_ref, of_ref, ob_ref = refs[2 * n_parts:]
    i = pl.program_id(0)
    j = pl.program_id(1)

    def emit(x_refs, r_ref):
        x = jnp.concatenate([xr[...] for xr in x_refs], axis=1)
        y = alpha * r_ref[...] + jnp.dot(x, w_ref[...], preferred_element_type=F32)
        for jj in range(nj):
            @pl.when(j == jj)
            def _(jj=jj):
                of_ref[:, jj * bn:(jj + 1) * bn] = y

    @pl.when(i < first_rows)
    def _():
        emit(xa, ra_ref)

    @pl.when(i >= first_rows)
    def _():
        emit(xb, rb_ref)

    @pl.when(j == nj - 1)
    def _():
        for r0 in range(0, of_ref.shape[0], LN_ROWS):
            rs = slice(r0, r0 + LN_ROWS)
            out = _layer_norm(of_ref[rs, :], g_ref[...], b_ref[...])
            of_ref[rs, :] = out
            ob_ref[rs, :] = out.astype(BF16)


def _mm_ln_cols(x_first, x_second, w, layer, res_first, res_second, res_second_row0, g, b, *, bm, bn, alpha, name):
    m1, m2 = x_first[0].shape[0], x_second[0].shape[0]
    d = w.shape[-1]
    k = w.shape[-2]
    nj = d // bn
    assert m1 % bm == 0 and m2 % bm == 0 and res_second_row0 % bm == 0 and d % bn == 0
    nb1, nb2, r0b = m1 // bm, m2 // bm, res_second_row0 // bm
    first = lambda i: jnp.minimum(i, nb1 - 1)
    second = lambda i: jnp.maximum(i - nb1, 0)
    once = pl.Buffered(1)
    in_specs = [pl.BlockSpec((bm, xp.shape[1]), lambda i, j: (first(i), 0), pipeline_mode=once) for xp in x_first]
    in_specs += [pl.BlockSpec((bm, xp.shape[1]), lambda i, j: (second(i), 0), pipeline_mode=once)
                 for xp in x_second]
    in_specs += [pl.BlockSpec((None, k, bn), lambda i, j: (layer, 0, j)),
                 pl.BlockSpec((bm, bn), lambda i, j: (first(i), j)),
                 pl.BlockSpec((bm, bn), lambda i, j: (second(i) + r0b, j)),
                 pl.BlockSpec((1, d), lambda i, j: (0, 0)),
                 pl.BlockSpec((1, d), lambda i, j: (0, 0))]
    return pl.pallas_call(
        functools.partial(_mm_ln_cols_kernel, n_parts=len(x_first), nj=nj, bn=bn, alpha=alpha, first_rows=nb1),
        grid=(nb1 + nb2, nj),
        in_specs=in_specs,
        out_specs=[pl.BlockSpec((bm, d), lambda i, j: (i, 0)),
                   pl.BlockSpec((bm, d), lambda i, j: (i, 0))],
        out_shape=[jax.ShapeDtypeStruct((m1 + m2, d), F32), jax.ShapeDtypeStruct((m1 + m2, d), BF16)],
        compiler_params=_params("parallel", "arbitrary"),
        name=name,
    )(*x_first, *x_second, w, res_first, res_second, g, b)


def _tflash_kernel(nfull_ref, nkv_ref, *refs, n_parts, k_shared, hb, bk, mode, q_off):
    refs = list(refs)
    q_refs = [refs.pop(0) for _ in range(n_parts)]
    k_refs = [refs.pop(0) for _ in range(n_parts)]
    v_ref = refs.pop(0)
    o_ref, acc_ref = refs
    qi = pl.program_id(2)
    bq = o_ref.shape[0]
    qs = []
    for h in range(hb):
        hs = slice(h * LANES, (h + 1) * LANES)
        qs.append(jnp.concatenate([qr[:, hs].astype(F32).T.astype(BF16) for qr in q_refs], axis=0))
    acc_ref[...] = jnp.zeros(acc_ref.shape, F32)

    def block(j, carry, masked):
        ms, ls = carry
        off = pl.multiple_of(j * bk, bk)
        if masked:
            kpos = off + lax.broadcasted_iota(I32, (bk, bq), 0)
            qpos = q_off + qi * bq + lax.broadcasted_iota(I32, (bk, bq), 1)
            if mode == "chunk":
                mask = (kpos >> CHUNK_SHIFT) <= (qpos >> CHUNK_SHIFT)
            else:
                mask = kpos <= qpos
        new_m, new_l = [], []
        for h in range(hb):
            parts = []
            for p_i, kr in enumerate(k_refs):
                hs = 0 if k_shared[p_i] else h
                parts.append(kr[pl.ds(off, bk), hs * LANES:(hs + 1) * LANES])
            k = jnp.concatenate(parts, axis=1) if n_parts > 1 else parts[0]
            s = jnp.dot(k, qs[h], preferred_element_type=F32)
            if masked:
                s = jnp.where(mask, s, -jnp.inf)
            m_new = jnp.maximum(ms[h], jnp.max(s, axis=0, keepdims=True))
            p = jnp.exp2(s - m_new)
            alpha = jnp.exp2(ms[h] - m_new)
            new_l.append(alpha * ls[h] + jnp.sum(p, axis=0, keepdims=True))
            new_m.append(m_new)
            v = v_ref[pl.ds(off, bk), h * LANES:(h + 1) * LANES]
            acc_ref[h] = alpha * acc_ref[h] + lax.dot_general(v, p.astype(BF16), (((0,), (0,)), ((), ())),
                                                              preferred_element_type=F32)
        return tuple(new_m), tuple(new_l)

    init = (tuple(jnp.full((1, bq), M_INIT, F32) for _ in range(hb)),
            tuple(jnp.zeros((1, bq), F32) for _ in range(hb)))
    carry = lax.fori_loop(0, nfull_ref[qi], functools.partial(block, masked=False), init)
    if mode != "none":
        carry = lax.fori_loop(nfull_ref[qi], nkv_ref[qi], functools.partial(block, masked=True), carry)
    for h in range(hb):
        o_ref[:, h * LANES:(h + 1) * LANES] = (acc_ref[h] / carry[1][h]).T.astype(o_ref.dtype)


def _tflash(q_parts, q_head0, k_parts, k_shared, v, v_head0, nfull, nkv, *, heads, s, hb, bq, bk, mode, q_off,
            name):
    nb = v.shape[0]
    nq = nfull.shape[0]
    assert heads % hb == 0 and v_head0 % hb == 0 and all(h0 % hb == 0 for h0 in q_head0) and s % bk == 0
    in_specs = []
    for h0 in q_head0:
        in_specs.append(pl.BlockSpec((None, bq, hb * LANES),
                                     lambda b, g, qi, nf, nk, h0=h0: (b, qi, h0 // hb + g)))
    for shared in k_shared:
        if shared:
            in_specs.append(pl.BlockSpec((None, s, LANES), lambda b, g, qi, nf, nk: (b, 0, 0)))
        else:
            in_specs.append(pl.BlockSpec((None, s, hb * LANES), lambda b, g, qi, nf, nk: (b, 0, g)))
    in_specs.append(pl.BlockSpec((None, s, hb * LANES), lambda b, g, qi, nf, nk: (b, 0, v_head0 // hb + g)))
    grid_spec = pltpu.PrefetchScalarGridSpec(
        num_scalar_prefetch=2,
        grid=(nb, heads // hb, nq),
        in_specs=in_specs,
        out_specs=pl.BlockSpec((None, bq, hb * LANES), lambda b, g, qi, nf, nk: (b, qi, g)),
        scratch_shapes=[pltpu.VMEM((hb, LANES, bq), F32)],
    )
    return pl.pallas_call(
        functools.partial(_tflash_kernel, n_parts=len(q_parts), k_shared=tuple(k_shared), hb=hb, bk=bk,
                          mode=mode, q_off=q_off),
        grid_spec=grid_spec,
        out_shape=jax.ShapeDtypeStruct((nb, nq * bq, heads * LANES), BF16),
        compiler_params=_params("parallel", "parallel", "arbitrary"),
        name=name,
    )(nfull, nkv, *q_parts, *k_parts, v)


def _sparse_kernel(nvis_ref, aq_in, aqi_in, aw_in, ka_ref, kb_ref, k_ref, v_ref,
                   o_ref, aq_ref, aqi_ref, aw_ref, key_ref, bias_ref, acc_ref, j_ref,
                   *, nq, kb, s_real, topk, index_bits, q_off):
    b = pl.program_id(0)
    qi = pl.program_id(1)
    nblk = nvis_ref[b * nq + qi]
    bq = aq_in.shape[0]
    cols = min(bq, SCORE_COLS)
    int_min = jnp.int32(INT_MIN)
    qchunk = (q_off + qi * bq + lax.broadcasted_iota(I32, (1, bq), 1)) >> CHUNK_SHIFT

    for g in range(aq_in.shape[1] // LANES):
        gs = slice(g * LANES, (g + 1) * LANES)
        aq_ref[gs, :] = aq_in[:, gs].astype(F32).T.astype(BF16)
    for g in range(aqi_in.shape[1] // LANES):
        gs = slice(g * LANES, (g + 1) * LANES)
        aqi_ref[gs, :] = aqi_in[:, gs].astype(F32).T.astype(BF16)
    aw_ref[...] = aw_in[...].T

    def score_tile(t, carry):
        off = pl.multiple_of(t * SCORE_ROWS, SCORE_ROWS)
        ka = ka_ref[pl.ds(off, SCORE_ROWS), :]
        kbb = kb_ref[pl.ds(off, SCORE_ROWS), :]
        kidx = off + lax.broadcasted_iota(I32, (SCORE_ROWS, cols), 0)
        kchunk = jnp.where(kidx < s_real, kidx >> CHUNK_SHIFT, jnp.int32(INT_MAX))
        for c in range(bq // cols):
            cs = slice(c * cols, (c + 1) * cols)
            acc = jnp.zeros((SCORE_ROWS, cols), F32)
            for hp in range(IDX_HEADS // 2):
                qp = aqi_ref[hp * LANES:(hp + 1) * LANES, cs]
                ze = jnp.dot(ka, qp, preferred_element_type=F32)
                zo = jnp.dot(kbb, qp, preferred_element_type=F32)
                acc = acc + jnp.maximum(ze, 0.0) * aw_ref[2 * hp:2 * hp + 1, cs]
                acc = acc + jnp.maximum(zo, 0.0) * aw_ref[2 * hp + 1:2 * hp + 2, cs]
            bits = pltpu.bitcast(acc, I32)
            key = bits ^ ((bits >> 31) & jnp.int32(INT_MAX))
            key_ref[pl.ds(off, SCORE_ROWS), cs] = jnp.where(kchunk <= qchunk[:, cs], key, int_min)
        return carry

    vis_end = (((q_off + (qi + 1) * bq - 1) >> CHUNK_SHIFT) + 1) << CHUNK_SHIFT
    rows_vis = jnp.minimum((vis_end + COUNT_ROWS - 1) // COUNT_ROWS * COUNT_ROWS, nblk * kb)
    lax.fori_loop(0, rows_vis // SCORE_ROWS, score_tile, 0)

    def fill_tile(t, carry):
        off = pl.multiple_of(t * SCORE_ROWS, SCORE_ROWS)
        key_ref[pl.ds(off, SCORE_ROWS), :] = jnp.full((SCORE_ROWS, bq), INT_MIN, I32)
        return carry

    lax.fori_loop(rows_vis // SCORE_ROWS, nblk * (kb // SCORE_ROWS), fill_tile, 0)

    def count(pred):
        def body(t, c):
            off = pl.multiple_of(t * COUNT_ROWS, COUNT_ROWS)
            kk = key_ref[pl.ds(off, COUNT_ROWS), :]
            idx = off + lax.broadcasted_iota(I32, (COUNT_ROWS, bq), 0)
            hit = jnp.where(pred(kk, idx), 1.0, 0.0)
            return c + jnp.sum(hit.reshape(COUNT_ROWS // SUBLANES, SUBLANES, bq), axis=0)
        c = lax.fori_loop(0, rows_vis // COUNT_ROWS, body, jnp.zeros((SUBLANES, bq), F32))
        return jnp.sum(c, axis=0, keepdims=True)

    def bit_step(i, carry):
        t_u, n_keep = carry
        cand_u = t_u | (jnp.int32(1) << (31 - i))
        cand_s = cand_u ^ int_min
        cnt = count(lambda kk, idx: kk >= cand_s)
        take = cnt >= topk
        return jnp.where(take, cand_u, t_u), jnp.where(take, cnt, n_keep)

    t_u, n_ge = lax.fori_loop(0, 32, bit_step, (jnp.zeros((1, bq), I32), jnp.zeros((1, bq), F32)))
    t_s = t_u ^ int_min
    need = jnp.logical_and(n_ge > topk, t_u != 0)
    j_ref[...] = jnp.full(j_ref.shape, INT_MAX, I32)

    @pl.when(jnp.max(jnp.where(need, 1.0, 0.0)) > 0.0)
    def _():
        want = topk - count(lambda kk, idx: kk > t_s)

        def idx_step(i, jc):
            cand = jc | (jnp.int32(1) << (index_bits - 1 - i))
            cnt = count(lambda kk, idx: jnp.logical_and(kk == t_s, idx < cand))
            return jnp.where(cnt < want, cand, jc)
        jc = lax.fori_loop(0, index_bits, idx_step, jnp.zeros((1, bq), I32))
        j_ref[...] = jnp.where(need, jc, jnp.int32(INT_MAX))

    jcut = j_ref[...]
    acc_ref[...] = jnp.zeros(acc_ref.shape, F32)

    def attend_block(j, carry):
        ms, ls = carry
        off = pl.multiple_of(j * kb, kb)
        kk = key_ref[pl.ds(off, kb), :]
        idx = off + lax.broadcasted_iota(I32, (kb, bq), 0)
        sel = jnp.logical_or(kk > t_s, jnp.logical_and(kk == t_s, idx <= jcut))
        sel = jnp.logical_and(sel, kk != int_min)
        bias_ref[...] = jnp.where(sel, 0.0, -jnp.inf)
        kblk = k_ref[pl.ds(off, kb), :]
        vblk = v_ref[pl.ds(off, kb), :]
        new_m, new_l = [], []
        for h in range(A_HEADS):
            hs = slice(h * HEAD_DIM, (h + 1) * HEAD_DIM)
            s = jnp.dot(kblk, aq_ref[hs, :], preferred_element_type=F32) + bias_ref[...]
            m_new = jnp.maximum(ms[h], jnp.max(s, axis=0, keepdims=True))
            p = jnp.exp2(s - m_new)
            alpha = jnp.exp2(ms[h] - m_new)
            new_l.append(alpha * ls[h] + jnp.sum(p, axis=0, keepdims=True))
            new_m.append(m_new)
            acc_ref[hs, :] = alpha * acc_ref[hs, :] + lax.dot_general(
                vblk, p.astype(BF16), (((0,), (0,)), ((), ())), preferred_element_type=F32)
        return tuple(new_m), tuple(new_l)

    init = (tuple(jnp.full((1, bq), M_INIT, F32) for _ in range(A_HEADS)),
            tuple(jnp.zeros((1, bq), F32) for _ in range(A_HEADS)))
    _, ls = lax.fori_loop(0, nblk, attend_block, init)
    for h in range(A_HEADS):
        hs = slice(h * HEAD_DIM, (h + 1) * HEAD_DIM)
        o_ref[:, hs] = (acc_ref[hs, :] / ls[h]).T.astype(o_ref.dtype)


def _sparse_attention(aq, aqi, aw, ka, kb_, k, v, nvis, *, bq, kb, s_real, topk, q_off, name):
    nb, s = k.shape[0], k.shape[1]
    nq = nvis.shape[0] // nb
    assert s % kb == 0 and kb % COUNT_ROWS == 0 and kb % SCORE_ROWS == 0 and bq % min(bq, SCORE_COLS) == 0
    qspec = lambda w: pl.BlockSpec((None, bq, w), lambda b, qi, nv: (b, qi, 0))
    kspec = pl.BlockSpec((None, s, LANES), lambda b, qi, nv: (b, 0, 0))
    aq_w, aqi_w = A_HEADS * HEAD_DIM, IDX_HEADS * IDX_DIM
    grid_spec = pltpu.PrefetchScalarGridSpec(
        num_scalar_prefetch=1,
        grid=(nb, nq),
        in_specs=[qspec(aq_w), qspec(aqi_w), qspec(LANES), kspec, kspec, kspec, kspec],
        out_specs=qspec(aq_w),
        scratch_shapes=[pltpu.VMEM((aq_w, bq), BF16), pltpu.VMEM((aqi_w, bq), BF16), pltpu.VMEM((LANES, bq), F32),
                        pltpu.VMEM((s, bq), I32), pltpu.VMEM((kb, bq), F32),
                        pltpu.VMEM((aq_w, bq), F32), pltpu.VMEM((1, bq), I32)],
    )
    return pl.pallas_call(
        functools.partial(_sparse_kernel, nq=nq, kb=kb, s_real=s_real, topk=topk,
                          index_bits=max(1, (s - 1).bit_length()), q_off=q_off),
        grid_spec=grid_spec,
        out_shape=jax.ShapeDtypeStruct((nb, nq * bq, aq_w), BF16),
        compiler_params=_params("parallel", "arbitrary"),
        name=name,
    )(nvis, aq, aqi, aw, ka, kb_, k, v)


def _router_kernel(x_ref, whi_ref, wlo_ref, bias_ref, ids_ref, g0_ref, g1_ref):
    x = x_ref[...]
    xh = x.astype(BF16)
    xl = (x - xh.astype(F32)).astype(BF16)
    whi, wlo = whi_ref[...], wlo_ref[...]
    lg = (jnp.dot(xh, whi, preferred_element_type=F32) + jnp.dot(xh, wlo, preferred_element_type=F32)
          + jnp.dot(xl, whi, preferred_element_type=F32)) + bias_ref[...]
    lane = lax.broadcasted_iota(I32, lg.shape, 1)
    neg = -jnp.inf
    is_group = lane < N_GROUPS
    gl = jnp.where(is_group, lg, neg)
    gmax = jnp.max(gl, axis=-1, keepdims=True)
    grp = jnp.min(jnp.where(gl == gmax, lane, LANES), axis=-1, keepdims=True)
    g1 = 1.0 / jnp.sum(jnp.where(is_group, jnp.exp(gl - gmax), 0.0), axis=-1, keepdims=True)
    lo = N_GROUPS + grp * EXPERTS_PER_GROUP
    el = jnp.where(jnp.logical_and(lane >= lo, lane < lo + EXPERTS_PER_GROUP), lg, neg)
    v1 = jnp.max(el, axis=-1, keepdims=True)
    i1 = jnp.min(jnp.where(el == v1, lane, LANES), axis=-1, keepdims=True)
    el2 = jnp.where(lane == i1, neg, el)
    v2 = jnp.max(el2, axis=-1, keepdims=True)
    i2 = jnp.min(jnp.where(el2 == v2, lane, LANES), axis=-1, keepdims=True)
    e21 = jnp.exp(v2 - v1)
    den = 1.0 + e21
    ids_ref[...] = jnp.where(lane == 0, i1 - N_GROUPS, jnp.where(lane == 1, i2 - N_GROUPS, 0))
    g0_ref[...] = jnp.broadcast_to(g1 * (1.0 / den), g0_ref.shape)
    g1_ref[...] = jnp.broadcast_to(g1 * (e21 / den), g1_ref.shape)


def _router(x, whi, wlo, bias, *, bm):
    n, d = x.shape
    row = pl.BlockSpec((bm, LANES), lambda i: (i, 0))
    return pl.pallas_call(
        _router_kernel,
        grid=(n // bm,),
        in_specs=[pl.BlockSpec((bm, d), lambda i: (i, 0)),
                  pl.BlockSpec((d, LANES), lambda i: (0, 0)),
                  pl.BlockSpec((d, LANES), lambda i: (0, 0)),
                  pl.BlockSpec((1, LANES), lambda i: (0, 0))],
        out_specs=[row, row, row],
        out_shape=[jax.ShapeDtypeStruct((n, LANES), I32), jax.ShapeDtypeStruct((n, LANES), F32),
                   jax.ShapeDtypeStruct((n, LANES), F32)],
        compiler_params=_params("parallel"),
        name="router",
    )(x, whi, wlo, bias)


def _dispatch_kernel(ppos_ref, x_ref, buf_hbm, xs_hbm, sems, *, bm):
    del buf_hbm
    base = pl.program_id(0) * bm * 2

    def row_copy(a):
        return pltpu.make_async_copy(x_ref.at[pl.ds(a >> 1, 1), :],
                                     xs_hbm.at[pl.ds(ppos_ref[base + a], 1), :], sems.at[a & 1])

    def start(a, c):
        row_copy(a).start()
        return c
    lax.fori_loop(0, 2 * bm, start, 0, unroll=4)

    def wait(a, c):
        row_copy(a).wait()
        return c
    lax.fori_loop(0, 2 * bm, wait, 0, unroll=4)


def _dispatch(x, ppos, buf, *, bm):
    n, d = x.shape
    assert n % bm == 0
    grid_spec = pltpu.PrefetchScalarGridSpec(
        num_scalar_prefetch=1,
        grid=(n // bm,),
        in_specs=[pl.BlockSpec((bm, d), lambda i, pp: (i, 0)), pl.BlockSpec(memory_space=pl.ANY)],
        out_specs=pl.BlockSpec(memory_space=pl.ANY),
        scratch_shapes=[pltpu.SemaphoreType.DMA((2,))],
    )
    return pl.pallas_call(
        functools.partial(_dispatch_kernel, bm=bm),
        grid_spec=grid_spec,
        out_shape=jax.ShapeDtypeStruct(buf.shape, buf.dtype),
        input_output_aliases={2: 0},
        compiler_params=_params("arbitrary"),
        name="dispatch",
    )(ppos, x, buf)


def _expert_kernel(te_ref, nv_ref, x_ref, wg_ref, wu_ref, wd_ref, y_ref):
    t = pl.program_id(0)

    @pl.when(nv_ref[t] > 0)
    def _():
        x = x_ref[...].astype(BF16)
        hg = jnp.dot(x, wg_ref[...], preferred_element_type=F32)
        hu = jnp.dot(x, wu_ref[...], preferred_element_type=F32)
        act = (hg * (1.0 / (1.0 + jnp.exp(-hg)))) * hu
        y_ref[...] = jnp.dot(act.astype(BF16), wd_ref[...], preferred_element_type=F32)

    @pl.when(nv_ref[t] == 0)
    def _():
        y_ref[...] = jnp.zeros(y_ref.shape, F32)


def _experts(xs, wg, wu, wd, layer, tile_e, tile_nvalid, *, bm):
    p, d = xs.shape
    f = wg.shape[-1]
    grid_spec = pltpu.PrefetchScalarGridSpec(
        num_scalar_prefetch=2,
        grid=(p // bm,),
        in_specs=[pl.BlockSpec((bm, d), lambda t, te, nv: (t, 0)),
                  pl.BlockSpec((None, None, d, f), lambda t, te, nv: (layer, te[t], 0, 0)),
                  pl.BlockSpec((None, None, d, f), lambda t, te, nv: (layer, te[t], 0, 0)),
                  pl.BlockSpec((None, None, f, d), lambda t, te, nv: (layer, te[t], 0, 0))],
        out_specs=pl.BlockSpec((bm, d), lambda t, te, nv: (t, 0)),
    )
    return pl.pallas_call(
        _expert_kernel,
        grid_spec=grid_spec,
        out_shape=jax.ShapeDtypeStruct((p, d), F32),
        compiler_params=_params("arbitrary"),
        name="experts",
    )(tile_e, tile_nvalid, xs, wg, wu, wd)


def _expert_plan(eid, *, bm):
    na = eid.shape[0] * 2
    flat_e = eid.reshape(na)
    onehot = (flat_e[:, None] == jnp.arange(N_EXPERTS, dtype=I32)[None, :]).astype(I32)
    csum = jnp.cumsum(onehot, axis=0)
    counts = csum[-1]
    padded = ((counts + bm - 1) // bm) * bm
    pend = jnp.cumsum(padded)
    pstart = pend - padded
    ppos = jnp.sum(onehot * (pstart[None, :] + csum - 1), axis=1).astype(I32)
    total = na + N_EXPERTS * bm
    tile_start = jnp.arange(total // bm, dtype=I32) * bm
    tile_e = jnp.minimum(jnp.sum((tile_start[:, None] >= pend[None, :]).astype(I32), axis=1), N_EXPERTS - 1)
    tile_nvalid = jnp.clip(pstart[tile_e] + counts[tile_e] - tile_start, 0, bm).astype(I32)
    return ppos, tile_e.astype(I32), tile_nvalid, total


def _combine_ln_kernel(ppos_ref, x_ref, g0_ref, g1_ref, y_hbm, g_ref, b_ref, of_ref, ob_ref, ybuf, sem,
                       *, bm, alpha):
    i = pl.program_id(0)
    d = x_ref.shape[-1]

    def row_copy(blk, a):
        par = blk & 1
        return pltpu.make_async_copy(y_hbm.at[pl.ds(ppos_ref[blk * (2 * bm) + a], 1), :],
                                     ybuf.at[par, a & 1, pl.ds(a >> 1, 1), :], sem.at[par, a & 1])

    def start_block(blk):
        def start(a, c):
            row_copy(blk, a).start()
            return c
        lax.fori_loop(0, 2 * bm, start, 0, unroll=4)

    @pl.when(i == 0)
    def _():
        start_block(i)

    @pl.when(i + 1 < pl.num_programs(0))
    def _():
        start_block(i + 1)

    def wait(a, c):
        row_copy(i, a).wait()
        return c
    lax.fori_loop(0, 2 * bm, wait, 0, unroll=4)

    reps = d // LANES
    par = i & 1
    y = (alpha * x_ref[...] + jnp.tile(g0_ref[...], (1, reps)) * ybuf[par, 0]
         + jnp.tile(g1_ref[...], (1, reps)) * ybuf[par, 1])
    out = _layer_norm(y, g_ref[...], b_ref[...])
    of_ref[...] = out
    ob_ref[...] = out.astype(BF16)


def _combine_ln(x, ys, ppos, g0, g1, g, b, *, bm, alpha):
    n, d = x.shape
    row = lambda w: pl.BlockSpec((bm, w), lambda i, pp: (i, 0))
    grid_spec = pltpu.PrefetchScalarGridSpec(
        num_scalar_prefetch=1,
        grid=(n // bm,),
        in_specs=[row(d), row(LANES), row(LANES), pl.BlockSpec(memory_space=pl.ANY),
                  pl.BlockSpec((1, d), lambda i, pp: (0, 0)), pl.BlockSpec((1, d), lambda i, pp: (0, 0))],
        out_specs=[row(d), row(d)],
        scratch_shapes=[pltpu.VMEM((2, 2, bm, d), F32), pltpu.SemaphoreType.DMA((2, 2))],
    )
    return pl.pallas_call(
        functools.partial(_combine_ln_kernel, bm=bm, alpha=alpha),
        grid_spec=grid_spec,
        out_shape=[jax.ShapeDtypeStruct((n, d), F32), jax.ShapeDtypeStruct((n, d), BF16)],
        compiler_params=_params("arbitrary"),
        name="combine_ln",
    )(ppos, x, g0, g1, ys, g, b)


def _rope_tables(pos):
    posf = pos.astype(F32)[:, None]
    inv128 = ROPE_THETA ** (-jnp.arange(HEAD_DIM // 2, dtype=F32) * 2.0 / HEAD_DIM)
    ang = posf * inv128[None, :]
    cos, sin = jnp.cos(ang), jnp.sin(ang)
    a128 = jnp.concatenate([cos, cos], axis=-1)
    b128 = jnp.concatenate([-sin, sin], axis=-1)
    inv64 = ROPE_THETA ** (-jnp.arange(IDX_DIM // 2, dtype=F32) * 2.0 / IDX_DIM)
    ang = posf * inv64[None, :]
    cos, sin = jnp.cos(ang), jnp.sin(ang)
    zero = jnp.zeros_like(sin)
    a64 = jnp.concatenate([cos, cos, cos, cos], axis=-1)
    b64 = jnp.concatenate([-sin, zero, -sin, zero], axis=-1)
    c64 = jnp.concatenate([zero, sin, zero, sin], axis=-1)
    return (a128, b128), (a64, b64, c64)


def _z_weight(w_in):
    o = _IN_OFF
    d = w_in.shape[0]
    cols = [w_in[:, o[0]:o[4]],
            w_in[:, o[6]:o[9]],
            w_in[:, o[10]:o[12]],
            w_in[:, o[4]:o[5]],
            w_in[:, o[12]:o[13]],
            w_in[:, o[5]:o[6]],
            w_in[:, o[9]:o[10]],
            jnp.zeros((d, LANES - IDX_HEADS - B_HEADS), w_in.dtype)]
    return jnp.concatenate(cols, axis=1).astype(BF16)


def _uq_weight(w):
    r = w.shape[0]
    w3 = w.reshape(r, C_HEADS, C_NOPE + C_ROPE)
    nope = w3[:, :, :C_NOPE].reshape(r, C_HEADS * C_NOPE)
    rope = jnp.pad(w3[:, :, C_NOPE:], ((0, 0), (0, 0), (0, LANES - C_ROPE))).reshape(r, C_HEADS * LANES)
    return jnp.concatenate([nope, rope], axis=1).astype(BF16)


def _pad_lanes(x):
    return jnp.pad(x, [(0, 0)] * (x.ndim - 1) + [(0, LANES - x.shape[-1])])


def _pad_rows(x, rows):
    return jnp.pad(x, [(0, 0), (0, rows - x.shape[1])] + [(0, 0)] * (x.ndim - 2))


def _bf16_head(c):
    bits = lax.bitcast_convert_type(c, I32) & jnp.int32(-65536)
    return lax.bitcast_convert_type(bits, F32)


def _fox_bias_kernel(c_ref, q_ref, k_ref):
    c = c_ref[...] * LOG2E
    lane = lax.broadcasted_iota(I32, c.shape, 1)
    head_mask = jnp.int32(-65536)
    for h in range(B_HEADS):
        ch = jnp.sum(jnp.where(lane == h, c, 0.0), axis=1, keepdims=True)
        hi = pltpu.bitcast(pltpu.bitcast(ch, I32) & head_mask, F32)
        r = ch - hi
        mid = pltpu.bitcast(pltpu.bitcast(r, I32) & head_mask, F32)
        lo = r - mid
        parts = jnp.where(lane == 0, hi, jnp.where(lane == 1, mid, jnp.where(lane == 2, lo, 0.0)))
        ones = jnp.where(jnp.logical_and(lane >= 3, lane < 6), 1.0, 0.0)
        q_ref[:, h * LANES:(h + 1) * LANES] = (parts + ones).astype(BF16)
        k_ref[:, h * LANES:(h + 1) * LANES] = (pltpu.roll(ones, LANES - 3, 1) - pltpu.roll(parts, 3, 1)).astype(BF16)


def _fox_bias_parts(csum):
    nb, s, h = csum.shape
    rows = nb * s
    bm = _pick(rows, 512, 256, 128, 64, 32, 16)
    spec = lambda w: pl.BlockSpec((bm, w), lambda i: (i, 0))
    qb, kb = pl.pallas_call(
        _fox_bias_kernel,
        grid=(rows // bm,),
        in_specs=[spec(LANES)],
        out_specs=[spec(h * LANES), spec(h * LANES)],
        out_shape=[jax.ShapeDtypeStruct((rows, h * LANES), BF16)] * 2,
        compiler_params=_params("parallel"),
        name="fox_bias",
    )(_pad_lanes(csum.reshape(rows, h)))
    return qb.reshape(nb, s, -1), kb.reshape(nb, s, -1)


def _pick(n, *cands):
    for c in cands:
        if n % c == 0:
            return c
    return n


def kernel(x_prompt, x_sample, cache_a_k, cache_a_v, cache_a_kidx, cache_b_k, cache_b_v, cache_b_logf,
           cache_c_latent, cache_c_krope, cache_mem_k, cache_mem_v, mem_prompt,
           w_in, b_f, c_q_norm, w_c_uq, c_kv_norm, w_c_uk, w_c_uv, w_o, ln1_g, ln1_b,
           w_mq, w_mk, w_mv, w_mo, ln2_g, ln2_b, w_rg, b_rg, w_re, b_re,
           w_gate, w_up, w_down, ln3_g, ln3_b):
    depth = w_in.shape[0]
    bp, tp, d = x_prompt.shape
    bs, ts, _ = x_sample.shape
    past = cache_a_k.shape[2]
    n_mem = mem_prompt.shape[1]
    assert bp == 1
    n_p, n_s = bp * tp, bs * ts
    n = n_p + n_s
    alpha = (2 * depth) ** 0.25
    s_s = past + ts
    s_pad = -(-s_s // COUNT_ROWS) * COUNT_ROWS

    bm_tok = _pick(n, 1024, 512, 256, 128, 64)
    bm_post = _pick(n, 256, 128, 64)
    bm_ln = _pick(n, 256, 128, 64)
    bq_f = _pick(tp, 1024, 512, 256, 128)
    bq_a = _pick(tp, 512, 256, 128)
    kb_a = _pick(tp, 1024, 512, 256)
    bm_moe = 256

    res_first, res_second, res_second_row0 = x_prompt.reshape(n_p, d), x_sample.reshape(n_s, d), 0
    xb = jnp.concatenate([res_first.astype(BF16), res_second.astype(BF16)], axis=0)
    pos_p = jnp.arange(tp, dtype=I32)
    pos_s = past + jnp.arange(ts, dtype=I32)
    tabs128, tabs64 = _rope_tables(jnp.concatenate([pos_p, jnp.tile(pos_s, bs)]))

    nq_f = tp // bq_f
    nfull_p = jnp.arange(nq_f, dtype=I32)
    nkv_p = nfull_p + 1
    one = jnp.ones((1,), I32)
    zero = jnp.zeros((1,), I32)
    nvis_p = (((jnp.arange(tp // bq_a, dtype=I32) + 1) * bq_a + kb_a - 1) // kb_a).astype(I32)
    nvis_s = jnp.ones((bs,), I32)

    xs_buf = jnp.zeros((2 * n + N_EXPERTS * bm_moe, d), F32)
    w_o_b = w_o.astype(BF16)
    w_gate_b, w_up_b, w_down_b = w_gate.astype(BF16), w_up.astype(BF16), w_down.astype(BF16)
    state_p = [[] for _ in range(8)]
    state_s = [[] for _ in range(8)]
    mem_k_out, mem_v_out = [], []

    for l in range(depth):
        z = _matmul(xb, _z_weight(w_in[l]), bm=bm_tok, bn=512, out_dtype=F32, name="in_proj")
        bf_row = jnp.zeros((1, LANES), F32).at[0, SM2_BF:SM2_BF + B_HEADS].set(b_f[l].astype(F32))
        aq, ak, aqi, bq_, cqn, clat, sm1, sm2, akb, avb, bkb, bvb = _post(
            z, tabs128, tabs64, bf_row, c_q_norm[l].reshape(1, -1).astype(F32),
            c_kv_norm[l].reshape(1, -1).astype(F32), bm=bm_post)
        cq = _matmul_rope(cqn, _uq_weight(w_c_uq[l]), tabs64, bm=bm_tok, bn=512,
                          first_rope_col=C_HEADS * C_NOPE, out_scale=(C_NOPE + C_ROPE) ** -0.5 * LOG2E,
                          name="c_q_up")
        av = z[:, Z_AV:Z_AV + HEAD_DIM]
        aki = sm1[:, :IDX_DIM]
        ckr = sm1[:, IDX_DIM:]
        blogf = sm2[:, SM2_BF:SM2_BF + B_HEADS]
        bk_ = z[:, Z_BK:Z_BK + B_HEADS * HEAD_DIM]
        bv_ = z[:, Z_BV:Z_BV + B_HEADS * HEAD_DIM]
        rows = (ak, av, aki, bk_.reshape(n, B_HEADS, HEAD_DIM), bv_.reshape(n, B_HEADS, HEAD_DIM),
                blogf, clat, ckr)
        for i, r in enumerate(rows):
            state_p[i].append(r[:n_p].reshape((bp, tp) + r.shape[1:]))
            state_s[i].append(r[n_p:].reshape((bs, ts) + r.shape[1:]))
        grp_p = lambda a: a[:n_p].reshape(bp, tp, -1)
        grp_s = lambda a: a[n_p:].reshape(bs, ts, -1)
        cat_s = lambda c, r: jnp.concatenate([c.reshape(bs, past, -1).astype(r.dtype), grp_s(r)], axis=1)
        all_n = lambda a: a.reshape(1, n, -1)

        def sparse(aq_g, aqi_g, aw_g, ki, k, v, nvis, *, bq, kb, s_real, q_off, name):
            ka = _pad_lanes(ki).astype(BF16)
            kb_ = jnp.concatenate([jnp.zeros_like(ki), ki], axis=-1).astype(BF16)
            out = _sparse_attention(aq_g, aqi_g, aw_g, ka, kb_, k, v, nvis, bq=bq, kb=kb, s_real=s_real,
                                    topk=min(TOPK_MAX, s_real // 4), q_off=q_off, name=name)
            return out.reshape(-1, A_HEADS * HEAD_DIM)

        oa_p = sparse(all_n(aq), all_n(aqi), all_n(sm2), grp_p(aki), grp_p(akb), grp_p(avb), nvis_p,
                      bq=bq_a, kb=kb_a, s_real=tp, q_off=0, name="sparse_prompt")
        oa_s = sparse(grp_s(aq), grp_s(aqi), grp_s(sm2),
                      _pad_rows(cat_s(cache_a_kidx[l], aki), s_pad), _pad_rows(cat_s(cache_a_k[l], akb), s_pad),
                      _pad_rows(cat_s(cache_a_v[l], avb), s_pad), nvis_s,
                      bq=ts, kb=s_pad, s_real=s_s, q_off=past, name="sparse_sample")

        csum_p = jnp.cumsum(grp_p(blogf), axis=1)
        qb_p, kbias_p = _fox_bias_parts(csum_p)
        ob_p = _tflash([all_n(bq_), qb_p], (0, 0), [all_n(bkb), kbias_p], (False, False), all_n(bvb), 0,
                       nfull_p, nkv_p, heads=B_HEADS, s=tp, hb=2, bq=bq_f, bk=bq_f, mode="causal", q_off=0,
                       name="fox_prompt")
        csum_s = jnp.cumsum(cat_s(cache_b_logf[l], blogf), axis=1)
        qb_all_s, kbias_s = _fox_bias_parts(csum_s)
        qb_s = qb_all_s[:, past:]
        ob_s = _tflash([grp_s(bq_), qb_s], (0, 0), [cat_s(cache_b_k[l], bkb), kbias_s], (False, False),
                       cat_s(cache_b_v[l], bvb), 0, zero, one, heads=B_HEADS, s=s_s, hb=B_HEADS, bq=ts, bk=s_s,
                       mode="causal", q_off=past, name="fox_sample")

        w_ukv = jnp.concatenate([w_c_uk[l], w_c_uv[l]], axis=1).astype(BF16)
        kv_p = _matmul(clat[:n_p], w_ukv, bm=_pick(n_p, 1024, 512, 256), bn=w_ukv.shape[1], out_dtype=BF16,
                       name="c_kv_up_prompt").reshape(bp, tp, -1)
        oc_p = _tflash([all_n(cq), all_n(cq)], (0, C_HEADS),
                       [kv_p, _pad_lanes(grp_p(ckr)).astype(BF16)], (False, True), kv_p, C_HEADS,
                       nfull_p, nkv_p, heads=C_HEADS, s=tp, hb=2, bq=bq_f, bk=bq_f, mode="chunk", q_off=0,
                       name="mla_prompt")
        lat_s = cat_s(cache_c_latent[l], clat)
        kv_s = _matmul(lat_s.reshape(bs * s_s, -1), w_ukv, bm=_pick(bs * s_s, 1024, 512, 256, 128, 64),
                       bn=w_ukv.shape[1], out_dtype=BF16, name="c_kv_up_sample").reshape(bs, s_s, -1)
        oc_s = _tflash([grp_s(cq), grp_s(cq)], (0, C_HEADS),
                       [kv_s, _pad_lanes(cat_s(cache_c_krope[l], ckr)).astype(BF16)], (False, True), kv_s, C_HEADS,
                       zero, one, heads=C_HEADS, s=s_s, hb=C_HEADS // 2, bq=ts, bk=s_s, mode="chunk", q_off=past,
                       name="mla_sample")

        x1, x1b = _mm_ln_cols([oa_p, ob_p.reshape(n_p, -1), oc_p.reshape(n_p, -1)],
                              [oa_s, ob_s.reshape(n_s, -1), oc_s.reshape(n_s, -1)],
                              w_o_b, l, res_first, res_second, res_second_row0,
                              ln1_g[l].reshape(1, d), ln1_b[l].reshape(1, d),
                              bm=_pick(math.gcd(n_p, n_s), 512, 256, 128, 64), bn=_pick(d, 512, 256, 128),
                              alpha=alpha, name="out_proj_ln1")

        mw = MEM_HEADS * MEM_DIM
        w_mkv = jnp.concatenate([w_mk[l], w_mv[l]], axis=1).astype(BF16)
        mkv = _matmul(mem_prompt.reshape(bp * n_mem, d), w_mkv, bm=_pick(bp * n_mem, 256, 128, 64), bn=512,
                      out_dtype=F32, name="mem_kv")
        mem_k_out.append(mkv[:, :mw].reshape(bp, n_mem, MEM_HEADS, MEM_DIM))
        mem_v_out.append(mkv[:, mw:].reshape(bp, n_mem, MEM_HEADS, MEM_DIM))
        qm = _matmul(x1b, w_mq[l].astype(BF16), bm=bm_tok, bn=mw, out_dtype=BF16, name="mem_q",
                     out_scale=MEM_DIM ** -0.5 * LOG2E)
        mkv3 = mkv.reshape(bp, n_mem, 2 * mw).astype(BF16)
        om_p = _tflash([all_n(qm)], (0,), [mkv3], (False,), mkv3, MEM_HEADS,
                       jnp.ones((nq_f,), I32), jnp.ones((nq_f,), I32), heads=MEM_HEADS, s=n_mem,
                       hb=2, bq=bq_f, bk=n_mem, mode="none", q_off=0, name="mem_prompt")
        om_s = _tflash([grp_s(qm)], (0,), [cache_mem_k[l].reshape(bs, n_mem, mw).astype(BF16)], (False,),
                       cache_mem_v[l].reshape(bs, n_mem, mw).astype(BF16), 0, one, one, heads=MEM_HEADS, s=n_mem,
                       hb=MEM_HEADS, bq=ts, bk=n_mem, mode="none", q_off=0, name="mem_sample")
        om = jnp.concatenate([om_p.reshape(n_p, mw), om_s.reshape(n_s, mw)], axis=0)
        x2, _ = _mm_res_ln(om, w_mo[l].astype(BF16), x1, ln2_g[l].reshape(1, d), ln2_b[l].reshape(1, d),
                           bm=bm_ln, bk=mw, alpha=alpha, name="mem_out_ln2")

        w_r = jnp.concatenate([w_rg[l], jnp.transpose(w_re[l], (1, 0, 2)).reshape(d, N_EXPERTS)], axis=1).astype(F32)
        w_r = _pad_lanes(w_r)
        w_r_head = _bf16_head(w_r)
        w_r_hi = w_r_head.astype(BF16)
        w_r_lo = (w_r - w_r_head).astype(BF16)
        b_r = _pad_lanes(jnp.concatenate([b_rg[l], b_re[l].reshape(-1)]).astype(F32).reshape(1, -1))
        ids, g0, g1 = _router(x2, w_r_hi, w_r_lo, b_r, bm=_pick(n, 512, 256, 128, 64))
        ppos, tile_e, tile_nvalid, total = _expert_plan(ids[:, :2], bm=bm_moe)
        assert total == xs_buf.shape[0]
        xs_buf = _dispatch(x2, ppos, xs_buf, bm=_pick(n, 256, 128, 64))
        ys = _experts(xs_buf, w_gate_b, w_up_b, w_down_b, l, tile_e, tile_nvalid, bm=bm_moe)
        x, xb = _combine_ln(x2, ys, ppos, g0, g1, ln3_g[l].reshape(1, d), ln3_b[l].reshape(1, d),
                            bm=_pick(n, 256, 128, 64), alpha=alpha)
        res_first, res_second, res_second_row0 = x, x, n_p

    outs = [x[:n_p].reshape(bp, tp, d), x[n_p:].reshape(bs, ts, d)]
    outs += [jnp.stack(c) for c in state_p]
    outs += [jnp.stack(mem_k_out), jnp.stack(mem_v_out)]
    outs += [jnp.stack(c) for c in state_s]
    return tuple(outs)
```
